```python
import numpy as np
import jax
import jax.numpy as jnp
from jax import lax

D_MODEL = 2048
BATCH = 1
SEQ = 8192
DEPTH = 1

A_HEADS = 12
A_HEAD_DIM = 128
A_WIDTH = A_HEADS * A_HEAD_DIM
A_SCALE = A_HEAD_DIM ** -0.5
DILATED_CONFIGS = ((128, 1), (512, 4), (2048, 16))
B_HEADS = 8
B_KEY_DIM = 128
B_VAL_DIM = 128
B_WIDTH = B_HEADS * B_KEY_DIM
B_CHUNK = 64
N_GROUPS = 4
EXPERTS_PER_GROUP = 8
EXPERT_FF = 512
TOP_K_IN_GROUP = 2
NORM_EPS = 1e-6
IN_SPLITS = (A_WIDTH, A_WIDTH, A_WIDTH, B_WIDTH, B_WIDTH, B_WIDTH, B_WIDTH, D_MODEL, D_MODEL)
IN_WIDTH = A_WIDTH * 3 + B_WIDTH * 4 + D_MODEL * 2

kernel_name = 'hybrid_dilated_attn_hgrn2_hmoe'


def rms_norm(x, g):
    xf = x.astype(jnp.float32)
    xf = xf * lax.rsqrt(jnp.mean(xf * xf, axis=-1, keepdims=True) + NORM_EPS)
    return xf * g.astype(jnp.float32)


def dilated_window_attention(q, k, v, window, dilation):
    b, s, h, dh = q.shape
    blk = window // dilation
    m = s // dilation
    nb = -(-m // blk)
    mp = nb * blk

    def to_sub(t):
        t = t.reshape(b, m, dilation, h, dh).transpose(0, 2, 3, 1, 4)
        return jnp.pad(t, ((0, 0), (0, 0), (0, 0), (0, mp - m), (0, 0)))

    def kv_blocks(t):
        tp = jnp.pad(t, ((0, 0), (0, 0), (0, 0), (blk, 0), (0, 0)))
        prev = tp[:, :, :, :mp].reshape(b, dilation, h, nb, blk, dh)
        cur = tp[:, :, :, blk:].reshape(b, dilation, h, nb, blk, dh)
        return jnp.concatenate([prev, cur], axis=4)

    qb = to_sub(q).reshape(b, dilation, h, nb, blk, dh)
    kb = kv_blocks(to_sub(k))
    vb = kv_blocks(to_sub(v))
    qi = jnp.arange(blk)[:, None]
    kj = jnp.arange(2 * blk)[None, :]
    dist = qi - kj + blk
    key_pos = jnp.arange(nb)[:, None, None] * blk - blk + kj[None]
    mask = (dist >= 0)[None] & (dist <= blk)[None] & (key_pos >= 0)
    scores = jnp.einsum('bdhnqe,bdhnke->bdhnqk', qb, kb) * A_SCALE
    scores = jnp.where(mask, scores, -jnp.inf)
    mx = jnp.max(scores, axis=-1, keepdims=True)
    p = jnp.exp(scores - mx)
    den = jnp.sum(p, axis=-1, keepdims=True)
    o = jnp.einsum('bdhnqk,bdhnke->bdhnqe', p, vb) / den
    lse = (mx + jnp.log(den))[..., 0]
    o = o.reshape(b, dilation, h, mp, dh)[:, :, :, :m].transpose(0, 3, 1, 2, 4).reshape(b, s, h, dh)
    lse = lse.reshape(b, dilation, h, mp)[:, :, :, :m].transpose(0, 3, 1, 2).reshape(b, s, h)
    return o, lse


def hgrn2_chunked(q, log_f, k, v):
    b, s, h, dk = q.shape
    dv = v.shape[-1]
    nc = s // B_CHUNK

    def chunks(t):
        return t.reshape(b, nc, B_CHUNK, h, t.shape[-1]).transpose(1, 0, 3, 2, 4)

    causal = jnp.tril(jnp.ones((B_CHUNK, B_CHUNK), dtype=bool))

    def step(state, inp):
        qc, lfc, kc, vc = inp
        bc = jnp.cumsum(lfc, axis=2)
        rel = bc[:, :, :, None, :] - bc[:, :, None, :, :]
        decay = jnp.exp(jnp.where(causal[:, :, None], rel, -jnp.inf))
        attn = jnp.einsum('bhtk,bhsk,bhtsk->bhts', qc, kc, decay)
        o = (jnp.einsum('bhts,bhsv->bhtv', attn, vc)
             + jnp.einsum('bhtk,bhkv->bhtv', qc * jnp.exp(bc), state))
        b_last = bc[:, :, -1:, :]
        state = (jnp.exp(b_last[:, :, 0, :])[..., None] * state
                 + jnp.einsum('bhsk,bhsv->bhkv', kc * jnp.exp(b_last - bc), vc))
        return state, o

    s0 = jnp.zeros((b, h, dk, dv), jnp.float32)
    _, o = lax.scan(step, s0, (chunks(q), chunks(log_f), chunks(k), chunks(v)))
    return o.transpose(1, 0, 3, 2, 4).reshape(b, s, h, dv)


def hierarchical_moe(xn, rwg, rbg, rwe, rbe, w1, w3, w2):
    b, s, d = xn.shape
    xf = xn.reshape(b * s, d)
    lg = (xf @ rwg).astype(jnp.float32) + rbg.astype(jnp.float32)
    pg = jax.nn.softmax(lg, axis=-1)
    g_idx = jnp.argmax(lg, axis=-1)
    pg_top = jnp.take_along_axis(pg, g_idx[:, None], axis=1)
    le = ((xf @ rwe).astype(jnp.float32) + rbe.astype(jnp.float32)).reshape(-1, N_GROUPS, EXPERTS_PER_GROUP)
    le_sel = jnp.take_along_axis(le, g_idx[:, None, None], axis=1)[:, 0]
    top_vals, top_idx = lax.top_k(le_sel, TOP_K_IN_GROUP)
    wk = jax.nn.softmax(top_vals, axis=-1) * pg_top
    within = jnp.einsum('tk,tke->te', wk, jax.nn.one_hot(top_idx, EXPERTS_PER_GROUP, dtype=jnp.float32))
    comb = jax.nn.one_hot(g_idx, N_GROUPS, dtype=jnp.float32)[:, :, None] * within[:, None, :]
    y = jnp.zeros((b * s, d), jnp.float32)
    for gi in range(N_GROUPS):
        hid = jax.nn.silu(jnp.einsum('td,edf->tef', xf, w1[gi])) * jnp.einsum('td,edf->tef', xf, w3[gi])
        y = y + jnp.einsum('tef,efd->td', hid * comb[:, gi, :, None], w2[gi])
    return y.reshape(b, s, d)


def setup_inputs(seed: int = 0) -> dict:
    key = jax.random.key(seed)
    ks = jax.random.split(key, 20)
    f32 = jnp.float32
    def nrm(k, shape, fan_in):
        return jax.random.normal(k, shape, f32) * (fan_in ** -0.5)
    E = N_GROUPS * EXPERTS_PER_GROUP
    return {
        'x': jax.random.normal(ks[0], (BATCH, SEQ, D_MODEL), f32),
        'mix_norm_g': 1.0 + 0.1 * jax.random.normal(ks[1], (DEPTH, D_MODEL), f32),
        'w_in': nrm(ks[2], (DEPTH, D_MODEL, IN_WIDTH), D_MODEL),
        'hgrn_lb_logits': jax.random.normal(ks[3], (DEPTH + 1, B_WIDTH), f32),
        'hgrn_norm_g': 1.0 + 0.1 * jax.random.normal(ks[4], (DEPTH, B_WIDTH), f32),
        'w_branch_a': nrm(ks[5], (DEPTH, A_WIDTH, D_MODEL), A_WIDTH),
        'w_branch_b': nrm(ks[6], (DEPTH, B_WIDTH, D_MODEL), B_WIDTH),
        'w_out': nrm(ks[7], (DEPTH, D_MODEL, D_MODEL), D_MODEL),
        'ffn_norm_g': 1.0 + 0.1 * jax.random.normal(ks[8], (DEPTH, D_MODEL), f32),
        'router_w_group': nrm(ks[9], (DEPTH, D_MODEL, N_GROUPS), D_MODEL),
        'router_b_group': 0.01 * jax.random.normal(ks[10], (DEPTH, N_GROUPS), f32),
        'router_w_expert': nrm(ks[11], (DEPTH, D_MODEL, E), D_MODEL),
        'router_b_expert': 0.01 * jax.random.normal(ks[12], (DEPTH, E), f32),
        'expert_w1': nrm(ks[13], (DEPTH, N_GROUPS, EXPERTS_PER_GROUP, D_MODEL, EXPERT_FF), D_MODEL),
        'expert_w3': nrm(ks[14], (DEPTH, N_GROUPS, EXPERTS_PER_GROUP, D_MODEL, EXPERT_FF), D_MODEL),
        'expert_w2': nrm(ks[15], (DEPTH, N_GROUPS, EXPERTS_PER_GROUP, EXPERT_FF, D_MODEL), EXPERT_FF),
        'final_norm_g': 1.0 + 0.1 * jax.random.normal(ks[16], (D_MODEL,), f32),
    }


def reference(x, mix_norm_g, w_in, hgrn_lb_logits, hgrn_norm_g, w_branch_a, w_branch_b, w_out,
              ffn_norm_g, router_w_group, router_b_group, router_w_expert, router_b_expert,
              expert_w1, expert_w3, expert_w2, final_norm_g):
    b, s, _ = x.shape
    f32 = jnp.float32
    h = x.astype(f32)
    offsets = np.cumsum(IN_SPLITS)[:-1].tolist()
    lb_all = jnp.cumsum(jax.nn.softmax(hgrn_lb_logits.astype(f32), axis=0), axis=0)
    for l in range(DEPTH):
        xn = rms_norm(h, mix_norm_g[l])
        proj = (xn @ w_in[l].astype(f32))
        qa, ka, va, qb, fb, ib, gb, gate_a, gate_b = jnp.split(proj, offsets, axis=-1)
        qa = qa.reshape(b, s, A_HEADS, A_HEAD_DIM)
        ka = ka.reshape(b, s, A_HEADS, A_HEAD_DIM)
        va = va.reshape(b, s, A_HEADS, A_HEAD_DIM)
        outs, lses = [], []
        for window, dilation in DILATED_CONFIGS:
            o_c, lse_c = dilated_window_attention(qa, ka, va, window, dilation)
            outs.append(o_c)
            lses.append(lse_c)
        mix_w = jax.nn.softmax(jnp.stack(lses, axis=0), axis=0)
        oa = jnp.einsum('cbsh,cbshe->bshe', mix_w, jnp.stack(outs, axis=0)).reshape(b, s, A_WIDTH)
        ya = oa @ w_branch_a[l].astype(f32)
        lb = lb_all[l]
        f = lb + (1.0 - lb) * jax.nn.sigmoid(fb)
        q_h = jax.nn.silu(qb).reshape(b, s, B_HEADS, B_KEY_DIM)
        logf_h = jnp.log(f).reshape(b, s, B_HEADS, B_KEY_DIM)
        k_h = (1.0 - f).reshape(b, s, B_HEADS, B_KEY_DIM)
        v_h = ib.reshape(b, s, B_HEADS, B_VAL_DIM)
        ob = hgrn2_chunked(q_h, logf_h, k_h, v_h)
        ob = ob * lax.rsqrt(jnp.mean(ob * ob, axis=-1, keepdims=True) + NORM_EPS)
        ob = ob.reshape(b, s, B_WIDTH) * hgrn_norm_g[l].astype(f32) * jax.nn.silu(gb)
        yb = ob @ w_branch_b[l].astype(f32)
        merged = jax.nn.sigmoid(gate_a) * ya + jax.nn.sigmoid(gate_b) * yb
        h = h + merged @ w_out[l].astype(f32)
        hn = rms_norm(h, ffn_norm_g[l])
        h = h + hierarchical_moe(hn, router_w_group[l].astype(f32), router_b_group[l], router_w_expert[l].astype(f32),
                                 router_b_expert[l], expert_w1[l].astype(f32), expert_w3[l].astype(f32),
                                 expert_w2[l].astype(f32))
    out = rms_norm(h, final_norm_g)
    return out.astype(x.dtype)
```

```python
import functools

import jax
import jax.numpy as jnp
from jax import lax
from jax.experimental import pallas as pl
from jax.experimental.pallas import tpu as pltpu

F32 = jnp.float32
BF16 = jnp.bfloat16

D_MODEL = 2048
A_HEADS = 12
A_HEAD_DIM = 128
A_WIDTH = A_HEADS * A_HEAD_DIM
A_SCALE = A_HEAD_DIM ** -0.5
DILATED_CONFIGS = ((128, 1), (512, 4), (2048, 16))
B_HEADS = 8
B_KEY_DIM = 128
B_WIDTH = B_HEADS * B_KEY_DIM
N_GROUPS = 4
EXPERTS_PER_GROUP = 8
N_EXPERTS = N_GROUPS * EXPERTS_PER_GROUP
EXPERT_FF = 512
NORM_EPS = 1e-6

LANES = 128
SUBLANES = 8
VMEM_LIMIT = 56 * 1024 * 1024

_QA_BLK = 0
_KA_BLK = A_HEADS
_VA_BLK = 2 * A_HEADS
_QB_BLK = 3 * A_HEADS
_FB_BLK = _QB_BLK + B_HEADS
_IB_BLK = _FB_BLK + B_HEADS
_GB_BLK = _IB_BLK + B_HEADS
_GATE_A_COL = 3 * A_WIDTH + 4 * B_WIDTH
_GATE_B_COL = _GATE_A_COL + D_MODEL
IN_WIDTH = _GATE_B_COL + D_MODEL


def _params(*sem):
    return pltpu.CompilerParams(dimension_semantics=sem, vmem_limit_bytes=VMEM_LIMIT)


def _rmsnorm_rows(x, g):
    ms = jnp.mean(x * x, axis=-1, keepdims=True)
    return x * lax.rsqrt(ms + NORM_EPS) * g


def _rmsnorm_body(x_ref, g_ref, o_ref):
    o_ref[...] = _rmsnorm_rows(x_ref[...].astype(F32), g_ref[...]).astype(o_ref.dtype)


def _rmsnorm(x, g, out_dtype, tm=512):
    m, d = x.shape
    return pl.pallas_call(
        _rmsnorm_body,
        grid=(m // tm,),
        in_specs=[pl.BlockSpec((tm, d), lambda i: (i, 0)), pl.BlockSpec((1, d), lambda i: (0, 0))],
        out_specs=pl.BlockSpec((tm, d), lambda i: (i, 0)),
        out_shape=jax.ShapeDtypeStruct((m, d), out_dtype),
        compiler_params=_params("arbitrary"),
        name="rmsnorm",
    )(x, g.reshape(1, d).astype(F32))


def _matmul_body(a_ref, w_ref, o_ref, wb_ref):
    @pl.when(pl.program_id(1) == 0)
    def _():
        wb_ref[...] = w_ref[...].astype(BF16)

    o_ref[...] = jnp.dot(a_ref[...], wb_ref[...], preferred_element_type=F32).astype(o_ref.dtype)


def _matmul(a, w, out_dtype, tm, tn, name):
    m, k = a.shape
    n = w.shape[1]
    return pl.pallas_call(
        _matmul_body,
        grid=(n // tn, m // tm),
        in_specs=[pl.BlockSpec((tm, k), lambda j, i: (i, 0)), pl.BlockSpec((k, tn), lambda j, i: (0, j))],
        out_specs=pl.BlockSpec((tm, tn), lambda j, i: (i, j)),
        out_shape=jax.ShapeDtypeStruct((m, n), out_dtype),
        scratch_shapes=[pltpu.VMEM((k, tn), BF16)],
        compiler_params=_params("arbitrary", "arbitrary"),
        name=name,
    )(a, w)


def _matmul_residual_body(a_ref, w_ref, r_ref, o_ref, wb_ref):
    @pl.when(pl.program_id(1) == 0)
    def _():
        wb_ref[...] = w_ref[...].astype(BF16)

    o_ref[...] = r_ref[...] + jnp.dot(a_ref[...], wb_ref[...], preferred_element_type=F32)


def _matmul_residual(a, w, res, tm, tn, name):
    m, k = a.shape
    n = w.shape[1]
    return pl.pallas_call(
        _matmul_residual_body,
        grid=(n // tn, m // tm),
        in_specs=[pl.BlockSpec((tm, k), lambda j, i: (i, 0)), pl.BlockSpec((k, tn), lambda j, i: (0, j)),
                  pl.BlockSpec((tm, tn), lambda j, i: (i, j))],
        out_specs=pl.BlockSpec((tm, tn), lambda j, i: (i, j)),
        out_shape=jax.ShapeDtypeStruct((m, n), F32),
        scratch_shapes=[pltpu.VMEM((k, tn), BF16)],
        compiler_params=_params("arbitrary", "arbitrary"),
        name=name,
    )(a, w, res)


ATT_BLK = 128
ATT_SUPER = 2048


def _attn_body(q_ref, k_ref, v_ref, o_ref, o_scr, lse_scr):
    sb = pl.program_id(1)
    row0 = sb * ATT_SUPER
    diff = (lax.broadcasted_iota(jnp.int32, (ATT_BLK, 2 * ATT_BLK), 1)
            - lax.broadcasted_iota(jnp.int32, (ATT_BLK, 2 * ATT_BLK), 0))

    for c, (window, dil) in enumerate(DILATED_CONFIGS):
        span = ATT_BLK * dil
        tiles_per_res = ATT_SUPER // span

        def tile(t, carry, c=c, dil=dil, span=span, tiles_per_res=tiles_per_res):
            n = t % tiles_per_res
            r = t // tiles_per_res
            q_start = n * span + r
            first = jnp.logical_and(sb == 0, n == 0)
            kv_start = row0 + q_start - jnp.where(first, 0, span)
            lo = jnp.where(first, -ATT_BLK, 0)
            if dil == 1:
                q_start = pl.multiple_of(q_start, ATT_BLK)
                kv_start = pl.multiple_of(kv_start, ATT_BLK)
                qs = pl.ds(q_start, ATT_BLK)
                ks = pl.ds(kv_start, 2 * ATT_BLK)
            else:
                qs = pl.ds(q_start, ATT_BLK, stride=dil)
                ks = pl.ds(kv_start, 2 * ATT_BLK, stride=dil)
            q = (q_ref[qs, :] * A_SCALE).astype(BF16)
            k = k_ref[ks, :].astype(BF16)
            v = v_ref[ks, :].astype(BF16)
            s = lax.dot_general(q, k, (((1,), (1,)), ((), ())), preferred_element_type=F32)
            neg = jnp.float32(-jnp.inf)
            s = jnp.where(diff >= lo, jnp.where(diff <= lo + ATT_BLK, s, neg), neg)
            mx = jnp.max(s, axis=-1, keepdims=True)
            p = jnp.exp(s - mx)
            den = jnp.sum(p, axis=-1, keepdims=True)
            acc = jnp.dot(p.astype(BF16), v, preferred_element_type=F32)
            o_scr[c, qs, :] = acc / den
            lse_scr[c, qs, :] = jnp.broadcast_to(mx + jnp.log(den), (ATT_BLK, A_HEAD_DIM))
            return carry

        lax.fori_loop(0, ATT_SUPER // ATT_BLK, tile, 0)

    rows = 256

    def merge(i, carry):
        sl = pl.ds(pl.multiple_of(i * rows, rows), rows)
        l0, l1, l2 = lse_scr[0, sl, :], lse_scr[1, sl, :], lse_scr[2, sl, :]
        m = jnp.maximum(jnp.maximum(l0, l1), l2)
        w0, w1, w2 = jnp.exp(l0 - m), jnp.exp(l1 - m), jnp.exp(l2 - m)
        num = w0 * o_scr[0, sl, :] + w1 * o_scr[1, sl, :] + w2 * o_scr[2, sl, :]
        o_ref[sl, :] = (num / (w0 + w1 + w2)).astype(o_ref.dtype)
        return carry

    lax.fori_loop(0, ATT_SUPER // rows, merge, 0)


def _dilated_attention(proj, seq):
    n_super = seq // ATT_SUPER
    blk = lambda off: pl.BlockSpec((seq, A_HEAD_DIM), lambda h, s: (0, off + h))
    return pl.pallas_call(
        _attn_body,
        grid=(A_HEADS, n_super),
        in_specs=[pl.BlockSpec((ATT_SUPER, A_HEAD_DIM), lambda h, s: (s, _QA_BLK + h)), blk(_KA_BLK), blk(_VA_BLK)],
        out_specs=pl.BlockSpec((ATT_SUPER, A_HEAD_DIM), lambda h, s: (s, h)),
        out_shape=jax.ShapeDtypeStruct((seq, A_WIDTH), BF16),
        scratch_shapes=[pltpu.VMEM((len(DILATED_CONFIGS), ATT_SUPER, A_HEAD_DIM), F32),
                        pltpu.VMEM((len(DILATED_CONFIGS), ATT_SUPER, A_HEAD_DIM), F32)],
        compiler_params=_params("arbitrary", "arbitrary"),
        name="dilated_attention",
    )(proj, proj, proj)


HG_CHUNK = 64
HG_TB = 512
HG_MIN_LEVEL = 4


def _hgrn_body(qb_ref, fb_ref, ib_ref, gb_ref, lbl_ref, g_ref, o_ref, st_ref):
    @pl.when(pl.program_id(1) == 0)
    def _():
        st_ref[...] = jnp.zeros_like(st_ref)

    c_ = HG_CHUNK
    lbl = lbl_ref[...]
    e = jnp.exp(lbl - jnp.max(lbl, axis=0, keepdims=True))
    lb = e[0:1, :] / jnp.sum(e, axis=0, keepdims=True)
    gnorm = g_ref[...]

    ti = lax.broadcasted_iota(jnp.int32, (c_, c_), 0)
    si = lax.broadcasted_iota(jnp.int32, (c_, c_), 1)
    tri = (si <= ti).astype(F32)
    xor = jnp.bitwise_xor(ti, si)
    row = lax.broadcasted_iota(jnp.int32, (c_, B_KEY_DIM), 0)

    def chunk(ci, carry):
        sl = pl.ds(pl.multiple_of(ci * c_, c_), c_)
        f = lb + (1.0 - lb) * jax.nn.sigmoid(fb_ref[sl, :])
        logf = jnp.log(f)
        kk = 1.0 - f
        q = jax.nn.silu(qb_ref[sl, :])
        v = ib_ref[sl, :]
        vb = v.astype(BF16)
        bc = jnp.dot(tri, logf, precision=lax.Precision.HIGHEST, preferred_element_type=F32)

        attn = jnp.zeros((c_, c_), F32)
        half = c_ // 2
        while half >= HG_MIN_LEVEL:
            blk = 2 * half
            ref_rows = [jnp.broadcast_to(bc[b0 + half - 1:b0 + half, :], (blk, B_KEY_DIM)) for b0 in range(0, c_, blk)]
            ref = jnp.concatenate(ref_rows, axis=0) if len(ref_rows) > 1 else ref_rows[0]
            dec = jnp.exp(-jnp.abs(bc - ref))
            s = lax.dot_general((q * dec).astype(BF16), (kk * dec).astype(BF16), (((1,), (1,)), ((), ())),
                                preferred_element_type=F32)
            keep = jnp.logical_and(jnp.logical_and(xor >= half, xor < blk), ti > si)
            attn = attn + jnp.where(keep, s, 0.0)
            half //= 2
        o = jnp.dot(attn.astype(BF16), vb, preferred_element_type=F32)

        for delta in range(HG_MIN_LEVEL):
            if delta == 0:
                w = q * kk
                vd = v
            else:
                valid = jnp.bitwise_and(row, HG_MIN_LEVEL - 1) >= delta
                rel = jnp.where(valid, bc - pltpu.roll(bc, delta, 0), 0.0)
                w = jnp.where(valid, q * pltpu.roll(kk, delta, 0) * jnp.exp(rel), 0.0)
                vd = pltpu.roll(v, delta, 0)
            o = o + jnp.sum(w, axis=-1, keepdims=True) * vd

        st = st_ref[...]
        o = o + lax.dot_general((q * jnp.exp(bc)).astype(BF16), st.astype(BF16), (((1,), (1,)), ((), ())),
                                preferred_element_type=F32)
        b_last = bc[c_ - 1:c_, :]
        kdec = (kk * jnp.exp(b_last - bc)).astype(BF16)
        upd = lax.dot_general(vb, kdec, (((0,), (0,)), ((), ())), preferred_element_type=F32)
        st_ref[...] = st * jnp.exp(b_last) + upd

        o = o * lax.rsqrt(jnp.mean(o * o, axis=-1, keepdims=True) + NORM_EPS)
        o_ref[sl, :] = (o * gnorm * jax.nn.silu(gb_ref[sl, :])).astype(o_ref.dtype)
        return carry

    lax.fori_loop(0, HG_TB // c_, chunk, 0)


def _hgrn2(proj, lb_logits, norm_g, seq):
    col = lambda off: pl.BlockSpec((HG_TB, B_KEY_DIM), lambda h, t: (t, off + h))
    n_lb = lb_logits.shape[0]
    return pl.pallas_call(
        _hgrn_body,
        grid=(B_HEADS, seq // HG_TB),
        in_specs=[col(_QB_BLK), col(_FB_BLK), col(_IB_BLK), col(_GB_BLK),
                  pl.BlockSpec((n_lb, B_KEY_DIM), lambda h, t: (0, h)),
                  pl.BlockSpec((1, B_KEY_DIM), lambda h, t: (0, h))],
        out_specs=pl.BlockSpec((HG_TB, B_KEY_DIM), lambda h, t: (t, h)),
        out_shape=jax.ShapeDtypeStruct((seq, B_WIDTH), BF16),
        scratch_shapes=[pltpu.VMEM((B_KEY_DIM, B_KEY_DIM), F32)],
        compiler_params=_params("arbitrary", "arbitrary"),
        name="hgrn2",
    )(proj, proj, proj, proj, lb_logits.astype(F32), norm_g.reshape(1, B_WIDTH).astype(F32))


def _merge_body(oa_ref, ob_ref, ga_ref, gb_ref, wa_ref, wb_ref, o_ref, wa_s, wb_s):
    @pl.when(pl.program_id(1) == 0)
    def _():
        wa_s[...] = wa_ref[...].astype(BF16)
        wb_s[...] = wb_ref[...].astype(BF16)

    ya = jnp.dot(oa_ref[...], wa_s[...], preferred_element_type=F32)
    yb = jnp.dot(ob_ref[...], wb_s[...], preferred_element_type=F32)
    o_ref[...] = (jax.nn.sigmoid(ga_ref[...]) * ya + jax.nn.sigmoid(gb_ref[...]) * yb).astype(o_ref.dtype)


def _branch_merge(oa, ob, proj, wa, wb, tm=512, tn=512):
    m = oa.shape[0]
    ga0, gb0 = _GATE_A_COL // tn, _GATE_B_COL // tn
    return pl.pallas_call(
        _merge_body,
        grid=(D_MODEL // tn, m // tm),
        in_specs=[pl.BlockSpec((tm, A_WIDTH), lambda j, i: (i, 0)), pl.BlockSpec((tm, B_WIDTH), lambda j, i: (i, 0)),
                  pl.BlockSpec((tm, tn), lambda j, i: (i, ga0 + j)), pl.BlockSpec((tm, tn), lambda j, i: (i, gb0 + j)),
                  pl.BlockSpec((A_WIDTH, tn), lambda j, i: (0, j)), pl.BlockSpec((B_WIDTH, tn), lambda j, i: (0, j))],
        out_specs=pl.BlockSpec((tm, tn), lambda j, i: (i, j)),
        out_shape=jax.ShapeDtypeStruct((m, D_MODEL), BF16),
        scratch_shapes=[pltpu.VMEM((A_WIDTH, tn), BF16), pltpu.VMEM((B_WIDTH, tn), BF16)],
        compiler_params=_params("arbitrary", "arbitrary"),
        name="branch_merge",
    )(oa, ob, proj, proj, wa, wb)


def _router_body(h_ref, g_ref, rw_ref, rb_ref, hn_ref, eid_ref, wk_ref):
    hn = _rmsnorm_rows(h_ref[...], g_ref[...])
    hn_ref[...] = hn
    logits = jnp.dot(hn, rw_ref[...], precision=lax.Precision.HIGHEST, preferred_element_type=F32) + rb_ref[...]
    lane = lax.broadcasted_iota(jnp.int32, logits.shape, 1)
    neg = jnp.float32(-jnp.inf)
    big = jnp.int32(LANES)

    is_g = lane < N_GROUPS
    lg = jnp.where(is_g, logits, neg)
    mg = jnp.max(lg, axis=-1, keepdims=True)
    g_idx = jnp.min(jnp.where(lg == mg, lane, big), axis=-1, keepdims=True)
    pg_top = 1.0 / jnp.sum(jnp.where(is_g, jnp.exp(lg - mg), 0.0), axis=-1, keepdims=True)

    lo = N_GROUPS + g_idx * EXPERTS_PER_GROUP
    in_grp = jnp.logical_and(lane >= lo, lane < lo + EXPERTS_PER_GROUP)
    le = jnp.where(in_grp, logits, neg)
    v1 = jnp.max(le, axis=-1, keepdims=True)
    i1 = jnp.min(jnp.where(le == v1, lane, big), axis=-1, keepdims=True)
    le2 = jnp.where(lane == i1, neg, le)
    v2 = jnp.max(le2, axis=-1, keepdims=True)
    i2 = jnp.min(jnp.where(le2 == v2, lane, big), axis=-1, keepdims=True)
    e2 = jnp.exp(v2 - v1)
    w1 = pg_top / (1.0 + e2)
    w2 = pg_top * e2 / (1.0 + e2)
    eid_ref[...] = jnp.where(lane == 0, i1 - N_GROUPS, jnp.where(lane == 1, i2 - N_GROUPS, 0))
    wk_ref[...] = jnp.where(lane == 0, w1, jnp.where(lane == 1, w2, 0.0))


def _router(h, g, rw, rb, tm=256):
    m, d = h.shape
    row = lambda w: pl.BlockSpec((tm, w), lambda i: (i, 0))
    return pl.pallas_call(
        _router_body,
        grid=(m // tm,),
        in_specs=[row(d), pl.BlockSpec((1, d), lambda i: (0, 0)), pl.BlockSpec((d, LANES), lambda i: (0, 0)),
                  pl.BlockSpec((1, LANES), lambda i: (0, 0))],
        out_specs=[row(d), row(LANES), row(LANES)],
        out_shape=[jax.ShapeDtypeStruct((m, d), F32), jax.ShapeDtypeStruct((m, LANES), jnp.int32),
                   jax.ShapeDtypeStruct((m, LANES), F32)],
        compiler_params=_params("arbitrary"),
        name="router",
    )(h, g.reshape(1, d).astype(F32), rw, rb)


GATHER_ROWS = 256


def _gather_body(n_steps_ref, idx_ref, src_ref, out_ref, sem):
    step = pl.program_id(0)
    base = step * GATHER_ROWS

    def copy(r):
        return pltpu.make_async_copy(src_ref.at[idx_ref[0, 0, r]], out_ref.at[base + r], sem)

    def start(r, carry):
        copy(r).start()
        return carry

    def wait(r, carry):
        copy(r).wait()
        return carry

    n_rows = jnp.where(step < n_steps_ref[0], GATHER_ROWS, 0)
    lax.fori_loop(0, n_rows, start, 0)
    lax.fori_loop(0, n_rows, wait, 0)

    @pl.when(step >= n_steps_ref[0])
    def _():
        fill = pltpu.make_async_copy(src_ref.at[pl.ds(0, GATHER_ROWS)], out_ref.at[pl.ds(base, GATHER_ROWS)], sem)
        fill.start()
        fill.wait()


def _gather_rows(src, idx, n_steps_used):
    p = idx.shape[0]
    d = src.shape[1]
    steps = p // GATHER_ROWS
    grid_spec = pltpu.PrefetchScalarGridSpec(
        num_scalar_prefetch=1,
        grid=(steps,),
        in_specs=[pl.BlockSpec((1, 1, GATHER_ROWS), lambda i, nu: (i, 0, 0), memory_space=pltpu.SMEM),
                  pl.BlockSpec(memory_space=pl.ANY)],
        out_specs=pl.BlockSpec(memory_space=pl.ANY),
        scratch_shapes=[pltpu.SemaphoreType.DMA(())],
    )
    return pl.pallas_call(
        _gather_body,
        grid_spec=grid_spec,
        out_shape=jax.ShapeDtypeStruct((p, d), src.dtype),
        compiler_params=_params("arbitrary"),
        name="gather_rows",
    )(n_steps_used, idx.reshape(steps, 1, GATHER_ROWS), src)


MOE_TM = 256


def _moe_body(tile_e_ref, tile_blk_ref, n_used_ref, dst_ref, x_ref, w1_ref, w3_ref, w2_ref, y_ref,
              w1_s, w3_s, w2_s, o_s, sem):
    j = pl.program_id(0)
    prev_e = tile_e_ref[jnp.maximum(j - 1, 0)]
    new_expert = jnp.logical_or(j == 0, tile_e_ref[j] != prev_e)

    @pl.when(jnp.logical_and(j < n_used_ref[0], new_expert))
    def _():
        w1_s[...] = w1_ref[0].astype(BF16)
        w3_s[...] = w3_ref[0].astype(BF16)
        w2_s[...] = w2_ref[0].astype(BF16)

    @pl.when(j < n_used_ref[0])
    def _():
        x = x_ref[...].astype(BF16)
        hid = (jax.nn.silu(jnp.dot(x, w1_s[...], preferred_element_type=F32))
               * jnp.dot(x, w3_s[...], preferred_element_type=F32))
        o_s[...] = jnp.dot(hid.astype(BF16), w2_s[...], preferred_element_type=F32)

        n_out = y_ref.shape[0]

        def copy(r):
            return pltpu.make_async_copy(o_s.at[r], y_ref.at[dst_ref[0, 0, r]], sem)

        def start(r, carry):
            @pl.when(dst_ref[0, 0, r] < n_out)
            def _():
                copy(r).start()
            return carry

        def wait(r, carry):
            @pl.when(dst_ref[0, 0, r] < n_out)
            def _():
                copy(r).wait()
            return carry

        lax.fori_loop(0, MOE_TM, start, 0)
        lax.fori_loop(0, MOE_TM, wait, 0)


def _moe_experts(xs, w1, w3, w2, tile_e, tile_blk, n_used, dst, n_out_rows):
    p, d = xs.shape
    n_tiles = p // MOE_TM
    ff = w1.shape[-1]
    grid_spec = pltpu.PrefetchScalarGridSpec(
        num_scalar_prefetch=3,
        grid=(n_tiles,),
        in_specs=[
            pl.BlockSpec((1, 1, MOE_TM), lambda j, te, tb, nu: (tb[j], 0, 0), memory_space=pltpu.SMEM),
            pl.BlockSpec((MOE_TM, d), lambda j, te, tb, nu: (tb[j], 0)),
            pl.BlockSpec((1, d, ff), lambda j, te, tb, nu: (te[j], 0, 0)),
            pl.BlockSpec((1, d, ff), lambda j, te, tb, nu: (te[j], 0, 0)),
            pl.BlockSpec((1, ff, d), lambda j, te, tb, nu: (te[j], 0, 0)),
        ],
        out_specs=pl.BlockSpec(memory_space=pl.ANY),
        scratch_shapes=[pltpu.VMEM((d, ff), BF16), pltpu.VMEM((d, ff), BF16), pltpu.VMEM((ff, d), BF16),
                        pltpu.VMEM((MOE_TM, d), F32), pltpu.SemaphoreType.DMA(())],
    )
    return pl.pallas_call(
        _moe_body,
        grid_spec=grid_spec,
        out_shape=jax.ShapeDtypeStruct((n_out_rows, d), F32),
        compiler_params=_params("arbitrary"),
        name="moe_experts",
    )(tile_e, tile_blk, n_used, dst.reshape(n_tiles, 1, MOE_TM), xs, w1, w3, w2)


def _combine_body(h_ref, y0_ref, y1_ref, wk_ref, g_ref, o_ref):
    wk = wk_ref[...]
    y = h_ref[...] + wk[:, 0:1] * y0_ref[...] + wk[:, 1:2] * y1_ref[...]
    o_ref[...] = _rmsnorm_rows(y, g_ref[...]).astype(o_ref.dtype)


def _combine(h, y2, wk, g, out_dtype, tm=256):
    m, d = h.shape
    nblk = m // tm
    return pl.pallas_call(
        _combine_body,
        grid=(nblk,),
        in_specs=[pl.BlockSpec((tm, d), lambda i: (i, 0)), pl.BlockSpec((tm, d), lambda i: (i, 0)),
                  pl.BlockSpec((tm, d), lambda i: (nblk + i, 0)), pl.BlockSpec((tm, LANES), lambda i: (i, 0)),
                  pl.BlockSpec((1, d), lambda i: (0, 0))],
        out_specs=pl.BlockSpec((tm, d), lambda i: (i, 0)),
        out_shape=jax.ShapeDtypeStruct((m, d), out_dtype),
        compiler_params=_params("arbitrary"),
        name="combine_final_norm",
    )(h, y2, y2, wk, g.reshape(1, d).astype(F32))


def _routing_tables(eid, n_tokens):
    n_pairs = 2 * n_tokens
    n_tiles = n_pairs // MOE_TM + N_EXPERTS
    e_flat = eid.reshape(n_pairs)
    order = jnp.argsort(e_flat, stable=True).astype(jnp.int32)
    counts = jnp.sum(e_flat[:, None] == jnp.arange(N_EXPERTS, dtype=jnp.int32)[None, :], axis=0, dtype=jnp.int32)
    tiles_per_e = (counts + MOE_TM - 1) // MOE_TM
    tile_end = jnp.cumsum(tiles_per_e)
    tile_start = tile_end - tiles_per_e
    n_used = tile_end[-1]
    sorted_start = jnp.cumsum(counts) - counts
    j = jnp.arange(n_tiles, dtype=jnp.int32)
    tile_blk = jnp.minimum(j, n_used - 1)
    tile_e = jnp.searchsorted(tile_end, tile_blk, side="right").astype(jnp.int32)
    row_in_e = (tile_blk - tile_start[tile_e])[:, None] * MOE_TM + jnp.arange(MOE_TM, dtype=jnp.int32)[None, :]
    valid = jnp.logical_and(row_in_e < counts[tile_e][:, None], (j == tile_blk)[:, None])
    pair = order[jnp.clip(sorted_start[tile_e][:, None] + row_in_e, 0, n_pairs - 1)]
    src_tok = jnp.where(valid, pair // 2, 0).astype(jnp.int32)
    dst_row = jnp.where(valid, (pair % 2) * n_tokens + pair // 2, n_pairs).astype(jnp.int32)
    return (tile_e, tile_blk.astype(jnp.int32), n_used.reshape(1).astype(jnp.int32),
            src_tok.reshape(-1), dst_row.reshape(-1))


def kernel(x, mix_norm_g, w_in, hgrn_lb_logits, hgrn_norm_g, w_branch_a, w_branch_b, w_out, ffn_norm_g,
           router_w_group, router_b_group, router_w_expert, router_b_expert, expert_w1, expert_w3, expert_w2,
           final_norm_g):
    b, seq, d = x.shape
    assert b == 1 and d == D_MODEL and seq % ATT_SUPER == 0 and w_in.shape == (1, D_MODEL, IN_WIDTH)
    h0 = x.reshape(seq, d).astype(F32)

    xn = _rmsnorm(h0, mix_norm_g[0], BF16)
    proj = _matmul(xn, w_in[0], F32, tm=512, tn=1280, name="in_proj")
    oa = _dilated_attention(proj, seq)
    ob = _hgrn2(proj, hgrn_lb_logits, hgrn_norm_g[0], seq)
    merged = _branch_merge(oa, ob, proj, w_branch_a[0], w_branch_b[0])
    h1 = _matmul_residual(merged, w_out[0], h0, tm=512, tn=1024, name="out_proj")

    pad = LANES - N_GROUPS - N_EXPERTS
    rw = jnp.concatenate([router_w_group[0], router_w_expert[0], jnp.zeros((d, pad), F32)], axis=1).astype(F32)
    rb = jnp.concatenate([router_b_group[0], router_b_expert[0], jnp.zeros((pad,), F32)]).reshape(1, LANES).astype(F32)
    hn, eid, wk = _router(h1, ffn_norm_g[0], rw, rb)
    tile_e, tile_blk, n_used, src_tok, dst_row = _routing_tables(eid[:, :2], seq)
    xs = _gather_rows(hn, src_tok, n_used)
    w1 = expert_w1[0].reshape(N_EXPERTS, D_MODEL, EXPERT_FF)
    w3 = expert_w3[0].reshape(N_EXPERTS, D_MODEL, EXPERT_FF)
    w2 = expert_w2[0].reshape(N_EXPERTS, EXPERT_FF, D_MODEL)
    y2 = _moe_experts(xs, w1, w3, w2, tile_e, tile_blk, n_used, dst_row, 2 * seq)

    out = _combine(h1, y2, wk, final_norm_g, x.dtype)
    return out.reshape(b, seq, d)
```

```python
import functools

import jax
import jax.numpy as jnp
from jax import lax
from jax.experimental import pallas as pl
from jax.experimental.pallas import tpu as pltpu

F32 = jnp.float32
BF16 = jnp.bfloat16

D_MODEL = 2048
A_HEADS = 12
A_HEAD_DIM = 128
A_WIDTH = A_HEADS * A_HEAD_DIM
A_SCALE = A_HEAD_DIM ** -0.5
DILATED_CONFIGS = ((128, 1), (512, 4), (2048, 16))
B_HEADS = 8
B_KEY_DIM = 128
B_WIDTH = B_HEADS * B_KEY_DIM
N_GROUPS = 4
EXPERTS_PER_GROUP = 8
N_EXPERTS = N_GROUPS * EXPERTS_PER_GROUP
EXPERT_FF = 512
NORM_EPS = 1e-6

LANES = 128
SUBLANES = 8
VMEM_LIMIT = 56 * 1024 * 1024

_QA_BLK = 0
_KA_BLK = A_HEADS
_VA_BLK = 2 * A_HEADS
_QB_BLK = 3 * A_HEADS
_FB_BLK = _QB_BLK + B_HEADS
_IB_BLK = _FB_BLK + B_HEADS
_GB_BLK = _IB_BLK + B_HEADS
_GATE_A_COL = 3 * A_WIDTH + 4 * B_WIDTH
_GATE_B_COL = _GATE_A_COL + D_MODEL
IN_WIDTH = _GATE_B_COL + D_MODEL


def _params(*sem):
    return pltpu.CompilerParams(dimension_semantics=sem, vmem_limit_bytes=VMEM_LIMIT)


def _rmsnorm_rows(x, g):
    ms = jnp.mean(x * x, axis=-1, keepdims=True)
    return x * lax.rsqrt(ms + NORM_EPS) * g


def _rmsnorm_body(x_ref, g_ref, o_ref):
    o_ref[...] = _rmsnorm_rows(x_ref[...].astype(F32), g_ref[...]).astype(o_ref.dtype)


def _rmsnorm(x, g, out_dtype, tm=512):
    m, d = x.shape
    return pl.pallas_call(
        _rmsnorm_body,
        grid=(m // tm,),
        in_specs=[pl.BlockSpec((tm, d), lambda i: (i, 0)), pl.BlockSpec((1, d), lambda i: (0, 0))],
        out_specs=pl.BlockSpec((tm, d), lambda i: (i, 0)),
        out_shape=jax.ShapeDtypeStruct((m, d), out_dtype),
        compiler_params=_params("arbitrary"),
        name="rmsnorm",
    )(x, g.reshape(1, d).astype(F32))


def _matmul_body(a_ref, w_ref, o_ref, wb_ref):
    @pl.when(pl.program_id(1) == 0)
    def _():
        wb_ref[...] = w_ref[...].astype(BF16)

    o_ref[...] = jnp.dot(a_ref[...], wb_ref[...], preferred_element_type=F32).astype(o_ref.dtype)


def _matmul(a, w, out_dtype, tm, tn, name):
    m, k = a.shape
    n = w.shape[1]
    return pl.pallas_call(
        _matmul_body,
        grid=(n // tn, m // tm),
        in_specs=[pl.BlockSpec((tm, k), lambda j, i: (i, 0)), pl.BlockSpec((k, tn), lambda j, i: (0, j))],
        out_specs=pl.BlockSpec((tm, tn), lambda j, i: (i, j)),
        out_shape=jax.ShapeDtypeStruct((m, n), out_dtype),
        scratch_shapes=[pltpu.VMEM((k, tn), BF16)],
        compiler_params=_params("arbitrary", "arbitrary"),
        name=name,
    )(a, w)


def _matmul_residual_body(a_ref, w_ref, r_ref, o_ref, wb_ref):
    @pl.when(pl.program_id(1) == 0)
    def _():
        wb_ref[...] = w_ref[...].astype(BF16)

    o_ref[...] = r_ref[...] + jnp.dot(a_ref[...], wb_ref[...], preferred_element_type=F32)


def _matmul_residual(a, w, res, tm, tn, name):
    m, k = a.shape
    n = w.shape[1]
    return pl.pallas_call(
        _matmul_residual_body,
        grid=(n // tn, m // tm),
        in_specs=[pl.BlockSpec((tm, k), lambda j, i: (i, 0)), pl.BlockSpec((k, tn), lambda j, i: (0, j)),
                  pl.BlockSpec((tm, tn), lambda j, i: (i, j))],
        out_specs=pl.BlockSpec((tm, tn), lambda j, i: (i, j)),
        out_shape=jax.ShapeDtypeStruct((m, n), F32),
        scratch_shapes=[pltpu.VMEM((k, tn), BF16)],
        compiler_params=_params("arbitrary", "arbitrary"),
        name=name,
    )(a, w, res)


ATT_BLK = 128
ATT_SUPER = 2048
ATT_UNROLL = 4


def _attn_body(q_ref, k_ref, v_ref, o_ref, o_scr, lse_scr):
    sb = pl.program_id(1)
    row0 = sb * ATT_SUPER
    diff = (lax.broadcasted_iota(jnp.int32, (ATT_BLK, 2 * ATT_BLK), 1)
            - lax.broadcasted_iota(jnp.int32, (ATT_BLK, 2 * ATT_BLK), 0))

    for c, (window, dil) in enumerate(DILATED_CONFIGS):
        span = ATT_BLK * dil
        tiles_per_res = ATT_SUPER // span

        def tile(t, carry, c=c, dil=dil, span=span, tiles_per_res=tiles_per_res):
            n = t % tiles_per_res
            r = t // tiles_per_res
            q_start = n * span + r
            first = jnp.logical_and(sb == 0, n == 0)
            kv_start = row0 + q_start - jnp.where(first, 0, span)
            lo = jnp.where(first, -ATT_BLK, 0)
            if dil == 1:
                q_start = pl.multiple_of(q_start, ATT_BLK)
                kv_start = pl.multiple_of(kv_start, ATT_BLK)
                qs = pl.ds(q_start, ATT_BLK)
                ks = pl.ds(kv_start, 2 * ATT_BLK)
            else:
                qs = pl.ds(q_start, ATT_BLK, stride=dil)
                ks = pl.ds(kv_start, 2 * ATT_BLK, stride=dil)
            q = (q_ref[qs, :] * A_SCALE).astype(BF16)
            k = k_ref[ks, :].astype(BF16)
            v = v_ref[ks, :].astype(BF16)
            s = lax.dot_general(q, k, (((1,), (1,)), ((), ())), preferred_element_type=F32)
            neg = jnp.float32(-jnp.inf)
            s = jnp.where(diff >= lo, jnp.where(diff <= lo + ATT_BLK, s, neg), neg)
            mx = jnp.max(s, axis=-1, keepdims=True)
            p = jnp.exp(s - mx)
            den = jnp.sum(p, axis=-1, keepdims=True)
            acc = jnp.dot(p.astype(BF16), v, preferred_element_type=F32)
            o_scr[c, qs, :] = acc / den
            lse_scr[c, qs, :] = jnp.broadcast_to(mx + jnp.log(den), (ATT_BLK, A_HEAD_DIM))
            return carry

        def tiles(i, carry, tile=tile):
            for u in range(ATT_UNROLL):
                tile(i * ATT_UNROLL + u, carry)
            return carry

        lax.fori_loop(0, ATT_SUPER // ATT_BLK // ATT_UNROLL, tiles, 0)

    rows = 256

    def merge(i, carry):
        sl = pl.ds(pl.multiple_of(i * rows, rows), rows)
        l0, l1, l2 = lse_scr[0, sl, :], lse_scr[1, sl, :], lse_scr[2, sl, :]
        m = jnp.maximum(jnp.maximum(l0, l1), l2)
        w0, w1, w2 = jnp.exp(l0 - m), jnp.exp(l1 - m), jnp.exp(l2 - m)
        num = w0 * o_scr[0, sl, :] + w1 * o_scr[1, sl, :] + w2 * o_scr[2, sl, :]
        o_ref[sl, :] = (num / (w0 + w1 + w2)).astype(o_ref.dtype)
        return carry

    lax.fori_loop(0, ATT_SUPER // rows, merge, 0)


def _dilated_attention(proj, seq):
    n_super = seq // ATT_SUPER
    blk = lambda off: pl.BlockSpec((seq, A_HEAD_DIM), lambda h, s: (0, off + h))
    return pl.pallas_call(
        _attn_body,
        grid=(A_HEADS, n_super),
        in_specs=[pl.BlockSpec((ATT_SUPER, A_HEAD_DIM), lambda h, s: (s, _QA_BLK + h)), blk(_KA_BLK), blk(_VA_BLK)],
        out_specs=pl.BlockSpec((ATT_SUPER, A_HEAD_DIM), lambda h, s: (s, h)),
        out_shape=jax.ShapeDtypeStruct((seq, A_WIDTH), BF16),
        scratch_shapes=[pltpu.VMEM((len(DILATED_CONFIGS), ATT_SUPER, A_HEAD_DIM), F32),
                        pltpu.VMEM((len(DILATED_CONFIGS), ATT_SUPER, A_HEAD_DIM), F32)],
        compiler_params=_params("arbitrary", "arbitrary"),
        name="dilated_attention",
    )(proj, proj, proj)


HG_CHUNK = 64
HG_TB = 512
HG_MIN_LEVEL = 4
HG_HEADS_PER_STEP = 2
HG_CHUNKS_PER_ITER = 2


def _hgrn_body(qb_ref, fb_ref, ib_ref, gb_ref, lbl_ref, g_ref, o_ref, st_ref):
    @pl.when(pl.program_id(1) == 0)
    def _():
        st_ref[...] = jnp.zeros_like(st_ref)

    c_ = HG_CHUNK
    lbl = lbl_ref[...]
    e = jnp.exp(lbl - jnp.max(lbl, axis=0, keepdims=True))
    lb_all = e[0:1, :] / jnp.sum(e, axis=0, keepdims=True)
    g_all = g_ref[...]

    ti = lax.broadcasted_iota(jnp.int32, (c_, c_), 0)
    si = lax.broadcasted_iota(jnp.int32, (c_, c_), 1)
    tri = (si <= ti).astype(F32)
    xor = jnp.bitwise_xor(ti, si)
    row = lax.broadcasted_iota(jnp.int32, (c_, B_KEY_DIM), 0)

    def chunk(ci, hp):
        sl = pl.ds(pl.multiple_of(ci * c_, c_), c_)
        ln = slice(hp * B_KEY_DIM, (hp + 1) * B_KEY_DIM)
        lb = lb_all[:, ln]
        f = lb + (1.0 - lb) * jax.nn.sigmoid(fb_ref[sl, ln])
        logf = jnp.log(f)
        kk = 1.0 - f
        q = jax.nn.silu(qb_ref[sl, ln])
        v = ib_ref[sl, ln]
        vb = v.astype(BF16)
        bc = jnp.dot(tri, logf, precision=lax.Precision.HIGHEST, preferred_element_type=F32)

        attn = jnp.zeros((c_, c_), F32)
        half = c_ // 2
        while half >= HG_MIN_LEVEL:
            blk = 2 * half
            ref_rows = [jnp.broadcast_to(bc[b0 + half - 1:b0 + half, :], (blk, B_KEY_DIM)) for b0 in range(0, c_, blk)]
            ref = jnp.concatenate(ref_rows, axis=0) if len(ref_rows) > 1 else ref_rows[0]
            dec = jnp.exp(-jnp.abs(bc - ref))
            s = lax.dot_general((q * dec).astype(BF16), (kk * dec).astype(BF16), (((1,), (1,)), ((), ())),
                                preferred_element_type=F32)
            keep = jnp.logical_and(jnp.logical_and(xor >= half, xor < blk), ti > si)
            attn = attn + jnp.where(keep, s, 0.0)
            half //= 2
        o = jnp.dot(attn.astype(BF16), vb, preferred_element_type=F32)

        for delta in range(HG_MIN_LEVEL):
            if delta == 0:
                w = q * kk
                vd = v
            else:
                valid = jnp.bitwise_and(row, HG_MIN_LEVEL - 1) >= delta
                rel = jnp.where(valid, bc - pltpu.roll(bc, delta, 0), 0.0)
                w = jnp.where(valid, q * pltpu.roll(kk, delta, 0) * jnp.exp(rel), 0.0)
                vd = pltpu.roll(v, delta, 0)
            o = o + jnp.sum(w, axis=-1, keepdims=True) * vd

        st = st_ref[hp]
        o = o + lax.dot_general((q * jnp.exp(bc)).astype(BF16), st.astype(BF16), (((1,), (1,)), ((), ())),
                                preferred_element_type=F32)
        b_last = bc[c_ - 1:c_, :]
        kdec = (kk * jnp.exp(b_last - bc)).astype(BF16)
        upd = lax.dot_general(vb, kdec, (((0,), (0,)), ((), ())), preferred_element_type=F32)
        st_ref[hp] = st * jnp.exp(b_last) + upd

        o = o * lax.rsqrt(jnp.mean(o * o, axis=-1, keepdims=True) + NORM_EPS)
        o_ref[sl, ln] = (o * g_all[:, ln] * jax.nn.silu(gb_ref[sl, ln])).astype(o_ref.dtype)

    def chunks(i, carry):
        for u in range(HG_CHUNKS_PER_ITER):
            for hp in range(HG_HEADS_PER_STEP):
                chunk(i * HG_CHUNKS_PER_ITER + u, hp)
        return carry

    lax.fori_loop(0, HG_TB // c_ // HG_CHUNKS_PER_ITER, chunks, 0)


def _hgrn2(proj, lb_logits, norm_g, seq):
    hp = HG_HEADS_PER_STEP
    width = hp * B_KEY_DIM
    col = lambda off: pl.BlockSpec((HG_TB, width), lambda h, t: (t, off // hp + h))
    n_lb = lb_logits.shape[0]
    return pl.pallas_call(
        _hgrn_body,
        grid=(B_HEADS // hp, seq // HG_TB),
        in_specs=[col(_QB_BLK), col(_FB_BLK), col(_IB_BLK), col(_GB_BLK),
                  pl.BlockSpec((n_lb, width), lambda h, t: (0, h)),
                  pl.BlockSpec((1, width), lambda h, t: (0, h))],
        out_specs=pl.BlockSpec((HG_TB, width), lambda h, t: (t, h)),
        out_shape=jax.ShapeDtypeStruct((seq, B_WIDTH), BF16),
        scratch_shapes=[pltpu.VMEM((hp, B_KEY_DIM, B_KEY_DIM), F32)],
        compiler_params=_params("arbitrary", "arbitrary"),
        name="hgrn2",
    )(proj, proj, proj, proj, lb_logits.astype(F32), norm_g.reshape(1, B_WIDTH).astype(F32))


def _merge_body(oa_ref, ob_ref, ga_ref, gb_ref, wa_ref, wb_ref, o_ref, wa_s, wb_s):
    @pl.when(pl.program_id(1) == 0)
    def _():
        wa_s[...] = wa_ref[...].astype(BF16)
        wb_s[...] = wb_ref[...].astype(BF16)

    ya = jnp.dot(oa_ref[...], wa_s[...], preferred_element_type=F32)
    yb = jnp.dot(ob_ref[...], wb_s[...], preferred_element_type=F32)
    o_ref[...] = (jax.nn.sigmoid(ga_ref[...]) * ya + jax.nn.sigmoid(gb_ref[...]) * yb).astype(o_ref.dtype)


def _branch_merge(oa, ob, proj, wa, wb, tm=512, tn=512):
    m = oa.shape[0]
    ga0, gb0 = _GATE_A_COL // tn, _GATE_B_COL // tn
    return pl.pallas_call(
        _merge_body,
        grid=(D_MODEL // tn, m // tm),
        in_specs=[pl.BlockSpec((tm, A_WIDTH), lambda j, i: (i, 0)), pl.BlockSpec((tm, B_WIDTH), lambda j, i: (i, 0)),
                  pl.BlockSpec((tm, tn), lambda j, i: (i, ga0 + j)), pl.BlockSpec((tm, tn), lambda j, i: (i, gb0 + j)),
                  pl.BlockSpec((A_WIDTH, tn), lambda j, i: (0, j)), pl.BlockSpec((B_WIDTH, tn), lambda j, i: (0, j))],
        out_specs=pl.BlockSpec((tm, tn), lambda j, i: (i, j)),
        out_shape=jax.ShapeDtypeStruct((m, D_MODEL), BF16),
        scratch_shapes=[pltpu.VMEM((A_WIDTH, tn), BF16), pltpu.VMEM((B_WIDTH, tn), BF16)],
        compiler_params=_params("arbitrary", "arbitrary"),
        name="branch_merge",
    )(oa, ob, proj, proj, wa, wb)


def _router_body(h_ref, g_ref, rw_ref, rb_ref, hn_ref, eid_ref, wk_ref):
    hn = _rmsnorm_rows(h_ref[...], g_ref[...])
    hn_ref[...] = hn
    logits = jnp.dot(hn, rw_ref[...], precision=lax.Precision.HIGHEST, preferred_element_type=F32) + rb_ref[...]
    lane = lax.broadcasted_iota(jnp.int32, logits.shape, 1)
    neg = jnp.float32(-jnp.inf)
    big = jnp.int32(LANES)

    is_g = lane < N_GROUPS
    lg = jnp.where(is_g, logits, neg)
    mg = jnp.max(lg, axis=-1, keepdims=True)
    g_idx = jnp.min(jnp.where(lg == mg, lane, big), axis=-1, keepdims=True)
    pg_top = 1.0 / jnp.sum(jnp.where(is_g, jnp.exp(lg - mg), 0.0), axis=-1, keepdims=True)

    lo = N_GROUPS + g_idx * EXPERTS_PER_GROUP
    in_grp = jnp.logical_and(lane >= lo, lane < lo + EXPERTS_PER_GROUP)
    le = jnp.where(in_grp, logits, neg)
    v1 = jnp.max(le, axis=-1, keepdims=True)
    i1 = jnp.min(jnp.where(le == v1, lane, big), axis=-1, keepdims=True)
    le2 = jnp.where(lane == i1, neg, le)
    v2 = jnp.max(le2, axis=-1, keepdims=True)
    i2 = jnp.min(jnp.where(le2 == v2, lane, big), axis=-1, keepdims=True)
    e2 = jnp.exp(v2 - v1)
    w1 = pg_top / (1.0 + e2)
    w2 = pg_top * e2 / (1.0 + e2)
    eid_ref[...] = jnp.where(lane == 0, i1 - N_GROUPS, jnp.where(lane == 1, i2 - N_GROUPS, 0))
    wk_ref[...] = jnp.where(lane == 0, w1, jnp.where(lane == 1, w2, 0.0))


def _router(h, g, rw, rb, tm=256):
    m, d = h.shape
    row = lambda w: pl.BlockSpec((tm, w), lambda i: (i, 0))
    return pl.pallas_call(
        _router_body,
        grid=(m // tm,),
        in_specs=[row(d), pl.BlockSpec((1, d), lambda i: (0, 0)), pl.BlockSpec((d, LANES), lambda i: (0, 0)),
                  pl.BlockSpec((1, LANES), lambda i: (0, 0))],
        out_specs=[row(d), row(LANES), row(LANES)],
        out_shape=[jax.ShapeDtypeStruct((m, d), F32), jax.ShapeDtypeStruct((m, LANES), jnp.int32),
                   jax.ShapeDtypeStruct((m, LANES), F32)],
        compiler_params=_params("arbitrary"),
        name="router",
    )(h, g.reshape(1, d).astype(F32), rw, rb)


MOE_TM = 256
MOE_DMA_UNROLL = 8


def _moe_body(tile_e_ref, tile_blk_ref, tile_rows_ref, n_used_ref,
              src0_ref, src_next_ref, dst_ref, hn_ref, w1_ref, w3_ref, w2_ref, y_ref,
              w1_s, w3_s, w2_s, x_s, o_s, gsem, ssem):
    j = pl.program_id(0)
    n_tiles = pl.num_programs(0)
    n_used = n_used_ref[0]
    slot = j % 2

    def gather_start(idx_ref, s):
        def body(i, carry):
            for u in range(MOE_DMA_UNROLL):
                r = i * MOE_DMA_UNROLL + u
                pltpu.make_async_copy(hn_ref.at[idx_ref[0, 0, r]], x_s.at[s, r], gsem.at[s]).start()
            return carry

        lax.fori_loop(0, MOE_TM // MOE_DMA_UNROLL, body, 0)

    def gather_wait(s):
        pltpu.make_async_copy(hn_ref.at[pl.ds(0, MOE_TM)], x_s.at[s], gsem.at[s]).wait()

    def scatter_start(s, n):
        def one(r):
            pltpu.make_async_copy(o_s.at[s, r], y_ref.at[dst_ref[0, 0, r]], ssem.at[s]).start()

        def body(i, carry):
            for u in range(MOE_DMA_UNROLL):
                one(i * MOE_DMA_UNROLL + u)
            return carry

        def tail(r, carry):
            one(r)
            return carry

        n_groups = n // MOE_DMA_UNROLL
        lax.fori_loop(0, n_groups, body, 0)
        lax.fori_loop(n_groups * MOE_DMA_UNROLL, n, tail, 0)

    def scatter_wait(s, n):
        p = MOE_TM
        while p >= 1:
            @pl.when(jnp.bitwise_and(n, p) != 0)
            def _(p=p):
                pltpu.make_async_copy(o_s.at[s, pl.ds(0, p)], y_ref.at[pl.ds(0, p)], ssem.at[s]).wait()
            p //= 2

    @pl.when(j == 0)
    def _():
        gather_start(src0_ref, 0)

    @pl.when(j < n_used)
    def _():
        gather_wait(slot)

        @pl.when(j + 1 < n_used)
        def _():
            gather_start(src_next_ref, 1 - slot)

        @pl.when(j >= 2)
        def _():
            scatter_wait(slot, tile_rows_ref[jnp.maximum(j - 2, 0)])

        prev_e = tile_e_ref[jnp.maximum(j - 1, 0)]

        @pl.when(jnp.logical_or(j == 0, tile_e_ref[j] != prev_e))
        def _():
            w1_s[...] = w1_ref[0].astype(BF16)
            w3_s[...] = w3_ref[0].astype(BF16)
            w2_s[...] = w2_ref[0].astype(BF16)

        x = x_s[slot].astype(BF16)
        hid = (jax.nn.silu(jnp.dot(x, w1_s[...], preferred_element_type=F32))
               * jnp.dot(x, w3_s[...], preferred_element_type=F32))
        o_s[slot] = jnp.dot(hid.astype(BF16), w2_s[...], preferred_element_type=F32)
        scatter_start(slot, tile_rows_ref[j])

    @pl.when(j == n_tiles - 1)
    def _():
        last = n_used - 1
        scatter_wait(last % 2, tile_rows_ref[last])

        @pl.when(last >= 1)
        def _():
            scatter_wait((last - 1) % 2, tile_rows_ref[jnp.maximum(last - 1, 0)])


def _moe_experts(hn, w1, w3, w2, tile_e, tile_blk, tile_rows, n_used, src, dst, n_out_rows):
    d = hn.shape[1]
    n_tiles = tile_e.shape[0]
    ff = w1.shape[-1]
    idx_block = lambda fn: pl.BlockSpec((1, 1, MOE_TM), fn, memory_space=pltpu.SMEM)
    grid_spec = pltpu.PrefetchScalarGridSpec(
        num_scalar_prefetch=4,
        grid=(n_tiles,),
        in_specs=[
            idx_block(lambda j, te, tb, tr, nu: (0, 0, 0)),
            idx_block(lambda j, te, tb, tr, nu: (tb[jnp.minimum(j + 1, n_tiles - 1)], 0, 0)),
            idx_block(lambda j, te, tb, tr, nu: (tb[j], 0, 0)),
            pl.BlockSpec(memory_space=pl.ANY),
            pl.BlockSpec((1, d, ff), lambda j, te, tb, tr, nu: (te[j], 0, 0)),
            pl.BlockSpec((1, d, ff), lambda j, te, tb, tr, nu: (te[j], 0, 0)),
            pl.BlockSpec((1, ff, d), lambda j, te, tb, tr, nu: (te[j], 0, 0)),
        ],
        out_specs=pl.BlockSpec(memory_space=pl.ANY),
        scratch_shapes=[pltpu.VMEM((d, ff), BF16), pltpu.VMEM((d, ff), BF16), pltpu.VMEM((ff, d), BF16),
                        pltpu.VMEM((2, MOE_TM, d), F32), pltpu.VMEM((2, MOE_TM, d), F32),
                        pltpu.SemaphoreType.DMA((2,)), pltpu.SemaphoreType.DMA((2,))],
    )
    src3 = src.reshape(n_tiles, 1, MOE_TM)
    return pl.pallas_call(
        _moe_body,
        grid_spec=grid_spec,
        out_shape=jax.ShapeDtypeStruct((n_out_rows, d), F32),
        compiler_params=_params("arbitrary"),
        name="moe_experts",
    )(tile_e, tile_blk, tile_rows, n_used, src3, src3, dst.reshape(n_tiles, 1, MOE_TM), hn, w1, w3, w2)


def _combine_body(h_ref, y0_ref, y1_ref, wk_ref, g_ref, o_ref):
    wk = wk_ref[...]
    y = h_ref[...] + wk[:, 0:1] * y0_ref[...] + wk[:, 1:2] * y1_ref[...]
    o_ref[...] = _rmsnorm_rows(y, g_ref[...]).astype(o_ref.dtype)


def _combine(h, y2, wk, g, out_dtype, tm=256):
    m, d = h.shape
    nblk = m // tm
    return pl.pallas_call(
        _combine_body,
        grid=(nblk,),
        in_specs=[pl.BlockSpec((tm, d), lambda i: (i, 0)), pl.BlockSpec((tm, d), lambda i: (i, 0)),
                  pl.BlockSpec((tm, d), lambda i: (nblk + i, 0)), pl.BlockSpec((tm, LANES), lambda i: (i, 0)),
                  pl.BlockSpec((1, d), lambda i: (0, 0))],
        out_specs=pl.BlockSpec((tm, d), lambda i: (i, 0)),
        out_shape=jax.ShapeDtypeStruct((m, d), out_dtype),
        compiler_params=_params("arbitrary"),
        name="combine_final_norm",
    )(h, y2, y2, wk, g.reshape(1, d).astype(F32))


def _routing_tables(eid, n_tokens):
    n_pairs = 2 * n_tokens
    n_tiles = n_pairs // MOE_TM + N_EXPERTS
    e_flat = eid.reshape(n_pairs)
    order = jnp.argsort(e_flat, stable=True).astype(jnp.int32)
    counts = jnp.sum(e_flat[:, None] == jnp.arange(N_EXPERTS, dtype=jnp.int32)[None, :], axis=0, dtype=jnp.int32)
    tiles_per_e = (counts + MOE_TM - 1) // MOE_TM
    tile_end = jnp.cumsum(tiles_per_e)
    tile_start = tile_end - tiles_per_e
    n_used = tile_end[-1]
    sorted_start = jnp.cumsum(counts) - counts
    j = jnp.arange(n_tiles, dtype=jnp.int32)
    tile_blk = jnp.minimum(j, n_used - 1)
    tile_e = jnp.searchsorted(tile_end, tile_blk, side="right").astype(jnp.int32)
    tile_row0 = (tile_blk - tile_start[tile_e]) * MOE_TM
    tile_rows = jnp.where(j == tile_blk, jnp.clip(counts[tile_e] - tile_row0, 0, MOE_TM), 0).astype(jnp.int32)
    r = jnp.arange(MOE_TM, dtype=jnp.int32)[None, :]
    valid = r < tile_rows[:, None]
    pair = order[jnp.clip((sorted_start[tile_e] + tile_row0)[:, None] + r, 0, n_pairs - 1)]
    src_tok = jnp.where(valid, pair // 2, 0).astype(jnp.int32)
    dst_row = jnp.where(valid, (pair % 2) * n_tokens + pair // 2, 0).astype(jnp.int32)
    return (tile_e, tile_blk.astype(jnp.int32), tile_rows, n_used.reshape(1).astype(jnp.int32),
            src_tok.reshape(-1), dst_row.reshape(-1))


def kernel(x, mix_norm_g, w_in, hgrn_lb_logits, hgrn_norm_g, w_branch_a, w_branch_b, w_out, ffn_norm_g,
           router_w_group, router_b_group, router_w_expert, router_b_expert, expert_w1, expert_w3, expert_w2,
           final_norm_g):
    b, seq, d = x.shape
    assert b == 1 and d == D_MODEL and seq % ATT_SUPER == 0 and w_in.shape == (1, D_MODEL, IN_WIDTH)
    h0 = x.reshape(seq, d).astype(F32)

    xn = _rmsnorm(h0, mix_norm_g[0], BF16)
    proj = _matmul(xn, w_in[0], F32, tm=512, tn=1280, name="in_proj")
    oa = _dilated_attention(proj, seq)
    ob = _hgrn2(proj, hgrn_lb_logits, hgrn_norm_g[0], seq)
    merged = _branch_merge(oa, ob, proj, w_branch_a[0], w_branch_b[0])
    h1 = _matmul_residual(merged, w_out[0], h0, tm=512, tn=1024, name="out_proj")

    pad = LANES - N_GROUPS - N_EXPERTS
    rw = jnp.concatenate([router_w_group[0], router_w_expert[0], jnp.zeros((d, pad), F32)], axis=1).astype(F32)
    rb = jnp.concatenate([router_b_group[0], router_b_expert[0], jnp.zeros((pad,), F32)]).reshape(1, LANES).astype(F32)
    hn, eid, wk = _router(h1, ffn_norm_g[0], rw, rb)
    tile_e, tile_blk, tile_rows, n_used, src_tok, dst_row = _routing_tables(eid[:, :2], seq)
    w1 = expert_w1[0].reshape(N_EXPERTS, D_MODEL, EXPERT_FF)
    w3 = expert_w3[0].reshape(N_EXPERTS, D_MODEL, EXPERT_FF)
    w2 = expert_w2[0].reshape(N_EXPERTS, EXPERT_FF, D_MODEL)
    y2 = _moe_experts(hn, w1, w3, w2, tile_e, tile_blk, tile_rows, n_used, src_tok, dst_row, 2 * seq)

    out = _combine(h1, y2, wk, final_norm_g, x.dtype)
    return out.reshape(b, seq, d)
```

```python
import functools

import jax
import jax.numpy as jnp
from jax import lax
from jax.experimental import pallas as pl
from jax.experimental.pallas import tpu as pltpu

F32 = jnp.float32
BF16 = jnp.bfloat16

D_MODEL = 2048
A_HEADS = 12
A_HEAD_DIM = 128
A_WIDTH = A_HEADS * A_HEAD_DIM
A_SCALE = A_HEAD_DIM ** -0.5
DILATED_CONFIGS = ((128, 1), (512, 4), (2048, 16))
B_HEADS = 8
B_KEY_DIM = 128
B_WIDTH = B_HEADS * B_KEY_DIM
N_GROUPS = 4
EXPERTS_PER_GROUP = 8
N_EXPERTS = N_GROUPS * EXPERTS_PER_GROUP
EXPERT_FF = 512
NORM_EPS = 1e-6

LANES = 128
SUBLANES = 8
VMEM_LIMIT = 56 * 1024 * 1024

_QA_BLK = 0
_KA_BLK = A_HEADS
_VA_BLK = 2 * A_HEADS
_QB_BLK = 3 * A_HEADS
_FB_BLK = _QB_BLK + B_HEADS
_IB_BLK = _FB_BLK + B_HEADS
_GB_BLK = _IB_BLK + B_HEADS
_GATE_A_COL = 3 * A_WIDTH + 4 * B_WIDTH
_GATE_B_COL = _GATE_A_COL + D_MODEL
IN_WIDTH = _GATE_B_COL + D_MODEL


def _params(*sem):
    return pltpu.CompilerParams(dimension_semantics=sem, vmem_limit_bytes=VMEM_LIMIT)


def _rmsnorm_rows(x, g):
    ms = jnp.mean(x * x, axis=-1, keepdims=True)
    return x * lax.rsqrt(ms + NORM_EPS) * g


def _rmsnorm_body(x_ref, g_ref, o_ref):
    o_ref[...] = _rmsnorm_rows(x_ref[...].astype(F32), g_ref[...]).astype(o_ref.dtype)


def _rmsnorm(x, g, out_dtype, tm=512):
    m, d = x.shape
    return pl.pallas_call(
        _rmsnorm_body,
        grid=(m // tm,),
        in_specs=[pl.BlockSpec((tm, d), lambda i: (i, 0)), pl.BlockSpec((1, d), lambda i: (0, 0))],
        out_specs=pl.BlockSpec((tm, d), lambda i: (i, 0)),
        out_shape=jax.ShapeDtypeStruct((m, d), out_dtype),
        compiler_params=_params("arbitrary"),
        name="rmsnorm",
    )(x, g.reshape(1, d).astype(F32))


def _matmul_body(a_ref, w_ref, o_ref, wb_ref):
    @pl.when(pl.program_id(1) == 0)
    def _():
        wb_ref[...] = w_ref[...].astype(BF16)

    o_ref[...] = jnp.dot(a_ref[...], wb_ref[...], preferred_element_type=F32).astype(o_ref.dtype)


def _matmul(a, w, out_dtype, tm, tn, name):
    m, k = a.shape
    n = w.shape[1]
    return pl.pallas_call(
        _matmul_body,
        grid=(n // tn, m // tm),
        in_specs=[pl.BlockSpec((tm, k), lambda j, i: (i, 0)), pl.BlockSpec((k, tn), lambda j, i: (0, j))],
        out_specs=pl.BlockSpec((tm, tn), lambda j, i: (i, j)),
        out_shape=jax.ShapeDtypeStruct((m, n), out_dtype),
        scratch_shapes=[pltpu.VMEM((k, tn), BF16)],
        compiler_params=_params("arbitrary", "arbitrary"),
        name=name,
    )(a, w)


def _matmul_residual_body(a_ref, w_ref, r_ref, o_ref, wb_ref):
    @pl.when(pl.program_id(1) == 0)
    def _():
        wb_ref[...] = w_ref[...].astype(BF16)

    o_ref[...] = r_ref[...] + jnp.dot(a_ref[...], wb_ref[...], preferred_element_type=F32)


def _matmul_residual(a, w, res, tm, tn, name):
    m, k = a.shape
    n = w.shape[1]
    return pl.pallas_call(
        _matmul_residual_body,
        grid=(n // tn, m // tm),
        in_specs=[pl.BlockSpec((tm, k), lambda j, i: (i, 0)), pl.BlockSpec((k, tn), lambda j, i: (0, j)),
                  pl.BlockSpec((tm, tn), lambda j, i: (i, j))],
        out_specs=pl.BlockSpec((tm, tn), lambda j, i: (i, j)),
        out_shape=jax.ShapeDtypeStruct((m, n), F32),
        scratch_shapes=[pltpu.VMEM((k, tn), BF16)],
        compiler_params=_params("arbitrary", "arbitrary"),
        name=name,
    )(a, w, res)


ATT_BLK = 128
ATT_SUPER = 2048
ATT_UNROLL = 8


def _attn_body(q_ref, k_ref, v_ref, o_ref, o_scr, lse_scr, bias_scr):
    sb = pl.program_id(1)
    row0 = sb * ATT_SUPER
    diff = (lax.broadcasted_iota(jnp.int32, (ATT_BLK, 2 * ATT_BLK), 1)
            - lax.broadcasted_iota(jnp.int32, (ATT_BLK, 2 * ATT_BLK), 0))
    neg = jnp.float32(-jnp.inf)
    bias_scr[0] = jnp.where(jnp.logical_and(diff >= 0, diff <= ATT_BLK), 0.0, neg)
    bias_scr[1] = jnp.where(diff <= 0, 0.0, neg)

    for c, (window, dil) in enumerate(DILATED_CONFIGS):
        span = ATT_BLK * dil
        tiles_per_res = ATT_SUPER // span

        def tile(t, carry, c=c, dil=dil, span=span, tiles_per_res=tiles_per_res):
            n = t % tiles_per_res
            r = t // tiles_per_res
            q_start = n * span + r
            first = jnp.logical_and(sb == 0, n == 0)
            kv_start = row0 + q_start - jnp.where(first, 0, span)
            if dil == 1:
                q_start = pl.multiple_of(q_start, ATT_BLK)
                kv_start = pl.multiple_of(kv_start, ATT_BLK)
                qs = pl.ds(q_start, ATT_BLK)
                ks = pl.ds(kv_start, 2 * ATT_BLK)
            else:
                qs = pl.ds(q_start, ATT_BLK, stride=dil)
                ks = pl.ds(kv_start, 2 * ATT_BLK, stride=dil)
            q = (q_ref[qs, :] * A_SCALE).astype(BF16)
            k = k_ref[ks, :].astype(BF16)
            v = v_ref[ks, :].astype(BF16)
            s = lax.dot_general(q, k, (((1,), (1,)), ((), ())), preferred_element_type=F32)
            s = s + bias_scr[first.astype(jnp.int32)]
            mx = jnp.max(s, axis=-1, keepdims=True)
            p = jnp.exp(s - mx)
            den = jnp.sum(p, axis=-1, keepdims=True)
            acc = jnp.dot(p.astype(BF16), v, preferred_element_type=F32)
            o_scr[c, qs, :] = acc / den
            lse_scr[c, qs, :] = jnp.broadcast_to(mx + jnp.log(den), (ATT_BLK, A_HEAD_DIM))
            return carry

        def tiles(i, carry, tile=tile):
            for u in range(ATT_UNROLL):
                tile(i * ATT_UNROLL + u, carry)
            return carry

        lax.fori_loop(0, ATT_SUPER // ATT_BLK // ATT_UNROLL, tiles, 0)

    rows = 256

    def merge(i, carry):
        sl = pl.ds(pl.multiple_of(i * rows, rows), rows)
        l0, l1, l2 = lse_scr[0, sl, :], lse_scr[1, sl, :], lse_scr[2, sl, :]
        m = jnp.maximum(jnp.maximum(l0, l1), l2)
        w0, w1, w2 = jnp.exp(l0 - m), jnp.exp(l1 - m), jnp.exp(l2 - m)
        num = w0 * o_scr[0, sl, :] + w1 * o_scr[1, sl, :] + w2 * o_scr[2, sl, :]
        o_ref[sl, :] = (num / (w0 + w1 + w2)).astype(o_ref.dtype)
        return carry

    lax.fori_loop(0, ATT_SUPER // rows, merge, 0)


def _dilated_attention(proj, seq):
    n_super = seq // ATT_SUPER
    blk = lambda off: pl.BlockSpec((seq, A_HEAD_DIM), lambda h, s: (0, off + h))
    return pl.pallas_call(
        _attn_body,
        grid=(A_HEADS, n_super),
        in_specs=[pl.BlockSpec((ATT_SUPER, A_HEAD_DIM), lambda h, s: (s, _QA_BLK + h)), blk(_KA_BLK), blk(_VA_BLK)],
        out_specs=pl.BlockSpec((ATT_SUPER, A_HEAD_DIM), lambda h, s: (s, h)),
        out_shape=jax.ShapeDtypeStruct((seq, A_WIDTH), BF16),
        scratch_shapes=[pltpu.VMEM((len(DILATED_CONFIGS), ATT_SUPER, A_HEAD_DIM), F32),
                        pltpu.VMEM((len(DILATED_CONFIGS), ATT_SUPER, A_HEAD_DIM), F32),
                        pltpu.VMEM((2, ATT_BLK, 2 * ATT_BLK), F32)],
        compiler_params=_params("arbitrary", "arbitrary"),
        name="dilated_attention",
    )(proj, proj, proj)


HG_CHUNK = 64
HG_TB = 512
HG_MIN_LEVEL = 4
HG_HEADS_PER_STEP = 2
HG_CHUNKS_PER_ITER = 4


def _hgrn_body(qb_ref, fb_ref, ib_ref, gb_ref, lbl_ref, g_ref, o_ref, st_ref):
    @pl.when(pl.program_id(1) == 0)
    def _():
        st_ref[...] = jnp.zeros_like(st_ref)

    c_ = HG_CHUNK
    lbl = lbl_ref[...]
    e = jnp.exp(lbl - jnp.max(lbl, axis=0, keepdims=True))
    lb_all = e[0:1, :] / jnp.sum(e, axis=0, keepdims=True)
    g_all = g_ref[...]

    ti = lax.broadcasted_iota(jnp.int32, (c_, c_), 0)
    si = lax.broadcasted_iota(jnp.int32, (c_, c_), 1)
    tri = jnp.where(si <= ti, 1.0, 0.0).astype(BF16)
    xor = jnp.bitwise_xor(ti, si)
    row = lax.broadcasted_iota(jnp.int32, (c_, B_KEY_DIM), 0)

    def chunk(ci, hp):
        sl = pl.ds(pl.multiple_of(ci * c_, c_), c_)
        ln = slice(hp * B_KEY_DIM, (hp + 1) * B_KEY_DIM)
        lb = lb_all[:, ln]
        f = lb + (1.0 - lb) * jax.nn.sigmoid(fb_ref[sl, ln])
        logf = jnp.log(f)
        kk = 1.0 - f
        q = jax.nn.silu(qb_ref[sl, ln])
        v = ib_ref[sl, ln]
        vb = v.astype(BF16)
        hi = logf.astype(BF16)
        rem = logf - hi.astype(F32)
        mid = rem.astype(BF16)
        low = (rem - mid.astype(F32)).astype(BF16)
        parts = jnp.dot(tri, jnp.concatenate([hi, mid, low], axis=1), preferred_element_type=F32)
        bc = parts[:, :B_KEY_DIM] + (parts[:, B_KEY_DIM:2 * B_KEY_DIM] + parts[:, 2 * B_KEY_DIM:])

        attn = jnp.zeros((c_, c_), F32)
        half = c_ // 2
        while half >= HG_MIN_LEVEL:
            blk = 2 * half
            ref_rows = [jnp.broadcast_to(bc[b0 + half - 1:b0 + half, :], (blk, B_KEY_DIM)) for b0 in range(0, c_, blk)]
            ref = jnp.concatenate(ref_rows, axis=0) if len(ref_rows) > 1 else ref_rows[0]
            dec = jnp.exp(-jnp.abs(bc - ref))
            s = lax.dot_general((q * dec).astype(BF16), (kk * dec).astype(BF16), (((1,), (1,)), ((), ())),
                                preferred_element_type=F32)
            keep = jnp.logical_and(jnp.logical_and(xor >= half, xor < blk), ti > si)
            attn = attn + jnp.where(keep, s, 0.0)
            half //= 2
        o = jnp.dot(attn.astype(BF16), vb, preferred_element_type=F32)

        for delta in range(HG_MIN_LEVEL):
            if delta == 0:
                w = q * kk
                vd = v
            else:
                valid = jnp.bitwise_and(row, HG_MIN_LEVEL - 1) >= delta
                rel = jnp.where(valid, bc - pltpu.roll(bc, delta, 0), 0.0)
                w = jnp.where(valid, q * pltpu.roll(kk, delta, 0) * jnp.exp(rel), 0.0)
                vd = pltpu.roll(v, delta, 0)
            o = o + jnp.sum(w, axis=-1, keepdims=True) * vd

        st = st_ref[hp]
        o = o + lax.dot_general((q * jnp.exp(bc)).astype(BF16), st.astype(BF16), (((1,), (1,)), ((), ())),
                                preferred_element_type=F32)
        b_last = bc[c_ - 1:c_, :]
        kdec = (kk * jnp.exp(b_last - bc)).astype(BF16)
        upd = lax.dot_general(vb, kdec, (((0,), (0,)), ((), ())), preferred_element_type=F32)
        st_ref[hp] = st * jnp.exp(b_last) + upd

        o = o * lax.rsqrt(jnp.mean(o * o, axis=-1, keepdims=True) + NORM_EPS)
        o_ref[sl, ln] = (o * g_all[:, ln] * jax.nn.silu(gb_ref[sl, ln])).astype(o_ref.dtype)

    def chunks(i, carry):
        for u in range(HG_CHUNKS_PER_ITER):
            for hp in range(HG_HEADS_PER_STEP):
                chunk(i * HG_CHUNKS_PER_ITER + u, hp)
        return carry

    lax.fori_loop(0, HG_TB // c_ // HG_CHUNKS_PER_ITER, chunks, 0)


def _hgrn2(proj, lb_logits, norm_g, seq):
    hp = HG_HEADS_PER_STEP
    width = hp * B_KEY_DIM
    col = lambda off: pl.BlockSpec((HG_TB, width), lambda h, t: (t, off // hp + h))
    n_lb = lb_logits.shape[0]
    return pl.pallas_call(
        _hgrn_body,
        grid=(B_HEADS // hp, seq // HG_TB),
        in_specs=[col(_QB_BLK), col(_FB_BLK), col(_IB_BLK), col(_GB_BLK),
                  pl.BlockSpec((n_lb, width), lambda h, t: (0, h)),
                  pl.BlockSpec((1, width), lambda h, t: (0, h))],
        out_specs=pl.BlockSpec((HG_TB, width), lambda h, t: (t, h)),
        out_shape=jax.ShapeDtypeStruct((seq, B_WIDTH), BF16),
        scratch_shapes=[pltpu.VMEM((hp, B_KEY_DIM, B_KEY_DIM), F32)],
        compiler_params=_params("arbitrary", "arbitrary"),
        name="hgrn2",
    )(proj, proj, proj, proj, lb_logits.astype(F32), norm_g.reshape(1, B_WIDTH).astype(F32))


def _merge_body(oa_ref, ob_ref, ga_ref, gb_ref, wa_ref, wb_ref, o_ref, wa_s, wb_s):
    @pl.when(pl.program_id(1) == 0)
    def _():
        wa_s[...] = wa_ref[...].astype(BF16)
        wb_s[...] = wb_ref[...].astype(BF16)

    ya = jnp.dot(oa_ref[...], wa_s[...], preferred_element_type=F32)
    yb = jnp.dot(ob_ref[...], wb_s[...], preferred_element_type=F32)
    o_ref[...] = (jax.nn.sigmoid(ga_ref[...]) * ya + jax.nn.sigmoid(gb_ref[...]) * yb).astype(o_ref.dtype)


def _branch_merge(oa, ob, proj, wa, wb, tm=512, tn=512):
    m = oa.shape[0]
    ga0, gb0 = _GATE_A_COL // tn, _GATE_B_COL // tn
    return pl.pallas_call(
        _merge_body,
        grid=(D_MODEL // tn, m // tm),
        in_specs=[pl.BlockSpec((tm, A_WIDTH), lambda j, i: (i, 0)), pl.BlockSpec((tm, B_WIDTH), lambda j, i: (i, 0)),
                  pl.BlockSpec((tm, tn), lambda j, i: (i, ga0 + j)), pl.BlockSpec((tm, tn), lambda j, i: (i, gb0 + j)),
                  pl.BlockSpec((A_WIDTH, tn), lambda j, i: (0, j)), pl.BlockSpec((B_WIDTH, tn), lambda j, i: (0, j))],
        out_specs=pl.BlockSpec((tm, tn), lambda j, i: (i, j)),
        out_shape=jax.ShapeDtypeStruct((m, D_MODEL), BF16),
        scratch_shapes=[pltpu.VMEM((A_WIDTH, tn), BF16), pltpu.VMEM((B_WIDTH, tn), BF16)],
        compiler_params=_params("arbitrary", "arbitrary"),
        name="branch_merge",
    )(oa, ob, proj, proj, wa, wb)


def _router_body(h_ref, g_ref, rw_ref, rb_ref, hn_ref, eid_ref, wk_ref):
    hn = _rmsnorm_rows(h_ref[...], g_ref[...])
    hn_ref[...] = hn
    logits = jnp.dot(hn, rw_ref[...], precision=lax.Precision.HIGHEST, preferred_element_type=F32) + rb_ref[...]
    lane = lax.broadcasted_iota(jnp.int32, logits.shape, 1)
    neg = jnp.float32(-jnp.inf)
    big = jnp.int32(LANES)

    is_g = lane < N_GROUPS
    lg = jnp.where(is_g, logits, neg)
    mg = jnp.max(lg, axis=-1, keepdims=True)
    g_idx = jnp.min(jnp.where(lg == mg, lane, big), axis=-1, keepdims=True)
    pg_top = 1.0 / jnp.sum(jnp.where(is_g, jnp.exp(lg - mg), 0.0), axis=-1, keepdims=True)

    lo = N_GROUPS + g_idx * EXPERTS_PER_GROUP
    in_grp = jnp.logical_and(lane >= lo, lane < lo + EXPERTS_PER_GROUP)
    le = jnp.where(in_grp, logits, neg)
    v1 = jnp.max(le, axis=-1, keepdims=True)
    i1 = jnp.min(jnp.where(le == v1, lane, big), axis=-1, keepdims=True)
    le2 = jnp.where(lane == i1, neg, le)
    v2 = jnp.max(le2, axis=-1, keepdims=True)
    i2 = jnp.min(jnp.where(le2 == v2, lane, big), axis=-1, keepdims=True)
    e2 = jnp.exp(v2 - v1)
    w1 = pg_top / (1.0 + e2)
    w2 = pg_top * e2 / (1.0 + e2)
    eid_ref[...] = jnp.where(lane == 0, i1 - N_GROUPS, jnp.where(lane == 1, i2 - N_GROUPS, 0))
    wk_ref[...] = jnp.where(lane == 0, w1, jnp.where(lane == 1, w2, 0.0))


def _router(h, g, rw, rb, tm=256):
    m, d = h.shape
    row = lambda w: pl.BlockSpec((tm, w), lambda i: (i, 0))
    return pl.pallas_call(
        _router_body,
        grid=(m // tm,),
        in_specs=[row(d), pl.BlockSpec((1, d), lambda i: (0, 0)), pl.BlockSpec((d, LANES), lambda i: (0, 0)),
                  pl.BlockSpec((1, LANES), lambda i: (0, 0))],
        out_specs=[row(d), row(LANES), row(LANES)],
        out_shape=[jax.ShapeDtypeStruct((m, d), F32), jax.ShapeDtypeStruct((m, LANES), jnp.int32),
                   jax.ShapeDtypeStruct((m, LANES), F32)],
        compiler_params=_params("arbitrary"),
        name="router",
    )(h, g.reshape(1, d).astype(F32), rw, rb)


MOE_TM = 256
MOE_DMA_UNROLL = 8


def _moe_body(tile_e_ref, tile_blk_ref, tile_rows_ref, n_used_ref,
              src0_ref, src_next_ref, dst_ref, hn_ref, w1_ref, w3_ref, w2_ref, y_ref,
              w1_s, w3_s, w2_s, x_s, o_s, gsem, ssem):
    j = pl.program_id(0)
    n_tiles = pl.num_programs(0)
    n_used = n_used_ref[0]
    slot = j % 2

    def gather_start(idx_ref, s):
        def body(i, carry):
            rows = [idx_ref[0, 0, i * MOE_DMA_UNROLL + u] for u in range(MOE_DMA_UNROLL)]
            for u in range(MOE_DMA_UNROLL):
                pltpu.make_async_copy(hn_ref.at[rows[u]], x_s.at[s, i * MOE_DMA_UNROLL + u], gsem.at[s]).start()
            return carry

        lax.fori_loop(0, MOE_TM // MOE_DMA_UNROLL, body, 0)

    def gather_wait(s):
        pltpu.make_async_copy(hn_ref.at[pl.ds(0, MOE_TM)], x_s.at[s], gsem.at[s]).wait()

    def scatter_start(s, n):
        def one(r):
            pltpu.make_async_copy(o_s.at[s, r], y_ref.at[dst_ref[0, 0, r]], ssem.at[s]).start()

        def body(i, carry):
            rows = [dst_ref[0, 0, i * MOE_DMA_UNROLL + u] for u in range(MOE_DMA_UNROLL)]
            for u in range(MOE_DMA_UNROLL):
                pltpu.make_async_copy(o_s.at[s, i * MOE_DMA_UNROLL + u], y_ref.at[rows[u]], ssem.at[s]).start()
            return carry

        def tail(r, carry):
            one(r)
            return carry

        n_groups = n // MOE_DMA_UNROLL
        lax.fori_loop(0, n_groups, body, 0)
        lax.fori_loop(n_groups * MOE_DMA_UNROLL, n, tail, 0)

    def scatter_wait(s, n):
        p = MOE_TM
        while p >= 1:
            @pl.when(jnp.bitwise_and(n, p) != 0)
            def _(p=p):
                pltpu.make_async_copy(o_s.at[s, pl.ds(0, p)], y_ref.at[pl.ds(0, p)], ssem.at[s]).wait()
            p //= 2

    @pl.when(j == 0)
    def _():
        gather_start(src0_ref, 0)

    @pl.when(j < n_used)
    def _():
        gather_wait(slot)

        @pl.when(j + 1 < n_used)
        def _():
            gather_start(src_next_ref, 1 - slot)

        @pl.when(j >= 2)
        def _():
            scatter_wait(slot, tile_rows_ref[jnp.maximum(j - 2, 0)])

        prev_e = tile_e_ref[jnp.maximum(j - 1, 0)]

        @pl.when(jnp.logical_or(j == 0, tile_e_ref[j] != prev_e))
        def _():
            w1_s[...] = w1_ref[0].astype(BF16)
            w3_s[...] = w3_ref[0].astype(BF16)
            w2_s[...] = w2_ref[0].astype(BF16)

        x = x_s[slot].astype(BF16)
        hid = (jax.nn.silu(jnp.dot(x, w1_s[...], preferred_element_type=F32))
               * jnp.dot(x, w3_s[...], preferred_element_type=F32))
        o_s[slot] = jnp.dot(hid.astype(BF16), w2_s[...], preferred_element_type=F32)
        scatter_start(slot, tile_rows_ref[j])

    @pl.when(j == n_tiles - 1)
    def _():
        last = n_used - 1
        scatter_wait(last % 2, tile_rows_ref[last])

        @pl.when(last >= 1)
        def _():
            scatter_wait((last - 1) % 2, tile_rows_ref[jnp.maximum(last - 1, 0)])


def _moe_experts(hn, w1, w3, w2, tile_e, tile_blk, tile_rows, n_used, src, dst, n_out_rows):
    d = hn.shape[1]
    n_tiles = tile_e.shape[0]
    ff = w1.shape[-1]
    idx_block = lambda fn: pl.BlockSpec((1, 1, MOE_TM), fn, memory_space=pltpu.SMEM)
    grid_spec = pltpu.PrefetchScalarGridSpec(
        num_scalar_prefetch=4,
        grid=(n_tiles,),
        in_specs=[
            idx_block(lambda j, te, tb, tr, nu: (0, 0, 0)),
            idx_block(lambda j, te, tb, tr, nu: (tb[jnp.minimum(j + 1, n_tiles - 1)], 0, 0)),
            idx_block(lambda j, te, tb, tr, nu: (tb[j], 0, 0)),
            pl.BlockSpec(memory_space=pl.ANY),
            pl.BlockSpec((1, d, ff), lambda j, te, tb, tr, nu: (te[j], 0, 0)),
            pl.BlockSpec((1, d, ff), lambda j, te, tb, tr, nu: (te[j], 0, 0)),
            pl.BlockSpec((1, ff, d), lambda j, te, tb, tr, nu: (te[j], 0, 0)),
        ],
        out_specs=pl.BlockSpec(memory_space=pl.ANY),
        scratch_shapes=[pltpu.VMEM((d, ff), BF16), pltpu.VMEM((d, ff), BF16), pltpu.VMEM((ff, d), BF16),
                        pltpu.VMEM((2, MOE_TM, d), F32), pltpu.VMEM((2, MOE_TM, d), F32),
                        pltpu.SemaphoreType.DMA((2,)), pltpu.SemaphoreType.DMA((2,))],
    )
    src3 = src.reshape(n_tiles, 1, MOE_TM)
    return pl.pallas_call(
        _moe_body,
        grid_spec=grid_spec,
        out_shape=jax.ShapeDtypeStruct((n_out_rows, d), F32),
        compiler_params=_params("arbitrary"),
        name="moe_experts",
    )(tile_e, tile_blk, tile_rows, n_used, src3, src3, dst.reshape(n_tiles, 1, MOE_TM), hn, w1, w3, w2)


def _combine_body(h_ref, y0_ref, y1_ref, wk_ref, g_ref, o_ref):
    wk = wk_ref[...]
    y = h_ref[...] + wk[:, 0:1] * y0_ref[...] + wk[:, 1:2] * y1_ref[...]
    o_ref[...] = _rmsnorm_rows(y, g_ref[...]).astype(o_ref.dtype)


def _combine(h, y2, wk, g, out_dtype, tm=256):
    m, d = h.shape
    nblk = m // tm
    return pl.pallas_call(
        _combine_body,
        grid=(nblk,),
        in_specs=[pl.BlockSpec((tm, d), lambda i: (i, 0)), pl.BlockSpec((tm, d), lambda i: (i, 0)),
                  pl.BlockSpec((tm, d), lambda i: (nblk + i, 0)), pl.BlockSpec((tm, LANES), lambda i: (i, 0)),
                  pl.BlockSpec((1, d), lambda i: (0, 0))],
        out_specs=pl.BlockSpec((tm, d), lambda i: (i, 0)),
        out_shape=jax.ShapeDtypeStruct((m, d), out_dtype),
        compiler_params=_params("arbitrary"),
        name="combine_final_norm",
    )(h, y2, y2, wk, g.reshape(1, d).astype(F32))


def _routing_tables(eid, n_tokens):
    n_pairs = 2 * n_tokens
    n_tiles = n_pairs // MOE_TM + N_EXPERTS
    e_flat = eid.reshape(n_pairs)
    order = jnp.argsort(e_flat, stable=True).astype(jnp.int32)
    counts = jnp.sum(e_flat[:, None] == jnp.arange(N_EXPERTS, dtype=jnp.int32)[None, :], axis=0, dtype=jnp.int32)
    tiles_per_e = (counts + MOE_TM - 1) // MOE_TM
    tile_end = jnp.cumsum(tiles_per_e)
    tile_start = tile_end - tiles_per_e
    n_used = tile_end[-1]
    sorted_start = jnp.cumsum(counts) - counts
    j = jnp.arange(n_tiles, dtype=jnp.int32)
    tile_blk = jnp.minimum(j, n_used - 1)
    tile_e = jnp.searchsorted(tile_end, tile_blk, side="right").astype(jnp.int32)
    tile_row0 = (tile_blk - tile_start[tile_e]) * MOE_TM
    tile_rows = jnp.where(j == tile_blk, jnp.clip(counts[tile_e] - tile_row0, 0, MOE_TM), 0).astype(jnp.int32)
    r = jnp.arange(MOE_TM, dtype=jnp.int32)[None, :]
    valid = r < tile_rows[:, None]
    pair = order[jnp.clip((sorted_start[tile_e] + tile_row0)[:, None] + r, 0, n_pairs - 1)]
    src_tok = jnp.where(valid, pair // 2, 0).astype(jnp.int32)
    dst_row = jnp.where(valid, (pair % 2) * n_tokens + pair // 2, 0).astype(jnp.int32)
    return (tile_e, tile_blk.astype(jnp.int32), tile_rows, n_used.reshape(1).astype(jnp.int32),
            src_tok.reshape(-1), dst_row.reshape(-1))


def kernel(x, mix_norm_g, w_in, hgrn_lb_logits, hgrn_norm_g, w_branch_a, w_branch_b, w_out, ffn_norm_g,
           router_w_group, router_b_group, router_w_expert, router_b_expert, expert_w1, expert_w3, expert_w2,
           final_norm_g):
    b, seq, d = x.shape
    assert b == 1 and d == D_MODEL and seq % ATT_SUPER == 0 and w_in.shape == (1, D_MODEL, IN_WIDTH)
    h0 = x.reshape(seq, d).astype(F32)

    xn = _rmsnorm(h0, mix_norm_g[0], BF16)
    proj = _matmul(xn, w_in[0], F32, tm=512, tn=1280, name="in_proj")
    oa = _dilated_attention(proj, seq)
    ob = _hgrn2(proj, hgrn_lb_logits, hgrn_norm_g[0], seq)
    merged = _branch_merge(oa, ob, proj, w_branch_a[0], w_branch_b[0])
    h1 = _matmul_residual(merged, w_out[0], h0, tm=512, tn=1024, name="out_proj")

    pad = LANES - N_GROUPS - N_EXPERTS
    rw = jnp.concatenate([router_w_group[0], router_w_expert[0], jnp.zeros((d, pad), F32)], axis=1).astype(F32)
    rb = jnp.concatenate([router_b_group[0], router_b_expert[0], jnp.zeros((pad,), F32)]).reshape(1, LANES).astype(F32)
    hn, eid, wk = _router(h1, ffn_norm_g[0], rw, rb)
    tile_e, tile_blk, tile_rows, n_used, src_tok, dst_row = _routing_tables(eid[:, :2], seq)
    w1 = expert_w1[0].reshape(N_EXPERTS, D_MODEL, EXPERT_FF)
    w3 = expert_w3[0].reshape(N_EXPERTS, D_MODEL, EXPERT_FF)
    w2 = expert_w2[0].reshape(N_EXPERTS, EXPERT_FF, D_MODEL)
    y2 = _moe_experts(hn, w1, w3, w2, tile_e, tile_blk, tile_rows, n_used, src_tok, dst_row, 2 * seq)

    out = _combine(h1, y2, wk, final_norm_g, x.dtype)
    return out.reshape(b, seq, d)
```

```python
import functools

import jax
import jax.numpy as jnp
from jax import lax
from jax.experimental import pallas as pl
from jax.experimental.pallas import tpu as pltpu

F32 = jnp.float32
BF16 = jnp.bfloat16

D_MODEL = 2048
A_HEADS = 12
A_HEAD_DIM = 128
A_WIDTH = A_HEADS * A_HEAD_DIM
A_SCALE = A_HEAD_DIM ** -0.5
DILATED_CONFIGS = ((128, 1), (512, 4), (2048, 16))
B_HEADS = 8
B_KEY_DIM = 128
B_WIDTH = B_HEADS * B_KEY_DIM
N_GROUPS = 4
EXPERTS_PER_GROUP = 8
N_EXPERTS = N_GROUPS * EXPERTS_PER_GROUP
EXPERT_FF = 512
NORM_EPS = 1e-6

LANES = 128
SUBLANES = 8
VMEM_LIMIT = 56 * 1024 * 1024

_QA_BLK = 0
_KA_BLK = A_HEADS
_VA_BLK = 2 * A_HEADS
_QB_BLK = 3 * A_HEADS
_FB_BLK = _QB_BLK + B_HEADS
_IB_BLK = _FB_BLK + B_HEADS
_GB_BLK = _IB_BLK + B_HEADS
_GATE_A_COL = 3 * A_WIDTH + 4 * B_WIDTH
_GATE_B_COL = _GATE_A_COL + D_MODEL
IN_WIDTH = _GATE_B_COL + D_MODEL


def _params(*sem):
    return pltpu.CompilerParams(dimension_semantics=sem, vmem_limit_bytes=VMEM_LIMIT)


def _rmsnorm_rows(x, g):
    ms = jnp.mean(x * x, axis=-1, keepdims=True)
    return x * lax.rsqrt(ms + NORM_EPS) * g


def _rmsnorm_body(x_ref, g_ref, o_ref):
    o_ref[...] = _rmsnorm_rows(x_ref[...].astype(F32), g_ref[...]).astype(o_ref.dtype)


def _rmsnorm(x, g, out_dtype, tm=512):
    m, d = x.shape
    return pl.pallas_call(
        _rmsnorm_body,
        grid=(m // tm,),
        in_specs=[pl.BlockSpec((tm, d), lambda i: (i, 0)), pl.BlockSpec((1, d), lambda i: (0, 0))],
        out_specs=pl.BlockSpec((tm, d), lambda i: (i, 0)),
        out_shape=jax.ShapeDtypeStruct((m, d), out_dtype),
        compiler_params=_params("arbitrary"),
        name="rmsnorm",
    )(x, g.reshape(1, d).astype(F32))


def _matmul_body(a_ref, w_ref, o_ref, wb_ref):
    @pl.when(pl.program_id(1) == 0)
    def _():
        wb_ref[...] = w_ref[...].astype(BF16)

    o_ref[...] = jnp.dot(a_ref[...], wb_ref[...], preferred_element_type=F32).astype(o_ref.dtype)


def _matmul(a, w, out_dtype, tm, tn, name):
    m, k = a.shape
    n = w.shape[1]
    return pl.pallas_call(
        _matmul_body,
        grid=(n // tn, m // tm),
        in_specs=[pl.BlockSpec((tm, k), lambda j, i: (i, 0)), pl.BlockSpec((k, tn), lambda j, i: (0, j))],
        out_specs=pl.BlockSpec((tm, tn), lambda j, i: (i, j)),
        out_shape=jax.ShapeDtypeStruct((m, n), out_dtype),
        scratch_shapes=[pltpu.VMEM((k, tn), BF16)],
        compiler_params=_params("arbitrary", "arbitrary"),
        name=name,
    )(a, w)


def _matmul_residual_body(a_ref, w_ref, r_ref, o_ref, wb_ref):
    @pl.when(pl.program_id(1) == 0)
    def _():
        wb_ref[...] = w_ref[...].astype(BF16)

    o_ref[...] = r_ref[...] + jnp.dot(a_ref[...], wb_ref[...], preferred_element_type=F32)


def _matmul_residual(a, w, res, tm, tn, name):
    m, k = a.shape
    n = w.shape[1]
    return pl.pallas_call(
        _matmul_residual_body,
        grid=(n // tn, m // tm),
        in_specs=[pl.BlockSpec((tm, k), lambda j, i: (i, 0)), pl.BlockSpec((k, tn), lambda j, i: (0, j)),
                  pl.BlockSpec((tm, tn), lambda j, i: (i, j))],
        out_specs=pl.BlockSpec((tm, tn), lambda j, i: (i, j)),
        out_shape=jax.ShapeDtypeStruct((m, n), F32),
        scratch_shapes=[pltpu.VMEM((k, tn), BF16)],
        compiler_params=_params("arbitrary", "arbitrary"),
        name=name,
    )(a, w, res)


ATT_BLK = 128
ATT_SUPER = 2048
ATT_UNROLL = 8


def _attn_body(q_ref, k_ref, v_ref, o_ref, o_scr, lse_scr, bias_scr):
    sb = pl.program_id(1)
    row0 = sb * ATT_SUPER
    diff = (lax.broadcasted_iota(jnp.int32, (ATT_BLK, 2 * ATT_BLK), 1)
            - lax.broadcasted_iota(jnp.int32, (ATT_BLK, 2 * ATT_BLK), 0))
    neg = jnp.float32(-jnp.inf)
    bias_scr[0] = jnp.where(jnp.logical_and(diff >= 0, diff <= ATT_BLK), 0.0, neg)
    bias_scr[1] = jnp.where(diff <= 0, 0.0, neg)

    for c, (window, dil) in enumerate(DILATED_CONFIGS):
        span = ATT_BLK * dil
        tiles_per_res = ATT_SUPER // span

        def tile(t, carry, c=c, dil=dil, span=span, tiles_per_res=tiles_per_res):
            n = t % tiles_per_res
            r = t // tiles_per_res
            q_start = n * span + r
            first = jnp.logical_and(sb == 0, n == 0)
            kv_start = row0 + q_start - jnp.where(first, 0, span)
            if dil == 1:
                q_start = pl.multiple_of(q_start, ATT_BLK)
                kv_start = pl.multiple_of(kv_start, ATT_BLK)
                qs = pl.ds(q_start, ATT_BLK)
                ks = pl.ds(kv_start, 2 * ATT_BLK)
            else:
                qs = pl.ds(q_start, ATT_BLK, stride=dil)
                ks = pl.ds(kv_start, 2 * ATT_BLK, stride=dil)
            q = (q_ref[qs, :] * A_SCALE).astype(BF16)
            k = k_ref[ks, :].astype(BF16)
            v = v_ref[ks, :].astype(BF16)
            s = lax.dot_general(q, k, (((1,), (1,)), ((), ())), preferred_element_type=F32)
            s = s + bias_scr[first.astype(jnp.int32)]
            mx = jnp.max(s, axis=-1, keepdims=True)
            p = jnp.exp(s - mx)
            den = jnp.sum(p, axis=-1, keepdims=True)
            acc = jnp.dot(p.astype(BF16), v, preferred_element_type=F32)
            o_scr[c, qs, :] = acc / den
            lse_scr[c, qs, :] = jnp.broadcast_to(mx + jnp.log(den), (ATT_BLK, A_HEAD_DIM))
            return carry

        def tiles(i, carry, tile=tile):
            for u in range(ATT_UNROLL):
                tile(i * ATT_UNROLL + u, carry)
            return carry

        lax.fori_loop(0, ATT_SUPER // ATT_BLK // ATT_UNROLL, tiles, 0)

    rows = 256

    def merge(i, carry):
        sl = pl.ds(pl.multiple_of(i * rows, rows), rows)
        l0, l1, l2 = lse_scr[0, sl, :], lse_scr[1, sl, :], lse_scr[2, sl, :]
        m = jnp.maximum(jnp.maximum(l0, l1), l2)
        w0, w1, w2 = jnp.exp(l0 - m), jnp.exp(l1 - m), jnp.exp(l2 - m)
        num = w0 * o_scr[0, sl, :] + w1 * o_scr[1, sl, :] + w2 * o_scr[2, sl, :]
        o_ref[sl, :] = (num / (w0 + w1 + w2)).astype(o_ref.dtype)
        return carry

    lax.fori_loop(0, ATT_SUPER // rows, merge, 0)


def _dilated_attention(proj, seq):
    n_super = seq // ATT_SUPER
    blk = lambda off: pl.BlockSpec((seq, A_HEAD_DIM), lambda h, s: (0, off + h))
    return pl.pallas_call(
        _attn_body,
        grid=(A_HEADS, n_super),
        in_specs=[pl.BlockSpec((ATT_SUPER, A_HEAD_DIM), lambda h, s: (s, _QA_BLK + h)), blk(_KA_BLK), blk(_VA_BLK)],
        out_specs=pl.BlockSpec((ATT_SUPER, A_HEAD_DIM), lambda h, s: (s, h)),
        out_shape=jax.ShapeDtypeStruct((seq, A_WIDTH), BF16),
        scratch_shapes=[pltpu.VMEM((len(DILATED_CONFIGS), ATT_SUPER, A_HEAD_DIM), F32),
                        pltpu.VMEM((len(DILATED_CONFIGS), ATT_SUPER, A_HEAD_DIM), F32),
                        pltpu.VMEM((2, ATT_BLK, 2 * ATT_BLK), F32)],
        compiler_params=_params("arbitrary", "arbitrary"),
        name="dilated_attention",
    )(proj, proj, proj)


HG_CHUNK = 64
HG_TB = 512
HG_MIN_LEVEL = 4
HG_HEADS_PER_STEP = 2
HG_CHUNKS_PER_ITER = 4


def _hgrn_body(qb_ref, fb_ref, ib_ref, gb_ref, lbl_ref, g_ref, o_ref, st_ref):
    @pl.when(pl.program_id(1) == 0)
    def _():
        st_ref[...] = jnp.zeros_like(st_ref)

    c_ = HG_CHUNK
    lbl = lbl_ref[...]
    e = jnp.exp(lbl - jnp.max(lbl, axis=0, keepdims=True))
    lb_all = e[0:1, :] / jnp.sum(e, axis=0, keepdims=True)
    g_all = g_ref[...]

    ti = lax.broadcasted_iota(jnp.int32, (c_, c_), 0)
    si = lax.broadcasted_iota(jnp.int32, (c_, c_), 1)
    tri = jnp.where(si <= ti, 1.0, 0.0).astype(BF16)
    xor = jnp.bitwise_xor(ti, si)
    row = lax.broadcasted_iota(jnp.int32, (c_, B_KEY_DIM), 0)

    def chunk(ci, hp):
        sl = pl.ds(pl.multiple_of(ci * c_, c_), c_)
        ln = slice(hp * B_KEY_DIM, (hp + 1) * B_KEY_DIM)
        lb = lb_all[:, ln]
        f = lb + (1.0 - lb) * jax.nn.sigmoid(fb_ref[sl, ln])
        logf = jnp.log(f)
        kk = 1.0 - f
        q = jax.nn.silu(qb_ref[sl, ln])
        v = ib_ref[sl, ln]
        vb = v.astype(BF16)
        hi = logf.astype(BF16)
        rem = logf - hi.astype(F32)
        mid = rem.astype(BF16)
        low = (rem - mid.astype(F32)).astype(BF16)
        parts = jnp.dot(tri, jnp.concatenate([hi, mid, low], axis=1), preferred_element_type=F32)
        bc = parts[:, :B_KEY_DIM] + (parts[:, B_KEY_DIM:2 * B_KEY_DIM] + parts[:, 2 * B_KEY_DIM:])

        attn = jnp.zeros((c_, c_), F32)
        half = c_ // 2
        while half >= HG_MIN_LEVEL:
            blk = 2 * half
            ref_rows = [jnp.broadcast_to(bc[b0 + half - 1:b0 + half, :], (blk, B_KEY_DIM)) for b0 in range(0, c_, blk)]
            ref = jnp.concatenate(ref_rows, axis=0) if len(ref_rows) > 1 else ref_rows[0]
            dec = jnp.exp(-jnp.abs(bc - ref))
            s = lax.dot_general((q * dec).astype(BF16), (kk * dec).astype(BF16), (((1,), (1,)), ((), ())),
                                preferred_element_type=F32)
            keep = jnp.logical_and(jnp.logical_and(xor >= half, xor < blk), ti > si)
            attn = attn + jnp.where(keep, s, 0.0)
            half //= 2
        o = jnp.dot(attn.astype(BF16), vb, preferred_element_type=F32)

        for delta in range(HG_MIN_LEVEL):
            if delta == 0:
                w = q * kk
                vd = v
            else:
                valid = jnp.bitwise_and(row, HG_MIN_LEVEL - 1) >= delta
                rel = jnp.where(valid, bc - pltpu.roll(bc, delta, 0), 0.0)
                w = jnp.where(valid, q * pltpu.roll(kk, delta, 0) * jnp.exp(rel), 0.0)
                vd = pltpu.roll(v, delta, 0)
            o = o + jnp.sum(w, axis=-1, keepdims=True) * vd

        st = st_ref[hp]
        o = o + lax.dot_general((q * jnp.exp(bc)).astype(BF16), st.astype(BF16), (((1,), (1,)), ((), ())),
                                preferred_element_type=F32)
        b_last = bc[c_ - 1:c_, :]
        kdec = (kk * jnp.exp(b_last - bc)).astype(BF16)
        upd = lax.dot_general(vb, kdec, (((0,), (0,)), ((), ())), preferred_element_type=F32)
        st_ref[hp] = st * jnp.exp(b_last) + upd

        o = o * lax.rsqrt(jnp.mean(o * o, axis=-1, keepdims=True) + NORM_EPS)
        o_ref[sl, ln] = (o * g_all[:, ln] * jax.nn.silu(gb_ref[sl, ln])).astype(o_ref.dtype)

    def chunks(i, carry):
        for u in range(HG_CHUNKS_PER_ITER):
            for hp in range(HG_HEADS_PER_STEP):
                chunk(i * HG_CHUNKS_PER_ITER + u, hp)
        return carry

    lax.fori_loop(0, HG_TB // c_ // HG_CHUNKS_PER_ITER, chunks, 0)


def _hgrn2(proj, lb_logits, norm_g, seq):
    hp = HG_HEADS_PER_STEP
    width = hp * B_KEY_DIM
    col = lambda off: pl.BlockSpec((HG_TB, width), lambda h, t: (t, off // hp + h))
    n_lb = lb_logits.shape[0]
    return pl.pallas_call(
        _hgrn_body,
        grid=(B_HEADS // hp, seq // HG_TB),
        in_specs=[col(_QB_BLK), col(_FB_BLK), col(_IB_BLK), col(_GB_BLK),
                  pl.BlockSpec((n_lb, width), lambda h, t: (0, h)),
                  pl.BlockSpec((1, width), lambda h, t: (0, h))],
        out_specs=pl.BlockSpec((HG_TB, width), lambda h, t: (t, h)),
        out_shape=jax.ShapeDtypeStruct((seq, B_WIDTH), BF16),
        scratch_shapes=[pltpu.VMEM((hp, B_KEY_DIM, B_KEY_DIM), F32)],
        compiler_params=_params("arbitrary", "arbitrary"),
        name="hgrn2",
    )(proj, proj, proj, proj, lb_logits.astype(F32), norm_g.reshape(1, B_WIDTH).astype(F32))


def _merge_body(oa_ref, ob_ref, ga_ref, gb_ref, wa_ref, wb_ref, o_ref, wa_s, wb_s):
    @pl.when(pl.program_id(1) == 0)
    def _():
        wa_s[...] = wa_ref[...].astype(BF16)
        wb_s[...] = wb_ref[...].astype(BF16)

    ya = jnp.dot(oa_ref[...], wa_s[...], preferred_element_type=F32)
    yb = jnp.dot(ob_ref[...], wb_s[...], preferred_element_type=F32)
    o_ref[...] = (jax.nn.sigmoid(ga_ref[...]) * ya + jax.nn.sigmoid(gb_ref[...]) * yb).astype(o_ref.dtype)


def _branch_merge(oa, ob, proj, wa, wb, tm=512, tn=512):
    m = oa.shape[0]
    ga0, gb0 = _GATE_A_COL // tn, _GATE_B_COL // tn
    return pl.pallas_call(
        _merge_body,
        grid=(D_MODEL // tn, m // tm),
        in_specs=[pl.BlockSpec((tm, A_WIDTH), lambda j, i: (i, 0)), pl.BlockSpec((tm, B_WIDTH), lambda j, i: (i, 0)),
                  pl.BlockSpec((tm, tn), lambda j, i: (i, ga0 + j)), pl.BlockSpec((tm, tn), lambda j, i: (i, gb0 + j)),
                  pl.BlockSpec((A_WIDTH, tn), lambda j, i: (0, j)), pl.BlockSpec((B_WIDTH, tn), lambda j, i: (0, j))],
        out_specs=pl.BlockSpec((tm, tn), lambda j, i: (i, j)),
        out_shape=jax.ShapeDtypeStruct((m, D_MODEL), BF16),
        scratch_shapes=[pltpu.VMEM((A_WIDTH, tn), BF16), pltpu.VMEM((B_WIDTH, tn), BF16)],
        compiler_params=_params("arbitrary", "arbitrary"),
        name="branch_merge",
    )(oa, ob, proj, proj, wa, wb)


def _router_body(h_ref, g_ref, rw_ref, rb_ref, hn_ref, eid_ref, wk_ref):
    hn = _rmsnorm_rows(h_ref[...], g_ref[...])
    hn_ref[...] = hn
    logits = jnp.dot(hn, rw_ref[...], precision=lax.Precision.HIGHEST, preferred_element_type=F32) + rb_ref[...]
    lane = lax.broadcasted_iota(jnp.int32, logits.shape, 1)
    neg = jnp.float32(-jnp.inf)
    big = jnp.int32(LANES)

    is_g = lane < N_GROUPS
    lg = jnp.where(is_g, logits, neg)
    mg = jnp.max(lg, axis=-1, keepdims=True)
    g_idx = jnp.min(jnp.where(lg == mg, lane, big), axis=-1, keepdims=True)
    pg_top = 1.0 / jnp.sum(jnp.where(is_g, jnp.exp(lg - mg), 0.0), axis=-1, keepdims=True)

    lo = N_GROUPS + g_idx * EXPERTS_PER_GROUP
    in_grp = jnp.logical_and(lane >= lo, lane < lo + EXPERTS_PER_GROUP)
    le = jnp.where(in_grp, logits, neg)
    v1 = jnp.max(le, axis=-1, keepdims=True)
    i1 = jnp.min(jnp.where(le == v1, lane, big), axis=-1, keepdims=True)
    le2 = jnp.where(lane == i1, neg, le)
    v2 = jnp.max(le2, axis=-1, keepdims=True)
    i2 = jnp.min(jnp.where(le2 == v2, lane, big), axis=-1, keepdims=True)
    e2 = jnp.exp(v2 - v1)
    w1 = pg_top / (1.0 + e2)
    w2 = pg_top * e2 / (1.0 + e2)
    eid_ref[...] = jnp.where(lane == 0, i1 - N_GROUPS, jnp.where(lane == 1, i2 - N_GROUPS, 0))
    wk_ref[...] = jnp.where(lane == 0, w1, jnp.where(lane == 1, w2, 0.0))


def _router(h, g, rw, rb, tm=256):
    m, d = h.shape
    row = lambda w: pl.BlockSpec((tm, w), lambda i: (i, 0))
    return pl.pallas_call(
        _router_body,
        grid=(m // tm,),
        in_specs=[row(d), pl.BlockSpec((1, d), lambda i: (0, 0)), pl.BlockSpec((d, LANES), lambda i: (0, 0)),
                  pl.BlockSpec((1, LANES), lambda i: (0, 0))],
        out_specs=[row(d), row(LANES), row(LANES)],
        out_shape=[jax.ShapeDtypeStruct((m, d), F32), jax.ShapeDtypeStruct((m, LANES), jnp.int32),
                   jax.ShapeDtypeStruct((m, LANES), F32)],
        compiler_params=_params("arbitrary"),
        name="router",
    )(h, g.reshape(1, d).astype(F32), rw, rb)


MOE_TM = 256


def _moe_body(tile_e_ref, tile_blk_ref, n_used_ref,
              src0_ref, src_next_ref, dst_prev_ref, hn_ref, w1_ref, w3_ref, w2_ref, y_ref,
              w1_s, w3_s, w2_s, x0, x1, o0, o1, gsem, ssem):
    j = pl.program_id(0)
    n_used = n_used_ref[0]
    xs = (x0, x1)
    os_ = (o0, o1)
    n_real = y_ref.shape[0] - 2 * MOE_TM

    def gather_issue(idx_ref, b):
        for r in range(MOE_TM):
            pltpu.make_async_copy(hn_ref.at[idx_ref[0, 0, r]], xs[b].at[r], gsem.at[b]).start()

    def gather_wait(b):
        pltpu.make_async_copy(hn_ref.at[pl.ds(0, MOE_TM)], xs[b], gsem.at[b]).wait()

    def scatter_issue(idx_ref, b):
        for r in range(MOE_TM):
            pltpu.make_async_copy(os_[b].at[r], y_ref.at[idx_ref[0, 0, r]], ssem.at[b]).start()

    def scatter_wait(b):
        pltpu.make_async_copy(os_[b], y_ref.at[pl.ds(0, MOE_TM)], ssem.at[b]).wait()

    @pl.when(j == 0)
    def _():
        o1[...] = jnp.zeros_like(o1)
        fill = pltpu.make_async_copy(o1, y_ref.at[pl.ds(n_real, MOE_TM)], ssem.at[1])
        fill.start()
        fill.wait()
        gather_issue(src0_ref, 0)

    for b in (0, 1):
        @pl.when(jnp.logical_and(j < n_used, j % 2 == b))
        def _(b=b):
            gather_wait(b)

            @pl.when(j >= 1)
            def _():
                scatter_wait(b)

            prev_e = tile_e_ref[jnp.maximum(j - 1, 0)]

            @pl.when(jnp.logical_or(j == 0, tile_e_ref[j] != prev_e))
            def _():
                w1_s[...] = w1_ref[0].astype(BF16)
                w3_s[...] = w3_ref[0].astype(BF16)
                w2_s[...] = w2_ref[0].astype(BF16)

            gather_issue(src_next_ref, 1 - b)
            scatter_issue(dst_prev_ref, 1 - b)
            x = xs[b][...].astype(BF16)
            hid = (jax.nn.silu(jnp.dot(x, w1_s[...], preferred_element_type=F32))
                   * jnp.dot(x, w3_s[...], preferred_element_type=F32))
            os_[b][...] = jnp.dot(hid.astype(BF16), w2_s[...], preferred_element_type=F32)

    for b in (0, 1):
        @pl.when(jnp.logical_and(j == n_used, j % 2 == b))
        def _(b=b):
            gather_wait(b)
            scatter_wait(b)
            scatter_issue(dst_prev_ref, 1 - b)
            scatter_wait(1 - b)


def _moe_experts(hn, w1, w3, w2, tile_e, tile_blk, n_used, src, dst, n_tokens):
    d = hn.shape[1]
    n_tiles = tile_e.shape[0]
    ff = w1.shape[-1]
    idx_block = lambda fn: pl.BlockSpec((1, 1, MOE_TM), fn, memory_space=pltpu.SMEM)
    grid_spec = pltpu.PrefetchScalarGridSpec(
        num_scalar_prefetch=3,
        grid=(n_tiles,),
        in_specs=[
            idx_block(lambda j, te, tb, nu: (0, 0, 0)),
            idx_block(lambda j, te, tb, nu: (tb[jnp.minimum(j + 1, n_tiles - 1)], 0, 0)),
            idx_block(lambda j, te, tb, nu: (jnp.where(j == 0, n_tiles, tb[jnp.maximum(j - 1, 0)]), 0, 0)),
            pl.BlockSpec(memory_space=pl.ANY),
            pl.BlockSpec((1, d, ff), lambda j, te, tb, nu: (te[j], 0, 0)),
            pl.BlockSpec((1, d, ff), lambda j, te, tb, nu: (te[j], 0, 0)),
            pl.BlockSpec((1, ff, d), lambda j, te, tb, nu: (te[j], 0, 0)),
        ],
        out_specs=pl.BlockSpec(memory_space=pl.ANY),
        scratch_shapes=[pltpu.VMEM((d, ff), BF16), pltpu.VMEM((d, ff), BF16), pltpu.VMEM((ff, d), BF16),
                        pltpu.VMEM((MOE_TM, d), F32), pltpu.VMEM((MOE_TM, d), F32),
                        pltpu.VMEM((MOE_TM, d), F32), pltpu.VMEM((MOE_TM, d), F32),
                        pltpu.SemaphoreType.DMA((2,)), pltpu.SemaphoreType.DMA((2,))],
    )
    src3 = src.reshape(n_tiles, 1, MOE_TM)
    return pl.pallas_call(
        _moe_body,
        grid_spec=grid_spec,
        out_shape=jax.ShapeDtypeStruct((2 * n_tokens + 2 * MOE_TM, d), F32),
        compiler_params=_params("arbitrary"),
        name="moe_experts",
    )(tile_e, tile_blk, n_used, src3, src3, dst.reshape(n_tiles + 1, 1, MOE_TM), hn, w1, w3, w2)


def _combine_body(h_ref, y0_ref, y1_ref, wk_ref, g_ref, o_ref):
    wk = wk_ref[...]
    y = h_ref[...] + wk[:, 0:1] * y0_ref[...] + wk[:, 1:2] * y1_ref[...]
    o_ref[...] = _rmsnorm_rows(y, g_ref[...]).astype(o_ref.dtype)


def _combine(h, y2, wk, g, out_dtype, tm=256):
    m, d = h.shape
    nblk = m // tm
    return pl.pallas_call(
        _combine_body,
        grid=(nblk,),
        in_specs=[pl.BlockSpec((tm, d), lambda i: (i, 0)), pl.BlockSpec((tm, d), lambda i: (i, 0)),
                  pl.BlockSpec((tm, d), lambda i: (nblk + i, 0)), pl.BlockSpec((tm, LANES), lambda i: (i, 0)),
                  pl.BlockSpec((1, d), lambda i: (0, 0))],
        out_specs=pl.BlockSpec((tm, d), lambda i: (i, 0)),
        out_shape=jax.ShapeDtypeStruct((m, d), out_dtype),
        compiler_params=_params("arbitrary"),
        name="combine_final_norm",
    )(h, y2, y2, wk, g.reshape(1, d).astype(F32))


def _routing_tables(eid, n_tokens):
    n_pairs = 2 * n_tokens
    n_tiles = n_pairs // MOE_TM + N_EXPERTS + 1
    e_flat = eid.reshape(n_pairs)
    order = jnp.argsort(e_flat, stable=True).astype(jnp.int32)
    counts = jnp.sum(e_flat[:, None] == jnp.arange(N_EXPERTS, dtype=jnp.int32)[None, :], axis=0, dtype=jnp.int32)
    tiles_per_e = (counts + MOE_TM - 1) // MOE_TM
    tile_end = jnp.cumsum(tiles_per_e)
    tile_start = tile_end - tiles_per_e
    n_used = tile_end[-1]
    sorted_start = jnp.cumsum(counts) - counts
    j = jnp.arange(n_tiles, dtype=jnp.int32)
    tile_blk = jnp.minimum(j, n_used - 1)
    tile_e = jnp.searchsorted(tile_end, tile_blk, side="right").astype(jnp.int32)
    tile_row0 = (tile_blk - tile_start[tile_e]) * MOE_TM
    tile_rows = jnp.where(j == tile_blk, jnp.clip(counts[tile_e] - tile_row0, 0, MOE_TM), 0).astype(jnp.int32)
    r = jnp.arange(MOE_TM, dtype=jnp.int32)[None, :]
    valid = r < tile_rows[:, None]
    pair = order[jnp.clip((sorted_start[tile_e] + tile_row0)[:, None] + r, 0, n_pairs - 1)]
    src_tok = jnp.where(valid, pair // 2, 0).astype(jnp.int32)
    spare = n_pairs + (j % 2)[:, None] * MOE_TM + r
    dst_row = jnp.where(valid, (pair % 2) * n_tokens + pair // 2, spare).astype(jnp.int32)
    dst_row = jnp.concatenate([dst_row, n_pairs + MOE_TM + r], axis=0)
    return (tile_e, tile_blk.astype(jnp.int32), n_used.reshape(1).astype(jnp.int32),
            src_tok.reshape(-1), dst_row.reshape(-1))


def kernel(x, mix_norm_g, w_in, hgrn_lb_logits, hgrn_norm_g, w_branch_a, w_branch_b, w_out, ffn_norm_g,
           router_w_group, router_b_group, router_w_expert, router_b_expert, expert_w1, expert_w3, expert_w2,
           final_norm_g):
    b, seq, d = x.shape
    assert b == 1 and d == D_MODEL and seq % ATT_SUPER == 0 and w_in.shape == (1, D_MODEL, IN_WIDTH)
    h0 = x.reshape(seq, d).astype(F32)

    xn = _rmsnorm(h0, mix_norm_g[0], BF16)
    proj = _matmul(xn, w_in[0], F32, tm=512, tn=1280, name="in_proj")
    oa = _dilated_attention(proj, seq)
    ob = _hgrn2(proj, hgrn_lb_logits, hgrn_norm_g[0], seq)
    merged = _branch_merge(oa, ob, proj, w_branch_a[0], w_branch_b[0])
    h1 = _matmul_residual(merged, w_out[0], h0, tm=512, tn=1024, name="out_proj")

    pad = LANES - N_GROUPS - N_EXPERTS
    rw = jnp.concatenate([router_w_group[0], router_w_expert[0], jnp.zeros((d, pad), F32)], axis=1).astype(F32)
    rb = jnp.concatenate([router_b_group[0], router_b_expert[0], jnp.zeros((pad,), F32)]).reshape(1, LANES).astype(F32)
    hn, eid, wk = _router(h1, ffn_norm_g[0], rw, rb)
    tile_e, tile_blk, n_used, src_tok, dst_row = _routing_tables(eid[:, :2], seq)
    w1 = expert_w1[0].reshape(N_EXPERTS, D_MODEL, EXPERT_FF)
    w3 = expert_w3[0].reshape(N_EXPERTS, D_MODEL, EXPERT_FF)
    w2 = expert_w2[0].reshape(N_EXPERTS, EXPERT_FF, D_MODEL)
    y2 = _moe_experts(hn, w1, w3, w2, tile_e, tile_blk, n_used, src_tok, dst_row, seq)

    out = _combine(h1, y2, wk, final_norm_g, x.dtype)
    return out.reshape(b, seq, d)
```

```python
import functools

import jax
import jax.numpy as jnp
from jax import lax
from jax.experimental import pallas as pl
from jax.experimental.pallas import tpu as pltpu

F32 = jnp.float32
BF16 = jnp.bfloat16

D_MODEL = 2048
A_HEADS = 12
A_HEAD_DIM = 128
A_WIDTH = A_HEADS * A_HEAD_DIM
A_SCALE = A_HEAD_DIM ** -0.5
DILATED_CONFIGS = ((128, 1), (512, 4), (2048, 16))
B_HEADS = 8
B_KEY_DIM = 128
B_WIDTH = B_HEADS * B_KEY_DIM
N_GROUPS = 4
EXPERTS_PER_GROUP = 8
N_EXPERTS = N_GROUPS * EXPERTS_PER_GROUP
EXPERT_FF = 512
NORM_EPS = 1e-6

LANES = 128
SUBLANES = 8
VMEM_LIMIT = 56 * 1024 * 1024

_QA_BLK = 0
_KA_BLK = A_HEADS
_VA_BLK = 2 * A_HEADS
_QB_BLK = 3 * A_HEADS
_FB_BLK = _QB_BLK + B_HEADS
_IB_BLK = _FB_BLK + B_HEADS
_GB_BLK = _IB_BLK + B_HEADS
_GATE_A_COL = 3 * A_WIDTH + 4 * B_WIDTH
_GATE_B_COL = _GATE_A_COL + D_MODEL
IN_WIDTH = _GATE_B_COL + D_MODEL


def _params(*sem):
    return pltpu.CompilerParams(dimension_semantics=sem, vmem_limit_bytes=VMEM_LIMIT)


def _rmsnorm_rows(x, g):
    ms = jnp.mean(x * x, axis=-1, keepdims=True)
    return x * lax.rsqrt(ms + NORM_EPS) * g


def _pack_bf16_pairs(x):
    w = x.shape[1] // 2
    lo = pltpu.bitcast(x[:, :w].astype(BF16).astype(F32), jnp.uint32)
    hi = pltpu.bitcast(x[:, w:].astype(BF16).astype(F32), jnp.uint32)
    return jnp.bitwise_or(jnp.bitwise_and(hi, jnp.uint32(0xFFFF0000)), jnp.right_shift(lo, jnp.uint32(16)))


def _unpack_bf16_pairs(u):
    lo = pltpu.bitcast(jnp.left_shift(u, jnp.uint32(16)), F32)
    hi = pltpu.bitcast(jnp.bitwise_and(u, jnp.uint32(0xFFFF0000)), F32)
    return jnp.concatenate([lo, hi], axis=1)


def _rmsnorm_body(x_ref, g_ref, o_ref):
    o_ref[...] = _rmsnorm_rows(x_ref[...].astype(F32), g_ref[...]).astype(o_ref.dtype)


def _rmsnorm(x, g, out_dtype, tm=512):
    m, d = x.shape
    return pl.pallas_call(
        _rmsnorm_body,
        grid=(m // tm,),
        in_specs=[pl.BlockSpec((tm, d), lambda i: (i, 0)), pl.BlockSpec((1, d), lambda i: (0, 0))],
        out_specs=pl.BlockSpec((tm, d), lambda i: (i, 0)),
        out_shape=jax.ShapeDtypeStruct((m, d), out_dtype),
        compiler_params=_params("arbitrary"),
        name="rmsnorm",
    )(x, g.reshape(1, d).astype(F32))


def _matmul_body(a_ref, w_ref, o_ref, wb_ref):
    @pl.when(pl.program_id(1) == 0)
    def _():
        wb_ref[...] = w_ref[...].astype(BF16)

    o_ref[...] = jnp.dot(a_ref[...], wb_ref[...], preferred_element_type=F32).astype(o_ref.dtype)


def _matmul(a, w, out_dtype, tm, tn, name):
    m, k = a.shape
    n = w.shape[1]
    return pl.pallas_call(
        _matmul_body,
        grid=(n // tn, m // tm),
        in_specs=[pl.BlockSpec((tm, k), lambda j, i: (i, 0)), pl.BlockSpec((k, tn), lambda j, i: (0, j))],
        out_specs=pl.BlockSpec((tm, tn), lambda j, i: (i, j)),
        out_shape=jax.ShapeDtypeStruct((m, n), out_dtype),
        scratch_shapes=[pltpu.VMEM((k, tn), BF16)],
        compiler_params=_params("arbitrary", "arbitrary"),
        name=name,
    )(a, w)


def _matmul_residual_body(a_ref, w_ref, r_ref, o_ref, wb_ref):
    @pl.when(pl.program_id(1) == 0)
    def _():
        wb_ref[...] = w_ref[...].astype(BF16)

    o_ref[...] = r_ref[...] + jnp.dot(a_ref[...], wb_ref[...], preferred_element_type=F32)


def _matmul_residual(a, w, res, tm, tn, name):
    m, k = a.shape
    n = w.shape[1]
    return pl.pallas_call(
        _matmul_residual_body,
        grid=(n // tn, m // tm),
        in_specs=[pl.BlockSpec((tm, k), lambda j, i: (i, 0)), pl.BlockSpec((k, tn), lambda j, i: (0, j)),
                  pl.BlockSpec((tm, tn), lambda j, i: (i, j))],
        out_specs=pl.BlockSpec((tm, tn), lambda j, i: (i, j)),
        out_shape=jax.ShapeDtypeStruct((m, n), F32),
        scratch_shapes=[pltpu.VMEM((k, tn), BF16)],
        compiler_params=_params("arbitrary", "arbitrary"),
        name=name,
    )(a, w, res)


ATT_BLK = 128
ATT_SUPER = 2048
ATT_UNROLL = 8


def _attn_body(q_ref, k_ref, v_ref, o_ref, o_scr, lse_scr, bias_scr):
    sb = pl.program_id(1)
    row0 = sb * ATT_SUPER
    diff = (lax.broadcasted_iota(jnp.int32, (ATT_BLK, 2 * ATT_BLK), 1)
            - lax.broadcasted_iota(jnp.int32, (ATT_BLK, 2 * ATT_BLK), 0))
    neg = jnp.float32(-jnp.inf)
    bias_scr[0] = jnp.where(jnp.logical_and(diff >= 0, diff <= ATT_BLK), 0.0, neg)
    bias_scr[1] = jnp.where(diff <= 0, 0.0, neg)

    for c, (window, dil) in enumerate(DILATED_CONFIGS):
        span = ATT_BLK * dil
        tiles_per_res = ATT_SUPER // span

        def tile(t, carry, c=c, dil=dil, span=span, tiles_per_res=tiles_per_res):
            n = t % tiles_per_res
            r = t // tiles_per_res
            q_start = n * span + r
            first = jnp.logical_and(sb == 0, n == 0)
            kv_start = row0 + q_start - jnp.where(first, 0, span)
            if dil == 1:
                q_start = pl.multiple_of(q_start, ATT_BLK)
                kv_start = pl.multiple_of(kv_start, ATT_BLK)
                qs = pl.ds(q_start, ATT_BLK)
                ks = pl.ds(kv_start, 2 * ATT_BLK)
            else:
                qs = pl.ds(q_start, ATT_BLK, stride=dil)
                ks = pl.ds(kv_start, 2 * ATT_BLK, stride=dil)
            q = (q_ref[qs, :] * A_SCALE).astype(BF16)
            k = k_ref[ks, :].astype(BF16)
            v = v_ref[ks, :].astype(BF16)
            s = lax.dot_general(q, k, (((1,), (1,)), ((), ())), preferred_element_type=F32)
            s = s + bias_scr[first.astype(jnp.int32)]
            mx = jnp.max(s, axis=-1, keepdims=True)
            p = jnp.exp(s - mx)
            den = jnp.sum(p, axis=-1, keepdims=True)
            acc = jnp.dot(p.astype(BF16), v, preferred_element_type=F32)
            o_scr[c, qs, :] = acc / den
            lse_scr[c, qs, :] = jnp.broadcast_to(mx + jnp.log(den), (ATT_BLK, A_HEAD_DIM))
            return carry

        def tiles(i, carry, tile=tile):
            for u in range(ATT_UNROLL):
                tile(i * ATT_UNROLL + u, carry)
            return carry

        lax.fori_loop(0, ATT_SUPER // ATT_BLK // ATT_UNROLL, tiles, 0)

    rows = 256

    def merge(i, carry):
        sl = pl.ds(pl.multiple_of(i * rows, rows), rows)
        l0, l1, l2 = lse_scr[0, sl, :], lse_scr[1, sl, :], lse_scr[2, sl, :]
        m = jnp.maximum(jnp.maximum(l0, l1), l2)
        w0, w1, w2 = jnp.exp(l0 - m), jnp.exp(l1 - m), jnp.exp(l2 - m)
        num = w0 * o_scr[0, sl, :] + w1 * o_scr[1, sl, :] + w2 * o_scr[2, sl, :]
        o_ref[sl, :] = (num / (w0 + w1 + w2)).astype(o_ref.dtype)
        return carry

    lax.fori_loop(0, ATT_SUPER // rows, merge, 0)


def _dilated_attention(proj, seq):
    n_super = seq // ATT_SUPER
    blk = lambda off: pl.BlockSpec((seq, A_HEAD_DIM), lambda h, s: (0, off + h))
    return pl.pallas_call(
        _attn_body,
        grid=(A_HEADS, n_super),
        in_specs=[pl.BlockSpec((ATT_SUPER, A_HEAD_DIM), lambda h, s: (s, _QA_BLK + h)), blk(_KA_BLK), blk(_VA_BLK)],
        out_specs=pl.BlockSpec((ATT_SUPER, A_HEAD_DIM), lambda h, s: (s, h)),
        out_shape=jax.ShapeDtypeStruct((seq, A_WIDTH), BF16),
        scratch_shapes=[pltpu.VMEM((len(DILATED_CONFIGS), ATT_SUPER, A_HEAD_DIM), F32),
                        pltpu.VMEM((len(DILATED_CONFIGS), ATT_SUPER, A_HEAD_DIM), F32),
                        pltpu.VMEM((2, ATT_BLK, 2 * ATT_BLK), F32)],
        compiler_params=_params("arbitrary", "arbitrary"),
        name="dilated_attention",
    )(proj, proj, proj)


HG_CHUNK = 64
HG_TB = 512
HG_MIN_LEVEL = 4
HG_HEADS_PER_STEP = 2
HG_CHUNKS_PER_ITER = 4


def _hgrn_body(qb_ref, fb_ref, ib_ref, gb_ref, lbl_ref, g_ref, o_ref, st_ref):
    @pl.when(pl.program_id(1) == 0)
    def _():
        st_ref[...] = jnp.zeros_like(st_ref)

    c_ = HG_CHUNK
    lbl = lbl_ref[...]
    e = jnp.exp(lbl - jnp.max(lbl, axis=0, keepdims=True))
    lb_all = e[0:1, :] / jnp.sum(e, axis=0, keepdims=True)
    g_all = g_ref[...]

    ti = lax.broadcasted_iota(jnp.int32, (c_, c_), 0)
    si = lax.broadcasted_iota(jnp.int32, (c_, c_), 1)
    tri = jnp.where(si <= ti, 1.0, 0.0).astype(BF16)
    xor = jnp.bitwise_xor(ti, si)
    row = lax.broadcasted_iota(jnp.int32, (c_, B_KEY_DIM), 0)

    def chunk(ci, hp):
        sl = pl.ds(pl.multiple_of(ci * c_, c_), c_)
        ln = slice(hp * B_KEY_DIM, (hp + 1) * B_KEY_DIM)
        lb = lb_all[:, ln]
        f = lb + (1.0 - lb) * jax.nn.sigmoid(fb_ref[sl, ln])
        logf = jnp.log(f)
        kk = 1.0 - f
        q = jax.nn.silu(qb_ref[sl, ln])
        v = ib_ref[sl, ln]
        vb = v.astype(BF16)
        hi = logf.astype(BF16)
        rem = logf - hi.astype(F32)
        mid = rem.astype(BF16)
        low = (rem - mid.astype(F32)).astype(BF16)
        parts = jnp.dot(tri, jnp.concatenate([hi, mid, low], axis=1), preferred_element_type=F32)
        bc = parts[:, :B_KEY_DIM] + (parts[:, B_KEY_DIM:2 * B_KEY_DIM] + parts[:, 2 * B_KEY_DIM:])

        attn = jnp.zeros((c_, c_), F32)
        half = c_ // 2
        while half >= HG_MIN_LEVEL:
            blk = 2 * half
            ref_rows = [jnp.broadcast_to(bc[b0 + half - 1:b0 + half, :], (blk, B_KEY_DIM)) for b0 in range(0, c_, blk)]
            ref = jnp.concatenate(ref_rows, axis=0) if len(ref_rows) > 1 else ref_rows[0]
            dec = jnp.exp(-jnp.abs(bc - ref))
            s = lax.dot_general((q * dec).astype(BF16), (kk * dec).astype(BF16), (((1,), (1,)), ((), ())),
                                preferred_element_type=F32)
            keep = jnp.logical_and(jnp.logical_and(xor >= half, xor < blk), ti > si)
            attn = attn + jnp.where(keep, s, 0.0)
            half //= 2
        o = jnp.dot(attn.astype(BF16), vb, preferred_element_type=F32)

        for delta in range(HG_MIN_LEVEL):
            if delta == 0:
                w = q * kk
                vd = v
            else:
                valid = jnp.bitwise_and(row, HG_MIN_LEVEL - 1) >= delta
                rel = jnp.where(valid, bc - pltpu.roll(bc, delta, 0), 0.0)
                w = jnp.where(valid, q * pltpu.roll(kk, delta, 0) * jnp.exp(rel), 0.0)
                vd = pltpu.roll(v, delta, 0)
            o = o + jnp.sum(w, axis=-1, keepdims=True) * vd

        st = st_ref[hp]
        o = o + lax.dot_general((q * jnp.exp(bc)).astype(BF16), st.astype(BF16), (((1,), (1,)), ((), ())),
                                preferred_element_type=F32)
        b_last = bc[c_ - 1:c_, :]
        kdec = (kk * jnp.exp(b_last - bc)).astype(BF16)
        upd = lax.dot_general(vb, kdec, (((0,), (0,)), ((), ())), preferred_element_type=F32)
        st_ref[hp] = st * jnp.exp(b_last) + upd

        o = o * lax.rsqrt(jnp.mean(o * o, axis=-1, keepdims=True) + NORM_EPS)
        o_ref[sl, ln] = (o * g_all[:, ln] * jax.nn.silu(gb_ref[sl, ln])).astype(o_ref.dtype)

    def chunks(i, carry):
        for u in range(HG_CHUNKS_PER_ITER):
            for hp in range(HG_HEADS_PER_STEP):
                chunk(i * HG_CHUNKS_PER_ITER + u, hp)
        return carry

    lax.fori_loop(0, HG_TB // c_ // HG_CHUNKS_PER_ITER, chunks, 0)


def _hgrn2(proj, lb_logits, norm_g, seq):
    hp = HG_HEADS_PER_STEP
    width = hp * B_KEY_DIM
    col = lambda off: pl.BlockSpec((HG_TB, width), lambda h, t: (t, off // hp + h))
    n_lb = lb_logits.shape[0]
    return pl.pallas_call(
        _hgrn_body,
        grid=(B_HEADS // hp, seq // HG_TB),
        in_specs=[col(_QB_BLK), col(_FB_BLK), col(_IB_BLK), col(_GB_BLK),
                  pl.BlockSpec((n_lb, width), lambda h, t: (0, h)),
                  pl.BlockSpec((1, width), lambda h, t: (0, h))],
        out_specs=pl.BlockSpec((HG_TB, width), lambda h, t: (t, h)),
        out_shape=jax.ShapeDtypeStruct((seq, B_WIDTH), BF16),
        scratch_shapes=[pltpu.VMEM((hp, B_KEY_DIM, B_KEY_DIM), F32)],
        compiler_params=_params("arbitrary", "arbitrary"),
        name="hgrn2",
    )(proj, proj, proj, proj, lb_logits.astype(F32), norm_g.reshape(1, B_WIDTH).astype(F32))


def _merge_body(oa_ref, ob_ref, ga_ref, gb_ref, wa_ref, wb_ref, o_ref, wa_s, wb_s):
    @pl.when(pl.program_id(1) == 0)
    def _():
        wa_s[...] = wa_ref[...].astype(BF16)
        wb_s[...] = wb_ref[...].astype(BF16)

    ya = jnp.dot(oa_ref[...], wa_s[...], preferred_element_type=F32)
    yb = jnp.dot(ob_ref[...], wb_s[...], preferred_element_type=F32)
    o_ref[...] = (jax.nn.sigmoid(ga_ref[...]) * ya + jax.nn.sigmoid(gb_ref[...]) * yb).astype(o_ref.dtype)


def _branch_merge(oa, ob, proj, wa, wb, tm=512, tn=512):
    m = oa.shape[0]
    ga0, gb0 = _GATE_A_COL // tn, _GATE_B_COL // tn
    return pl.pallas_call(
        _merge_body,
        grid=(D_MODEL // tn, m // tm),
        in_specs=[pl.BlockSpec((tm, A_WIDTH), lambda j, i: (i, 0)), pl.BlockSpec((tm, B_WIDTH), lambda j, i: (i, 0)),
                  pl.BlockSpec((tm, tn), lambda j, i: (i, ga0 + j)), pl.BlockSpec((tm, tn), lambda j, i: (i, gb0 + j)),
                  pl.BlockSpec((A_WIDTH, tn), lambda j, i: (0, j)), pl.BlockSpec((B_WIDTH, tn), lambda j, i: (0, j))],
        out_specs=pl.BlockSpec((tm, tn), lambda j, i: (i, j)),
        out_shape=jax.ShapeDtypeStruct((m, D_MODEL), BF16),
        scratch_shapes=[pltpu.VMEM((A_WIDTH, tn), BF16), pltpu.VMEM((B_WIDTH, tn), BF16)],
        compiler_params=_params("arbitrary", "arbitrary"),
        name="branch_merge",
    )(oa, ob, proj, proj, wa, wb)


def _router_body(h_ref, g_ref, rw_ref, rb_ref, hn_ref, eid_ref, wk_ref):
    hn = _rmsnorm_rows(h_ref[...], g_ref[...])
    hn_ref[...] = _pack_bf16_pairs(hn)
    logits = jnp.dot(hn, rw_ref[...], precision=lax.Precision.HIGHEST, preferred_element_type=F32) + rb_ref[...]
    lane = lax.broadcasted_iota(jnp.int32, logits.shape, 1)
    neg = jnp.float32(-jnp.inf)
    big = jnp.int32(LANES)

    is_g = lane < N_GROUPS
    lg = jnp.where(is_g, logits, neg)
    mg = jnp.max(lg, axis=-1, keepdims=True)
    g_idx = jnp.min(jnp.where(lg == mg, lane, big), axis=-1, keepdims=True)
    pg_top = 1.0 / jnp.sum(jnp.where(is_g, jnp.exp(lg - mg), 0.0), axis=-1, keepdims=True)

    lo = N_GROUPS + g_idx * EXPERTS_PER_GROUP
    in_grp = jnp.logical_and(lane >= lo, lane < lo + EXPERTS_PER_GROUP)
    le = jnp.where(in_grp, logits, neg)
    v1 = jnp.max(le, axis=-1, keepdims=True)
    i1 = jnp.min(jnp.where(le == v1, lane, big), axis=-1, keepdims=True)
    le2 = jnp.where(lane == i1, neg, le)
    v2 = jnp.max(le2, axis=-1, keepdims=True)
    i2 = jnp.min(jnp.where(le2 == v2, lane, big), axis=-1, keepdims=True)
    e2 = jnp.exp(v2 - v1)
    w1 = pg_top / (1.0 + e2)
    w2 = pg_top * e2 / (1.0 + e2)
    eid_ref[...] = jnp.where(lane == 0, i1 - N_GROUPS, jnp.where(lane == 1, i2 - N_GROUPS, 0))
    wk_ref[...] = jnp.where(lane == 0, w1, jnp.where(lane == 1, w2, 0.0))


def _router(h, g, rw, rb, tm=256):
    m, d = h.shape
    row = lambda w: pl.BlockSpec((tm, w), lambda i: (i, 0))
    return pl.pallas_call(
        _router_body,
        grid=(m // tm,),
        in_specs=[row(d), pl.BlockSpec((1, d), lambda i: (0, 0)), pl.BlockSpec((d, LANES), lambda i: (0, 0)),
                  pl.BlockSpec((1, LANES), lambda i: (0, 0))],
        out_specs=[row(d // 2), row(LANES), row(LANES)],
        out_shape=[jax.ShapeDtypeStruct((m, d // 2), jnp.uint32), jax.ShapeDtypeStruct((m, LANES), jnp.int32),
                   jax.ShapeDtypeStruct((m, LANES), F32)],
        compiler_params=_params("arbitrary"),
        name="router",
    )(h, g.reshape(1, d).astype(F32), rw, rb)


MOE_TM = 256


def _moe_body(tile_e_ref, tile_blk_ref, n_used_ref,
              src0_ref, src_next_ref, dst_prev_ref, hn_ref, w1_ref, w3_ref, w2_ref, y_ref,
              w1_s, w3_s, w2_s, x0, x1, o0, o1, gsem, ssem):
    j = pl.program_id(0)
    n_used = n_used_ref[0]
    xs = (x0, x1)
    os_ = (o0, o1)
    n_real = y_ref.shape[0] - 2 * MOE_TM

    def gather_issue(idx_ref, b):
        for r in range(MOE_TM):
            pltpu.make_async_copy(hn_ref.at[idx_ref[0, 0, r]], xs[b].at[r], gsem.at[b]).start()

    def gather_wait(b):
        pltpu.make_async_copy(hn_ref.at[pl.ds(0, MOE_TM)], xs[b], gsem.at[b]).wait()

    def scatter_issue(idx_ref, b):
        for r in range(MOE_TM):
            pltpu.make_async_copy(os_[b].at[r], y_ref.at[idx_ref[0, 0, r]], ssem.at[b]).start()

    def scatter_wait(b):
        pltpu.make_async_copy(os_[b], y_ref.at[pl.ds(0, MOE_TM)], ssem.at[b]).wait()

    @pl.when(j == 0)
    def _():
        o1[...] = jnp.zeros_like(o1)
        fill = pltpu.make_async_copy(o1, y_ref.at[pl.ds(n_real, MOE_TM)], ssem.at[1])
        fill.start()
        fill.wait()
        gather_issue(src0_ref, 0)

    for b in (0, 1):
        @pl.when(jnp.logical_and(j < n_used, j % 2 == b))
        def _(b=b):
            gather_wait(b)

            @pl.when(j >= 1)
            def _():
                scatter_wait(b)

            prev_e = tile_e_ref[jnp.maximum(j - 1, 0)]

            @pl.when(jnp.logical_or(j == 0, tile_e_ref[j] != prev_e))
            def _():
                w1_s[...] = w1_ref[0].astype(BF16)
                w3_s[...] = w3_ref[0].astype(BF16)
                w2_s[...] = w2_ref[0].astype(BF16)

            gather_issue(src_next_ref, 1 - b)
            scatter_issue(dst_prev_ref, 1 - b)
            x = _unpack_bf16_pairs(xs[b][...]).astype(BF16)
            hid = (jax.nn.silu(jnp.dot(x, w1_s[...], preferred_element_type=F32))
                   * jnp.dot(x, w3_s[...], preferred_element_type=F32))
            os_[b][...] = _pack_bf16_pairs(jnp.dot(hid.astype(BF16), w2_s[...], preferred_element_type=F32))

    for b in (0, 1):
        @pl.when(jnp.logical_and(j == n_used, j % 2 == b))
        def _(b=b):
            gather_wait(b)
            scatter_wait(b)
            scatter_issue(dst_prev_ref, 1 - b)
            scatter_wait(1 - b)


def _moe_experts(hn, w1, w3, w2, tile_e, tile_blk, n_used, src, dst, n_tokens):
    dp = hn.shape[1]
    d = 2 * dp
    n_tiles = tile_e.shape[0]
    ff = w1.shape[-1]
    idx_block = lambda fn: pl.BlockSpec((1, 1, MOE_TM), fn, memory_space=pltpu.SMEM)
    grid_spec = pltpu.PrefetchScalarGridSpec(
        num_scalar_prefetch=3,
        grid=(n_tiles,),
        in_specs=[
            idx_block(lambda j, te, tb, nu: (0, 0, 0)),
            idx_block(lambda j, te, tb, nu: (tb[jnp.minimum(j + 1, n_tiles - 1)], 0, 0)),
            idx_block(lambda j, te, tb, nu: (jnp.where(j == 0, n_tiles, tb[jnp.maximum(j - 1, 0)]), 0, 0)),
            pl.BlockSpec(memory_space=pl.ANY),
            pl.BlockSpec((1, d, ff), lambda j, te, tb, nu: (te[j], 0, 0)),
            pl.BlockSpec((1, d, ff), lambda j, te, tb, nu: (te[j], 0, 0)),
            pl.BlockSpec((1, ff, d), lambda j, te, tb, nu: (te[j], 0, 0)),
        ],
        out_specs=pl.BlockSpec(memory_space=pl.ANY),
        scratch_shapes=[pltpu.VMEM((d, ff), BF16), pltpu.VMEM((d, ff), BF16), pltpu.VMEM((ff, d), BF16),
                        pltpu.VMEM((MOE_TM, dp), jnp.uint32), pltpu.VMEM((MOE_TM, dp), jnp.uint32),
                        pltpu.VMEM((MOE_TM, dp), jnp.uint32), pltpu.VMEM((MOE_TM, dp), jnp.uint32),
                        pltpu.SemaphoreType.DMA((2,)), pltpu.SemaphoreType.DMA((2,))],
    )
    src3 = src.reshape(n_tiles, 1, MOE_TM)
    return pl.pallas_call(
        _moe_body,
        grid_spec=grid_spec,
        out_shape=jax.ShapeDtypeStruct((2 * n_tokens + 2 * MOE_TM, dp), jnp.uint32),
        compiler_params=_params("arbitrary"),
        name="moe_experts",
    )(tile_e, tile_blk, n_used, src3, src3, dst.reshape(n_tiles + 1, 1, MOE_TM), hn, w1, w3, w2)


def _combine_body(h_ref, y0_ref, y1_ref, wk_ref, g_ref, o_ref):
    wk = wk_ref[...]
    y = (h_ref[...] + wk[:, 0:1] * _unpack_bf16_pairs(y0_ref[...])
         + wk[:, 1:2] * _unpack_bf16_pairs(y1_ref[...]))
    o_ref[...] = _rmsnorm_rows(y, g_ref[...]).astype(o_ref.dtype)


def _combine(h, y2, wk, g, out_dtype, tm=256):
    m, d = h.shape
    nblk = m // tm
    return pl.pallas_call(
        _combine_body,
        grid=(nblk,),
        in_specs=[pl.BlockSpec((tm, d), lambda i: (i, 0)), pl.BlockSpec((tm, d // 2), lambda i: (i, 0)),
                  pl.BlockSpec((tm, d // 2), lambda i: (nblk + i, 0)), pl.BlockSpec((tm, LANES), lambda i: (i, 0)),
                  pl.BlockSpec((1, d), lambda i: (0, 0))],
        out_specs=pl.BlockSpec((tm, d), lambda i: (i, 0)),
        out_shape=jax.ShapeDtypeStruct((m, d), out_dtype),
        compiler_params=_params("arbitrary"),
        name="combine_final_norm",
    )(h, y2, y2, wk, g.reshape(1, d).astype(F32))


def _routing_tables(eid, n_tokens):
    n_pairs = 2 * n_tokens
    n_tiles = n_pairs // MOE_TM + N_EXPERTS + 1
    e_flat = eid.reshape(n_pairs)
    order = jnp.argsort(e_flat, stable=True).astype(jnp.int32)
    counts = jnp.sum(e_flat[:, None] == jnp.arange(N_EXPERTS, dtype=jnp.int32)[None, :], axis=0, dtype=jnp.int32)
    tiles_per_e = (counts + MOE_TM - 1) // MOE_TM
    tile_end = jnp.cumsum(tiles_per_e)
    tile_start = tile_end - tiles_per_e
    n_used = tile_end[-1]
    sorted_start = jnp.cumsum(counts) - counts
    j = jnp.arange(n_tiles, dtype=jnp.int32)
    tile_blk = jnp.minimum(j, n_used - 1)
    tile_e = jnp.searchsorted(tile_end, tile_blk, side="right").astype(jnp.int32)
    tile_row0 = (tile_blk - tile_start[tile_e]) * MOE_TM
    tile_rows = jnp.where(j == tile_blk, jnp.clip(counts[tile_e] - tile_row0, 0, MOE_TM), 0).astype(jnp.int32)
    r = jnp.arange(MOE_TM, dtype=jnp.int32)[None, :]
    valid = r < tile_rows[:, None]
    pair = order[jnp.clip((sorted_start[tile_e] + tile_row0)[:, None] + r, 0, n_pairs - 1)]
    src_tok = jnp.where(valid, pair // 2, 0).astype(jnp.int32)
    spare = n_pairs + (j % 2)[:, None] * MOE_TM + r
    dst_row = jnp.where(valid, (pair % 2) * n_tokens + pair // 2, spare).astype(jnp.int32)
    dst_row = jnp.concatenate([dst_row, n_pairs + MOE_TM + r], axis=0)
    return (tile_e, tile_blk.astype(jnp.int32), n_used.reshape(1).astype(jnp.int32),
            src_tok.reshape(-1), dst_row.reshape(-1))


def kernel(x, mix_norm_g, w_in, hgrn_lb_logits, hgrn_norm_g, w_branch_a, w_branch_b, w_out, ffn_norm_g,
           router_w_group, router_b_group, router_w_expert, router_b_expert, expert_w1, expert_w3, expert_w2,
           final_norm_g):
    b, seq, d = x.shape
    assert b == 1 and d == D_MODEL and seq % ATT_SUPER == 0 and w_in.shape == (1, D_MODEL, IN_WIDTH)
    h0 = x.reshape(seq, d).astype(F32)

    xn = _rmsnorm(h0, mix_norm_g[0], BF16)
    proj = _matmul(xn, w_in[0], F32, tm=512, tn=1280, name="in_proj")
    oa = _dilated_attention(proj, seq)
    ob = _hgrn2(proj, hgrn_lb_logits, hgrn_norm_g[0], seq)
    merged = _branch_merge(oa, ob, proj, w_branch_a[0], w_branch_b[0])
    h1 = _matmul_residual(merged, w_out[0], h0, tm=512, tn=1024, name="out_proj")

    pad = LANES - N_GROUPS - N_EXPERTS
    rw = jnp.concatenate([router_w_group[0], router_w_expert[0], jnp.zeros((d, pad), F32)], axis=1).astype(F32)
    rb = jnp.concatenate([router_b_group[0], router_b_expert[0], jnp.zeros((pad,), F32)]).reshape(1, LANES).astype(F32)
    hn, eid, wk = _router(h1, ffn_norm_g[0], rw, rb)
    tile_e, tile_blk, n_used, src_tok, dst_row = _routing_tables(eid[:, :2], seq)
    w1 = expert_w1[0].reshape(N_EXPERTS, D_MODEL, EXPERT_FF)
    w3 = expert_w3[0].reshape(N_EXPERTS, D_MODEL, EXPERT_FF)
    w2 = expert_w2[0].reshape(N_EXPERTS, EXPERT_FF, D_MODEL)
    y2 = _moe_experts(hn, w1, w3, w2, tile_e, tile_blk, n_used, src_tok, dst_row, seq)

    out = _combine(h1, y2, wk, final_norm_g, x.dtype)
    return out.reshape(b, seq, d)
```

```python
import functools

import jax
import jax.numpy as jnp
from jax import lax
from jax.experimental import pallas as pl
from jax.experimental.pallas import tpu as pltpu

F32 = jnp.float32
BF16 = jnp.bfloat16

D_MODEL = 2048
A_HEADS = 12
A_HEAD_DIM = 128
A_WIDTH = A_HEADS * A_HEAD_DIM
A_SCALE = A_HEAD_DIM ** -0.5
DILATED_CONFIGS = ((128, 1), (512, 4), (2048, 16))
B_HEADS = 8
B_KEY_DIM = 128
B_WIDTH = B_HEADS * B_KEY_DIM
N_GROUPS = 4
EXPERTS_PER_GROUP = 8
N_EXPERTS = N_GROUPS * EXPERTS_PER_GROUP
EXPERT_FF = 512
NORM_EPS = 1e-6

LANES = 128
SUBLANES = 8
VMEM_LIMIT = 56 * 1024 * 1024

_QA_BLK = 0
_KA_BLK = A_HEADS
_VA_BLK = 2 * A_HEADS
_QB_BLK = 3 * A_HEADS
_FB_BLK = _QB_BLK + B_HEADS
_IB_BLK = _FB_BLK + B_HEADS
_GB_BLK = _IB_BLK + B_HEADS
_GATE_A_COL = 3 * A_WIDTH + 4 * B_WIDTH
_GATE_B_COL = _GATE_A_COL + D_MODEL
IN_WIDTH = _GATE_B_COL + D_MODEL


def _params(*sem):
    return pltpu.CompilerParams(dimension_semantics=sem, vmem_limit_bytes=VMEM_LIMIT)


def _rmsnorm_rows(x, g):
    ms = jnp.mean(x * x, axis=-1, keepdims=True)
    return x * lax.rsqrt(ms + NORM_EPS) * g


def _pack_bf16_pairs(x):
    w = x.shape[1] // 2
    lo = pltpu.bitcast(x[:, :w].astype(BF16).astype(F32), jnp.uint32)
    hi = pltpu.bitcast(x[:, w:].astype(BF16).astype(F32), jnp.uint32)
    return jnp.bitwise_or(jnp.bitwise_and(hi, jnp.uint32(0xFFFF0000)), jnp.right_shift(lo, jnp.uint32(16)))


def _unpack_bf16_pairs(u):
    lo = pltpu.bitcast(jnp.left_shift(u, jnp.uint32(16)), F32)
    hi = pltpu.bitcast(jnp.bitwise_and(u, jnp.uint32(0xFFFF0000)), F32)
    return jnp.concatenate([lo, hi], axis=1)


def _rmsnorm_body(x_ref, g_ref, o_ref):
    o_ref[...] = _rmsnorm_rows(x_ref[...].astype(F32), g_ref[...]).astype(o_ref.dtype)


def _rmsnorm(x, g, out_dtype, tm=512):
    m, d = x.shape
    return pl.pallas_call(
        _rmsnorm_body,
        grid=(m // tm,),
        in_specs=[pl.BlockSpec((tm, d), lambda i: (i, 0)), pl.BlockSpec((1, d), lambda i: (0, 0))],
        out_specs=pl.BlockSpec((tm, d), lambda i: (i, 0)),
        out_shape=jax.ShapeDtypeStruct((m, d), out_dtype),
        compiler_params=_params("arbitrary"),
        name="rmsnorm",
    )(x, g.reshape(1, d).astype(F32))


def _matmul_body(a_ref, w_ref, o_ref, wb_ref):
    @pl.when(pl.program_id(1) == 0)
    def _():
        wb_ref[...] = w_ref[...].astype(BF16)

    o_ref[...] = jnp.dot(a_ref[...], wb_ref[...], preferred_element_type=F32).astype(o_ref.dtype)


def _matmul(a, w, out_dtype, tm, tn, name):
    m, k = a.shape
    n = w.shape[1]
    return pl.pallas_call(
        _matmul_body,
        grid=(n // tn, m // tm),
        in_specs=[pl.BlockSpec((tm, k), lambda j, i: (i, 0)), pl.BlockSpec((k, tn), lambda j, i: (0, j))],
        out_specs=pl.BlockSpec((tm, tn), lambda j, i: (i, j)),
        out_shape=jax.ShapeDtypeStruct((m, n), out_dtype),
        scratch_shapes=[pltpu.VMEM((k, tn), BF16)],
        compiler_params=_params("arbitrary", "arbitrary"),
        name=name,
    )(a, w)


def _matmul_residual_body(a_ref, w_ref, r_ref, o_ref, wb_ref):
    @pl.when(pl.program_id(1) == 0)
    def _():
        wb_ref[...] = w_ref[...].astype(BF16)

    o_ref[...] = r_ref[...] + jnp.dot(a_ref[...], wb_ref[...], preferred_element_type=F32)


def _matmul_residual(a, w, res, tm, tn, name):
    m, k = a.shape
    n = w.shape[1]
    return pl.pallas_call(
        _matmul_residual_body,
        grid=(n // tn, m // tm),
        in_specs=[pl.BlockSpec((tm, k), lambda j, i: (i, 0)), pl.BlockSpec((k, tn), lambda j, i: (0, j)),
                  pl.BlockSpec((tm, tn), lambda j, i: (i, j))],
        out_specs=pl.BlockSpec((tm, tn), lambda j, i: (i, j)),
        out_shape=jax.ShapeDtypeStruct((m, n), F32),
        scratch_shapes=[pltpu.VMEM((k, tn), BF16)],
        compiler_params=_params("arbitrary", "arbitrary"),
        name=name,
    )(a, w, res)


ATT_BLK = 128
ATT_SUPER = 2048
ATT_UNROLL = 8


def _attn_body(q_ref, k_ref, v_ref, o_ref, o_scr, lse_scr, bias_scr):
    sb = pl.program_id(1)
    row0 = sb * ATT_SUPER
    diff = (lax.broadcasted_iota(jnp.int32, (ATT_BLK, 2 * ATT_BLK), 1)
            - lax.broadcasted_iota(jnp.int32, (ATT_BLK, 2 * ATT_BLK), 0))
    neg = jnp.float32(-jnp.inf)
    bias_scr[0] = jnp.where(jnp.logical_and(diff >= 0, diff <= ATT_BLK), 0.0, neg)
    bias_scr[1] = jnp.where(diff <= 0, 0.0, neg)

    for c, (window, dil) in enumerate(DILATED_CONFIGS):
        span = ATT_BLK * dil
        tiles_per_res = ATT_SUPER // span

        def tile(t, carry, c=c, dil=dil, span=span, tiles_per_res=tiles_per_res):
            n = t % tiles_per_res
            r = t // tiles_per_res
            q_start = n * span + r
            first = jnp.logical_and(sb == 0, n == 0)
            kv_start = row0 + q_start - jnp.where(first, 0, span)
            if dil == 1:
                q_start = pl.multiple_of(q_start, ATT_BLK)
                kv_start = pl.multiple_of(kv_start, ATT_BLK)
                qs = pl.ds(q_start, ATT_BLK)
                ks = pl.ds(kv_start, 2 * ATT_BLK)
            else:
                qs = pl.ds(q_start, ATT_BLK, stride=dil)
                ks = pl.ds(kv_start, 2 * ATT_BLK, stride=dil)
            q = (q_ref[qs, :] * A_SCALE).astype(BF16)
            k = k_ref[ks, :].astype(BF16)
            v = v_ref[ks, :].astype(BF16)
            s = lax.dot_general(q, k, (((1,), (1,)), ((), ())), preferred_element_type=F32)
            s = s + bias_scr[first.astype(jnp.int32)]
            mx = jnp.max(s, axis=-1, keepdims=True)
            p = jnp.exp(s - mx)
            den = jnp.sum(p, axis=-1, keepdims=True)
            acc = jnp.dot(p.astype(BF16), v, preferred_element_type=F32)
            o_scr[c, qs, :] = acc / den
            lse_scr[c, qs, :] = jnp.broadcast_to(mx + jnp.log(den), (ATT_BLK, A_HEAD_DIM))
            return carry

        def tiles(i, carry, tile=tile):
            for u in range(ATT_UNROLL):
                tile(i * ATT_UNROLL + u, carry)
            return carry

        lax.fori_loop(0, ATT_SUPER // ATT_BLK // ATT_UNROLL, tiles, 0)

    rows = 256

    def merge(i, carry):
        sl = pl.ds(pl.multiple_of(i * rows, rows), rows)
        l0, l1, l2 = lse_scr[0, sl, :], lse_scr[1, sl, :], lse_scr[2, sl, :]
        m = jnp.maximum(jnp.maximum(l0, l1), l2)
        w0, w1, w2 = jnp.exp(l0 - m), jnp.exp(l1 - m), jnp.exp(l2 - m)
        num = w0 * o_scr[0, sl, :] + w1 * o_scr[1, sl, :] + w2 * o_scr[2, sl, :]
        o_ref[sl, :] = (num / (w0 + w1 + w2)).astype(o_ref.dtype)
        return carry

    lax.fori_loop(0, ATT_SUPER // rows, merge, 0)


def _dilated_attention(proj, seq):
    n_super = seq // ATT_SUPER
    blk = lambda off: pl.BlockSpec((seq, A_HEAD_DIM), lambda h, s: (0, off + h))
    return pl.pallas_call(
        _attn_body,
        grid=(A_HEADS, n_super),
        in_specs=[pl.BlockSpec((ATT_SUPER, A_HEAD_DIM), lambda h, s: (s, _QA_BLK + h)), blk(_KA_BLK), blk(_VA_BLK)],
        out_specs=pl.BlockSpec((ATT_SUPER, A_HEAD_DIM), lambda h, s: (s, h)),
        out_shape=jax.ShapeDtypeStruct((seq, A_WIDTH), BF16),
        scratch_shapes=[pltpu.VMEM((len(DILATED_CONFIGS), ATT_SUPER, A_HEAD_DIM), F32),
                        pltpu.VMEM((len(DILATED_CONFIGS), ATT_SUPER, A_HEAD_DIM), F32),
                        pltpu.VMEM((2, ATT_BLK, 2 * ATT_BLK), F32)],
        compiler_params=_params("arbitrary", "arbitrary"),
        name="dilated_attention",
    )(proj, proj, proj)


HG_CHUNK = 64
HG_TB = 512
HG_MIN_LEVEL = 4
HG_HEADS_PER_STEP = 2
HG_CHUNKS_PER_ITER = 4


def _hgrn_body(qb_ref, fb_ref, ib_ref, gb_ref, lbl_ref, g_ref, o_ref, st_ref):
    @pl.when(pl.program_id(1) == 0)
    def _():
        st_ref[...] = jnp.zeros_like(st_ref)

    c_ = HG_CHUNK
    lbl = lbl_ref[...]
    e = jnp.exp(lbl - jnp.max(lbl, axis=0, keepdims=True))
    lb_all = e[0:1, :] / jnp.sum(e, axis=0, keepdims=True)
    g_all = g_ref[...]

    ti = lax.broadcasted_iota(jnp.int32, (c_, c_), 0)
    si = lax.broadcasted_iota(jnp.int32, (c_, c_), 1)
    tri = jnp.where(si <= ti, 1.0, 0.0).astype(BF16)
    xor = jnp.bitwise_xor(ti, si)
    row = lax.broadcasted_iota(jnp.int32, (c_, B_KEY_DIM), 0)

    def chunk(ci, hp):
        sl = pl.ds(pl.multiple_of(ci * c_, c_), c_)
        ln = slice(hp * B_KEY_DIM, (hp + 1) * B_KEY_DIM)
        lb = lb_all[:, ln]
        f = lb + (1.0 - lb) * jax.nn.sigmoid(fb_ref[sl, ln])
        logf = jnp.log(f)
        kk = 1.0 - f
        q = jax.nn.silu(qb_ref[sl, ln])
        v = ib_ref[sl, ln]
        vb = v.astype(BF16)
        hi = logf.astype(BF16)
        rem = logf - hi.astype(F32)
        mid = rem.astype(BF16)
        low = (rem - mid.astype(F32)).astype(BF16)
        parts = jnp.dot(tri, jnp.concatenate([hi, mid, low], axis=1), preferred_element_type=F32)
        bc = parts[:, :B_KEY_DIM] + (parts[:, B_KEY_DIM:2 * B_KEY_DIM] + parts[:, 2 * B_KEY_DIM:])

        attn = jnp.zeros((c_, c_), F32)
        half = c_ // 2
        while half >= HG_MIN_LEVEL:
            blk = 2 * half
            ref_rows = [jnp.broadcast_to(bc[b0 + half - 1:b0 + half, :], (blk, B_KEY_DIM)) for b0 in range(0, c_, blk)]
            ref = jnp.concatenate(ref_rows, axis=0) if len(ref_rows) > 1 else ref_rows[0]
            dec = jnp.exp(-jnp.abs(bc - ref))
            s = lax.dot_general((q * dec).astype(BF16), (kk * dec).astype(BF16), (((1,), (1,)), ((), ())),
                                preferred_element_type=F32)
            keep = jnp.logical_and(jnp.logical_and(xor >= half, xor < blk), ti > si)
            attn = attn + jnp.where(keep, s, 0.0)
            half //= 2
        o = jnp.dot(attn.astype(BF16), vb, preferred_element_type=F32)

        for delta in range(HG_MIN_LEVEL):
            if delta == 0:
                w = q * kk
                vd = v
            else:
                valid = jnp.bitwise_and(row, HG_MIN_LEVEL - 1) >= delta
                rel = jnp.where(valid, bc - pltpu.roll(bc, delta, 0), 0.0)
                w = jnp.where(valid, q * pltpu.roll(kk, delta, 0) * jnp.exp(rel), 0.0)
                vd = pltpu.roll(v, delta, 0)
            o = o + jnp.sum(w, axis=-1, keepdims=True) * vd

        st = st_ref[hp]
        o = o + lax.dot_general((q * jnp.exp(bc)).astype(BF16), st.astype(BF16), (((1,), (1,)), ((), ())),
                                preferred_element_type=F32)
        b_last = bc[c_ - 1:c_, :]
        kdec = (kk * jnp.exp(b_last - bc)).astype(BF16)
        upd = lax.dot_general(vb, kdec, (((0,), (0,)), ((), ())), preferred_element_type=F32)
        st_ref[hp] = st * jnp.exp(b_last) + upd

        o = o * lax.rsqrt(jnp.mean(o * o, axis=-1, keepdims=True) + NORM_EPS)
        o_ref[sl, ln] = (o * g_all[:, ln] * jax.nn.silu(gb_ref[sl, ln])).astype(o_ref.dtype)

    def chunks(i, carry):
        for u in range(HG_CHUNKS_PER_ITER):
            for hp in range(HG_HEADS_PER_STEP):
                chunk(i * HG_CHUNKS_PER_ITER + u, hp)
        return carry

    lax.fori_loop(0, HG_TB // c_ // HG_CHUNKS_PER_ITER, chunks, 0)


def _hgrn2(proj, lb_logits, norm_g, seq):
    hp = HG_HEADS_PER_STEP
    width = hp * B_KEY_DIM
    col = lambda off: pl.BlockSpec((HG_TB, width), lambda h, t: (t, off // hp + h))
    n_lb = lb_logits.shape[0]
    return pl.pallas_call(
        _hgrn_body,
        grid=(B_HEADS // hp, seq // HG_TB),
        in_specs=[col(_QB_BLK), col(_FB_BLK), col(_IB_BLK), col(_GB_BLK),
                  pl.BlockSpec((n_lb, width), lambda h, t: (0, h)),
                  pl.BlockSpec((1, width), lambda h, t: (0, h))],
        out_specs=pl.BlockSpec((HG_TB, width), lambda h, t: (t, h)),
        out_shape=jax.ShapeDtypeStruct((seq, B_WIDTH), BF16),
        scratch_shapes=[pltpu.VMEM((hp, B_KEY_DIM, B_KEY_DIM), F32)],
        compiler_params=_params("arbitrary", "arbitrary"),
        name="hgrn2",
    )(proj, proj, proj, proj, lb_logits.astype(F32), norm_g.reshape(1, B_WIDTH).astype(F32))


MERGE_GATE_BLK = 512


def _merge_body(oa_ref, ob_ref, *rest):
    n_g = (len(rest) - 5) // 2
    ga_refs, gb_refs = rest[:n_g], rest[n_g:2 * n_g]
    wa_ref, wb_ref, o_ref, wa_s, wb_s = rest[2 * n_g:]

    @pl.when(pl.program_id(1) == 0)
    def _():
        wa_s[...] = wa_ref[...].astype(BF16)
        wb_s[...] = wb_ref[...].astype(BF16)

    ya = jnp.dot(oa_ref[...], wa_s[...], preferred_element_type=F32)
    yb = jnp.dot(ob_ref[...], wb_s[...], preferred_element_type=F32)
    ga = jnp.concatenate([r[...] for r in ga_refs], axis=1)
    gb = jnp.concatenate([r[...] for r in gb_refs], axis=1)
    o_ref[...] = (jax.nn.sigmoid(ga) * ya + jax.nn.sigmoid(gb) * yb).astype(o_ref.dtype)


def _branch_merge(oa, ob, proj, wa, wb, tm=512, tn=1024):
    m = oa.shape[0]
    gw = MERGE_GATE_BLK
    n_g = tn // gw
    ga0, gb0 = _GATE_A_COL // gw, _GATE_B_COL // gw
    gate = lambda off, u: pl.BlockSpec((tm, gw), lambda j, i: (i, off + j * n_g + u))
    return pl.pallas_call(
        _merge_body,
        grid=(D_MODEL // tn, m // tm),
        in_specs=[pl.BlockSpec((tm, A_WIDTH), lambda j, i: (i, 0)), pl.BlockSpec((tm, B_WIDTH), lambda j, i: (i, 0)),
                  *[gate(ga0, u) for u in range(n_g)], *[gate(gb0, u) for u in range(n_g)],
                  pl.BlockSpec((A_WIDTH, tn), lambda j, i: (0, j)), pl.BlockSpec((B_WIDTH, tn), lambda j, i: (0, j))],
        out_specs=pl.BlockSpec((tm, tn), lambda j, i: (i, j)),
        out_shape=jax.ShapeDtypeStruct((m, D_MODEL), BF16),
        scratch_shapes=[pltpu.VMEM((A_WIDTH, tn), BF16), pltpu.VMEM((B_WIDTH, tn), BF16)],
        compiler_params=_params("arbitrary", "arbitrary"),
        name="branch_merge",
    )(oa, ob, *([proj] * (2 * n_g)), wa, wb)


def _router_body(h_ref, g_ref, rw_ref, rb_ref, hn_ref, eid_ref, wk_ref):
    hn = _rmsnorm_rows(h_ref[...], g_ref[...])
    hn_ref[...] = _pack_bf16_pairs(hn)
    rw = rw_ref[...]
    hn_hi = hn.astype(BF16)
    hn_lo = (hn - hn_hi.astype(F32)).astype(BF16)
    rw_hi = rw.astype(BF16)
    rw_lo = (rw - rw_hi.astype(F32)).astype(BF16)
    logits = (jnp.dot(hn_hi, rw_hi, preferred_element_type=F32)
              + (jnp.dot(hn_lo, rw_hi, preferred_element_type=F32)
                 + jnp.dot(hn_hi, rw_lo, preferred_element_type=F32))) + rb_ref[...]
    lane = lax.broadcasted_iota(jnp.int32, logits.shape, 1)
    neg = jnp.float32(-jnp.inf)
    big = jnp.int32(LANES)

    is_g = lane < N_GROUPS
    lg = jnp.where(is_g, logits, neg)
    mg = jnp.max(lg, axis=-1, keepdims=True)
    g_idx = jnp.min(jnp.where(lg == mg, lane, big), axis=-1, keepdims=True)
    pg_top = 1.0 / jnp.sum(jnp.where(is_g, jnp.exp(lg - mg), 0.0), axis=-1, keepdims=True)

    lo = N_GROUPS + g_idx * EXPERTS_PER_GROUP
    in_grp = jnp.logical_and(lane >= lo, lane < lo + EXPERTS_PER_GROUP)
    le = jnp.where(in_grp, logits, neg)
    v1 = jnp.max(le, axis=-1, keepdims=True)
    i1 = jnp.min(jnp.where(le == v1, lane, big), axis=-1, keepdims=True)
    le2 = jnp.where(lane == i1, neg, le)
    v2 = jnp.max(le2, axis=-1, keepdims=True)
    i2 = jnp.min(jnp.where(le2 == v2, lane, big), axis=-1, keepdims=True)
    e2 = jnp.exp(v2 - v1)
    w1 = pg_top / (1.0 + e2)
    w2 = pg_top * e2 / (1.0 + e2)
    eid_ref[...] = jnp.where(lane == 0, i1 - N_GROUPS, jnp.where(lane == 1, i2 - N_GROUPS, 0))
    wk_ref[...] = jnp.where(lane == 0, w1, jnp.where(lane == 1, w2, 0.0))


def _router(h, g, rw, rb, tm=256):
    m, d = h.shape
    row = lambda w: pl.BlockSpec((tm, w), lambda i: (i, 0))
    return pl.pallas_call(
        _router_body,
        grid=(m // tm,),
        in_specs=[row(d), pl.BlockSpec((1, d), lambda i: (0, 0)), pl.BlockSpec((d, LANES), lambda i: (0, 0)),
                  pl.BlockSpec((1, LANES), lambda i: (0, 0))],
        out_specs=[row(d // 2), row(LANES), row(LANES)],
        out_shape=[jax.ShapeDtypeStruct((m, d // 2), jnp.uint32), jax.ShapeDtypeStruct((m, LANES), jnp.int32),
                   jax.ShapeDtypeStruct((m, LANES), F32)],
        compiler_params=_params("arbitrary"),
        name="router",
    )(h, g.reshape(1, d).astype(F32), rw, rb)


MOE_TM = 256
ROW_DMA_PRIORITY = 1


def _moe_body(tile_e_ref, tile_blk_ref, n_used_ref,
              src0_ref, src_next_ref, dst_prev_ref, hn_ref, w1_ref, w3_ref, w2_ref, y_ref,
              w1_s, w3_s, w2_s, x0, x1, o0, o1, gsem, ssem):
    j = pl.program_id(0)
    n_used = n_used_ref[0]
    xs = (x0, x1)
    os_ = (o0, o1)
    n_real = y_ref.shape[0] - 2 * MOE_TM

    def gather_issue(idx_ref, b):
        for r in range(MOE_TM):
            pltpu.make_async_copy(hn_ref.at[idx_ref[0, 0, r]], xs[b].at[r], gsem.at[b]).start(priority=ROW_DMA_PRIORITY)

    def gather_wait(b):
        pltpu.make_async_copy(hn_ref.at[pl.ds(0, MOE_TM)], xs[b], gsem.at[b]).wait()

    def scatter_issue(idx_ref, b):
        for r in range(MOE_TM):
            pltpu.make_async_copy(os_[b].at[r], y_ref.at[idx_ref[0, 0, r]], ssem.at[b]).start(priority=ROW_DMA_PRIORITY)

    def scatter_wait(b):
        pltpu.make_async_copy(os_[b], y_ref.at[pl.ds(0, MOE_TM)], ssem.at[b]).wait()

    @pl.when(j == 0)
    def _():
        o1[...] = jnp.zeros_like(o1)
        fill = pltpu.make_async_copy(o1, y_ref.at[pl.ds(n_real, MOE_TM)], ssem.at[1])
        fill.start()
        fill.wait()
        gather_issue(src0_ref, 0)

    for b in (0, 1):
        @pl.when(jnp.logical_and(j < n_used, j % 2 == b))
        def _(b=b):
            gather_wait(b)

            @pl.when(j >= 1)
            def _():
                scatter_wait(b)

            prev_e = tile_e_ref[jnp.maximum(j - 1, 0)]

            @pl.when(jnp.logical_or(j == 0, tile_e_ref[j] != prev_e))
            def _():
                w1_s[...] = w1_ref[0].astype(BF16)
                w3_s[...] = w3_ref[0].astype(BF16)
                w2_s[...] = w2_ref[0].astype(BF16)

            gather_issue(src_next_ref, 1 - b)
            scatter_issue(dst_prev_ref, 1 - b)
            x = _unpack_bf16_pairs(xs[b][...]).astype(BF16)
            hid = (jax.nn.silu(jnp.dot(x, w1_s[...], preferred_element_type=F32))
                   * jnp.dot(x, w3_s[...], preferred_element_type=F32))
            os_[b][...] = _pack_bf16_pairs(jnp.dot(hid.astype(BF16), w2_s[...], preferred_element_type=F32))

    for b in (0, 1):
        @pl.when(jnp.logical_and(j == n_used, j % 2 == b))
        def _(b=b):
            gather_wait(b)
            scatter_wait(b)
            scatter_issue(dst_prev_ref, 1 - b)
            scatter_wait(1 - b)


def _moe_experts(hn, w1, w3, w2, tile_e, tile_blk, n_used, src, dst, n_tokens):
    dp = hn.shape[1]
    d = 2 * dp
    n_tiles = tile_e.shape[0]
    ff = w1.shape[-1]
    idx_block = lambda fn: pl.BlockSpec((1, 1, MOE_TM), fn, memory_space=pltpu.SMEM)
    grid_spec = pltpu.PrefetchScalarGridSpec(
        num_scalar_prefetch=3,
        grid=(n_tiles,),
        in_specs=[
            idx_block(lambda j, te, tb, nu: (0, 0, 0)),
            idx_block(lambda j, te, tb, nu: (tb[jnp.minimum(j + 1, n_tiles - 1)], 0, 0)),
            idx_block(lambda j, te, tb, nu: (jnp.where(j == 0, n_tiles, tb[jnp.maximum(j - 1, 0)]), 0, 0)),
            pl.BlockSpec(memory_space=pl.ANY),
            pl.BlockSpec((1, d, ff), lambda j, te, tb, nu: (te[j], 0, 0)),
            pl.BlockSpec((1, d, ff), lambda j, te, tb, nu: (te[j], 0, 0)),
            pl.BlockSpec((1, ff, d), lambda j, te, tb, nu: (te[j], 0, 0)),
        ],
        out_specs=pl.BlockSpec(memory_space=pl.ANY),
        scratch_shapes=[pltpu.VMEM((d, ff), BF16), pltpu.VMEM((d, ff), BF16), pltpu.VMEM((ff, d), BF16),
                        pltpu.VMEM((MOE_TM, dp), jnp.uint32), pltpu.VMEM((MOE_TM, dp), jnp.uint32),
                        pltpu.VMEM((MOE_TM, dp), jnp.uint32), pltpu.VMEM((MOE_TM, dp), jnp.uint32),
                        pltpu.SemaphoreType.DMA((2,)), pltpu.SemaphoreType.DMA((2,))],
    )
    src3 = src.reshape(n_tiles, 1, MOE_TM)
    return pl.pallas_call(
        _moe_body,
        grid_spec=grid_spec,
        out_shape=jax.ShapeDtypeStruct((2 * n_tokens + 2 * MOE_TM, dp), jnp.uint32),
        compiler_params=_params("arbitrary"),
        name="moe_experts",
    )(tile_e, tile_blk, n_used, src3, src3, dst.reshape(n_tiles + 1, 1, MOE_TM), hn, w1, w3, w2)


def _combine_body(h_ref, y0_ref, y1_ref, wk_ref, g_ref, o_ref):
    wk = wk_ref[...]
    y = (h_ref[...] + wk[:, 0:1] * _unpack_bf16_pairs(y0_ref[...])
         + wk[:, 1:2] * _unpack_bf16_pairs(y1_ref[...]))
    o_ref[...] = _rmsnorm_rows(y, g_ref[...]).astype(o_ref.dtype)


def _combine(h, y2, wk, g, out_dtype, tm=256):
    m, d = h.shape
    nblk = m // tm
    return pl.pallas_call(
        _combine_body,
        grid=(nblk,),
        in_specs=[pl.BlockSpec((tm, d), lambda i: (i, 0)), pl.BlockSpec((tm, d // 2), lambda i: (i, 0)),
                  pl.BlockSpec((tm, d // 2), lambda i: (nblk + i, 0)), pl.BlockSpec((tm, LANES), lambda i: (i, 0)),
                  pl.BlockSpec((1, d), lambda i: (0, 0))],
        out_specs=pl.BlockSpec((tm, d), lambda i: (i, 0)),
        out_shape=jax.ShapeDtypeStruct((m, d), out_dtype),
        compiler_params=_params("arbitrary"),
        name="combine_final_norm",
    )(h, y2, y2, wk, g.reshape(1, d).astype(F32))


def _routing_tables(eid, n_tokens):
    n_pairs = 2 * n_tokens
    n_tiles = n_pairs // MOE_TM + N_EXPERTS + 1
    e_flat = eid.reshape(n_pairs)
    order = jnp.argsort(e_flat, stable=True).astype(jnp.int32)
    counts = jnp.sum(e_flat[:, None] == jnp.arange(N_EXPERTS, dtype=jnp.int32)[None, :], axis=0, dtype=jnp.int32)
    tiles_per_e = (counts + MOE_TM - 1) // MOE_TM
    tile_end = jnp.cumsum(tiles_per_e)
    tile_start = tile_end - tiles_per_e
    n_used = tile_end[-1]
    sorted_start = jnp.cumsum(counts) - counts
    j = jnp.arange(n_tiles, dtype=jnp.int32)
    tile_blk = jnp.minimum(j, n_used - 1)
    tile_e = jnp.searchsorted(tile_end, tile_blk, side="right").astype(jnp.int32)
    tile_row0 = (tile_blk - tile_start[tile_e]) * MOE_TM
    tile_rows = jnp.where(j == tile_blk, jnp.clip(counts[tile_e] - tile_row0, 0, MOE_TM), 0).astype(jnp.int32)
    r = jnp.arange(MOE_TM, dtype=jnp.int32)[None, :]
    valid = r < tile_rows[:, None]
    pair = order[jnp.clip((sorted_start[tile_e] + tile_row0)[:, None] + r, 0, n_pairs - 1)]
    src_tok = jnp.where(valid, pair // 2, 0).astype(jnp.int32)
    spare = n_pairs + (j % 2)[:, None] * MOE_TM + r
    dst_row = jnp.where(valid, (pair % 2) * n_tokens + pair // 2, spare).astype(jnp.int32)
    dst_row = jnp.concatenate([dst_row, n_pairs + MOE_TM + r], axis=0)
    return (tile_e, tile_blk.astype(jnp.int32), n_used.reshape(1).astype(jnp.int32),
            src_tok.reshape(-1), dst_row.reshape(-1))


def kernel(x, mix_norm_g, w_in, hgrn_lb_logits, hgrn_norm_g, w_branch_a, w_branch_b, w_out, ffn_norm_g,
           router_w_group, router_b_group, router_w_expert, router_b_expert, expert_w1, expert_w3, expert_w2,
           final_norm_g):
    b, seq, d = x.shape
    assert b == 1 and d == D_MODEL and seq % ATT_SUPER == 0 and w_in.shape == (1, D_MODEL, IN_WIDTH)
    h0 = x.reshape(seq, d).astype(F32)

    xn = _rmsnorm(h0, mix_norm_g[0], BF16)
    proj = _matmul(xn, w_in[0], F32, tm=512, tn=1280, name="in_proj")
    oa = _dilated_attention(proj, seq)
    ob = _hgrn2(proj, hgrn_lb_logits, hgrn_norm_g[0], seq)
    merged = _branch_merge(oa, ob, proj, w_branch_a[0], w_branch_b[0])
    h1 = _matmul_residual(merged, w_out[0], h0, tm=512, tn=1024, name="out_proj")

    pad = LANES - N_GROUPS - N_EXPERTS
    rw = jnp.concatenate([router_w_group[0], router_w_expert[0], jnp.zeros((d, pad), F32)], axis=1).astype(F32)
    rb = jnp.concatenate([router_b_group[0], router_b_expert[0], jnp.zeros((pad,), F32)]).reshape(1, LANES).astype(F32)
    hn, eid, wk = _router(h1, ffn_norm_g[0], rw, rb)
    tile_e, tile_blk, n_used, src_tok, dst_row = _routing_tables(eid[:, :2], seq)
    w1 = expert_w1[0].reshape(N_EXPERTS, D_MODEL, EXPERT_FF)
    w3 = expert_w3[0].reshape(N_EXPERTS, D_MODEL, EXPERT_FF)
    w2 = expert_w2[0].reshape(N_EXPERTS, EXPERT_FF, D_MODEL)
    y2 = _moe_experts(hn, w1, w3, w2, tile_e, tile_blk, n_used, src_tok, dst_row, seq)

    out = _combine(h1, y2, wk, final_norm_g, x.dtype)
    return out.reshape(b, seq, d)
```

```python
import functools

import jax
import jax.numpy as jnp
from jax import lax
from jax.experimental import pallas as pl
from jax.experimental.pallas import tpu as pltpu

F32 = jnp.float32
BF16 = jnp.bfloat16

D_MODEL = 2048
A_HEADS = 12
A_HEAD_DIM = 128
A_WIDTH = A_HEADS * A_HEAD_DIM
A_SCALE = A_HEAD_DIM ** -0.5
DILATED_CONFIGS = ((128, 1), (512, 4), (2048, 16))
B_HEADS = 8
B_KEY_DIM = 128
B_WIDTH = B_HEADS * B_KEY_DIM
N_GROUPS = 4
EXPERTS_PER_GROUP = 8
N_EXPERTS = N_GROUPS * EXPERTS_PER_GROUP
EXPERT_FF = 512
NORM_EPS = 1e-6

LANES = 128
SUBLANES = 8
VMEM_LIMIT = 56 * 1024 * 1024

_QA_BLK = 0
_KA_BLK = A_HEADS
_VA_BLK = 2 * A_HEADS
_QB_BLK = 3 * A_HEADS
_FB_BLK = _QB_BLK + B_HEADS
_IB_BLK = _FB_BLK + B_HEADS
_GB_BLK = _IB_BLK + B_HEADS
_GATE_A_COL = 3 * A_WIDTH + 4 * B_WIDTH
_GATE_B_COL = _GATE_A_COL + D_MODEL
IN_WIDTH = _GATE_B_COL + D_MODEL


def _params(*sem):
    return pltpu.CompilerParams(dimension_semantics=sem, vmem_limit_bytes=VMEM_LIMIT)


def _rmsnorm_rows(x, g):
    ms = jnp.mean(x * x, axis=-1, keepdims=True)
    return x * lax.rsqrt(ms + NORM_EPS) * g


def _pack_bf16_pairs(x):
    w = x.shape[1] // 2
    lo = pltpu.bitcast(x[:, :w].astype(BF16).astype(F32), jnp.uint32)
    hi = pltpu.bitcast(x[:, w:].astype(BF16).astype(F32), jnp.uint32)
    return jnp.bitwise_or(jnp.bitwise_and(hi, jnp.uint32(0xFFFF0000)), jnp.right_shift(lo, jnp.uint32(16)))


def _unpack_bf16_pairs(u):
    lo = pltpu.bitcast(jnp.left_shift(u, jnp.uint32(16)), F32)
    hi = pltpu.bitcast(jnp.bitwise_and(u, jnp.uint32(0xFFFF0000)), F32)
    return jnp.concatenate([lo, hi], axis=1)


def _rmsnorm_body(x_ref, g_ref, o_ref):
    o_ref[...] = _rmsnorm_rows(x_ref[...].astype(F32), g_ref[...]).astype(o_ref.dtype)


def _rmsnorm(x, g, out_dtype, tm=512):
    m, d = x.shape
    return pl.pallas_call(
        _rmsnorm_body,
        grid=(m // tm,),
        in_specs=[pl.BlockSpec((tm, d), lambda i: (i, 0)), pl.BlockSpec((1, d), lambda i: (0, 0))],
        out_specs=pl.BlockSpec((tm, d), lambda i: (i, 0)),
        out_shape=jax.ShapeDtypeStruct((m, d), out_dtype),
        compiler_params=_params("arbitrary"),
        name="rmsnorm",
    )(x, g.reshape(1, d).astype(F32))


def _matmul_body(a_ref, w_ref, o_ref, wb_ref):
    @pl.when(pl.program_id(1) == 0)
    def _():
        wb_ref[...] = w_ref[...].astype(BF16)

    o_ref[...] = jnp.dot(a_ref[...], wb_ref[...], preferred_element_type=F32).astype(o_ref.dtype)


def _matmul(a, w, out_dtype, tm, tn, name):
    m, k = a.shape
    n = w.shape[1]
    return pl.pallas_call(
        _matmul_body,
        grid=(n // tn, m // tm),
        in_specs=[pl.BlockSpec((tm, k), lambda j, i: (i, 0)), pl.BlockSpec((k, tn), lambda j, i: (0, j))],
        out_specs=pl.BlockSpec((tm, tn), lambda j, i: (i, j)),
        out_shape=jax.ShapeDtypeStruct((m, n), out_dtype),
        scratch_shapes=[pltpu.VMEM((k, tn), BF16)],
        compiler_params=_params("arbitrary", "arbitrary"),
        name=name,
    )(a, w)


def _matmul_residual_body(a_ref, w_ref, r_ref, o_ref, wb_ref):
    @pl.when(pl.program_id(1) == 0)
    def _():
        wb_ref[...] = w_ref[...].astype(BF16)

    o_ref[...] = r_ref[...] + jnp.dot(a_ref[...], wb_ref[...], preferred_element_type=F32)


def _matmul_residual(a, w, res, tm, tn, name):
    m, k = a.shape
    n = w.shape[1]
    return pl.pallas_call(
        _matmul_residual_body,
        grid=(n // tn, m // tm),
        in_specs=[pl.BlockSpec((tm, k), lambda j, i: (i, 0)), pl.BlockSpec((k, tn), lambda j, i: (0, j)),
                  pl.BlockSpec((tm, tn), lambda j, i: (i, j))],
        out_specs=pl.BlockSpec((tm, tn), lambda j, i: (i, j)),
        out_shape=jax.ShapeDtypeStruct((m, n), F32),
        scratch_shapes=[pltpu.VMEM((k, tn), BF16)],
        compiler_params=_params("arbitrary", "arbitrary"),
        name=name,
    )(a, w, res)


ATT_BLK = 128
ATT_SUPER = 2048
ATT_UNROLL = 8


def _attn_body(q_ref, k_ref, v_ref, o_ref, o_scr, lse_scr, bias_scr):
    sb = pl.program_id(1)
    row0 = sb * ATT_SUPER
    diff = (lax.broadcasted_iota(jnp.int32, (ATT_BLK, 2 * ATT_BLK), 1)
            - lax.broadcasted_iota(jnp.int32, (ATT_BLK, 2 * ATT_BLK), 0))
    neg = jnp.float32(-jnp.inf)
    bias_scr[0] = jnp.where(jnp.logical_and(diff >= 0, diff <= ATT_BLK), 0.0, neg)
    bias_scr[1] = jnp.where(diff <= 0, 0.0, neg)

    for c, (window, dil) in enumerate(DILATED_CONFIGS):
        span = ATT_BLK * dil
        tiles_per_res = ATT_SUPER // span

        def tile(t, carry, c=c, dil=dil, span=span, tiles_per_res=tiles_per_res):
            n = t % tiles_per_res
            r = t // tiles_per_res
            q_start = n * span + r
            first = jnp.logical_and(sb == 0, n == 0)
            kv_start = row0 + q_start - jnp.where(first, 0, span)
            if dil == 1:
                q_start = pl.multiple_of(q_start, ATT_BLK)
                kv_start = pl.multiple_of(kv_start, ATT_BLK)
                qs = pl.ds(q_start, ATT_BLK)
                ks = pl.ds(kv_start, 2 * ATT_BLK)
            else:
                qs = pl.ds(q_start, ATT_BLK, stride=dil)
                ks = pl.ds(kv_start, 2 * ATT_BLK, stride=dil)
            q = (q_ref[qs, :] * A_SCALE).astype(BF16)
            k = k_ref[ks, :].astype(BF16)
            v = v_ref[ks, :].astype(BF16)
            s = lax.dot_general(q, k, (((1,), (1,)), ((), ())), preferred_element_type=F32)
            s = s + bias_scr[first.astype(jnp.int32)]
            mx = jnp.max(s, axis=-1, keepdims=True)
            p = jnp.exp(s - mx)
            den = jnp.sum(p, axis=-1, keepdims=True)
            acc = jnp.dot(p.astype(BF16), v, preferred_element_type=F32)
            o_scr[c, qs, :] = acc / den
            lse_scr[c, qs, :] = jnp.broadcast_to(mx + jnp.log(den), (ATT_BLK, A_HEAD_DIM))
            return carry

        def tiles(i, carry, tile=tile):
            for u in range(ATT_UNROLL):
                tile(i * ATT_UNROLL + u, carry)
            return carry

        lax.fori_loop(0, ATT_SUPER // ATT_BLK // ATT_UNROLL, tiles, 0)

    rows = 256

    def merge(i, carry):
        sl = pl.ds(pl.multiple_of(i * rows, rows), rows)
        l0, l1, l2 = lse_scr[0, sl, :], lse_scr[1, sl, :], lse_scr[2, sl, :]
        m = jnp.maximum(jnp.maximum(l0, l1), l2)
        w0, w1, w2 = jnp.exp(l0 - m), jnp.exp(l1 - m), jnp.exp(l2 - m)
        num = w0 * o_scr[0, sl, :] + w1 * o_scr[1, sl, :] + w2 * o_scr[2, sl, :]
        o_ref[sl, :] = (num / (w0 + w1 + w2)).astype(o_ref.dtype)
        return carry

    lax.fori_loop(0, ATT_SUPER // rows, merge, 0)


def _dilated_attention(proj, seq):
    n_super = seq // ATT_SUPER
    blk = lambda off: pl.BlockSpec((seq, A_HEAD_DIM), lambda h, s: (0, off + h))
    return pl.pallas_call(
        _attn_body,
        grid=(A_HEADS, n_super),
        in_specs=[pl.BlockSpec((ATT_SUPER, A_HEAD_DIM), lambda h, s: (s, _QA_BLK + h)), blk(_KA_BLK), blk(_VA_BLK)],
        out_specs=pl.BlockSpec((ATT_SUPER, A_HEAD_DIM), lambda h, s: (s, h)),
        out_shape=jax.ShapeDtypeStruct((seq, A_WIDTH), BF16),
        scratch_shapes=[pltpu.VMEM((len(DILATED_CONFIGS), ATT_SUPER, A_HEAD_DIM), F32),
                        pltpu.VMEM((len(DILATED_CONFIGS), ATT_SUPER, A_HEAD_DIM), F32),
                        pltpu.VMEM((2, ATT_BLK, 2 * ATT_BLK), F32)],
        compiler_params=_params("arbitrary", "arbitrary"),
        name="dilated_attention",
    )(proj, proj, proj)


HG_CHUNK = 64
HG_TB = 512
HG_MIN_LEVEL = 4
HG_HEADS_PER_STEP = 2
HG_CHUNKS_PER_ITER = 4


def _hgrn_body(qb_ref, fb_ref, ib_ref, gb_ref, lbl_ref, g_ref, o_ref, st_ref):
    @pl.when(pl.program_id(1) == 0)
    def _():
        st_ref[...] = jnp.zeros_like(st_ref)

    c_ = HG_CHUNK
    lbl = lbl_ref[...]
    e = jnp.exp(lbl - jnp.max(lbl, axis=0, keepdims=True))
    lb_all = e[0:1, :] / jnp.sum(e, axis=0, keepdims=True)
    g_all = g_ref[...]

    ti = lax.broadcasted_iota(jnp.int32, (c_, c_), 0)
    si = lax.broadcasted_iota(jnp.int32, (c_, c_), 1)
    tri = jnp.where(si <= ti, 1.0, 0.0).astype(BF16)
    xor = jnp.bitwise_xor(ti, si)
    row = lax.broadcasted_iota(jnp.int32, (c_, B_KEY_DIM), 0)

    def chunk(ci, hp):
        sl = pl.ds(pl.multiple_of(ci * c_, c_), c_)
        ln = slice(hp * B_KEY_DIM, (hp + 1) * B_KEY_DIM)
        lb = lb_all[:, ln]
        f = lb + (1.0 - lb) * jax.nn.sigmoid(fb_ref[sl, ln])
        logf = jnp.log(f)
        kk = 1.0 - f
        q = jax.nn.silu(qb_ref[sl, ln])
        v = ib_ref[sl, ln]
        vb = v.astype(BF16)
        hi = logf.astype(BF16)
        rem = logf - hi.astype(F32)
        mid = rem.astype(BF16)
        low = (rem - mid.astype(F32)).astype(BF16)
        parts = jnp.dot(tri, jnp.concatenate([hi, mid, low], axis=1), preferred_element_type=F32)
        bc = parts[:, :B_KEY_DIM] + (parts[:, B_KEY_DIM:2 * B_KEY_DIM] + parts[:, 2 * B_KEY_DIM:])

        attn = jnp.zeros((c_, c_), F32)
        half = c_ // 2
        while half >= HG_MIN_LEVEL:
            blk = 2 * half
            ref_rows = [jnp.broadcast_to(bc[b0 + half - 1:b0 + half, :], (blk, B_KEY_DIM)) for b0 in range(0, c_, blk)]
            ref = jnp.concatenate(ref_rows, axis=0) if len(ref_rows) > 1 else ref_rows[0]
            dec = jnp.exp(-jnp.abs(bc - ref))
            s = lax.dot_general((q * dec).astype(BF16), (kk * dec).astype(BF16), (((1,), (1,)), ((), ())),
                                preferred_element_type=F32)
            keep = jnp.logical_and(jnp.logical_and(xor >= half, xor < blk), ti > si)
            attn = attn + jnp.where(keep, s, 0.0)
            half //= 2
        o = jnp.dot(attn.astype(BF16), vb, preferred_element_type=F32)

        for delta in range(HG_MIN_LEVEL):
            if delta == 0:
                w = q * kk
                vd = v
            else:
                valid = jnp.bitwise_and(row, HG_MIN_LEVEL - 1) >= delta
                rel = jnp.where(valid, bc - pltpu.roll(bc, delta, 0), 0.0)
                w = jnp.where(valid, q * pltpu.roll(kk, delta, 0) * jnp.exp(rel), 0.0)
                vd = pltpu.roll(v, delta, 0)
            o = o + jnp.sum(w, axis=-1, keepdims=True) * vd

        st = st_ref[hp]
        o = o + lax.dot_general((q * jnp.exp(bc)).astype(BF16), st.astype(BF16), (((1,), (1,)), ((), ())),
                                preferred_element_type=F32)
        b_last = bc[c_ - 1:c_, :]
        kdec = (kk * jnp.exp(b_last - bc)).astype(BF16)
        upd = lax.dot_general(vb, kdec, (((0,), (0,)), ((), ())), preferred_element_type=F32)
        st_ref[hp] = st * jnp.exp(b_last) + upd

        o = o * lax.rsqrt(jnp.mean(o * o, axis=-1, keepdims=True) + NORM_EPS)
        o_ref[sl, ln] = (o * g_all[:, ln] * jax.nn.silu(gb_ref[sl, ln])).astype(o_ref.dtype)

    def chunks(i, carry):
        for u in range(HG_CHUNKS_PER_ITER):
            for hp in range(HG_HEADS_PER_STEP):
                chunk(i * HG_CHUNKS_PER_ITER + u, hp)
        return carry

    lax.fori_loop(0, HG_TB // c_ // HG_CHUNKS_PER_ITER, chunks, 0)


def _hgrn2(proj, lb_logits, norm_g, seq):
    hp = HG_HEADS_PER_STEP
    width = hp * B_KEY_DIM
    col = lambda off: pl.BlockSpec((HG_TB, width), lambda h, t: (t, off // hp + h))
    n_lb = lb_logits.shape[0]
    return pl.pallas_call(
        _hgrn_body,
        grid=(B_HEADS // hp, seq // HG_TB),
        in_specs=[col(_QB_BLK), col(_FB_BLK), col(_IB_BLK), col(_GB_BLK),
                  pl.BlockSpec((n_lb, width), lambda h, t: (0, h)),
                  pl.BlockSpec((1, width), lambda h, t: (0, h))],
        out_specs=pl.BlockSpec((HG_TB, width), lambda h, t: (t, h)),
        out_shape=jax.ShapeDtypeStruct((seq, B_WIDTH), BF16),
        scratch_shapes=[pltpu.VMEM((hp, B_KEY_DIM, B_KEY_DIM), F32)],
        compiler_params=_params("arbitrary", "arbitrary"),
        name="hgrn2",
    )(proj, proj, proj, proj, lb_logits.astype(F32), norm_g.reshape(1, B_WIDTH).astype(F32))


MERGE_GATE_BLK = 512


def _merge_body(oa_ref, ob_ref, *rest):
    n_g = (len(rest) - 5) // 2
    ga_refs, gb_refs = rest[:n_g], rest[n_g:2 * n_g]
    wa_ref, wb_ref, o_ref, wa_s, wb_s = rest[2 * n_g:]

    @pl.when(pl.program_id(1) == 0)
    def _():
        wa_s[...] = wa_ref[...].astype(BF16)
        wb_s[...] = wb_ref[...].astype(BF16)

    ya = jnp.dot(oa_ref[...], wa_s[...], preferred_element_type=F32)
    yb = jnp.dot(ob_ref[...], wb_s[...], preferred_element_type=F32)
    ga = jnp.concatenate([r[...] for r in ga_refs], axis=1)
    gb = jnp.concatenate([r[...] for r in gb_refs], axis=1)
    o_ref[...] = (jax.nn.sigmoid(ga) * ya + jax.nn.sigmoid(gb) * yb).astype(o_ref.dtype)


def _branch_merge(oa, ob, proj, wa, wb, tm=512, tn=1024):
    m = oa.shape[0]
    gw = MERGE_GATE_BLK
    n_g = tn // gw
    ga0, gb0 = _GATE_A_COL // gw, _GATE_B_COL // gw
    gate = lambda off, u: pl.BlockSpec((tm, gw), lambda j, i: (i, off + j * n_g + u))
    return pl.pallas_call(
        _merge_body,
        grid=(D_MODEL // tn, m // tm),
        in_specs=[pl.BlockSpec((tm, A_WIDTH), lambda j, i: (i, 0)), pl.BlockSpec((tm, B_WIDTH), lambda j, i: (i, 0)),
                  *[gate(ga0, u) for u in range(n_g)], *[gate(gb0, u) for u in range(n_g)],
                  pl.BlockSpec((A_WIDTH, tn), lambda j, i: (0, j)), pl.BlockSpec((B_WIDTH, tn), lambda j, i: (0, j))],
        out_specs=pl.BlockSpec((tm, tn), lambda j, i: (i, j)),
        out_shape=jax.ShapeDtypeStruct((m, D_MODEL), BF16),
        scratch_shapes=[pltpu.VMEM((A_WIDTH, tn), BF16), pltpu.VMEM((B_WIDTH, tn), BF16)],
        compiler_params=_params("arbitrary", "arbitrary"),
        name="branch_merge",
    )(oa, ob, *([proj] * (2 * n_g)), wa, wb)


def _router_body(h_ref, g_ref, rw_ref, rb_ref, hn_ref, eid_ref, wk_ref):
    hn = _rmsnorm_rows(h_ref[...], g_ref[...])
    hn_ref[...] = _pack_bf16_pairs(hn)
    rw = rw_ref[...]
    hn_hi = hn.astype(BF16)
    hn_lo = (hn - hn_hi.astype(F32)).astype(BF16)
    rw_hi = rw.astype(BF16)
    rw_lo = (rw - rw_hi.astype(F32)).astype(BF16)
    logits = (jnp.dot(hn_hi, rw_hi, preferred_element_type=F32)
              + (jnp.dot(hn_lo, rw_hi, preferred_element_type=F32)
                 + jnp.dot(hn_hi, rw_lo, preferred_element_type=F32))) + rb_ref[...]
    lane = lax.broadcasted_iota(jnp.int32, logits.shape, 1)
    neg = jnp.float32(-jnp.inf)
    big = jnp.int32(LANES)

    is_g = lane < N_GROUPS
    lg = jnp.where(is_g, logits, neg)
    mg = jnp.max(lg, axis=-1, keepdims=True)
    g_idx = jnp.min(jnp.where(lg == mg, lane, big), axis=-1, keepdims=True)
    pg_top = 1.0 / jnp.sum(jnp.where(is_g, jnp.exp(lg - mg), 0.0), axis=-1, keepdims=True)

    lo = N_GROUPS + g_idx * EXPERTS_PER_GROUP
    in_grp = jnp.logical_and(lane >= lo, lane < lo + EXPERTS_PER_GROUP)
    le = jnp.where(in_grp, logits, neg)
    v1 = jnp.max(le, axis=-1, keepdims=True)
    i1 = jnp.min(jnp.where(le == v1, lane, big), axis=-1, keepdims=True)
    le2 = jnp.where(lane == i1, neg, le)
    v2 = jnp.max(le2, axis=-1, keepdims=True)
    i2 = jnp.min(jnp.where(le2 == v2, lane, big), axis=-1, keepdims=True)
    e2 = jnp.exp(v2 - v1)
    w1 = pg_top / (1.0 + e2)
    w2 = pg_top * e2 / (1.0 + e2)
    eid_ref[...] = jnp.where(lane == 0, i1 - N_GROUPS, jnp.where(lane == 1, i2 - N_GROUPS, 0))
    wk_ref[...] = jnp.where(lane == 0, w1, jnp.where(lane == 1, w2, 0.0))


def _router(h, g, rw, rb, tm=256):
    m, d = h.shape
    row = lambda w: pl.BlockSpec((tm, w), lambda i: (i, 0))
    return pl.pallas_call(
        _router_body,
        grid=(m // tm,),
        in_specs=[row(d), pl.BlockSpec((1, d), lambda i: (0, 0)), pl.BlockSpec((d, LANES), lambda i: (0, 0)),
                  pl.BlockSpec((1, LANES), lambda i: (0, 0))],
        out_specs=[row(d // 2), row(LANES), row(LANES)],
        out_shape=[jax.ShapeDtypeStruct((m, d // 2), jnp.uint32), jax.ShapeDtypeStruct((m, LANES), jnp.int32),
                   jax.ShapeDtypeStruct((m, LANES), F32)],
        compiler_params=_params("arbitrary"),
        name="router",
    )(h, g.reshape(1, d).astype(F32), rw, rb)


MOE_TM = 256
ROW_DMA_PRIORITY = 0


def _moe_body(tile_e_ref, tile_blk_ref, tile_rows_ref, n_used_ref,
              src0_ref, src_next_ref, dst_prev_ref, hn_ref, w1_ref, w3_ref, w2_ref, y_ref,
              w1_s, w3_s, w2_s, x0, x1, o0, o1, gsem, ssem):
    j = pl.program_id(0)
    n_tiles = pl.num_programs(0)
    n_used = n_used_ref[0]
    xs = (x0, x1)
    os_ = (o0, o1)

    def rows_of(t):
        return jnp.where(t >= 0, tile_rows_ref[jnp.clip(t, 0, n_tiles - 1)], 0)

    def gather_issue(idx_ref, b, n):
        for r in range(MOE_TM):
            @pl.when(r < n)
            def _(r=r):
                pltpu.make_async_copy(hn_ref.at[idx_ref[0, 0, r]], xs[b].at[r],
                                      gsem.at[b]).start(priority=ROW_DMA_PRIORITY)

    def scatter_issue(idx_ref, b, n):
        for r in range(MOE_TM):
            @pl.when(r < n)
            def _(r=r):
                pltpu.make_async_copy(os_[b].at[r], y_ref.at[idx_ref[0, 0, r]],
                                      ssem.at[b]).start(priority=ROW_DMA_PRIORITY)

    def wait_rows(make_copy, n):
        p = MOE_TM
        while p >= 1:
            @pl.when(jnp.bitwise_and(n, p) != 0)
            def _(p=p):
                make_copy(p).wait()
            p //= 2

    def gather_wait(b, n):
        wait_rows(lambda p: pltpu.make_async_copy(hn_ref.at[pl.ds(0, p)], xs[b].at[pl.ds(0, p)], gsem.at[b]), n)

    def scatter_wait(b, n):
        wait_rows(lambda p: pltpu.make_async_copy(os_[b].at[pl.ds(0, p)], y_ref.at[pl.ds(0, p)], ssem.at[b]), n)

    @pl.when(j == 0)
    def _():
        x0[...] = jnp.zeros_like(x0)
        x1[...] = jnp.zeros_like(x1)
        gather_issue(src0_ref, 0, rows_of(0))

    for b in (0, 1):
        @pl.when(jnp.logical_and(j < n_used, j % 2 == b))
        def _(b=b):
            gather_wait(b, rows_of(j))
            scatter_wait(b, rows_of(j - 2))

            prev_e = tile_e_ref[jnp.maximum(j - 1, 0)]

            @pl.when(jnp.logical_or(j == 0, tile_e_ref[j] != prev_e))
            def _():
                w1_s[...] = w1_ref[0].astype(BF16)
                w3_s[...] = w3_ref[0].astype(BF16)
                w2_s[...] = w2_ref[0].astype(BF16)

            gather_issue(src_next_ref, 1 - b, rows_of(j + 1))
            scatter_issue(dst_prev_ref, 1 - b, rows_of(j - 1))
            x = _unpack_bf16_pairs(xs[b][...]).astype(BF16)
            hid = (jax.nn.silu(jnp.dot(x, w1_s[...], preferred_element_type=F32))
                   * jnp.dot(x, w3_s[...], preferred_element_type=F32))
            os_[b][...] = _pack_bf16_pairs(jnp.dot(hid.astype(BF16), w2_s[...], preferred_element_type=F32))

    for b in (0, 1):
        @pl.when(jnp.logical_and(j == n_used, j % 2 == b))
        def _(b=b):
            scatter_wait(b, rows_of(j - 2))
            scatter_issue(dst_prev_ref, 1 - b, rows_of(j - 1))
            scatter_wait(1 - b, rows_of(j - 1))


def _moe_experts(hn, w1, w3, w2, tile_e, tile_blk, tile_rows, n_used, src, dst, n_tokens):
    dp = hn.shape[1]
    d = 2 * dp
    n_tiles = tile_e.shape[0]
    ff = w1.shape[-1]
    idx_block = lambda fn: pl.BlockSpec((1, 1, MOE_TM), fn, memory_space=pltpu.SMEM)
    grid_spec = pltpu.PrefetchScalarGridSpec(
        num_scalar_prefetch=4,
        grid=(n_tiles,),
        in_specs=[
            idx_block(lambda j, te, tb, tr, nu: (0, 0, 0)),
            idx_block(lambda j, te, tb, tr, nu: (tb[jnp.minimum(j + 1, n_tiles - 1)], 0, 0)),
            idx_block(lambda j, te, tb, tr, nu: (tb[jnp.maximum(j - 1, 0)], 0, 0)),
            pl.BlockSpec(memory_space=pl.ANY),
            pl.BlockSpec((1, d, ff), lambda j, te, tb, tr, nu: (te[j], 0, 0)),
            pl.BlockSpec((1, d, ff), lambda j, te, tb, tr, nu: (te[j], 0, 0)),
            pl.BlockSpec((1, ff, d), lambda j, te, tb, tr, nu: (te[j], 0, 0)),
        ],
        out_specs=pl.BlockSpec(memory_space=pl.ANY),
        scratch_shapes=[pltpu.VMEM((d, ff), BF16), pltpu.VMEM((d, ff), BF16), pltpu.VMEM((ff, d), BF16),
                        pltpu.VMEM((MOE_TM, dp), jnp.uint32), pltpu.VMEM((MOE_TM, dp), jnp.uint32),
                        pltpu.VMEM((MOE_TM, dp), jnp.uint32), pltpu.VMEM((MOE_TM, dp), jnp.uint32),
                        pltpu.SemaphoreType.DMA((2,)), pltpu.SemaphoreType.DMA((2,))],
    )
    src3 = src.reshape(n_tiles, 1, MOE_TM)
    return pl.pallas_call(
        _moe_body,
        grid_spec=grid_spec,
        out_shape=jax.ShapeDtypeStruct((2 * n_tokens, dp), jnp.uint32),
        compiler_params=_params("arbitrary"),
        name="moe_experts",
    )(tile_e, tile_blk, tile_rows, n_used, src3, src3, dst.reshape(n_tiles, 1, MOE_TM), hn, w1, w3, w2)


def _combine_body(h_ref, y0_ref, y1_ref, wk_ref, g_ref, o_ref):
    wk = wk_ref[...]
    y = (h_ref[...] + wk[:, 0:1] * _unpack_bf16_pairs(y0_ref[...])
         + wk[:, 1:2] * _unpack_bf16_pairs(y1_ref[...]))
    o_ref[...] = _rmsnorm_rows(y, g_ref[...]).astype(o_ref.dtype)


def _combine(h, y2, wk, g, out_dtype, tm=256):
    m, d = h.shape
    nblk = m // tm
    return pl.pallas_call(
        _combine_body,
        grid=(nblk,),
        in_specs=[pl.BlockSpec((tm, d), lambda i: (i, 0)), pl.BlockSpec((tm, d // 2), lambda i: (i, 0)),
                  pl.BlockSpec((tm, d // 2), lambda i: (nblk + i, 0)), pl.BlockSpec((tm, LANES), lambda i: (i, 0)),
                  pl.BlockSpec((1, d), lambda i: (0, 0))],
        out_specs=pl.BlockSpec((tm, d), lambda i: (i, 0)),
        out_shape=jax.ShapeDtypeStruct((m, d), out_dtype),
        compiler_params=_params("arbitrary"),
        name="combine_final_norm",
    )(h, y2, y2, wk, g.reshape(1, d).astype(F32))


def _routing_tables(eid, n_tokens):
    n_pairs = 2 * n_tokens
    n_tiles = n_pairs // MOE_TM + N_EXPERTS + 1
    e_flat = eid.reshape(n_pairs)
    order = jnp.argsort(e_flat, stable=True).astype(jnp.int32)
    counts = jnp.sum(e_flat[:, None] == jnp.arange(N_EXPERTS, dtype=jnp.int32)[None, :], axis=0, dtype=jnp.int32)
    tiles_per_e = (counts + MOE_TM - 1) // MOE_TM
    tile_end = jnp.cumsum(tiles_per_e)
    tile_start = tile_end - tiles_per_e
    n_used = tile_end[-1]
    sorted_start = jnp.cumsum(counts) - counts
    j = jnp.arange(n_tiles, dtype=jnp.int32)
    tile_blk = jnp.minimum(j, n_used - 1)
    tile_e = jnp.searchsorted(tile_end, tile_blk, side="right").astype(jnp.int32)
    tile_row0 = (tile_blk - tile_start[tile_e]) * MOE_TM
    tile_rows = jnp.where(j == tile_blk, jnp.clip(counts[tile_e] - tile_row0, 0, MOE_TM), 0).astype(jnp.int32)
    r = jnp.arange(MOE_TM, dtype=jnp.int32)[None, :]
    valid = r < tile_rows[:, None]
    pair = order[jnp.clip((sorted_start[tile_e] + tile_row0)[:, None] + r, 0, n_pairs - 1)]
    src_tok = jnp.where(valid, pair // 2, 0).astype(jnp.int32)
    dst_row = jnp.where(valid, (pair % 2) * n_tokens + pair // 2, 0).astype(jnp.int32)
    return (tile_e, tile_blk.astype(jnp.int32), tile_rows, n_used.reshape(1).astype(jnp.int32),
            src_tok.reshape(-1), dst_row.reshape(-1))


def kernel(x, mix_norm_g, w_in, hgrn_lb_logits, hgrn_norm_g, w_branch_a, w_branch_b, w_out, ffn_norm_g,
           router_w_group, router_b_group, router_w_expert, router_b_expert, expert_w1, expert_w3, expert_w2,
           final_norm_g):
    b, seq, d = x.shape
    assert b == 1 and d == D_MODEL and seq % ATT_SUPER == 0 and w_in.shape == (1, D_MODEL, IN_WIDTH)
    h0 = x.reshape(seq, d).astype(F32)

    xn = _rmsnorm(h0, mix_norm_g[0], BF16)
    proj = _matmul(xn, w_in[0], F32, tm=512, tn=1280, name="in_proj")
    oa = _dilated_attention(proj, seq)
    ob = _hgrn2(proj, hgrn_lb_logits, hgrn_norm_g[0], seq)
    merged = _branch_merge(oa, ob, proj, w_branch_a[0], w_branch_b[0])
    h1 = _matmul_residual(merged, w_out[0], h0, tm=512, tn=1024, name="out_proj")

    pad = LANES - N_GROUPS - N_EXPERTS
    rw = jnp.concatenate([router_w_group[0], router_w_expert[0], jnp.zeros((d, pad), F32)], axis=1).astype(F32)
    rb = jnp.concatenate([router_b_group[0], router_b_expert[0], jnp.zeros((pad,), F32)]).reshape(1, LANES).astype(F32)
    hn, eid, wk = _router(h1, ffn_norm_g[0], rw, rb)
    tile_e, tile_blk, tile_rows, n_used, src_tok, dst_row = _routing_tables(eid[:, :2], seq)
    w1 = expert_w1[0].reshape(N_EXPERTS, D_MODEL, EXPERT_FF)
    w3 = expert_w3[0].reshape(N_EXPERTS, D_MODEL, EXPERT_FF)
    w2 = expert_w2[0].reshape(N_EXPERTS, EXPERT_FF, D_MODEL)
    y2 = _moe_experts(hn, w1, w3, w2, tile_e, tile_blk, tile_rows, n_used, src_tok, dst_row, seq)

    out = _combine(h1, y2, wk, final_norm_g, x.dtype)
    return out.reshape(b, seq, d)
```

```python
import functools

import jax
import jax.numpy as jnp
from jax import lax
from jax.experimental import pallas as pl
from jax.experimental.pallas import tpu as pltpu

F32 = jnp.float32
BF16 = jnp.bfloat16

D_MODEL = 2048
A_HEADS = 12
A_HEAD_DIM = 128
A_WIDTH = A_HEADS * A_HEAD_DIM
A_SCALE = A_HEAD_DIM ** -0.5
DILATED_CONFIGS = ((128, 1), (512, 4), (2048, 16))
B_HEADS = 8
B_KEY_DIM = 128
B_WIDTH = B_HEADS * B_KEY_DIM
N_GROUPS = 4
EXPERTS_PER_GROUP = 8
N_EXPERTS = N_GROUPS * EXPERTS_PER_GROUP
EXPERT_FF = 512
NORM_EPS = 1e-6

LANES = 128
SUBLANES = 8
VMEM_LIMIT = 56 * 1024 * 1024

_QA_BLK = 0
_KA_BLK = A_HEADS
_VA_BLK = 2 * A_HEADS
_QB_BLK = 3 * A_HEADS
_FB_BLK = _QB_BLK + B_HEADS
_IB_BLK = _FB_BLK + B_HEADS
_GB_BLK = _IB_BLK + B_HEADS
_GATE_A_COL = 3 * A_WIDTH + 4 * B_WIDTH
_GATE_B_COL = _GATE_A_COL + D_MODEL
IN_WIDTH = _GATE_B_COL + D_MODEL


def _params(*sem):
    return pltpu.CompilerParams(dimension_semantics=sem, vmem_limit_bytes=VMEM_LIMIT)


def _rmsnorm_rows(x, g):
    ms = jnp.mean(x * x, axis=-1, keepdims=True)
    return x * lax.rsqrt(ms + NORM_EPS) * g


def _pack_bf16_pairs(x):
    w = x.shape[1] // 2
    lo = pltpu.bitcast(x[:, :w].astype(BF16).astype(F32), jnp.uint32)
    hi = pltpu.bitcast(x[:, w:].astype(BF16).astype(F32), jnp.uint32)
    return jnp.bitwise_or(jnp.bitwise_and(hi, jnp.uint32(0xFFFF0000)), jnp.right_shift(lo, jnp.uint32(16)))


def _unpack_bf16_pairs(u):
    lo = pltpu.bitcast(jnp.left_shift(u, jnp.uint32(16)), F32)
    hi = pltpu.bitcast(jnp.bitwise_and(u, jnp.uint32(0xFFFF0000)), F32)
    return jnp.concatenate([lo, hi], axis=1)


def _rmsnorm_body(x_ref, g_ref, o_ref):
    o_ref[...] = _rmsnorm_rows(x_ref[...].astype(F32), g_ref[...]).astype(o_ref.dtype)


def _rmsnorm(x, g, out_dtype, tm=512):
    m, d = x.shape
    return pl.pallas_call(
        _rmsnorm_body,
        grid=(m // tm,),
        in_specs=[pl.BlockSpec((tm, d), lambda i: (i, 0)), pl.BlockSpec((1, d), lambda i: (0, 0))],
        out_specs=pl.BlockSpec((tm, d), lambda i: (i, 0)),
        out_shape=jax.ShapeDtypeStruct((m, d), out_dtype),
        compiler_params=_params("arbitrary"),
        name="rmsnorm",
    )(x, g.reshape(1, d).astype(F32))


def _matmul_body(a_ref, w_ref, o_ref, wb_ref):
    @pl.when(pl.program_id(1) == 0)
    def _():
        wb_ref[...] = w_ref[...].astype(BF16)

    o_ref[...] = jnp.dot(a_ref[...], wb_ref[...], preferred_element_type=F32).astype(o_ref.dtype)


def _matmul(a, w, out_dtype, tm, tn, name):
    m, k = a.shape
    n = w.shape[1]
    return pl.pallas_call(
        _matmul_body,
        grid=(n // tn, m // tm),
        in_specs=[pl.BlockSpec((tm, k), lambda j, i: (i, 0)), pl.BlockSpec((k, tn), lambda j, i: (0, j))],
        out_specs=pl.BlockSpec((tm, tn), lambda j, i: (i, j)),
        out_shape=jax.ShapeDtypeStruct((m, n), out_dtype),
        scratch_shapes=[pltpu.VMEM((k, tn), BF16)],
        compiler_params=_params("arbitrary", "arbitrary"),
        name=name,
    )(a, w)


def _matmul_residual_body(a_ref, w_ref, r_ref, o_ref, wb_ref):
    @pl.when(pl.program_id(1) == 0)
    def _():
        wb_ref[...] = w_ref[...].astype(BF16)

    o_ref[...] = r_ref[...] + jnp.dot(a_ref[...], wb_ref[...], preferred_element_type=F32)


def _matmul_residual(a, w, res, tm, tn, name):
    m, k = a.shape
    n = w.shape[1]
    return pl.pallas_call(
        _matmul_residual_body,
        grid=(n // tn, m // tm),
        in_specs=[pl.BlockSpec((tm, k), lambda j, i: (i, 0)), pl.BlockSpec((k, tn), lambda j, i: (0, j)),
                  pl.BlockSpec((tm, tn), lambda j, i: (i, j))],
        out_specs=pl.BlockSpec((tm, tn), lambda j, i: (i, j)),
        out_shape=jax.ShapeDtypeStruct((m, n), F32),
        scratch_shapes=[pltpu.VMEM((k, tn), BF16)],
        compiler_params=_params("arbitrary", "arbitrary"),
        name=name,
    )(a, w, res)


LOG2_E = 1.4426950408889634
LN_2 = 0.6931471805599453
ATT_BLK = 128
ATT_SUPER = 2048
ATT_UNROLL = 16


def _attn_body(q_ref, k_ref, v_ref, o_ref, o_scr, lse_scr, bias_scr):
    sb = pl.program_id(1)
    row0 = sb * ATT_SUPER
    diff = (lax.broadcasted_iota(jnp.int32, (ATT_BLK, 2 * ATT_BLK), 1)
            - lax.broadcasted_iota(jnp.int32, (ATT_BLK, 2 * ATT_BLK), 0))
    neg = jnp.float32(-jnp.inf)
    bias_scr[0] = jnp.where(jnp.logical_and(diff >= 0, diff <= ATT_BLK), 0.0, neg)
    bias_scr[1] = jnp.where(diff <= 0, 0.0, neg)

    for c, (window, dil) in enumerate(DILATED_CONFIGS):
        span = ATT_BLK * dil
        tiles_per_res = ATT_SUPER // span

        def tile(t, carry, c=c, dil=dil, span=span, tiles_per_res=tiles_per_res):
            n = t % tiles_per_res
            r = t // tiles_per_res
            q_start = n * span + r
            first = jnp.logical_and(sb == 0, n == 0)
            kv_start = row0 + q_start - jnp.where(first, 0, span)
            if dil == 1:
                q_start = pl.multiple_of(q_start, ATT_BLK)
                kv_start = pl.multiple_of(kv_start, ATT_BLK)
                qs = pl.ds(q_start, ATT_BLK)
                ks = pl.ds(kv_start, 2 * ATT_BLK)
            else:
                qs = pl.ds(q_start, ATT_BLK, stride=dil)
                ks = pl.ds(kv_start, 2 * ATT_BLK, stride=dil)
            q = (q_ref[qs, :] * (A_SCALE * LOG2_E)).astype(BF16)
            k = k_ref[ks, :].astype(BF16)
            v = v_ref[ks, :].astype(BF16)
            s = lax.dot_general(q, k, (((1,), (1,)), ((), ())), preferred_element_type=F32)
            s = s + bias_scr[first.astype(jnp.int32)]
            mx = jnp.max(s, axis=-1, keepdims=True)
            p = jnp.exp2(s - mx)
            den = jnp.sum(p, axis=-1, keepdims=True)
            acc = jnp.dot(p.astype(BF16), v, preferred_element_type=F32)
            o_scr[c, qs, :] = acc / den
            lse_scr[c, qs, :] = jnp.broadcast_to(mx * LN_2 + jnp.log(den), (ATT_BLK, A_HEAD_DIM))
            return carry

        def tiles(i, carry, tile=tile):
            for u in range(ATT_UNROLL):
                tile(i * ATT_UNROLL + u, carry)
            return carry

        lax.fori_loop(0, ATT_SUPER // ATT_BLK // ATT_UNROLL, tiles, 0)

    rows = 256

    def merge(i, carry):
        sl = pl.ds(pl.multiple_of(i * rows, rows), rows)
        l0, l1, l2 = lse_scr[0, sl, :], lse_scr[1, sl, :], lse_scr[2, sl, :]
        m = jnp.maximum(jnp.maximum(l0, l1), l2)
        w0, w1, w2 = jnp.exp(l0 - m), jnp.exp(l1 - m), jnp.exp(l2 - m)
        num = w0 * o_scr[0, sl, :] + w1 * o_scr[1, sl, :] + w2 * o_scr[2, sl, :]
        o_ref[sl, :] = (num / (w0 + w1 + w2)).astype(o_ref.dtype)
        return carry

    lax.fori_loop(0, ATT_SUPER // rows, merge, 0)


def _dilated_attention(proj, seq):
    n_super = seq // ATT_SUPER
    blk = lambda off: pl.BlockSpec((seq, A_HEAD_DIM), lambda h, s: (0, off + h))
    return pl.pallas_call(
        _attn_body,
        grid=(A_HEADS, n_super),
        in_specs=[pl.BlockSpec((ATT_SUPER, A_HEAD_DIM), lambda h, s: (s, _QA_BLK + h)), blk(_KA_BLK), blk(_VA_BLK)],
        out_specs=pl.BlockSpec((ATT_SUPER, A_HEAD_DIM), lambda h, s: (s, h)),
        out_shape=jax.ShapeDtypeStruct((seq, A_WIDTH), BF16),
        scratch_shapes=[pltpu.VMEM((len(DILATED_CONFIGS), ATT_SUPER, A_HEAD_DIM), F32),
                        pltpu.VMEM((len(DILATED_CONFIGS), ATT_SUPER, A_HEAD_DIM), F32),
                        pltpu.VMEM((2, ATT_BLK, 2 * ATT_BLK), F32)],
        compiler_params=_params("arbitrary", "arbitrary"),
        name="dilated_attention",
    )(proj, proj, proj)


HG_CHUNK = 64
HG_TB = 512
HG_MIN_LEVEL = 4
HG_HEADS_PER_STEP = 2
HG_CHUNKS_PER_ITER = 4


def _hgrn_body(qb_ref, fb_ref, ib_ref, gb_ref, lbl_ref, g_ref, o_ref, st_ref):
    @pl.when(pl.program_id(1) == 0)
    def _():
        st_ref[...] = jnp.zeros_like(st_ref)

    c_ = HG_CHUNK
    lbl = lbl_ref[...]
    e = jnp.exp(lbl - jnp.max(lbl, axis=0, keepdims=True))
    lb_all = e[0:1, :] / jnp.sum(e, axis=0, keepdims=True)
    g_all = g_ref[...]

    ti = lax.broadcasted_iota(jnp.int32, (c_, c_), 0)
    si = lax.broadcasted_iota(jnp.int32, (c_, c_), 1)
    tri = jnp.where(si <= ti, 1.0, 0.0).astype(BF16)
    xor = jnp.bitwise_xor(ti, si)
    row = lax.broadcasted_iota(jnp.int32, (c_, B_KEY_DIM), 0)

    def chunk(ci, hp):
        sl = pl.ds(pl.multiple_of(ci * c_, c_), c_)
        ln = slice(hp * B_KEY_DIM, (hp + 1) * B_KEY_DIM)
        lb = lb_all[:, ln]
        f = lb + (1.0 - lb) * jax.nn.sigmoid(fb_ref[sl, ln])
        logf = jnp.log(f)
        kk = 1.0 - f
        q = jax.nn.silu(qb_ref[sl, ln])
        v = ib_ref[sl, ln]
        vb = v.astype(BF16)
        hi = logf.astype(BF16)
        rem = logf - hi.astype(F32)
        mid = rem.astype(BF16)
        low = (rem - mid.astype(F32)).astype(BF16)
        parts = jnp.dot(tri, jnp.concatenate([hi, mid, low], axis=1), preferred_element_type=F32)
        bc = parts[:, :B_KEY_DIM] + (parts[:, B_KEY_DIM:2 * B_KEY_DIM] + parts[:, 2 * B_KEY_DIM:])

        attn = jnp.zeros((c_, c_), F32)
        half = c_ // 2
        while half >= HG_MIN_LEVEL:
            blk = 2 * half
            ref_rows = [jnp.broadcast_to(bc[b0 + half - 1:b0 + half, :], (blk, B_KEY_DIM)) for b0 in range(0, c_, blk)]
            ref = jnp.concatenate(ref_rows, axis=0) if len(ref_rows) > 1 else ref_rows[0]
            dec = jnp.exp(-jnp.abs(bc - ref))
            s = lax.dot_general((q * dec).astype(BF16), (kk * dec).astype(BF16), (((1,), (1,)), ((), ())),
                                preferred_element_type=F32)
            keep = jnp.logical_and(jnp.logical_and(xor >= half, xor < blk), ti > si)
            attn = attn + jnp.where(keep, s, 0.0)
            half //= 2
        o = jnp.dot(attn.astype(BF16), vb, preferred_element_type=F32)

        for delta in range(HG_MIN_LEVEL):
            if delta == 0:
                w = q * kk
                vd = v
            else:
                valid = jnp.bitwise_and(row, HG_MIN_LEVEL - 1) >= delta
                rel = jnp.where(valid, bc - pltpu.roll(bc, delta, 0), 0.0)
                w = jnp.where(valid, q * pltpu.roll(kk, delta, 0) * jnp.exp(rel), 0.0)
                vd = pltpu.roll(v, delta, 0)
            o = o + jnp.sum(w, axis=-1, keepdims=True) * vd

        st = st_ref[hp]
        o = o + lax.dot_general((q * jnp.exp(bc)).astype(BF16), st.astype(BF16), (((1,), (1,)), ((), ())),
                                preferred_element_type=F32)
        b_last = bc[c_ - 1:c_, :]
        kdec = (kk * jnp.exp(b_last - bc)).astype(BF16)
        upd = lax.dot_general(vb, kdec, (((0,), (0,)), ((), ())), preferred_element_type=F32)
        st_ref[hp] = st * jnp.exp(b_last) + upd

        o = o * lax.rsqrt(jnp.mean(o * o, axis=-1, keepdims=True) + NORM_EPS)
        o_ref[sl, ln] = (o * g_all[:, ln] * jax.nn.silu(gb_ref[sl, ln])).astype(o_ref.dtype)

    def chunks(i, carry):
        for u in range(HG_CHUNKS_PER_ITER):
            for hp in range(HG_HEADS_PER_STEP):
                chunk(i * HG_CHUNKS_PER_ITER + u, hp)
        return carry

    lax.fori_loop(0, HG_TB // c_ // HG_CHUNKS_PER_ITER, chunks, 0)


def _hgrn2(proj, lb_logits, norm_g, seq):
    hp = HG_HEADS_PER_STEP
    width = hp * B_KEY_DIM
    col = lambda off: pl.BlockSpec((HG_TB, width), lambda h, t: (t, off // hp + h))
    n_lb = lb_logits.shape[0]
    return pl.pallas_call(
        _hgrn_body,
        grid=(B_HEADS // hp, seq // HG_TB),
        in_specs=[col(_QB_BLK), col(_FB_BLK), col(_IB_BLK), col(_GB_BLK),
                  pl.BlockSpec((n_lb, width), lambda h, t: (0, h)),
                  pl.BlockSpec((1, width), lambda h, t: (0, h))],
        out_specs=pl.BlockSpec((HG_TB, width), lambda h, t: (t, h)),
        out_shape=jax.ShapeDtypeStruct((seq, B_WIDTH), BF16),
        scratch_shapes=[pltpu.VMEM((hp, B_KEY_DIM, B_KEY_DIM), F32)],
        compiler_params=_params("arbitrary", "arbitrary"),
        name="hgrn2",
    )(proj, proj, proj, proj, lb_logits.astype(F32), norm_g.reshape(1, B_WIDTH).astype(F32))


MERGE_GATE_BLK = 512


def _merge_body(oa_ref, ob_ref, *rest):
    n_g = (len(rest) - 5) // 2
    ga_refs, gb_refs = rest[:n_g], rest[n_g:2 * n_g]
    wa_ref, wb_ref, o_ref, wa_s, wb_s = rest[2 * n_g:]

    @pl.when(pl.program_id(1) == 0)
    def _():
        wa_s[...] = wa_ref[...].astype(BF16)
        wb_s[...] = wb_ref[...].astype(BF16)

    ya = jnp.dot(oa_ref[...], wa_s[...], preferred_element_type=F32)
    yb = jnp.dot(ob_ref[...], wb_s[...], preferred_element_type=F32)
    ga = jnp.concatenate([r[...] for r in ga_refs], axis=1)
    gb = jnp.concatenate([r[...] for r in gb_refs], axis=1)
    o_ref[...] = (jax.nn.sigmoid(ga) * ya + jax.nn.sigmoid(gb) * yb).astype(o_ref.dtype)


def _branch_merge(oa, ob, proj, wa, wb, tm=512, tn=1024):
    m = oa.shape[0]
    gw = MERGE_GATE_BLK
    n_g = tn // gw
    ga0, gb0 = _GATE_A_COL // gw, _GATE_B_COL // gw
    gate = lambda off, u: pl.BlockSpec((tm, gw), lambda j, i: (i, off + j * n_g + u))
    return pl.pallas_call(
        _merge_body,
        grid=(D_MODEL // tn, m // tm),
        in_specs=[pl.BlockSpec((tm, A_WIDTH), lambda j, i: (i, 0)), pl.BlockSpec((tm, B_WIDTH), lambda j, i: (i, 0)),
                  *[gate(ga0, u) for u in range(n_g)], *[gate(gb0, u) for u in range(n_g)],
                  pl.BlockSpec((A_WIDTH, tn), lambda j, i: (0, j)), pl.BlockSpec((B_WIDTH, tn), lambda j, i: (0, j))],
        out_specs=pl.BlockSpec((tm, tn), lambda j, i: (i, j)),
        out_shape=jax.ShapeDtypeStruct((m, D_MODEL), BF16),
        scratch_shapes=[pltpu.VMEM((A_WIDTH, tn), BF16), pltpu.VMEM((B_WIDTH, tn), BF16)],
        compiler_params=_params("arbitrary", "arbitrary"),
        name="branch_merge",
    )(oa, ob, *([proj] * (2 * n_g)), wa, wb)


def _router_body(h_ref, g_ref, rw_ref, rb_ref, hn_ref, eid_ref, wk_ref):
    hn = _rmsnorm_rows(h_ref[...], g_ref[...])
    hn_ref[...] = _pack_bf16_pairs(hn)
    rw = rw_ref[...]
    hn_hi = hn.astype(BF16)
    hn_lo = (hn - hn_hi.astype(F32)).astype(BF16)
    rw_hi = rw.astype(BF16)
    rw_lo = (rw - rw_hi.astype(F32)).astype(BF16)
    logits = (jnp.dot(hn_hi, rw_hi, preferred_element_type=F32)
              + (jnp.dot(hn_lo, rw_hi, preferred_element_type=F32)
                 + jnp.dot(hn_hi, rw_lo, preferred_element_type=F32))) + rb_ref[...]
    lane = lax.broadcasted_iota(jnp.int32, logits.shape, 1)
    neg = jnp.float32(-jnp.inf)
    big = jnp.int32(LANES)

    is_g = lane < N_GROUPS
    lg = jnp.where(is_g, logits, neg)
    mg = jnp.max(lg, axis=-1, keepdims=True)
    g_idx = jnp.min(jnp.where(lg == mg, lane, big), axis=-1, keepdims=True)
    pg_top = 1.0 / jnp.sum(jnp.where(is_g, jnp.exp(lg - mg), 0.0), axis=-1, keepdims=True)

    lo = N_GROUPS + g_idx * EXPERTS_PER_GROUP
    in_grp = jnp.logical_and(lane >= lo, lane < lo + EXPERTS_PER_GROUP)
    le = jnp.where(in_grp, logits, neg)
    v1 = jnp.max(le, axis=-1, keepdims=True)
    i1 = jnp.min(jnp.where(le == v1, lane, big), axis=-1, keepdims=True)
    le2 = jnp.where(lane == i1, neg, le)
    v2 = jnp.max(le2, axis=-1, keepdims=True)
    i2 = jnp.min(jnp.where(le2 == v2, lane, big), axis=-1, keepdims=True)
    e2 = jnp.exp(v2 - v1)
    w1 = pg_top / (1.0 + e2)
    w2 = pg_top * e2 / (1.0 + e2)
    eid_ref[...] = jnp.where(lane == 0, i1 - N_GROUPS, jnp.where(lane == 1, i2 - N_GROUPS, 0))
    wk_ref[...] = jnp.where(lane == 0, w1, jnp.where(lane == 1, w2, 0.0))


def _router(h, g, rw, rb, tm=256):
    m, d = h.shape
    row = lambda w: pl.BlockSpec((tm, w), lambda i: (i, 0))
    return pl.pallas_call(
        _router_body,
        grid=(m // tm,),
        in_specs=[row(d), pl.BlockSpec((1, d), lambda i: (0, 0)), pl.BlockSpec((d, LANES), lambda i: (0, 0)),
                  pl.BlockSpec((1, LANES), lambda i: (0, 0))],
        out_specs=[row(d // 2), row(LANES), row(LANES)],
        out_shape=[jax.ShapeDtypeStruct((m, d // 2), jnp.uint32), jax.ShapeDtypeStruct((m, LANES), jnp.int32),
                   jax.ShapeDtypeStruct((m, LANES), F32)],
        compiler_params=_params("arbitrary"),
        name="router",
    )(h, g.reshape(1, d).astype(F32), rw, rb)


MOE_TM = 256
ROW_DMA_PRIORITIES = (0, 1)


def _moe_body(tile_e_ref, tile_blk_ref, tile_rows_ref, n_used_ref,
              src0_ref, src_next_ref, dst_prev_ref, hn_ref, w1_ref, w3_ref, w2_ref, y_ref,
              w1_s, w3_s, w2_s, x0, x1, o0, o1, gsem, ssem):
    j = pl.program_id(0)
    n_tiles = pl.num_programs(0)
    n_used = n_used_ref[0]
    xs = (x0, x1)
    os_ = (o0, o1)

    def rows_of(t):
        return jnp.where(t >= 0, tile_rows_ref[jnp.clip(t, 0, n_tiles - 1)], 0)

    def gather_issue(idx_ref, b, n):
        for r in range(MOE_TM):
            @pl.when(r < n)
            def _(r=r):
                pltpu.make_async_copy(hn_ref.at[idx_ref[0, 0, r]], xs[b].at[r],
                                      gsem.at[b]).start(priority=ROW_DMA_PRIORITIES[r % 2])

    def scatter_issue(idx_ref, b, n):
        for r in range(MOE_TM):
            @pl.when(r < n)
            def _(r=r):
                pltpu.make_async_copy(os_[b].at[r], y_ref.at[idx_ref[0, 0, r]],
                                      ssem.at[b]).start(priority=ROW_DMA_PRIORITIES[r % 2])

    def wait_rows(make_copy, n):
        p = MOE_TM
        while p >= 1:
            @pl.when(jnp.bitwise_and(n, p) != 0)
            def _(p=p):
                make_copy(p).wait()
            p //= 2

    def gather_wait(b, n):
        wait_rows(lambda p: pltpu.make_async_copy(hn_ref.at[pl.ds(0, p)], xs[b].at[pl.ds(0, p)], gsem.at[b]), n)

    def scatter_wait(b, n):
        wait_rows(lambda p: pltpu.make_async_copy(os_[b].at[pl.ds(0, p)], y_ref.at[pl.ds(0, p)], ssem.at[b]), n)

    @pl.when(j == 0)
    def _():
        x0[...] = jnp.zeros_like(x0)
        x1[...] = jnp.zeros_like(x1)
        gather_issue(src0_ref, 0, rows_of(0))

    for b in (0, 1):
        @pl.when(jnp.logical_and(j < n_used, j % 2 == b))
        def _(b=b):
            gather_wait(b, rows_of(j))
            scatter_wait(b, rows_of(j - 2))

            prev_e = tile_e_ref[jnp.maximum(j - 1, 0)]

            @pl.when(jnp.logical_or(j == 0, tile_e_ref[j] != prev_e))
            def _():
                w1_s[...] = w1_ref[0].astype(BF16)
                w3_s[...] = w3_ref[0].astype(BF16)
                w2_s[...] = w2_ref[0].astype(BF16)

            gather_issue(src_next_ref, 1 - b, rows_of(j + 1))
            scatter_issue(dst_prev_ref, 1 - b, rows_of(j - 1))
            x = _unpack_bf16_pairs(xs[b][...]).astype(BF16)
            hid = (jax.nn.silu(jnp.dot(x, w1_s[...], preferred_element_type=F32))
                   * jnp.dot(x, w3_s[...], preferred_element_type=F32))
            os_[b][...] = _pack_bf16_pairs(jnp.dot(hid.astype(BF16), w2_s[...], preferred_element_type=F32))

    for b in (0, 1):
        @pl.when(jnp.logical_and(j == n_used, j % 2 == b))
        def _(b=b):
            scatter_wait(b, rows_of(j - 2))
            scatter_issue(dst_prev_ref, 1 - b, rows_of(j - 1))
            scatter_wait(1 - b, rows_of(j - 1))


def _moe_experts(hn, w1, w3, w2, tile_e, tile_blk, tile_rows, n_used, src, dst, n_tokens):
    dp = hn.shape[1]
    d = 2 * dp
    n_tiles = tile_e.shape[0]
    ff = w1.shape[-1]
    idx_block = lambda fn: pl.BlockSpec((1, 1, MOE_TM), fn, memory_space=pltpu.SMEM)
    grid_spec = pltpu.PrefetchScalarGridSpec(
        num_scalar_prefetch=4,
        grid=(n_tiles,),
        in_specs=[
            idx_block(lambda j, te, tb, tr, nu: (0, 0, 0)),
            idx_block(lambda j, te, tb, tr, nu: (tb[jnp.minimum(j + 1, n_tiles - 1)], 0, 0)),
            idx_block(lambda j, te, tb, tr, nu: (tb[jnp.maximum(j - 1, 0)], 0, 0)),
            pl.BlockSpec(memory_space=pl.ANY),
            pl.BlockSpec((1, d, ff), lambda j, te, tb, tr, nu: (te[j], 0, 0)),
            pl.BlockSpec((1, d, ff), lambda j, te, tb, tr, nu: (te[j], 0, 0)),
            pl.BlockSpec((1, ff, d), lambda j, te, tb, tr, nu: (te[j], 0, 0)),
        ],
        out_specs=pl.BlockSpec(memory_space=pl.ANY),
        scratch_shapes=[pltpu.VMEM((d, ff), BF16), pltpu.VMEM((d, ff), BF16), pltpu.VMEM((ff, d), BF16),
                        pltpu.VMEM((MOE_TM, dp), jnp.uint32), pltpu.VMEM((MOE_TM, dp), jnp.uint32),
                        pltpu.VMEM((MOE_TM, dp), jnp.uint32), pltpu.VMEM((MOE_TM, dp), jnp.uint32),
                        pltpu.SemaphoreType.DMA((2,)), pltpu.SemaphoreType.DMA((2,))],
    )
    src3 = src.reshape(n_tiles, 1, MOE_TM)
    return pl.pallas_call(
        _moe_body,
        grid_spec=grid_spec,
        out_shape=jax.ShapeDtypeStruct((2 * n_tokens, dp), jnp.uint32),
        compiler_params=_params("arbitrary"),
        name="moe_experts",
    )(tile_e, tile_blk, tile_rows, n_used, src3, src3, dst.reshape(n_tiles, 1, MOE_TM), hn, w1, w3, w2)


def _combine_body(h_ref, y0_ref, y1_ref, wk_ref, g_ref, o_ref):
    wk = wk_ref[...]
    y = (h_ref[...] + wk[:, 0:1] * _unpack_bf16_pairs(y0_ref[...])
         + wk[:, 1:2] * _unpack_bf16_pairs(y1_ref[...]))
    o_ref[...] = _rmsnorm_rows(y, g_ref[...]).astype(o_ref.dtype)


def _combine(h, y2, wk, g, out_dtype, tm=256):
    m, d = h.shape
    nblk = m // tm
    return pl.pallas_call(
        _combine_body,
        grid=(nblk,),
        in_specs=[pl.BlockSpec((tm, d), lambda i: (i, 0)), pl.BlockSpec((tm, d // 2), lambda i: (i, 0)),
                  pl.BlockSpec((tm, d // 2), lambda i: (nblk + i, 0)), pl.BlockSpec((tm, LANES), lambda i: (i, 0)),
                  pl.BlockSpec((1, d), lambda i: (0, 0))],
        out_specs=pl.BlockSpec((tm, d), lambda i: (i, 0)),
        out_shape=jax.ShapeDtypeStruct((m, d), out_dtype),
        compiler_params=_params("arbitrary"),
        name="combine_final_norm",
    )(h, y2, y2, wk, g.reshape(1, d).astype(F32))


def _routing_tables(eid, n_tokens):
    n_pairs = 2 * n_tokens
    n_tiles = n_pairs // MOE_TM + N_EXPERTS + 1
    e_flat = eid.reshape(n_pairs)
    order = jnp.argsort(e_flat, stable=True).astype(jnp.int32)
    counts = jnp.sum(e_flat[:, None] == jnp.arange(N_EXPERTS, dtype=jnp.int32)[None, :], axis=0, dtype=jnp.int32)
    tiles_per_e = (counts + MOE_TM - 1) // MOE_TM
    tile_end = jnp.cumsum(tiles_per_e)
    tile_start = tile_end - tiles_per_e
    n_used = tile_end[-1]
    sorted_start = jnp.cumsum(counts) - counts
    j = jnp.arange(n_tiles, dtype=jnp.int32)
    tile_blk = jnp.minimum(j, n_used - 1)
    tile_e = jnp.searchsorted(tile_end, tile_blk, side="right").astype(jnp.int32)
    tile_row0 = (tile_blk - tile_start[tile_e]) * MOE_TM
    tile_rows = jnp.where(j == tile_blk, jnp.clip(counts[tile_e] - tile_row0, 0, MOE_TM), 0).astype(jnp.int32)
    r = jnp.arange(MOE_TM, dtype=jnp.int32)[None, :]
    valid = r < tile_rows[:, None]
    pair = order[jnp.clip((sorted_start[tile_e] + tile_row0)[:, None] + r, 0, n_pairs - 1)]
    src_tok = jnp.where(valid, pair // 2, 0).astype(jnp.int32)
    dst_row = jnp.where(valid, (pair % 2) * n_tokens + pair // 2, 0).astype(jnp.int32)
    return (tile_e, tile_blk.astype(jnp.int32), tile_rows, n_used.reshape(1).astype(jnp.int32),
            src_tok.reshape(-1), dst_row.reshape(-1))


def kernel(x, mix_norm_g, w_in, hgrn_lb_logits, hgrn_norm_g, w_branch_a, w_branch_b, w_out, ffn_norm_g,
           router_w_group, router_b_group, router_w_expert, router_b_expert, expert_w1, expert_w3, expert_w2,
           final_norm_g):
    b, seq, d = x.shape
    assert b == 1 and d == D_MODEL and seq % ATT_SUPER == 0 and w_in.shape == (1, D_MODEL, IN_WIDTH)
    h0 = x.reshape(seq, d).astype(F32)

    xn = _rmsnorm(h0, mix_norm_g[0], BF16)
    proj = _matmul(xn, w_in[0], F32, tm=1024, tn=1280, name="in_proj")
    oa = _dilated_attention(proj, seq)
    ob = _hgrn2(proj, hgrn_lb_logits, hgrn_norm_g[0], seq)
    merged = _branch_merge(oa, ob, proj, w_branch_a[0], w_branch_b[0])
    h1 = _matmul_residual(merged, w_out[0], h0, tm=512, tn=1024, name="out_proj")

    pad = LANES - N_GROUPS - N_EXPERTS
    rw = jnp.concatenate([router_w_group[0], router_w_expert[0], jnp.zeros((d, pad), F32)], axis=1).astype(F32)
    rb = jnp.concatenate([router_b_group[0], router_b_expert[0], jnp.zeros((pad,), F32)]).reshape(1, LANES).astype(F32)
    hn, eid, wk = _router(h1, ffn_norm_g[0], rw, rb)
    tile_e, tile_blk, tile_rows, n_used, src_tok, dst_row = _routing_tables(eid[:, :2], seq)
    w1 = expert_w1[0].reshape(N_EXPERTS, D_MODEL, EXPERT_FF)
    w3 = expert_w3[0].reshape(N_EXPERTS, D_MODEL, EXPERT_FF)
    w2 = expert_w2[0].reshape(N_EXPERTS, EXPERT_FF, D_MODEL)
    y2 = _moe_experts(hn, w1, w3, w2, tile_e, tile_blk, tile_rows, n_used, src_tok, dst_row, seq)

    out = _combine(h1, y2, wk, final_norm_g, x.dtype)
    return out.reshape(b, seq, d)
```

```python
import functools

import jax
import jax.numpy as jnp
from jax import lax
from jax.experimental import pallas as pl
from jax.experimental.pallas import tpu as pltpu

F32 = jnp.float32
BF16 = jnp.bfloat16

D_MODEL = 2048
A_HEADS = 12
A_HEAD_DIM = 128
A_WIDTH = A_HEADS * A_HEAD_DIM
A_SCALE = A_HEAD_DIM ** -0.5
DILATED_CONFIGS = ((128, 1), (512, 4), (2048, 16))
B_HEADS = 8
B_KEY_DIM = 128
B_WIDTH = B_HEADS * B_KEY_DIM
N_GROUPS = 4
EXPERTS_PER_GROUP = 8
N_EXPERTS = N_GROUPS * EXPERTS_PER_GROUP
EXPERT_FF = 512
NORM_EPS = 1e-6

LANES = 128
SUBLANES = 8
VMEM_LIMIT = 56 * 1024 * 1024

_QA_BLK = 0
_KA_BLK = A_HEADS
_VA_BLK = 2 * A_HEADS
_QB_BLK = 3 * A_HEADS
_FB_BLK = _QB_BLK + B_HEADS
_IB_BLK = _FB_BLK + B_HEADS
_GB_BLK = _IB_BLK + B_HEADS
_GATE_A_COL = 3 * A_WIDTH + 4 * B_WIDTH
_GATE_B_COL = _GATE_A_COL + D_MODEL
IN_WIDTH = _GATE_B_COL + D_MODEL


def _params(*sem):
    return pltpu.CompilerParams(dimension_semantics=sem, vmem_limit_bytes=VMEM_LIMIT)


def _rmsnorm_rows(x, g):
    ms = jnp.mean(x * x, axis=-1, keepdims=True)
    return x * lax.rsqrt(ms + NORM_EPS) * g


def _pack_bf16_pairs(x):
    w = x.shape[1] // 2
    lo = pltpu.bitcast(x[:, :w].astype(BF16).astype(F32), jnp.uint32)
    hi = pltpu.bitcast(x[:, w:].astype(BF16).astype(F32), jnp.uint32)
    return jnp.bitwise_or(jnp.bitwise_and(hi, jnp.uint32(0xFFFF0000)), jnp.right_shift(lo, jnp.uint32(16)))


def _unpack_bf16_pairs(u):
    lo = pltpu.bitcast(jnp.left_shift(u, jnp.uint32(16)), F32)
    hi = pltpu.bitcast(jnp.bitwise_and(u, jnp.uint32(0xFFFF0000)), F32)
    return jnp.concatenate([lo, hi], axis=1)


def _rmsnorm_body(x_ref, g_ref, o_ref):
    o_ref[...] = _rmsnorm_rows(x_ref[...].astype(F32), g_ref[...]).astype(o_ref.dtype)


def _rmsnorm(x, g, out_dtype, tm=512):
    m, d = x.shape
    return pl.pallas_call(
        _rmsnorm_body,
        grid=(m // tm,),
        in_specs=[pl.BlockSpec((tm, d), lambda i: (i, 0)), pl.BlockSpec((1, d), lambda i: (0, 0))],
        out_specs=pl.BlockSpec((tm, d), lambda i: (i, 0)),
        out_shape=jax.ShapeDtypeStruct((m, d), out_dtype),
        compiler_params=_params("arbitrary"),
        name="rmsnorm",
    )(x, g.reshape(1, d).astype(F32))


def _matmul_body(a_ref, w_ref, o_ref, wb_ref):
    @pl.when(pl.program_id(1) == 0)
    def _():
        wb_ref[...] = w_ref[...].astype(BF16)

    o_ref[...] = jnp.dot(a_ref[...], wb_ref[...], preferred_element_type=F32).astype(o_ref.dtype)


def _matmul(a, w, out_dtype, tm, tn, name):
    m, k = a.shape
    n = w.shape[1]
    return pl.pallas_call(
        _matmul_body,
        grid=(n // tn, m // tm),
        in_specs=[pl.BlockSpec((tm, k), lambda j, i: (i, 0)), pl.BlockSpec((k, tn), lambda j, i: (0, j))],
        out_specs=pl.BlockSpec((tm, tn), lambda j, i: (i, j)),
        out_shape=jax.ShapeDtypeStruct((m, n), out_dtype),
        scratch_shapes=[pltpu.VMEM((k, tn), BF16)],
        compiler_params=_params("arbitrary", "arbitrary"),
        name=name,
    )(a, w)


def _matmul_residual_body(a_ref, w_ref, r_ref, o_ref, wb_ref):
    @pl.when(pl.program_id(1) == 0)
    def _():
        wb_ref[...] = w_ref[...].astype(BF16)

    o_ref[...] = r_ref[...] + jnp.dot(a_ref[...], wb_ref[...], preferred_element_type=F32)


def _matmul_residual(a, w, res, tm, tn, name):
    m, k = a.shape
    n = w.shape[1]
    return pl.pallas_call(
        _matmul_residual_body,
        grid=(n // tn, m // tm),
        in_specs=[pl.BlockSpec((tm, k), lambda j, i: (i, 0)), pl.BlockSpec((k, tn), lambda j, i: (0, j)),
                  pl.BlockSpec((tm, tn), lambda j, i: (i, j))],
        out_specs=pl.BlockSpec((tm, tn), lambda j, i: (i, j)),
        out_shape=jax.ShapeDtypeStruct((m, n), F32),
        scratch_shapes=[pltpu.VMEM((k, tn), BF16)],
        compiler_params=_params("arbitrary", "arbitrary"),
        name=name,
    )(a, w, res)


LOG2_E = 1.4426950408889634
LN_2 = 0.6931471805599453
ATT_BLK = 128
ATT_SUPER = 2048
ATT_UNROLL = 16


def _attn_body(q_ref, k_ref, v_ref, o_ref, o_scr, lse_scr, bias_scr):
    sb = pl.program_id(1)
    row0 = sb * ATT_SUPER
    diff = (lax.broadcasted_iota(jnp.int32, (ATT_BLK, 2 * ATT_BLK), 1)
            - lax.broadcasted_iota(jnp.int32, (ATT_BLK, 2 * ATT_BLK), 0))
    neg = jnp.float32(-jnp.inf)
    bias_scr[0] = jnp.where(jnp.logical_and(diff >= 0, diff <= ATT_BLK), 0.0, neg)
    bias_scr[1] = jnp.where(diff <= 0, 0.0, neg)

    for c, (window, dil) in enumerate(DILATED_CONFIGS):
        span = ATT_BLK * dil
        tiles_per_res = ATT_SUPER // span

        def tile(t, carry, c=c, dil=dil, span=span, tiles_per_res=tiles_per_res):
            n = t % tiles_per_res
            r = t // tiles_per_res
            q_start = n * span + r
            first = jnp.logical_and(sb == 0, n == 0)
            kv_start = row0 + q_start - jnp.where(first, 0, span)
            if dil == 1:
                q_start = pl.multiple_of(q_start, ATT_BLK)
                kv_start = pl.multiple_of(kv_start, ATT_BLK)
                qs = pl.ds(q_start, ATT_BLK)
                ks = pl.ds(kv_start, 2 * ATT_BLK)
            else:
                qs = pl.ds(q_start, ATT_BLK, stride=dil)
                ks = pl.ds(kv_start, 2 * ATT_BLK, stride=dil)
            q = (q_ref[qs, :] * (A_SCALE * LOG2_E)).astype(BF16)
            k = k_ref[ks, :].astype(BF16)
            v = v_ref[ks, :].astype(BF16)
            s = lax.dot_general(q, k, (((1,), (1,)), ((), ())), preferred_element_type=F32)
            s = s + bias_scr[first.astype(jnp.int32)]
            mx = jnp.max(s, axis=-1, keepdims=True)
            p = jnp.exp2(s - mx)
            den = jnp.sum(p, axis=-1, keepdims=True)
            acc = jnp.dot(p.astype(BF16), v, preferred_element_type=F32)
            o_scr[c, qs, :] = acc / den
            lse_scr[c, qs, :] = jnp.broadcast_to(mx * LN_2 + jnp.log(den), (ATT_BLK, A_HEAD_DIM))
            return carry

        def tiles(i, carry, tile=tile):
            for u in range(ATT_UNROLL):
                tile(i * ATT_UNROLL + u, carry)
            return carry

        lax.fori_loop(0, ATT_SUPER // ATT_BLK // ATT_UNROLL, tiles, 0)

    rows = 256

    def merge(i, carry):
        sl = pl.ds(pl.multiple_of(i * rows, rows), rows)
        l0, l1, l2 = lse_scr[0, sl, :], lse_scr[1, sl, :], lse_scr[2, sl, :]
        m = jnp.maximum(jnp.maximum(l0, l1), l2)
        w0, w1, w2 = jnp.exp(l0 - m), jnp.exp(l1 - m), jnp.exp(l2 - m)
        num = w0 * o_scr[0, sl, :] + w1 * o_scr[1, sl, :] + w2 * o_scr[2, sl, :]
        o_ref[sl, :] = (num / (w0 + w1 + w2)).astype(o_ref.dtype)
        return carry

    lax.fori_loop(0, ATT_SUPER // rows, merge, 0)


def _dilated_attention(proj, seq):
    n_super = seq // ATT_SUPER
    blk = lambda off: pl.BlockSpec((seq, A_HEAD_DIM), lambda h, s: (0, off + h))
    return pl.pallas_call(
        _attn_body,
        grid=(A_HEADS, n_super),
        in_specs=[pl.BlockSpec((ATT_SUPER, A_HEAD_DIM), lambda h, s: (s, _QA_BLK + h)), blk(_KA_BLK), blk(_VA_BLK)],
        out_specs=pl.BlockSpec((ATT_SUPER, A_HEAD_DIM), lambda h, s: (s, h)),
        out_shape=jax.ShapeDtypeStruct((seq, A_WIDTH), BF16),
        scratch_shapes=[pltpu.VMEM((len(DILATED_CONFIGS), ATT_SUPER, A_HEAD_DIM), F32),
                        pltpu.VMEM((len(DILATED_CONFIGS), ATT_SUPER, A_HEAD_DIM), F32),
                        pltpu.VMEM((2, ATT_BLK, 2 * ATT_BLK), F32)],
        compiler_params=_params("arbitrary", "arbitrary"),
        name="dilated_attention",
    )(proj, proj, proj)


HG_CHUNK = 64
HG_TB = 512
HG_HEADS_PER_STEP = 2
HG_CHUNKS_PER_STACK = 2
HG_STACKS_PER_ITER = 2


def _hgrn_ref_rows(bc, half):
    n, width = bc.shape
    blk = 2 * half
    if blk >= SUBLANES:
        rows = [jnp.broadcast_to(bc[b0 + half - 1:b0 + half, :], (blk, width)) for b0 in range(0, n, blk)]
        return jnp.concatenate(rows, axis=0) if len(rows) > 1 else rows[0]
    sub = lax.broadcasted_iota(jnp.int32, (SUBLANES, width), 0)
    groups = []
    for g0 in range(0, n, SUBLANES):
        grp = bc[g0:g0 + SUBLANES, :]
        if half == 1:
            groups.append(jnp.where(jnp.bitwise_and(sub, 1) == 1, pltpu.roll(grp, 1, 0), grp))
        else:
            assert half == 2 and SUBLANES == 8
            groups.append(jnp.where(sub < 4, jnp.broadcast_to(grp[1:2, :], grp.shape),
                                    jnp.broadcast_to(grp[5:6, :], grp.shape)))
    return jnp.concatenate(groups, axis=0)


def _hgrn_body(qb_ref, fb_ref, ib_ref, gb_ref, lbl_ref, g_ref, o_ref, st_ref):
    @pl.when(pl.program_id(1) == 0)
    def _():
        st_ref[...] = jnp.zeros_like(st_ref)

    c_ = HG_CHUNK
    lbl = lbl_ref[...]
    e = jnp.exp(lbl - jnp.max(lbl, axis=0, keepdims=True))
    lb_all = e[0:1, :] / jnp.sum(e, axis=0, keepdims=True)
    g_all = g_ref[...]

    nh, nc = HG_HEADS_PER_STEP, HG_CHUNKS_PER_STACK
    rows = nc * c_
    n = nh * rows
    ti = lax.broadcasted_iota(jnp.int32, (n, n), 0)
    si = lax.broadcasted_iota(jnp.int32, (n, n), 1)
    xor = jnp.bitwise_xor(ti, si)
    causal = jnp.logical_and(xor < c_, si <= ti)
    tri = jnp.where(causal, 1.0, 0.0).astype(BF16)

    def stack_of(ref, r0):
        return jnp.concatenate([ref[pl.ds(r0, rows), h * B_KEY_DIM:(h + 1) * B_KEY_DIM] for h in range(nh)], axis=0)

    def per_head_rows(x):
        return jnp.concatenate([jnp.broadcast_to(x[:, h * B_KEY_DIM:(h + 1) * B_KEY_DIM], (rows, B_KEY_DIM))
                                for h in range(nh)], axis=0)

    lb = per_head_rows(lb_all)
    gnorm = per_head_rows(g_all)

    def stack(i):
        r0 = pl.multiple_of(i * rows, rows)
        f = lb + (1.0 - lb) * jax.nn.sigmoid(stack_of(fb_ref, r0))
        logf = jnp.log(f)
        kk = 1.0 - f
        q = jax.nn.silu(stack_of(qb_ref, r0))
        v = stack_of(ib_ref, r0)
        vb = v.astype(BF16)
        hi = logf.astype(BF16)
        rem = logf - hi.astype(F32)
        mid = rem.astype(BF16)
        low = (rem - mid.astype(F32)).astype(BF16)
        parts = jnp.dot(tri, jnp.concatenate([hi, mid, low], axis=1), preferred_element_type=F32)
        bc = (parts[:, :B_KEY_DIM] + (parts[:, B_KEY_DIM:2 * B_KEY_DIM] + parts[:, 2 * B_KEY_DIM:])) * LOG2_E

        attn = lax.dot_general(q.astype(BF16), kk.astype(BF16), (((1,), (1,)), ((), ())),
                               preferred_element_type=F32)
        half = 1
        while half < c_:
            dec = jnp.exp2(-jnp.abs(bc - _hgrn_ref_rows(bc, half)))
            s = lax.dot_general((q * dec).astype(BF16), (kk * dec).astype(BF16), (((1,), (1,)), ((), ())),
                                preferred_element_type=F32)
            attn = jnp.where(xor >= half, s, attn)
            half *= 2
        attn = jnp.where(causal, attn, 0.0)
        o = jnp.dot(attn.astype(BF16), vb, preferred_element_type=F32)

        last = jnp.concatenate([jnp.broadcast_to(bc[a + c_ - 1:a + c_, :], (c_, B_KEY_DIM))
                                for a in range(0, n, c_)], axis=0)
        q_dec = (q * jnp.exp2(bc)).astype(BF16)
        k_dec = (kk * jnp.exp2(last - bc)).astype(BF16)
        st_dec = jnp.exp2(last)
        inter = []
        for h in range(nh):
            st = st_ref[h]
            for c in range(nc):
                a = h * rows + c * c_
                inter.append(lax.dot_general(q_dec[a:a + c_], st.astype(BF16), (((1,), (1,)), ((), ())),
                                             preferred_element_type=F32))
                upd = lax.dot_general(vb[a:a + c_], k_dec[a:a + c_], (((0,), (0,)), ((), ())),
                                      preferred_element_type=F32)
                st = st * st_dec[a:a + 1] + upd
            st_ref[h] = st
        o = o + jnp.concatenate(inter, axis=0)

        o = o * lax.rsqrt(jnp.mean(o * o, axis=-1, keepdims=True) + NORM_EPS)
        res = (o * gnorm * jax.nn.silu(stack_of(gb_ref, r0))).astype(o_ref.dtype)
        for h in range(nh):
            o_ref[pl.ds(r0, rows), h * B_KEY_DIM:(h + 1) * B_KEY_DIM] = res[h * rows:(h + 1) * rows]

    def stacks(i, carry):
        for u in range(HG_STACKS_PER_ITER):
            stack(i * HG_STACKS_PER_ITER + u)
        return carry

    lax.fori_loop(0, HG_TB // rows // HG_STACKS_PER_ITER, stacks, 0)


def _hgrn2(proj, lb_logits, norm_g, seq):
    hp = HG_HEADS_PER_STEP
    width = hp * B_KEY_DIM
    col = lambda off: pl.BlockSpec((HG_TB, width), lambda h, t: (t, off // hp + h))
    n_lb = lb_logits.shape[0]
    return pl.pallas_call(
        _hgrn_body,
        grid=(B_HEADS // hp, seq // HG_TB),
        in_specs=[col(_QB_BLK), col(_FB_BLK), col(_IB_BLK), col(_GB_BLK),
                  pl.BlockSpec((n_lb, width), lambda h, t: (0, h)),
                  pl.BlockSpec((1, width), lambda h, t: (0, h))],
        out_specs=pl.BlockSpec((HG_TB, width), lambda h, t: (t, h)),
        out_shape=jax.ShapeDtypeStruct((seq, B_WIDTH), BF16),
        scratch_shapes=[pltpu.VMEM((hp, B_KEY_DIM, B_KEY_DIM), F32)],
        compiler_params=_params("arbitrary", "arbitrary"),
        name="hgrn2",
    )(proj, proj, proj, proj, lb_logits.astype(F32), norm_g.reshape(1, B_WIDTH).astype(F32))


MERGE_GATE_BLK = 512


def _merge_body(oa_ref, ob_ref, *rest):
    n_g = (len(rest) - 5) // 2
    ga_refs, gb_refs = rest[:n_g], rest[n_g:2 * n_g]
    wa_ref, wb_ref, o_ref, wa_s, wb_s = rest[2 * n_g:]

    @pl.when(pl.program_id(1) == 0)
    def _():
        wa_s[...] = wa_ref[...].astype(BF16)
        wb_s[...] = wb_ref[...].astype(BF16)

    ya = jnp.dot(oa_ref[...], wa_s[...], preferred_element_type=F32)
    yb = jnp.dot(ob_ref[...], wb_s[...], preferred_element_type=F32)
    ga = jnp.concatenate([r[...] for r in ga_refs], axis=1)
    gb = jnp.concatenate([r[...] for r in gb_refs], axis=1)
    o_ref[...] = (jax.nn.sigmoid(ga) * ya + jax.nn.sigmoid(gb) * yb).astype(o_ref.dtype)


def _branch_merge(oa, ob, proj, wa, wb, tm=512, tn=1024):
    m = oa.shape[0]
    gw = MERGE_GATE_BLK
    n_g = tn // gw
    ga0, gb0 = _GATE_A_COL // gw, _GATE_B_COL // gw
    gate = lambda off, u: pl.BlockSpec((tm, gw), lambda j, i: (i, off + j * n_g + u))
    return pl.pallas_call(
        _merge_body,
        grid=(D_MODEL // tn, m // tm),
        in_specs=[pl.BlockSpec((tm, A_WIDTH), lambda j, i: (i, 0)), pl.BlockSpec((tm, B_WIDTH), lambda j, i: (i, 0)),
                  *[gate(ga0, u) for u in range(n_g)], *[gate(gb0, u) for u in range(n_g)],
                  pl.BlockSpec((A_WIDTH, tn), lambda j, i: (0, j)), pl.BlockSpec((B_WIDTH, tn), lambda j, i: (0, j))],
        out_specs=pl.BlockSpec((tm, tn), lambda j, i: (i, j)),
        out_shape=jax.ShapeDtypeStruct((m, D_MODEL), BF16),
        scratch_shapes=[pltpu.VMEM((A_WIDTH, tn), BF16), pltpu.VMEM((B_WIDTH, tn), BF16)],
        compiler_params=_params("arbitrary", "arbitrary"),
        name="branch_merge",
    )(oa, ob, *([proj] * (2 * n_g)), wa, wb)


def _router_body(h_ref, g_ref, rw_ref, rb_ref, hn_ref, eid_ref, wk_ref):
    hn = _rmsnorm_rows(h_ref[...], g_ref[...])
    hn_ref[...] = _pack_bf16_pairs(hn)
    rw = rw_ref[...]
    hn_hi = hn.astype(BF16)
    hn_lo = (hn - hn_hi.astype(F32)).astype(BF16)
    rw_hi = rw.astype(BF16)
    rw_lo = (rw - rw_hi.astype(F32)).astype(BF16)
    logits = (jnp.dot(hn_hi, rw_hi, preferred_element_type=F32)
              + (jnp.dot(hn_lo, rw_hi, preferred_element_type=F32)
                 + jnp.dot(hn_hi, rw_lo, preferred_element_type=F32))) + rb_ref[...]
    lane = lax.broadcasted_iota(jnp.int32, logits.shape, 1)
    neg = jnp.float32(-jnp.inf)
    big = jnp.int32(LANES)

    is_g = lane < N_GROUPS
    lg = jnp.where(is_g, logits, neg)
    mg = jnp.max(lg, axis=-1, keepdims=True)
    g_idx = jnp.min(jnp.where(lg == mg, lane, big), axis=-1, keepdims=True)
    pg_top = 1.0 / jnp.sum(jnp.where(is_g, jnp.exp(lg - mg), 0.0), axis=-1, keepdims=True)

    lo = N_GROUPS + g_idx * EXPERTS_PER_GROUP
    in_grp = jnp.logical_and(lane >= lo, lane < lo + EXPERTS_PER_GROUP)
    le = jnp.where(in_grp, logits, neg)
    v1 = jnp.max(le, axis=-1, keepdims=True)
    i1 = jnp.min(jnp.where(le == v1, lane, big), axis=-1, keepdims=True)
    le2 = jnp.where(lane == i1, neg, le)
    v2 = jnp.max(le2, axis=-1, keepdims=True)
    i2 = jnp.min(jnp.where(le2 == v2, lane, big), axis=-1, keepdims=True)
    e2 = jnp.exp(v2 - v1)
    w1 = pg_top / (1.0 + e2)
    w2 = pg_top * e2 / (1.0 + e2)
    eid_ref[...] = jnp.where(lane == 0, i1 - N_GROUPS, jnp.where(lane == 1, i2 - N_GROUPS, 0))
    wk_ref[...] = jnp.where(lane == 0, w1, jnp.where(lane == 1, w2, 0.0))


def _router(h, g, rw, rb, tm=256):
    m, d = h.shape
    row = lambda w: pl.BlockSpec((tm, w), lambda i: (i, 0))
    return pl.pallas_call(
        _router_body,
        grid=(m // tm,),
        in_specs=[row(d), pl.BlockSpec((1, d), lambda i: (0, 0)), pl.BlockSpec((d, LANES), lambda i: (0, 0)),
                  pl.BlockSpec((1, LANES), lambda i: (0, 0))],
        out_specs=[row(d // 2), row(LANES), row(LANES)],
        out_shape=[jax.ShapeDtypeStruct((m, d // 2), jnp.uint32), jax.ShapeDtypeStruct((m, LANES), jnp.int32),
                   jax.ShapeDtypeStruct((m, LANES), F32)],
        compiler_params=_params("arbitrary"),
        name="router",
    )(h, g.reshape(1, d).astype(F32), rw, rb)


MOE_TM = 256
ROW_DMA_PRIORITIES = (0, 1)


def _moe_body(tile_e_ref, tile_blk_ref, tile_rows_ref, n_used_ref,
              src0_ref, src_next_ref, dst_prev_ref, hn_ref, w1_ref, w3_ref, w2_ref, y_ref,
              w1_s, w3_s, w2_s, x0, x1, o0, o1, gsem, ssem):
    j = pl.program_id(0)
    n_tiles = pl.num_programs(0)
    n_used = n_used_ref[0]
    xs = (x0, x1)
    os_ = (o0, o1)

    def rows_of(t):
        return jnp.where(t >= 0, tile_rows_ref[jnp.clip(t, 0, n_tiles - 1)], 0)

    def gather_issue(idx_ref, b, n):
        for r in range(MOE_TM):
            @pl.when(r < n)
            def _(r=r):
                pltpu.make_async_copy(hn_ref.at[idx_ref[0, 0, r]], xs[b].at[r],
                                      gsem.at[b]).start(priority=ROW_DMA_PRIORITIES[r % 2])

    def scatter_issue(idx_ref, b, n):
        for r in range(MOE_TM):
            @pl.when(r < n)
            def _(r=r):
                pltpu.make_async_copy(os_[b].at[r], y_ref.at[idx_ref[0, 0, r]],
                                      ssem.at[b]).start(priority=ROW_DMA_PRIORITIES[r % 2])

    def wait_rows(make_copy, n):
        p = MOE_TM
        while p >= 1:
            @pl.when(jnp.bitwise_and(n, p) != 0)
            def _(p=p):
                make_copy(p).wait()
            p //= 2

    def gather_wait(b, n):
        wait_rows(lambda p: pltpu.make_async_copy(hn_ref.at[pl.ds(0, p)], xs[b].at[pl.ds(0, p)], gsem.at[b]), n)

    def scatter_wait(b, n):
        wait_rows(lambda p: pltpu.make_async_copy(os_[b].at[pl.ds(0, p)], y_ref.at[pl.ds(0, p)], ssem.at[b]), n)

    @pl.when(j == 0)
    def _():
        x0[...] = jnp.zeros_like(x0)
        x1[...] = jnp.zeros_like(x1)
        gather_issue(src0_ref, 0, rows_of(0))

    for b in (0, 1):
        @pl.when(jnp.logical_and(j < n_used, j % 2 == b))
        def _(b=b):
            gather_wait(b, rows_of(j))
            scatter_wait(b, rows_of(j - 2))

            prev_e = tile_e_ref[jnp.maximum(j - 1, 0)]

            @pl.when(jnp.logical_or(j == 0, tile_e_ref[j] != prev_e))
            def _():
                w1_s[...] = w1_ref[0].astype(BF16)
                w3_s[...] = w3_ref[0].astype(BF16)
                w2_s[...] = w2_ref[0].astype(BF16)

            gather_issue(src_next_ref, 1 - b, rows_of(j + 1))
            scatter_issue(dst_prev_ref, 1 - b, rows_of(j - 1))
            x = _unpack_bf16_pairs(xs[b][...]).astype(BF16)
            hid = (jax.nn.silu(jnp.dot(x, w1_s[...], preferred_element_type=F32))
                   * jnp.dot(x, w3_s[...], preferred_element_type=F32))
            os_[b][...] = _pack_bf16_pairs(jnp.dot(hid.astype(BF16), w2_s[...], preferred_element_type=F32))

    for b in (0, 1):
        @pl.when(jnp.logical_and(j == n_used, j % 2 == b))
        def _(b=b):
            scatter_wait(b, rows_of(j - 2))
            scatter_issue(dst_prev_ref, 1 - b, rows_of(j - 1))
            scatter_wait(1 - b, rows_of(j - 1))


def _moe_experts(hn, w1, w3, w2, tile_e, tile_blk, tile_rows, n_used, src, dst, n_tokens):
    dp = hn.shape[1]
    d = 2 * dp
    n_tiles = tile_e.shape[0]
    ff = w1.shape[-1]
    idx_block = lambda fn: pl.BlockSpec((1, 1, MOE_TM), fn, memory_space=pltpu.SMEM)
    grid_spec = pltpu.PrefetchScalarGridSpec(
        num_scalar_prefetch=4,
        grid=(n_tiles,),
        in_specs=[
            idx_block(lambda j, te, tb, tr, nu: (0, 0, 0)),
            idx_block(lambda j, te, tb, tr, nu: (tb[jnp.minimum(j + 1, n_tiles - 1)], 0, 0)),
            idx_block(lambda j, te, tb, tr, nu: (tb[jnp.maximum(j - 1, 0)], 0, 0)),
            pl.BlockSpec(memory_space=pl.ANY),
            pl.BlockSpec((1, d, ff), lambda j, te, tb, tr, nu: (te[j], 0, 0)),
            pl.BlockSpec((1, d, ff), lambda j, te, tb, tr, nu: (te[j], 0, 0)),
            pl.BlockSpec((1, ff, d), lambda j, te, tb, tr, nu: (te[j], 0, 0)),
        ],
        out_specs=pl.BlockSpec(memory_space=pl.ANY),
        scratch_shapes=[pltpu.VMEM((d, ff), BF16), pltpu.VMEM((d, ff), BF16), pltpu.VMEM((ff, d), BF16),
                        pltpu.VMEM((MOE_TM, dp), jnp.uint32), pltpu.VMEM((MOE_TM, dp), jnp.uint32),
                        pltpu.VMEM((MOE_TM, dp), jnp.uint32), pltpu.VMEM((MOE_TM, dp), jnp.uint32),
                        pltpu.SemaphoreType.DMA((2,)), pltpu.SemaphoreType.DMA((2,))],
    )
    src3 = src.reshape(n_tiles, 1, MOE_TM)
    return pl.pallas_call(
        _moe_body,
        grid_spec=grid_spec,
        out_shape=jax.ShapeDtypeStruct((2 * n_tokens, dp), jnp.uint32),
        compiler_params=_params("arbitrary"),
        name="moe_experts",
    )(tile_e, tile_blk, tile_rows, n_used, src3, src3, dst.reshape(n_tiles, 1, MOE_TM), hn, w1, w3, w2)


def _combine_body(h_ref, y0_ref, y1_ref, wk_ref, g_ref, o_ref):
    wk = wk_ref[...]
    y = (h_ref[...] + wk[:, 0:1] * _unpack_bf16_pairs(y0_ref[...])
         + wk[:, 1:2] * _unpack_bf16_pairs(y1_ref[...]))
    o_ref[...] = _rmsnorm_rows(y, g_ref[...]).astype(o_ref.dtype)


def _combine(h, y2, wk, g, out_dtype, tm=256):
    m, d = h.shape
    nblk = m // tm
    return pl.pallas_call(
        _combine_body,
        grid=(nblk,),
        in_specs=[pl.BlockSpec((tm, d), lambda i: (i, 0)), pl.BlockSpec((tm, d // 2), lambda i: (i, 0)),
                  pl.BlockSpec((tm, d // 2), lambda i: (nblk + i, 0)), pl.BlockSpec((tm, LANES), lambda i: (i, 0)),
                  pl.BlockSpec((1, d), lambda i: (0, 0))],
        out_specs=pl.BlockSpec((tm, d), lambda i: (i, 0)),
        out_shape=jax.ShapeDtypeStruct((m, d), out_dtype),
        compiler_params=_params("arbitrary"),
        name="combine_final_norm",
    )(h, y2, y2, wk, g.reshape(1, d).astype(F32))


def _routing_tables(eid, n_tokens):
    n_pairs = 2 * n_tokens
    n_tiles = n_pairs // MOE_TM + N_EXPERTS + 1
    e_flat = eid.reshape(n_pairs)
    order = jnp.argsort(e_flat, stable=True).astype(jnp.int32)
    counts = jnp.sum(e_flat[:, None] == jnp.arange(N_EXPERTS, dtype=jnp.int32)[None, :], axis=0, dtype=jnp.int32)
    tiles_per_e = (counts + MOE_TM - 1) // MOE_TM
    tile_end = jnp.cumsum(tiles_per_e)
    tile_start = tile_end - tiles_per_e
    n_used = tile_end[-1]
    sorted_start = jnp.cumsum(counts) - counts
    j = jnp.arange(n_tiles, dtype=jnp.int32)
    tile_blk = jnp.minimum(j, n_used - 1)
    tile_e = jnp.searchsorted(tile_end, tile_blk, side="right").astype(jnp.int32)
    tile_row0 = (tile_blk - tile_start[tile_e]) * MOE_TM
    tile_rows = jnp.where(j == tile_blk, jnp.clip(counts[tile_e] - tile_row0, 0, MOE_TM), 0).astype(jnp.int32)
    r = jnp.arange(MOE_TM, dtype=jnp.int32)[None, :]
    valid = r < tile_rows[:, None]
    pair = order[jnp.clip((sorted_start[tile_e] + tile_row0)[:, None] + r, 0, n_pairs - 1)]
    src_tok = jnp.where(valid, pair // 2, 0).astype(jnp.int32)
    dst_row = jnp.where(valid, (pair % 2) * n_tokens + pair // 2, 0).astype(jnp.int32)
    return (tile_e, tile_blk.astype(jnp.int32), tile_rows, n_used.reshape(1).astype(jnp.int32),
            src_tok.reshape(-1), dst_row.reshape(-1))


def kernel(x, mix_norm_g, w_in, hgrn_lb_logits, hgrn_norm_g, w_branch_a, w_branch_b, w_out, ffn_norm_g,
           router_w_group, router_b_group, router_w_expert, router_b_expert, expert_w1, expert_w3, expert_w2,
           final_norm_g):
    b, seq, d = x.shape
    assert b == 1 and d == D_MODEL and seq % ATT_SUPER == 0 and w_in.shape == (1, D_MODEL, IN_WIDTH)
    h0 = x.reshape(seq, d).astype(F32)

    xn = _rmsnorm(h0, mix_norm_g[0], BF16)
    proj = _matmul(xn, w_in[0], F32, tm=1024, tn=1280, name="in_proj")
    oa = _dilated_attention(proj, seq)
    ob = _hgrn2(proj, hgrn_lb_logits, hgrn_norm_g[0], seq)
    merged = _branch_merge(oa, ob, proj, w_branch_a[0], w_branch_b[0])
    h1 = _matmul_residual(merged, w_out[0], h0, tm=512, tn=1024, name="out_proj")

    pad = LANES - N_GROUPS - N_EXPERTS
    rw = jnp.concatenate([router_w_group[0], router_w_expert[0], jnp.zeros((d, pad), F32)], axis=1).astype(F32)
    rb = jnp.concatenate([router_b_group[0], router_b_expert[0], jnp.zeros((pad,), F32)]).reshape(1, LANES).astype(F32)
    hn, eid, wk = _router(h1, ffn_norm_g[0], rw, rb)
    tile_e, tile_blk, tile_rows, n_used, src_tok, dst_row = _routing_tables(eid[:, :2], seq)
    w1 = expert_w1[0].reshape(N_EXPERTS, D_MODEL, EXPERT_FF)
    w3 = expert_w3[0].reshape(N_EXPERTS, D_MODEL, EXPERT_FF)
    w2 = expert_w2[0].reshape(N_EXPERTS, EXPERT_FF, D_MODEL)
    y2 = _moe_experts(hn, w1, w3, w2, tile_e, tile_blk, tile_rows, n_used, src_tok, dst_row, seq)

    out = _combine(h1, y2, wk, final_norm_g, x.dtype)
    return out.reshape(b, seq, d)
```

```python
import functools

import jax
import jax.numpy as jnp
from jax import lax
from jax.experimental import pallas as pl
from jax.experimental.pallas import tpu as pltpu

F32 = jnp.float32
BF16 = jnp.bfloat16

D_MODEL = 2048
A_HEADS = 12
A_HEAD_DIM = 128
A_WIDTH = A_HEADS * A_HEAD_DIM
A_SCALE = A_HEAD_DIM ** -0.5
DILATED_CONFIGS = ((128, 1), (512, 4), (2048, 16))
B_HEADS = 8
B_KEY_DIM = 128
B_WIDTH = B_HEADS * B_KEY_DIM
N_GROUPS = 4
EXPERTS_PER_GROUP = 8
N_EXPERTS = N_GROUPS * EXPERTS_PER_GROUP
EXPERT_FF = 512
NORM_EPS = 1e-6

LANES = 128
SUBLANES = 8
VMEM_LIMIT = 56 * 1024 * 1024

_QA_BLK = 0
_KA_BLK = A_HEADS
_VA_BLK = 2 * A_HEADS
_QB_BLK = 3 * A_HEADS
_FB_BLK = _QB_BLK + B_HEADS
_IB_BLK = _FB_BLK + B_HEADS
_GB_BLK = _IB_BLK + B_HEADS
_GATE_A_COL = 3 * A_WIDTH + 4 * B_WIDTH
_GATE_B_COL = _GATE_A_COL + D_MODEL
IN_WIDTH = _GATE_B_COL + D_MODEL


def _params(*sem):
    return pltpu.CompilerParams(dimension_semantics=sem, vmem_limit_bytes=VMEM_LIMIT)


def _rmsnorm_rows(x, g):
    ms = jnp.mean(x * x, axis=-1, keepdims=True)
    return x * lax.rsqrt(ms + NORM_EPS) * g


def _pack_bf16_pairs(x):
    w = x.shape[1] // 2
    lo = pltpu.bitcast(x[:, :w].astype(BF16).astype(F32), jnp.uint32)
    hi = pltpu.bitcast(x[:, w:].astype(BF16).astype(F32), jnp.uint32)
    return jnp.bitwise_or(jnp.bitwise_and(hi, jnp.uint32(0xFFFF0000)), jnp.right_shift(lo, jnp.uint32(16)))


def _unpack_bf16_pairs(u):
    lo = pltpu.bitcast(jnp.left_shift(u, jnp.uint32(16)), F32)
    hi = pltpu.bitcast(jnp.bitwise_and(u, jnp.uint32(0xFFFF0000)), F32)
    return jnp.concatenate([lo, hi], axis=1)


def _rmsnorm_body(x_ref, g_ref, o_ref):
    o_ref[...] = _rmsnorm_rows(x_ref[...].astype(F32), g_ref[...]).astype(o_ref.dtype)


def _rmsnorm(x, g, out_dtype, tm=512):
    m, d = x.shape
    return pl.pallas_call(
        _rmsnorm_body,
        grid=(m // tm,),
        in_specs=[pl.BlockSpec((tm, d), lambda i: (i, 0)), pl.BlockSpec((1, d), lambda i: (0, 0))],
        out_specs=pl.BlockSpec((tm, d), lambda i: (i, 0)),
        out_shape=jax.ShapeDtypeStruct((m, d), out_dtype),
        compiler_params=_params("arbitrary"),
        name="rmsnorm",
    )(x, g.reshape(1, d).astype(F32))


def _matmul_body(a_ref, w_ref, o_ref, wb_ref):
    @pl.when(pl.program_id(1) == 0)
    def _():
        wb_ref[...] = w_ref[...].astype(BF16)

    o_ref[...] = jnp.dot(a_ref[...], wb_ref[...], preferred_element_type=F32).astype(o_ref.dtype)


def _matmul(a, w, out_dtype, tm, tn, name):
    m, k = a.shape
    n = w.shape[1]
    return pl.pallas_call(
        _matmul_body,
        grid=(n // tn, m // tm),
        in_specs=[pl.BlockSpec((tm, k), lambda j, i: (i, 0)), pl.BlockSpec((k, tn), lambda j, i: (0, j))],
        out_specs=pl.BlockSpec((tm, tn), lambda j, i: (i, j)),
        out_shape=jax.ShapeDtypeStruct((m, n), out_dtype),
        scratch_shapes=[pltpu.VMEM((k, tn), BF16)],
        compiler_params=_params("arbitrary", "arbitrary"),
        name=name,
    )(a, w)


LOG2_E = 1.4426950408889634
LN_2 = 0.6931471805599453
ATT_BLK = 128
ATT_SUPER = 2048
ATT_UNROLL = 16


def _attn_body(q_ref, k_ref, v_ref, o_ref, o_scr, lse_scr, bias_scr):
    sb = pl.program_id(1)
    row0 = sb * ATT_SUPER
    diff = (lax.broadcasted_iota(jnp.int32, (ATT_BLK, 2 * ATT_BLK), 1)
            - lax.broadcasted_iota(jnp.int32, (ATT_BLK, 2 * ATT_BLK), 0))
    neg = jnp.float32(-jnp.inf)
    bias_scr[0] = jnp.where(jnp.logical_and(diff >= 0, diff <= ATT_BLK), 0.0, neg)
    bias_scr[1] = jnp.where(diff <= 0, 0.0, neg)

    for c, (window, dil) in enumerate(DILATED_CONFIGS):
        span = ATT_BLK * dil
        tiles_per_res = ATT_SUPER // span

        def tile(t, carry, c=c, dil=dil, span=span, tiles_per_res=tiles_per_res):
            n = t % tiles_per_res
            r = t // tiles_per_res
            q_start = n * span + r
            first = jnp.logical_and(sb == 0, n == 0)
            kv_start = row0 + q_start - jnp.where(first, 0, span)
            if dil == 1:
                q_start = pl.multiple_of(q_start, ATT_BLK)
                kv_start = pl.multiple_of(kv_start, ATT_BLK)
                qs = pl.ds(q_start, ATT_BLK)
                ks = pl.ds(kv_start, 2 * ATT_BLK)
            else:
                qs = pl.ds(q_start, ATT_BLK, stride=dil)
                ks = pl.ds(kv_start, 2 * ATT_BLK, stride=dil)
            q = (q_ref[qs, :] * (A_SCALE * LOG2_E)).astype(BF16)
            k = k_ref[ks, :].astype(BF16)
            v = v_ref[ks, :].astype(BF16)
            s = lax.dot_general(q, k, (((1,), (1,)), ((), ())), preferred_element_type=F32)
            s = s + bias_scr[first.astype(jnp.int32)]
            mx = jnp.max(s, axis=-1, keepdims=True)
            p = jnp.exp2(s - mx)
            den = jnp.sum(p, axis=-1, keepdims=True)
            acc = jnp.dot(p.astype(BF16), v, preferred_element_type=F32)
            o_scr[c, qs, :] = acc / den
            lse_scr[c, qs, :] = jnp.broadcast_to(mx * LN_2 + jnp.log(den), (ATT_BLK, A_HEAD_DIM))
            return carry

        def tiles(i, carry, tile=tile):
            for u in range(ATT_UNROLL):
                tile(i * ATT_UNROLL + u, carry)
            return carry

        lax.fori_loop(0, ATT_SUPER // ATT_BLK // ATT_UNROLL, tiles, 0)

    rows = 256

    def merge(i, carry):
        sl = pl.ds(pl.multiple_of(i * rows, rows), rows)
        l0, l1, l2 = lse_scr[0, sl, :], lse_scr[1, sl, :], lse_scr[2, sl, :]
        m = jnp.maximum(jnp.maximum(l0, l1), l2)
        w0, w1, w2 = jnp.exp(l0 - m), jnp.exp(l1 - m), jnp.exp(l2 - m)
        num = w0 * o_scr[0, sl, :] + w1 * o_scr[1, sl, :] + w2 * o_scr[2, sl, :]
        o_ref[sl, :] = (num / (w0 + w1 + w2)).astype(o_ref.dtype)
        return carry

    lax.fori_loop(0, ATT_SUPER // rows, merge, 0)


def _dilated_attention(proj, seq):
    n_super = seq // ATT_SUPER
    blk = lambda off: pl.BlockSpec((seq, A_HEAD_DIM), lambda h, s: (0, off + h))
    return pl.pallas_call(
        _attn_body,
        grid=(A_HEADS, n_super),
        in_specs=[pl.BlockSpec((ATT_SUPER, A_HEAD_DIM), lambda h, s: (s, _QA_BLK + h)), blk(_KA_BLK), blk(_VA_BLK)],
        out_specs=pl.BlockSpec((ATT_SUPER, A_HEAD_DIM), lambda h, s: (s, h)),
        out_shape=jax.ShapeDtypeStruct((seq, A_WIDTH), BF16),
        scratch_shapes=[pltpu.VMEM((len(DILATED_CONFIGS), ATT_SUPER, A_HEAD_DIM), F32),
                        pltpu.VMEM((len(DILATED_CONFIGS), ATT_SUPER, A_HEAD_DIM), F32),
                        pltpu.VMEM((2, ATT_BLK, 2 * ATT_BLK), F32)],
        compiler_params=_params("arbitrary", "arbitrary"),
        name="dilated_attention",
    )(proj, proj, proj)


HG_CHUNK = 64
HG_TB = 512
HG_HEADS_PER_STEP = 2
HG_CHUNKS_PER_STACK = 2
HG_STACKS_PER_ITER = 2


def _hgrn_ref_rows(bc, half):
    n, width = bc.shape
    blk = 2 * half
    if blk >= SUBLANES:
        rows = [jnp.broadcast_to(bc[b0 + half - 1:b0 + half, :], (blk, width)) for b0 in range(0, n, blk)]
        return jnp.concatenate(rows, axis=0) if len(rows) > 1 else rows[0]
    sub = lax.broadcasted_iota(jnp.int32, (SUBLANES, width), 0)
    groups = []
    for g0 in range(0, n, SUBLANES):
        grp = bc[g0:g0 + SUBLANES, :]
        if half == 1:
            groups.append(jnp.where(jnp.bitwise_and(sub, 1) == 1, pltpu.roll(grp, 1, 0), grp))
        else:
            assert half == 2 and SUBLANES == 8
            groups.append(jnp.where(sub < 4, jnp.broadcast_to(grp[1:2, :], grp.shape),
                                    jnp.broadcast_to(grp[5:6, :], grp.shape)))
    return jnp.concatenate(groups, axis=0)


def _hgrn_body(qb_ref, fb_ref, ib_ref, gb_ref, lbl_ref, g_ref, o_ref, st_ref):
    @pl.when(pl.program_id(1) == 0)
    def _():
        st_ref[...] = jnp.zeros_like(st_ref)

    c_ = HG_CHUNK
    lbl = lbl_ref[...]
    e = jnp.exp(lbl - jnp.max(lbl, axis=0, keepdims=True))
    lb_all = e[0:1, :] / jnp.sum(e, axis=0, keepdims=True)
    g_all = g_ref[...]

    nh, nc = HG_HEADS_PER_STEP, HG_CHUNKS_PER_STACK
    rows = nc * c_
    n = nh * rows
    ti = lax.broadcasted_iota(jnp.int32, (n, n), 0)
    si = lax.broadcasted_iota(jnp.int32, (n, n), 1)
    xor = jnp.bitwise_xor(ti, si)
    causal = jnp.logical_and(xor < c_, si <= ti)
    tri = jnp.where(causal, 1.0, 0.0).astype(BF16)

    def stack_of(ref, r0):
        return jnp.concatenate([ref[pl.ds(r0, rows), h * B_KEY_DIM:(h + 1) * B_KEY_DIM] for h in range(nh)], axis=0)

    def per_head_rows(x):
        return jnp.concatenate([jnp.broadcast_to(x[:, h * B_KEY_DIM:(h + 1) * B_KEY_DIM], (rows, B_KEY_DIM))
                                for h in range(nh)], axis=0)

    lb = per_head_rows(lb_all)
    gnorm = per_head_rows(g_all)

    def stack(i):
        r0 = pl.multiple_of(i * rows, rows)
        f = lb + (1.0 - lb) * jax.nn.sigmoid(stack_of(fb_ref, r0))
        logf = jnp.log(f)
        kk = 1.0 - f
        q = jax.nn.silu(stack_of(qb_ref, r0))
        v = stack_of(ib_ref, r0)
        vb = v.astype(BF16)
        hi = logf.astype(BF16)
        rem = logf - hi.astype(F32)
        mid = rem.astype(BF16)
        low = (rem - mid.astype(F32)).astype(BF16)
        parts = jnp.dot(tri, jnp.concatenate([hi, mid, low], axis=1), preferred_element_type=F32)
        bc = (parts[:, :B_KEY_DIM] + (parts[:, B_KEY_DIM:2 * B_KEY_DIM] + parts[:, 2 * B_KEY_DIM:])) * LOG2_E

        attn = lax.dot_general(q.astype(BF16), kk.astype(BF16), (((1,), (1,)), ((), ())),
                               preferred_element_type=F32)
        half = 1
        while half < c_:
            dec = jnp.exp2(-jnp.abs(bc - _hgrn_ref_rows(bc, half)))
            s = lax.dot_general((q * dec).astype(BF16), (kk * dec).astype(BF16), (((1,), (1,)), ((), ())),
                                preferred_element_type=F32)
            attn = jnp.where(xor >= half, s, attn)
            half *= 2
        attn = jnp.where(causal, attn, 0.0)
        o = jnp.dot(attn.astype(BF16), vb, preferred_element_type=F32)

        last = jnp.concatenate([jnp.broadcast_to(bc[a + c_ - 1:a + c_, :], (c_, B_KEY_DIM))
                                for a in range(0, n, c_)], axis=0)
        q_dec = (q * jnp.exp2(bc)).astype(BF16)
        k_dec = (kk * jnp.exp2(last - bc)).astype(BF16)
        st_dec = jnp.exp2(last)
        inter = []
        for h in range(nh):
            st = st_ref[h]
            for c in range(nc):
                a = h * rows + c * c_
                inter.append(lax.dot_general(q_dec[a:a + c_], st.astype(BF16), (((1,), (1,)), ((), ())),
                                             preferred_element_type=F32))
                upd = lax.dot_general(vb[a:a + c_], k_dec[a:a + c_], (((0,), (0,)), ((), ())),
                                      preferred_element_type=F32)
                st = st * st_dec[a:a + 1] + upd
            st_ref[h] = st
        o = o + jnp.concatenate(inter, axis=0)

        o = o * lax.rsqrt(jnp.mean(o * o, axis=-1, keepdims=True) + NORM_EPS)
        res = (o * gnorm * jax.nn.silu(stack_of(gb_ref, r0))).astype(o_ref.dtype)
        for h in range(nh):
            o_ref[pl.ds(r0, rows), h * B_KEY_DIM:(h + 1) * B_KEY_DIM] = res[h * rows:(h + 1) * rows]

    def stacks(i, carry):
        for u in range(HG_STACKS_PER_ITER):
            stack(i * HG_STACKS_PER_ITER + u)
        return carry

    lax.fori_loop(0, HG_TB // rows // HG_STACKS_PER_ITER, stacks, 0)


def _hgrn2(proj, lb_logits, norm_g, seq):
    hp = HG_HEADS_PER_STEP
    width = hp * B_KEY_DIM
    col = lambda off: pl.BlockSpec((HG_TB, width), lambda h, t: (t, off // hp + h))
    n_lb = lb_logits.shape[0]
    return pl.pallas_call(
        _hgrn_body,
        grid=(B_HEADS // hp, seq // HG_TB),
        in_specs=[col(_QB_BLK), col(_FB_BLK), col(_IB_BLK), col(_GB_BLK),
                  pl.BlockSpec((n_lb, width), lambda h, t: (0, h)),
                  pl.BlockSpec((1, width), lambda h, t: (0, h))],
        out_specs=pl.BlockSpec((HG_TB, width), lambda h, t: (t, h)),
        out_shape=jax.ShapeDtypeStruct((seq, B_WIDTH), BF16),
        scratch_shapes=[pltpu.VMEM((hp, B_KEY_DIM, B_KEY_DIM), F32)],
        compiler_params=_params("arbitrary", "arbitrary"),
        name="hgrn2",
    )(proj, proj, proj, proj, lb_logits.astype(F32), norm_g.reshape(1, B_WIDTH).astype(F32))


MERGE_GATE_BLK = 512


def _merge_body(oa_ref, ob_ref, *rest):
    n_g = (len(rest) - 5) // 2
    ga_refs, gb_refs = rest[:n_g], rest[n_g:2 * n_g]
    wa_ref, wb_ref, o_ref, wa_s, wb_s = rest[2 * n_g:]

    @pl.when(pl.program_id(1) == 0)
    def _():
        wa_s[...] = wa_ref[...].astype(BF16)
        wb_s[...] = wb_ref[...].astype(BF16)

    ya = jnp.dot(oa_ref[...], wa_s[...], preferred_element_type=F32)
    yb = jnp.dot(ob_ref[...], wb_s[...], preferred_element_type=F32)
    ga = jnp.concatenate([r[...] for r in ga_refs], axis=1)
    gb = jnp.concatenate([r[...] for r in gb_refs], axis=1)
    o_ref[...] = (jax.nn.sigmoid(ga) * ya + jax.nn.sigmoid(gb) * yb).astype(o_ref.dtype)


def _branch_merge(oa, ob, proj, wa, wb, tm=512, tn=1024):
    m = oa.shape[0]
    gw = MERGE_GATE_BLK
    n_g = tn // gw
    ga0, gb0 = _GATE_A_COL // gw, _GATE_B_COL // gw
    gate = lambda off, u: pl.BlockSpec((tm, gw), lambda j, i: (i, off + j * n_g + u))
    return pl.pallas_call(
        _merge_body,
        grid=(D_MODEL // tn, m // tm),
        in_specs=[pl.BlockSpec((tm, A_WIDTH), lambda j, i: (i, 0)), pl.BlockSpec((tm, B_WIDTH), lambda j, i: (i, 0)),
                  *[gate(ga0, u) for u in range(n_g)], *[gate(gb0, u) for u in range(n_g)],
                  pl.BlockSpec((A_WIDTH, tn), lambda j, i: (0, j)), pl.BlockSpec((B_WIDTH, tn), lambda j, i: (0, j))],
        out_specs=pl.BlockSpec((tm, tn), lambda j, i: (i, j)),
        out_shape=jax.ShapeDtypeStruct((m, D_MODEL), BF16),
        scratch_shapes=[pltpu.VMEM((A_WIDTH, tn), BF16), pltpu.VMEM((B_WIDTH, tn), BF16)],
        compiler_params=_params("arbitrary", "arbitrary"),
        name="branch_merge",
    )(oa, ob, *([proj] * (2 * n_g)), wa, wb)


def _route_rows(h, g, rw, rb):
    hn = _rmsnorm_rows(h, g)
    packed = _pack_bf16_pairs(hn)
    hn_hi = hn.astype(BF16)
    hn_lo = (hn - hn_hi.astype(F32)).astype(BF16)
    rw_hi = rw.astype(BF16)
    rw_lo = (rw - rw_hi.astype(F32)).astype(BF16)
    logits = (jnp.dot(hn_hi, rw_hi, preferred_element_type=F32)
              + (jnp.dot(hn_lo, rw_hi, preferred_element_type=F32)
                 + jnp.dot(hn_hi, rw_lo, preferred_element_type=F32))) + rb
    lane = lax.broadcasted_iota(jnp.int32, logits.shape, 1)
    neg = jnp.float32(-jnp.inf)
    big = jnp.int32(LANES)

    is_g = lane < N_GROUPS
    lg = jnp.where(is_g, logits, neg)
    mg = jnp.max(lg, axis=-1, keepdims=True)
    g_idx = jnp.min(jnp.where(lg == mg, lane, big), axis=-1, keepdims=True)
    pg_top = 1.0 / jnp.sum(jnp.where(is_g, jnp.exp(lg - mg), 0.0), axis=-1, keepdims=True)

    lo = N_GROUPS + g_idx * EXPERTS_PER_GROUP
    in_grp = jnp.logical_and(lane >= lo, lane < lo + EXPERTS_PER_GROUP)
    le = jnp.where(in_grp, logits, neg)
    v1 = jnp.max(le, axis=-1, keepdims=True)
    i1 = jnp.min(jnp.where(le == v1, lane, big), axis=-1, keepdims=True)
    le2 = jnp.where(lane == i1, neg, le)
    v2 = jnp.max(le2, axis=-1, keepdims=True)
    i2 = jnp.min(jnp.where(le2 == v2, lane, big), axis=-1, keepdims=True)
    e2 = jnp.exp(v2 - v1)
    w1 = pg_top / (1.0 + e2)
    w2 = pg_top * e2 / (1.0 + e2)
    eid = jnp.where(lane == 0, i1 - N_GROUPS, jnp.where(lane == 1, i2 - N_GROUPS, 0))
    wk = jnp.where(lane == 0, w1, jnp.where(lane == 1, w2, 0.0))
    return packed, eid, wk


def _out_router_body(a_ref, w_ref, x_ref, g_ref, rw_ref, rb_ref, h1_ref, hn_ref, eid_ref, wk_ref, h_prev):
    @pl.when(pl.program_id(0) == 0)
    def _():
        h_prev[...] = jnp.zeros_like(h_prev)

    hn_ref[...], eid_ref[...], wk_ref[...] = _route_rows(h_prev[...], g_ref[...], rw_ref[...], rb_ref[...])
    h1 = x_ref[...] + jnp.dot(a_ref[...], w_ref[...], preferred_element_type=F32)
    h1_ref[...] = h1
    h_prev[...] = h1


def _out_proj_router(a, w_bf16, x, g, rw, rb, tm=512):
    m, d = x.shape
    n_blk = m // tm
    cur = lambda width: pl.BlockSpec((tm, width), lambda i: (jnp.minimum(i, n_blk - 1), 0))
    prev = lambda width: pl.BlockSpec((tm, width), lambda i: (jnp.maximum(i - 1, 0), 0))
    const = lambda shape, **kw: pl.BlockSpec(shape, lambda i: (0, 0), **kw)
    return pl.pallas_call(
        _out_router_body,
        grid=(n_blk + 1,),
        in_specs=[cur(a.shape[1]), const(w_bf16.shape, pipeline_mode=pl.Buffered(1)), cur(d),
                  const((1, d)), const((d, LANES)), const((1, LANES))],
        out_specs=[cur(d), prev(d // 2), prev(LANES), prev(LANES)],
        out_shape=[jax.ShapeDtypeStruct((m, d), F32), jax.ShapeDtypeStruct((m, d // 2), jnp.uint32),
                   jax.ShapeDtypeStruct((m, LANES), jnp.int32), jax.ShapeDtypeStruct((m, LANES), F32)],
        scratch_shapes=[pltpu.VMEM((tm, d), F32)],
        compiler_params=_params("arbitrary"),
        name="out_proj_router",
    )(a, w_bf16, x, g.reshape(1, d).astype(F32), rw, rb)


MOE_TM = 256
ROW_DMA_PRIORITIES = (0, 1)


def _moe_body(tile_e_ref, tile_blk_ref, tile_rows_ref, n_used_ref,
              src0_ref, src_next_ref, dst_prev_ref, hn_ref, w1_ref, w3_ref, w2_ref, y_ref,
              w1_s, w3_s, w2_s, x0, x1, o0, o1, gsem, ssem):
    j = pl.program_id(0)
    n_tiles = pl.num_programs(0)
    n_used = n_used_ref[0]
    xs = (x0, x1)
    os_ = (o0, o1)

    def rows_of(t):
        return jnp.where(t >= 0, tile_rows_ref[jnp.clip(t, 0, n_tiles - 1)], 0)

    def gather_issue(idx_ref, b, n):
        for r in range(MOE_TM):
            @pl.when(r < n)
            def _(r=r):
                pltpu.make_async_copy(hn_ref.at[idx_ref[0, 0, r]], xs[b].at[r],
                                      gsem.at[b]).start(priority=ROW_DMA_PRIORITIES[r % 2])

    def scatter_issue(idx_ref, b, n):
        for r in range(MOE_TM):
            @pl.when(r < n)
            def _(r=r):
                pltpu.make_async_copy(os_[b].at[r], y_ref.at[idx_ref[0, 0, r]],
                                      ssem.at[b]).start(priority=ROW_DMA_PRIORITIES[r % 2])

    def wait_rows(make_copy, n):
        p = MOE_TM
        while p >= 1:
            @pl.when(jnp.bitwise_and(n, p) != 0)
            def _(p=p):
                make_copy(p).wait()
            p //= 2

    def gather_wait(b, n):
        wait_rows(lambda p: pltpu.make_async_copy(hn_ref.at[pl.ds(0, p)], xs[b].at[pl.ds(0, p)], gsem.at[b]), n)

    def scatter_wait(b, n):
        wait_rows(lambda p: pltpu.make_async_copy(os_[b].at[pl.ds(0, p)], y_ref.at[pl.ds(0, p)], ssem.at[b]), n)

    @pl.when(j == 0)
    def _():
        x0[...] = jnp.zeros_like(x0)
        x1[...] = jnp.zeros_like(x1)
        gather_issue(src0_ref, 0, rows_of(0))

    for b in (0, 1):
        @pl.when(jnp.logical_and(j < n_used, j % 2 == b))
        def _(b=b):
            gather_wait(b, rows_of(j))
            scatter_wait(b, rows_of(j - 2))

            prev_e = tile_e_ref[jnp.maximum(j - 1, 0)]

            @pl.when(jnp.logical_or(j == 0, tile_e_ref[j] != prev_e))
            def _():
                w1_s[...] = w1_ref[0].astype(BF16)
                w3_s[...] = w3_ref[0].astype(BF16)
                w2_s[...] = w2_ref[0].astype(BF16)

            gather_issue(src_next_ref, 1 - b, rows_of(j + 1))
            scatter_issue(dst_prev_ref, 1 - b, rows_of(j - 1))
            x = _unpack_bf16_pairs(xs[b][...]).astype(BF16)
            hid = (jax.nn.silu(jnp.dot(x, w1_s[...], preferred_element_type=F32))
                   * jnp.dot(x, w3_s[...], preferred_element_type=F32))
            os_[b][...] = _pack_bf16_pairs(jnp.dot(hid.astype(BF16), w2_s[...], preferred_element_type=F32))

    for b in (0, 1):
        @pl.when(jnp.logical_and(j == n_used, j % 2 == b))
        def _(b=b):
            scatter_wait(b, rows_of(j - 2))
            scatter_issue(dst_prev_ref, 1 - b, rows_of(j - 1))
            scatter_wait(1 - b, rows_of(j - 1))


def _moe_experts(hn, w1, w3, w2, tile_e, tile_blk, tile_rows, n_used, src, dst, n_tokens):
    dp = hn.shape[1]
    d = 2 * dp
    n_tiles = tile_e.shape[0]
    ff = w1.shape[-1]
    idx_block = lambda fn: pl.BlockSpec((1, 1, MOE_TM), fn, memory_space=pltpu.SMEM)
    grid_spec = pltpu.PrefetchScalarGridSpec(
        num_scalar_prefetch=4,
        grid=(n_tiles,),
        in_specs=[
            idx_block(lambda j, te, tb, tr, nu: (0, 0, 0)),
            idx_block(lambda j, te, tb, tr, nu: (tb[jnp.minimum(j + 1, n_tiles - 1)], 0, 0)),
            idx_block(lambda j, te, tb, tr, nu: (tb[jnp.maximum(j - 1, 0)], 0, 0)),
            pl.BlockSpec(memory_space=pl.ANY),
            pl.BlockSpec((1, d, ff), lambda j, te, tb, tr, nu: (te[j], 0, 0)),
            pl.BlockSpec((1, d, ff), lambda j, te, tb, tr, nu: (te[j], 0, 0)),
            pl.BlockSpec((1, ff, d), lambda j, te, tb, tr, nu: (te[j], 0, 0)),
        ],
        out_specs=pl.BlockSpec(memory_space=pl.ANY),
        scratch_shapes=[pltpu.VMEM((d, ff), BF16), pltpu.VMEM((d, ff), BF16), pltpu.VMEM((ff, d), BF16),
                        pltpu.VMEM((MOE_TM, dp), jnp.uint32), pltpu.VMEM((MOE_TM, dp), jnp.uint32),
                        pltpu.VMEM((MOE_TM, dp), jnp.uint32), pltpu.VMEM((MOE_TM, dp), jnp.uint32),
                        pltpu.SemaphoreType.DMA((2,)), pltpu.SemaphoreType.DMA((2,))],
    )
    src3 = src.reshape(n_tiles, 1, MOE_TM)
    return pl.pallas_call(
        _moe_body,
        grid_spec=grid_spec,
        out_shape=jax.ShapeDtypeStruct((2 * n_tokens, dp), jnp.uint32),
        compiler_params=_params("arbitrary"),
        name="moe_experts",
    )(tile_e, tile_blk, tile_rows, n_used, src3, src3, dst.reshape(n_tiles, 1, MOE_TM), hn, w1, w3, w2)


def _combine_body(h_ref, y0_ref, y1_ref, wk_ref, g_ref, o_ref):
    wk = wk_ref[...]
    y = (h_ref[...] + wk[:, 0:1] * _unpack_bf16_pairs(y0_ref[...])
         + wk[:, 1:2] * _unpack_bf16_pairs(y1_ref[...]))
    o_ref[...] = _rmsnorm_rows(y, g_ref[...]).astype(o_ref.dtype)


def _combine(h, y2, wk, g, out_dtype, tm=256):
    m, d = h.shape
    nblk = m // tm
    return pl.pallas_call(
        _combine_body,
        grid=(nblk,),
        in_specs=[pl.BlockSpec((tm, d), lambda i: (i, 0)), pl.BlockSpec((tm, d // 2), lambda i: (i, 0)),
                  pl.BlockSpec((tm, d // 2), lambda i: (nblk + i, 0)), pl.BlockSpec((tm, LANES), lambda i: (i, 0)),
                  pl.BlockSpec((1, d), lambda i: (0, 0))],
        out_specs=pl.BlockSpec((tm, d), lambda i: (i, 0)),
        out_shape=jax.ShapeDtypeStruct((m, d), out_dtype),
        compiler_params=_params("arbitrary"),
        name="combine_final_norm",
    )(h, y2, y2, wk, g.reshape(1, d).astype(F32))


def _routing_tables(eid, n_tokens):
    n_pairs = 2 * n_tokens
    n_tiles = n_pairs // MOE_TM + N_EXPERTS + 1
    e_flat = eid.reshape(n_pairs)
    order = jnp.argsort(e_flat, stable=True).astype(jnp.int32)
    counts = jnp.sum(e_flat[:, None] == jnp.arange(N_EXPERTS, dtype=jnp.int32)[None, :], axis=0, dtype=jnp.int32)
    tiles_per_e = (counts + MOE_TM - 1) // MOE_TM
    tile_end = jnp.cumsum(tiles_per_e)
    tile_start = tile_end - tiles_per_e
    n_used = tile_end[-1]
    sorted_start = jnp.cumsum(counts) - counts
    j = jnp.arange(n_tiles, dtype=jnp.int32)
    tile_blk = jnp.minimum(j, n_used - 1)
    tile_e = jnp.searchsorted(tile_end, tile_blk, side="right").astype(jnp.int32)
    tile_row0 = (tile_blk - tile_start[tile_e]) * MOE_TM
    tile_rows = jnp.where(j == tile_blk, jnp.clip(counts[tile_e] - tile_row0, 0, MOE_TM), 0).astype(jnp.int32)
    r = jnp.arange(MOE_TM, dtype=jnp.int32)[None, :]
    valid = r < tile_rows[:, None]
    pair = order[jnp.clip((sorted_start[tile_e] + tile_row0)[:, None] + r, 0, n_pairs - 1)]
    src_tok = jnp.where(valid, pair // 2, 0).astype(jnp.int32)
    dst_row = jnp.where(valid, (pair % 2) * n_tokens + pair // 2, 0).astype(jnp.int32)
    return (tile_e, tile_blk.astype(jnp.int32), tile_rows, n_used.reshape(1).astype(jnp.int32),
            src_tok.reshape(-1), dst_row.reshape(-1))


def kernel(x, mix_norm_g, w_in, hgrn_lb_logits, hgrn_norm_g, w_branch_a, w_branch_b, w_out, ffn_norm_g,
           router_w_group, router_b_group, router_w_expert, router_b_expert, expert_w1, expert_w3, expert_w2,
           final_norm_g):
    b, seq, d = x.shape
    assert b == 1 and d == D_MODEL and seq % ATT_SUPER == 0 and w_in.shape == (1, D_MODEL, IN_WIDTH)
    h0 = x.reshape(seq, d).astype(F32)

    xn = _rmsnorm(h0, mix_norm_g[0], BF16)
    proj = _matmul(xn, w_in[0], F32, tm=1024, tn=1280, name="in_proj")
    oa = _dilated_attention(proj, seq)
    ob = _hgrn2(proj, hgrn_lb_logits, hgrn_norm_g[0], seq)
    merged = _branch_merge(oa, ob, proj, w_branch_a[0], w_branch_b[0])
    pad = LANES - N_GROUPS - N_EXPERTS
    rw = jnp.concatenate([router_w_group[0], router_w_expert[0], jnp.zeros((d, pad), F32)], axis=1).astype(F32)
    rb = jnp.concatenate([router_b_group[0], router_b_expert[0], jnp.zeros((pad,), F32)]).reshape(1, LANES).astype(F32)
    h1, hn, eid, wk = _out_proj_router(merged, w_out[0].astype(BF16), h0, ffn_norm_g[0], rw, rb)

    tile_e, tile_blk, tile_rows, n_used, src_tok, dst_row = _routing_tables(eid[:, :2], seq)
    w1 = expert_w1[0].reshape(N_EXPERTS, D_MODEL, EXPERT_FF)
    w3 = expert_w3[0].reshape(N_EXPERTS, D_MODEL, EXPERT_FF)
    w2 = expert_w2[0].reshape(N_EXPERTS, EXPERT_FF, D_MODEL)
    y2 = _moe_experts(hn, w1, w3, w2, tile_e, tile_blk, tile_rows, n_used, src_tok, dst_row, seq)

    out = _combine(h1, y2, wk, final_norm_g, x.dtype)
    return out.reshape(b, seq, d)
```

```python
import functools

import jax
import jax.numpy as jnp
from jax import lax
from jax.experimental import pallas as pl
from jax.experimental.pallas import tpu as pltpu

F32 = jnp.float32
BF16 = jnp.bfloat16

D_MODEL = 2048
A_HEADS = 12
A_HEAD_DIM = 128
A_WIDTH = A_HEADS * A_HEAD_DIM
A_SCALE = A_HEAD_DIM ** -0.5
DILATED_CONFIGS = ((128, 1), (512, 4), (2048, 16))
B_HEADS = 8
B_KEY_DIM = 128
B_WIDTH = B_HEADS * B_KEY_DIM
N_GROUPS = 4
EXPERTS_PER_GROUP = 8
N_EXPERTS = N_GROUPS * EXPERTS_PER_GROUP
EXPERT_FF = 512
NORM_EPS = 1e-6

LANES = 128
SUBLANES = 8
VMEM_LIMIT = 56 * 1024 * 1024

_QA_BLK = 0
_KA_BLK = A_HEADS
_VA_BLK = 2 * A_HEADS
_QB_BLK = 3 * A_HEADS
_FB_BLK = _QB_BLK + B_HEADS
_IB_BLK = _FB_BLK + B_HEADS
_GB_BLK = _IB_BLK + B_HEADS
_GATE_A_COL = 3 * A_WIDTH + 4 * B_WIDTH
_GATE_B_COL = _GATE_A_COL + D_MODEL
IN_WIDTH = _GATE_B_COL + D_MODEL


def _params(*sem):
    return pltpu.CompilerParams(dimension_semantics=sem, vmem_limit_bytes=VMEM_LIMIT)


def _rmsnorm_rows(x, g):
    ms = jnp.mean(x * x, axis=-1, keepdims=True)
    return x * lax.rsqrt(ms + NORM_EPS) * g


def _pack_bf16_pairs(x):
    w = x.shape[1] // 2
    lo = pltpu.bitcast(x[:, :w].astype(BF16).astype(F32), jnp.uint32)
    hi = pltpu.bitcast(x[:, w:].astype(BF16).astype(F32), jnp.uint32)
    return jnp.bitwise_or(jnp.bitwise_and(hi, jnp.uint32(0xFFFF0000)), jnp.right_shift(lo, jnp.uint32(16)))


def _unpack_bf16_pairs(u):
    lo = pltpu.bitcast(jnp.left_shift(u, jnp.uint32(16)), F32)
    hi = pltpu.bitcast(jnp.bitwise_and(u, jnp.uint32(0xFFFF0000)), F32)
    return jnp.concatenate([lo, hi], axis=1)


def _rmsnorm_body(x_ref, g_ref, o_ref):
    o_ref[...] = _rmsnorm_rows(x_ref[...].astype(F32), g_ref[...]).astype(o_ref.dtype)


def _rmsnorm(x, g, out_dtype, tm=512):
    m, d = x.shape
    return pl.pallas_call(
        _rmsnorm_body,
        grid=(m // tm,),
        in_specs=[pl.BlockSpec((tm, d), lambda i: (i, 0)), pl.BlockSpec((1, d), lambda i: (0, 0))],
        out_specs=pl.BlockSpec((tm, d), lambda i: (i, 0)),
        out_shape=jax.ShapeDtypeStruct((m, d), out_dtype),
        compiler_params=_params("arbitrary"),
        name="rmsnorm",
    )(x, g.reshape(1, d).astype(F32))


def _matmul_body(a_ref, w_ref, o_ref, wb_ref):
    @pl.when(pl.program_id(1) == 0)
    def _():
        wb_ref[...] = w_ref[...].astype(BF16)

    o_ref[...] = jnp.dot(a_ref[...], wb_ref[...], preferred_element_type=F32).astype(o_ref.dtype)


def _matmul(a, w, out_dtype, tm, tn, name):
    m, k = a.shape
    n = w.shape[1]
    return pl.pallas_call(
        _matmul_body,
        grid=(n // tn, m // tm),
        in_specs=[pl.BlockSpec((tm, k), lambda j, i: (i, 0)), pl.BlockSpec((k, tn), lambda j, i: (0, j))],
        out_specs=pl.BlockSpec((tm, tn), lambda j, i: (i, j)),
        out_shape=jax.ShapeDtypeStruct((m, n), out_dtype),
        scratch_shapes=[pltpu.VMEM((k, tn), BF16)],
        compiler_params=_params("arbitrary", "arbitrary"),
        name=name,
    )(a, w)


LOG2_E = 1.4426950408889634
LN_2 = 0.6931471805599453
ATT_BLK = 128
ATT_SUPER = 2048
ATT_DEINT = 4


def _attn_body(q_ref, k_ref, v_ref, o_ref, o_scr, lse_scr, bias_scr, k4, v4, q4, stage):
    sb = pl.program_id(1)
    seq = k_ref.shape[0]
    nd = ATT_DEINT
    sub = ATT_SUPER // nd
    diff = (lax.broadcasted_iota(jnp.int32, (ATT_BLK, 2 * ATT_BLK), 1)
            - lax.broadcasted_iota(jnp.int32, (ATT_BLK, 2 * ATT_BLK), 0))
    neg = jnp.float32(-jnp.inf)
    bias_scr[0] = jnp.where(jnp.logical_and(diff >= 0, diff <= ATT_BLK), 0.0, neg)
    bias_scr[1] = jnp.where(diff <= 0, 0.0, neg)

    @pl.when(sb == 0)
    def _():
        rows = 512

        def split(i, carry):
            for a in range(nd):
                src = pl.ds(i * (rows * nd) + a, rows, stride=nd)
                dst = pl.ds(pl.multiple_of(i * rows, rows), rows)
                k4[a, dst, :] = k_ref[src, :]
                v4[a, dst, :] = v_ref[src, :]
            return carry

        lax.fori_loop(0, seq // nd // rows, split, 0)

    for a in range(nd):
        q4[a] = q_ref[pl.ds(a, sub, stride=nd), :]

    def attend(q, k, v, bias):
        qb = (q * (A_SCALE * LOG2_E)).astype(BF16)
        s = lax.dot_general(qb, k.astype(BF16), (((1,), (1,)), ((), ())), preferred_element_type=F32)
        s = s + bias
        mx = jnp.max(s, axis=-1, keepdims=True)
        p = jnp.exp2(s - mx)
        den = jnp.sum(p, axis=-1, keepdims=True)
        acc = jnp.dot(p.astype(BF16), v.astype(BF16), preferred_element_type=F32)
        return acc / den, jnp.broadcast_to(mx * LN_2 + jnp.log(den), (ATT_BLK, A_HEAD_DIM))

    n_tiles = ATT_SUPER // ATT_BLK
    for c, (window, dil) in enumerate(DILATED_CONFIGS):
        assert window // dil == ATT_BLK and (dil == 1 or dil % nd == 0)
        tiles_per_res = n_tiles // dil
        for t in range(n_tiles):
            n, r = t % tiles_per_res, t // tiles_per_res
            if n == 0:
                first = (sb == 0).astype(jnp.int32)
                back = ATT_BLK * (1 - first)
                bias = bias_scr[first]
            else:
                back, bias = ATT_BLK, bias_scr[0]
            if dil == 1:
                q_rows = pl.ds(n * ATT_BLK, ATT_BLK)
                kv_rows = pl.ds(pl.multiple_of(sb * ATT_SUPER + n * ATT_BLK - back, ATT_BLK), 2 * ATT_BLK)
                o, lse = attend(q_ref[q_rows, :], k_ref[kv_rows, :], v_ref[kv_rows, :], bias)
                out_rows = q_rows
            else:
                st, a, r2 = dil // nd, r % nd, r // nd
                i0 = n * ATT_BLK
                q_rows = pl.ds(st * i0 + r2, ATT_BLK, stride=st) if st > 1 else pl.ds(i0, ATT_BLK)
                kv0 = st * (sb * (ATT_SUPER // dil) + i0 - back) + r2
                kv_rows = pl.ds(kv0, 2 * ATT_BLK, stride=st) if st > 1 else pl.ds(kv0, 2 * ATT_BLK)
                o, lse = attend(q4[a, q_rows, :], k4[a, kv_rows, :], v4[a, kv_rows, :], bias)
                out_rows = (pl.ds(a * sub + st * i0 + r2, ATT_BLK, stride=st) if st > 1
                            else pl.ds(a * sub + i0, ATT_BLK))
            o_scr[c, out_rows, :] = o
            lse_scr[c, out_rows, :] = lse

    rows = 128

    def merge(i, carry):
        for a in range(nd):
            tok = pl.ds(i * (rows * nd) + a, rows, stride=nd)
            cls = pl.ds(a * sub + pl.multiple_of(i * rows, rows), rows)
            l0, l1, l2 = lse_scr[0, tok, :], lse_scr[1, cls, :], lse_scr[2, cls, :]
            m = jnp.maximum(jnp.maximum(l0, l1), l2)
            w0, w1, w2 = jnp.exp(l0 - m), jnp.exp(l1 - m), jnp.exp(l2 - m)
            num = w0 * o_scr[0, tok, :] + w1 * o_scr[1, cls, :] + w2 * o_scr[2, cls, :]
            stage[tok, :] = num / (w0 + w1 + w2)
        return carry

    lax.fori_loop(0, sub // rows, merge, 0)
    o_ref[...] = stage[...].astype(o_ref.dtype)


def _dilated_attention(proj, seq):
    n_super = seq // ATT_SUPER
    blk = lambda off: pl.BlockSpec((seq, A_HEAD_DIM), lambda h, s: (0, off + h))
    return pl.pallas_call(
        _attn_body,
        grid=(A_HEADS, n_super),
        in_specs=[pl.BlockSpec((ATT_SUPER, A_HEAD_DIM), lambda h, s: (s, _QA_BLK + h)), blk(_KA_BLK), blk(_VA_BLK)],
        out_specs=pl.BlockSpec((ATT_SUPER, A_HEAD_DIM), lambda h, s: (s, h)),
        out_shape=jax.ShapeDtypeStruct((seq, A_WIDTH), BF16),
        scratch_shapes=[pltpu.VMEM((len(DILATED_CONFIGS), ATT_SUPER, A_HEAD_DIM), F32),
                        pltpu.VMEM((len(DILATED_CONFIGS), ATT_SUPER, A_HEAD_DIM), F32),
                        pltpu.VMEM((2, ATT_BLK, 2 * ATT_BLK), F32),
                        pltpu.VMEM((ATT_DEINT, seq // ATT_DEINT, A_HEAD_DIM), F32),
                        pltpu.VMEM((ATT_DEINT, seq // ATT_DEINT, A_HEAD_DIM), F32),
                        pltpu.VMEM((ATT_DEINT, ATT_SUPER // ATT_DEINT, A_HEAD_DIM), F32),
                        pltpu.VMEM((ATT_SUPER, A_HEAD_DIM), F32)],
        compiler_params=_params("arbitrary", "arbitrary"),
        name="dilated_attention",
    )(proj, proj, proj)


HG_CHUNK = 64
HG_TB = 512
HG_HEADS_PER_STEP = 2
HG_CHUNKS_PER_STACK = 2
HG_STACKS_PER_ITER = 2


def _hgrn_ref_rows(bc, half):
    n, width = bc.shape
    blk = 2 * half
    if blk >= SUBLANES:
        rows = [jnp.broadcast_to(bc[b0 + half - 1:b0 + half, :], (blk, width)) for b0 in range(0, n, blk)]
        return jnp.concatenate(rows, axis=0) if len(rows) > 1 else rows[0]
    sub = lax.broadcasted_iota(jnp.int32, (SUBLANES, width), 0)
    groups = []
    for g0 in range(0, n, SUBLANES):
        grp = bc[g0:g0 + SUBLANES, :]
        if half == 1:
            groups.append(jnp.where(jnp.bitwise_and(sub, 1) == 1, pltpu.roll(grp, 1, 0), grp))
        else:
            assert half == 2 and SUBLANES == 8
            groups.append(jnp.where(sub < 4, jnp.broadcast_to(grp[1:2, :], grp.shape),
                                    jnp.broadcast_to(grp[5:6, :], grp.shape)))
    return jnp.concatenate(groups, axis=0)


def _hgrn_body(qb_ref, fb_ref, ib_ref, gb_ref, lbl_ref, g_ref, o_ref, st_ref):
    @pl.when(pl.program_id(1) == 0)
    def _():
        st_ref[...] = jnp.zeros_like(st_ref)

    c_ = HG_CHUNK
    lbl = lbl_ref[...]
    e = jnp.exp(lbl - jnp.max(lbl, axis=0, keepdims=True))
    lb_all = e[0:1, :] / jnp.sum(e, axis=0, keepdims=True)
    g_all = g_ref[...]

    nh, nc = HG_HEADS_PER_STEP, HG_CHUNKS_PER_STACK
    rows = nc * c_
    n = nh * rows
    ti = lax.broadcasted_iota(jnp.int32, (n, n), 0)
    si = lax.broadcasted_iota(jnp.int32, (n, n), 1)
    xor = jnp.bitwise_xor(ti, si)
    causal = jnp.logical_and(xor < c_, si <= ti)
    tri = jnp.where(causal, 1.0, 0.0).astype(BF16)

    def stack_of(ref, r0):
        return jnp.concatenate([ref[pl.ds(r0, rows), h * B_KEY_DIM:(h + 1) * B_KEY_DIM] for h in range(nh)], axis=0)

    def per_head_rows(x):
        return jnp.concatenate([jnp.broadcast_to(x[:, h * B_KEY_DIM:(h + 1) * B_KEY_DIM], (rows, B_KEY_DIM))
                                for h in range(nh)], axis=0)

    lb = per_head_rows(lb_all)
    gnorm = per_head_rows(g_all)

    def stack(i):
        r0 = pl.multiple_of(i * rows, rows)
        f = lb + (1.0 - lb) * jax.nn.sigmoid(stack_of(fb_ref, r0))
        logf = jnp.log(f)
        kk = 1.0 - f
        q = jax.nn.silu(stack_of(qb_ref, r0))
        v = stack_of(ib_ref, r0)
        vb = v.astype(BF16)
        hi = logf.astype(BF16)
        rem = logf - hi.astype(F32)
        mid = rem.astype(BF16)
        low = (rem - mid.astype(F32)).astype(BF16)
        parts = jnp.dot(tri, jnp.concatenate([hi, mid, low], axis=1), preferred_element_type=F32)
        bc = (parts[:, :B_KEY_DIM] + (parts[:, B_KEY_DIM:2 * B_KEY_DIM] + parts[:, 2 * B_KEY_DIM:])) * LOG2_E

        attn = lax.dot_general(q.astype(BF16), kk.astype(BF16), (((1,), (1,)), ((), ())),
                               preferred_element_type=F32)
        half = 1
        while half < c_:
            dec = jnp.exp2(-jnp.abs(bc - _hgrn_ref_rows(bc, half)))
            s = lax.dot_general((q * dec).astype(BF16), (kk * dec).astype(BF16), (((1,), (1,)), ((), ())),
                                preferred_element_type=F32)
            attn = jnp.where(xor >= half, s, attn)
            half *= 2
        attn = jnp.where(causal, attn, 0.0)
        o = jnp.dot(attn.astype(BF16), vb, preferred_element_type=F32)

        last = jnp.concatenate([jnp.broadcast_to(bc[a + c_ - 1:a + c_, :], (c_, B_KEY_DIM))
                                for a in range(0, n, c_)], axis=0)
        q_dec = (q * jnp.exp2(bc)).astype(BF16)
        k_dec = (kk * jnp.exp2(last - bc)).astype(BF16)
        st_dec = jnp.exp2(last)
        inter = []
        for h in range(nh):
            st = st_ref[h]
            for c in range(nc):
                a = h * rows + c * c_
                inter.append(lax.dot_general(q_dec[a:a + c_], st.astype(BF16), (((1,), (1,)), ((), ())),
                                             preferred_element_type=F32))
                upd = lax.dot_general(vb[a:a + c_], k_dec[a:a + c_], (((0,), (0,)), ((), ())),
                                      preferred_element_type=F32)
                st = st * st_dec[a:a + 1] + upd
            st_ref[h] = st
        o = o + jnp.concatenate(inter, axis=0)

        o = o * lax.rsqrt(jnp.mean(o * o, axis=-1, keepdims=True) + NORM_EPS)
        res = (o * gnorm * jax.nn.silu(stack_of(gb_ref, r0))).astype(o_ref.dtype)
        for h in range(nh):
            o_ref[pl.ds(r0, rows), h * B_KEY_DIM:(h + 1) * B_KEY_DIM] = res[h * rows:(h + 1) * rows]

    def stacks(i, carry):
        for u in range(HG_STACKS_PER_ITER):
            stack(i * HG_STACKS_PER_ITER + u)
        return carry

    lax.fori_loop(0, HG_TB // rows // HG_STACKS_PER_ITER, stacks, 0)


def _hgrn2(proj, lb_logits, norm_g, seq):
    hp = HG_HEADS_PER_STEP
    width = hp * B_KEY_DIM
    col = lambda off: pl.BlockSpec((HG_TB, width), lambda h, t: (t, off // hp + h))
    n_lb = lb_logits.shape[0]
    return pl.pallas_call(
        _hgrn_body,
        grid=(B_HEADS // hp, seq // HG_TB),
        in_specs=[col(_QB_BLK), col(_FB_BLK), col(_IB_BLK), col(_GB_BLK),
                  pl.BlockSpec((n_lb, width), lambda h, t: (0, h)),
                  pl.BlockSpec((1, width), lambda h, t: (0, h))],
        out_specs=pl.BlockSpec((HG_TB, width), lambda h, t: (t, h)),
        out_shape=jax.ShapeDtypeStruct((seq, B_WIDTH), BF16),
        scratch_shapes=[pltpu.VMEM((hp, B_KEY_DIM, B_KEY_DIM), F32)],
        compiler_params=_params("arbitrary", "arbitrary"),
        name="hgrn2",
    )(proj, proj, proj, proj, lb_logits.astype(F32), norm_g.reshape(1, B_WIDTH).astype(F32))


MERGE_GATE_BLK = 512


def _merge_body(oa_ref, ob_ref, *rest):
    n_g = (len(rest) - 5) // 2
    ga_refs, gb_refs = rest[:n_g], rest[n_g:2 * n_g]
    wa_ref, wb_ref, o_ref, wa_s, wb_s = rest[2 * n_g:]

    @pl.when(pl.program_id(1) == 0)
    def _():
        wa_s[...] = wa_ref[...].astype(BF16)
        wb_s[...] = wb_ref[...].astype(BF16)

    ya = jnp.dot(oa_ref[...], wa_s[...], preferred_element_type=F32)
    yb = jnp.dot(ob_ref[...], wb_s[...], preferred_element_type=F32)
    ga = jnp.concatenate([r[...] for r in ga_refs], axis=1)
    gb = jnp.concatenate([r[...] for r in gb_refs], axis=1)
    o_ref[...] = (jax.nn.sigmoid(ga) * ya + jax.nn.sigmoid(gb) * yb).astype(o_ref.dtype)


def _branch_merge(oa, ob, proj, wa, wb, tm=512, tn=1024):
    m = oa.shape[0]
    gw = MERGE_GATE_BLK
    n_g = tn // gw
    ga0, gb0 = _GATE_A_COL // gw, _GATE_B_COL // gw
    gate = lambda off, u: pl.BlockSpec((tm, gw), lambda j, i: (i, off + j * n_g + u))
    return pl.pallas_call(
        _merge_body,
        grid=(D_MODEL // tn, m // tm),
        in_specs=[pl.BlockSpec((tm, A_WIDTH), lambda j, i: (i, 0)), pl.BlockSpec((tm, B_WIDTH), lambda j, i: (i, 0)),
                  *[gate(ga0, u) for u in range(n_g)], *[gate(gb0, u) for u in range(n_g)],
                  pl.BlockSpec((A_WIDTH, tn), lambda j, i: (0, j)), pl.BlockSpec((B_WIDTH, tn), lambda j, i: (0, j))],
        out_specs=pl.BlockSpec((tm, tn), lambda j, i: (i, j)),
        out_shape=jax.ShapeDtypeStruct((m, D_MODEL), BF16),
        scratch_shapes=[pltpu.VMEM((A_WIDTH, tn), BF16), pltpu.VMEM((B_WIDTH, tn), BF16)],
        compiler_params=_params("arbitrary", "arbitrary"),
        name="branch_merge",
    )(oa, ob, *([proj] * (2 * n_g)), wa, wb)


def _route_rows(h, g, rw, rb):
    hn = _rmsnorm_rows(h, g)
    packed = _pack_bf16_pairs(hn)
    hn_hi = hn.astype(BF16)
    hn_lo = (hn - hn_hi.astype(F32)).astype(BF16)
    rw_hi = rw.astype(BF16)
    rw_lo = (rw - rw_hi.astype(F32)).astype(BF16)
    logits = (jnp.dot(hn_hi, rw_hi, preferred_element_type=F32)
              + (jnp.dot(hn_lo, rw_hi, preferred_element_type=F32)
                 + jnp.dot(hn_hi, rw_lo, preferred_element_type=F32))) + rb
    lane = lax.broadcasted_iota(jnp.int32, logits.shape, 1)
    neg = jnp.float32(-jnp.inf)
    big = jnp.int32(LANES)

    is_g = lane < N_GROUPS
    lg = jnp.where(is_g, logits, neg)
    mg = jnp.max(lg, axis=-1, keepdims=True)
    g_idx = jnp.min(jnp.where(lg == mg, lane, big), axis=-1, keepdims=True)
    pg_top = 1.0 / jnp.sum(jnp.where(is_g, jnp.exp(lg - mg), 0.0), axis=-1, keepdims=True)

    lo = N_GROUPS + g_idx * EXPERTS_PER_GROUP
    in_grp = jnp.logical_and(lane >= lo, lane < lo + EXPERTS_PER_GROUP)
    le = jnp.where(in_grp, logits, neg)
    v1 = jnp.max(le, axis=-1, keepdims=True)
    i1 = jnp.min(jnp.where(le == v1, lane, big), axis=-1, keepdims=True)
    le2 = jnp.where(lane == i1, neg, le)
    v2 = jnp.max(le2, axis=-1, keepdims=True)
    i2 = jnp.min(jnp.where(le2 == v2, lane, big), axis=-1, keepdims=True)
    e2 = jnp.exp(v2 - v1)
    w1 = pg_top / (1.0 + e2)
    w2 = pg_top * e2 / (1.0 + e2)
    eid = jnp.where(lane == 0, i1 - N_GROUPS, jnp.where(lane == 1, i2 - N_GROUPS, 0))
    wk = jnp.where(lane == 0, w1, jnp.where(lane == 1, w2, 0.0))
    return packed, eid, wk


def _out_router_body(a_ref, w_ref, x_ref, g_ref, rw_ref, rb_ref, h1_ref, hn_ref, eid_ref, wk_ref, h_prev):
    @pl.when(pl.program_id(0) == 0)
    def _():
        h_prev[...] = jnp.zeros_like(h_prev)

    hn_ref[...], eid_ref[...], wk_ref[...] = _route_rows(h_prev[...], g_ref[...], rw_ref[...], rb_ref[...])
    h1 = x_ref[...] + jnp.dot(a_ref[...], w_ref[...], preferred_element_type=F32)
    h1_ref[...] = h1
    h_prev[...] = h1


def _out_proj_router(a, w_bf16, x, g, rw, rb, tm=512):
    m, d = x.shape
    n_blk = m // tm
    cur = lambda width: pl.BlockSpec((tm, width), lambda i: (jnp.minimum(i, n_blk - 1), 0))
    prev = lambda width: pl.BlockSpec((tm, width), lambda i: (jnp.maximum(i - 1, 0), 0))
    const = lambda shape, **kw: pl.BlockSpec(shape, lambda i: (0, 0), **kw)
    return pl.pallas_call(
        _out_router_body,
        grid=(n_blk + 1,),
        in_specs=[cur(a.shape[1]), const(w_bf16.shape, pipeline_mode=pl.Buffered(1)), cur(d),
                  const((1, d)), const((d, LANES)), const((1, LANES))],
        out_specs=[cur(d), prev(d // 2), prev(LANES), prev(LANES)],
        out_shape=[jax.ShapeDtypeStruct((m, d), F32), jax.ShapeDtypeStruct((m, d // 2), jnp.uint32),
                   jax.ShapeDtypeStruct((m, LANES), jnp.int32), jax.ShapeDtypeStruct((m, LANES), F32)],
        scratch_shapes=[pltpu.VMEM((tm, d), F32)],
        compiler_params=_params("arbitrary"),
        name="out_proj_router",
    )(a, w_bf16, x, g.reshape(1, d).astype(F32), rw, rb)


MOE_TM = 256
ROW_DMA_PRIORITIES = (0, 1)


def _moe_body(tile_e_ref, tile_blk_ref, tile_rows_ref, n_used_ref,
              src0_ref, src_next_ref, dst_prev_ref, hn_ref, w1_ref, w3_ref, w2_ref, y_ref,
              w1_s, w3_s, w2_s, x0, x1, o0, o1, gsem, ssem):
    j = pl.program_id(0)
    n_tiles = pl.num_programs(0)
    n_used = n_used_ref[0]
    xs = (x0, x1)
    os_ = (o0, o1)

    def rows_of(t):
        return jnp.where(t >= 0, tile_rows_ref[jnp.clip(t, 0, n_tiles - 1)], 0)

    def gather_issue(idx_ref, b, n):
        for r in range(MOE_TM):
            @pl.when(r < n)
            def _(r=r):
                pltpu.make_async_copy(hn_ref.at[idx_ref[0, 0, r]], xs[b].at[r],
                                      gsem.at[b]).start(priority=ROW_DMA_PRIORITIES[r % 2])

    def scatter_issue(idx_ref, b, n):
        for r in range(MOE_TM):
            @pl.when(r < n)
            def _(r=r):
                pltpu.make_async_copy(os_[b].at[r], y_ref.at[idx_ref[0, 0, r]],
                                      ssem.at[b]).start(priority=ROW_DMA_PRIORITIES[r % 2])

    def wait_rows(make_copy, n):
        p = MOE_TM
        while p >= 1:
            @pl.when(jnp.bitwise_and(n, p) != 0)
            def _(p=p):
                make_copy(p).wait()
            p //= 2

    def gather_wait(b, n):
        wait_rows(lambda p: pltpu.make_async_copy(hn_ref.at[pl.ds(0, p)], xs[b].at[pl.ds(0, p)], gsem.at[b]), n)

    def scatter_wait(b, n):
        wait_rows(lambda p: pltpu.make_async_copy(os_[b].at[pl.ds(0, p)], y_ref.at[pl.ds(0, p)], ssem.at[b]), n)

    @pl.when(j == 0)
    def _():
        x0[...] = jnp.zeros_like(x0)
        x1[...] = jnp.zeros_like(x1)
        gather_issue(src0_ref, 0, rows_of(0))

    for b in (0, 1):
        @pl.when(jnp.logical_and(j < n_used, j % 2 == b))
        def _(b=b):
            gather_wait(b, rows_of(j))
            scatter_wait(b, rows_of(j - 2))

            prev_e = tile_e_ref[jnp.maximum(j - 1, 0)]

            @pl.when(jnp.logical_or(j == 0, tile_e_ref[j] != prev_e))
            def _():
                w1_s[...] = w1_ref[0].astype(BF16)
                w3_s[...] = w3_ref[0].astype(BF16)
                w2_s[...] = w2_ref[0].astype(BF16)

            gather_issue(src_next_ref, 1 - b, rows_of(j + 1))
            scatter_issue(dst_prev_ref, 1 - b, rows_of(j - 1))
            x = _unpack_bf16_pairs(xs[b][...]).astype(BF16)
            hid = (jax.nn.silu(jnp.dot(x, w1_s[...], preferred_element_type=F32))
                   * jnp.dot(x, w3_s[...], preferred_element_type=F32))
            os_[b][...] = _pack_bf16_pairs(jnp.dot(hid.astype(BF16), w2_s[...], preferred_element_type=F32))

    for b in (0, 1):
        @pl.when(jnp.logical_and(j == n_used, j % 2 == b))
        def _(b=b):
            scatter_wait(b, rows_of(j - 2))
            scatter_issue(dst_prev_ref, 1 - b, rows_of(j - 1))
            scatter_wait(1 - b, rows_of(j - 1))


def _moe_experts(hn, w1, w3, w2, tile_e, tile_blk, tile_rows, n_used, src, dst, n_tokens):
    dp = hn.shape[1]
    d = 2 * dp
    n_tiles = tile_e.shape[0]
    ff = w1.shape[-1]
    idx_block = lambda fn: pl.BlockSpec((1, 1, MOE_TM), fn, memory_space=pltpu.SMEM)
    grid_spec = pltpu.PrefetchScalarGridSpec(
        num_scalar_prefetch=4,
        grid=(n_tiles,),
        in_specs=[
            idx_block(lambda j, te, tb, tr, nu: (0, 0, 0)),
            idx_block(lambda j, te, tb, tr, nu: (tb[jnp.minimum(j + 1, n_tiles - 1)], 0, 0)),
            idx_block(lambda j, te, tb, tr, nu: (tb[jnp.maximum(j - 1, 0)], 0, 0)),
            pl.BlockSpec(memory_space=pl.ANY),
            pl.BlockSpec((1, d, ff), lambda j, te, tb, tr, nu: (te[j], 0, 0)),
            pl.BlockSpec((1, d, ff), lambda j, te, tb, tr, nu: (te[j], 0, 0)),
            pl.BlockSpec((1, ff, d), lambda j, te, tb, tr, nu: (te[j], 0, 0)),
        ],
        out_specs=pl.BlockSpec(memory_space=pl.ANY),
        scratch_shapes=[pltpu.VMEM((d, ff), BF16), pltpu.VMEM((d, ff), BF16), pltpu.VMEM((ff, d), BF16),
                        pltpu.VMEM((MOE_TM, dp), jnp.uint32), pltpu.VMEM((MOE_TM, dp), jnp.uint32),
                        pltpu.VMEM((MOE_TM, dp), jnp.uint32), pltpu.VMEM((MOE_TM, dp), jnp.uint32),
                        pltpu.SemaphoreType.DMA((2,)), pltpu.SemaphoreType.DMA((2,))],
    )
    src3 = src.reshape(n_tiles, 1, MOE_TM)
    return pl.pallas_call(
        _moe_body,
        grid_spec=grid_spec,
        out_shape=jax.ShapeDtypeStruct((2 * n_tokens, dp), jnp.uint32),
        compiler_params=_params("arbitrary"),
        name="moe_experts",
    )(tile_e, tile_blk, tile_rows, n_used, src3, src3, dst.reshape(n_tiles, 1, MOE_TM), hn, w1, w3, w2)


def _combine_body(h_ref, y0_ref, y1_ref, wk_ref, g_ref, o_ref):
    wk = wk_ref[...]
    y = (h_ref[...] + wk[:, 0:1] * _unpack_bf16_pairs(y0_ref[...])
         + wk[:, 1:2] * _unpack_bf16_pairs(y1_ref[...]))
    o_ref[...] = _rmsnorm_rows(y, g_ref[...]).astype(o_ref.dtype)


def _combine(h, y2, wk, g, out_dtype, tm=256):
    m, d = h.shape
    nblk = m // tm
    return pl.pallas_call(
        _combine_body,
        grid=(nblk,),
        in_specs=[pl.BlockSpec((tm, d), lambda i: (i, 0)), pl.BlockSpec((tm, d // 2), lambda i: (i, 0)),
                  pl.BlockSpec((tm, d // 2), lambda i: (nblk + i, 0)), pl.BlockSpec((tm, LANES), lambda i: (i, 0)),
                  pl.BlockSpec((1, d), lambda i: (0, 0))],
        out_specs=pl.BlockSpec((tm, d), lambda i: (i, 0)),
        out_shape=jax.ShapeDtypeStruct((m, d), out_dtype),
        compiler_params=_params("arbitrary"),
        name="combine_final_norm",
    )(h, y2, y2, wk, g.reshape(1, d).astype(F32))


def _routing_tables(eid, n_tokens):
    n_pairs = 2 * n_tokens
    n_tiles = n_pairs // MOE_TM + N_EXPERTS + 1
    n_rows = n_tiles * MOE_TM
    experts = jnp.arange(N_EXPERTS, dtype=jnp.int32)[None, :]
    e_flat = eid.reshape(n_pairs).astype(jnp.int32)
    counts = jnp.sum(e_flat[:, None] == experts, axis=0, dtype=jnp.int32)
    tiles_per_e = (counts + MOE_TM - 1) // MOE_TM
    tile_end = jnp.cumsum(tiles_per_e)
    tile_start = tile_end - tiles_per_e
    n_used = tile_end[-1]
    filler_end = jnp.cumsum(tiles_per_e * MOE_TM - counts)
    filler = jnp.arange(n_rows - n_pairs, dtype=jnp.int32)[:, None]
    filler_e = jnp.sum(filler_end[None, :] <= filler, axis=1, dtype=jnp.int32)
    keys = jnp.concatenate([2 * e_flat, 2 * filler_e + 1])
    order = jnp.argsort(keys, stable=True).astype(jnp.int32)
    valid = order < n_pairs
    pair = jnp.where(valid, order, 0)
    src_tok = pair // 2
    dst_row = (pair % 2) * n_tokens + pair // 2
    j = jnp.arange(n_tiles, dtype=jnp.int32)
    tile_blk = jnp.minimum(j, n_used - 1)
    tile_e = jnp.sum(tile_end[None, :] <= tile_blk[:, None], axis=1, dtype=jnp.int32)
    own = tile_e[:, None] == experts
    tile_row0 = (tile_blk - jnp.sum(jnp.where(own, tile_start[None, :], 0), axis=1)) * MOE_TM
    own_count = jnp.sum(jnp.where(own, counts[None, :], 0), axis=1)
    tile_rows = jnp.where(j == tile_blk, jnp.clip(own_count - tile_row0, 0, MOE_TM), 0).astype(jnp.int32)
    return (tile_e, tile_blk.astype(jnp.int32), tile_rows, n_used.reshape(1).astype(jnp.int32),
            src_tok.astype(jnp.int32), dst_row.astype(jnp.int32))


def kernel(x, mix_norm_g, w_in, hgrn_lb_logits, hgrn_norm_g, w_branch_a, w_branch_b, w_out, ffn_norm_g,
           router_w_group, router_b_group, router_w_expert, router_b_expert, expert_w1, expert_w3, expert_w2,
           final_norm_g):
    b, seq, d = x.shape
    assert b == 1 and d == D_MODEL and seq % ATT_SUPER == 0 and w_in.shape == (1, D_MODEL, IN_WIDTH)
    h0 = x.reshape(seq, d).astype(F32)

    xn = _rmsnorm(h0, mix_norm_g[0], BF16)
    proj = _matmul(xn, w_in[0], F32, tm=1024, tn=1280, name="in_proj")
    oa = _dilated_attention(proj, seq)
    ob = _hgrn2(proj, hgrn_lb_logits, hgrn_norm_g[0], seq)
    merged = _branch_merge(oa, ob, proj, w_branch_a[0], w_branch_b[0])
    pad = LANES - N_GROUPS - N_EXPERTS
    rw = jnp.concatenate([router_w_group[0], router_w_expert[0], jnp.zeros((d, pad), F32)], axis=1).astype(F32)
    rb = jnp.concatenate([router_b_group[0], router_b_expert[0], jnp.zeros((pad,), F32)]).reshape(1, LANES).astype(F32)
    h1, hn, eid, wk = _out_proj_router(merged, w_out[0].astype(BF16), h0, ffn_norm_g[0], rw, rb)

    tile_e, tile_blk, tile_rows, n_used, src_tok, dst_row = _routing_tables(eid[:, :2], seq)
    w1 = expert_w1[0].reshape(N_EXPERTS, D_MODEL, EXPERT_FF)
    w3 = expert_w3[0].reshape(N_EXPERTS, D_MODEL, EXPERT_FF)
    w2 = expert_w2[0].reshape(N_EXPERTS, EXPERT_FF, D_MODEL)
    y2 = _moe_experts(hn, w1, w3, w2, tile_e, tile_blk, tile_rows, n_used, src_tok, dst_row, seq)

    out = _combine(h1, y2, wk, final_norm_g, x.dtype)
    return out.reshape(b, seq, d)
```

```python
import functools

import jax
import jax.numpy as jnp
from jax import lax
from jax.experimental import pallas as pl
from jax.experimental.pallas import tpu as pltpu

F32 = jnp.float32
BF16 = jnp.bfloat16

D_MODEL = 2048
A_HEADS = 12
A_HEAD_DIM = 128
A_WIDTH = A_HEADS * A_HEAD_DIM
A_SCALE = A_HEAD_DIM ** -0.5
DILATED_CONFIGS = ((128, 1), (512, 4), (2048, 16))
B_HEADS = 8
B_KEY_DIM = 128
B_WIDTH = B_HEADS * B_KEY_DIM
N_GROUPS = 4
EXPERTS_PER_GROUP = 8
N_EXPERTS = N_GROUPS * EXPERTS_PER_GROUP
EXPERT_FF = 512
NORM_EPS = 1e-6

LANES = 128
SUBLANES = 8
VMEM_LIMIT = 56 * 1024 * 1024

_QA_BLK = 0
_KA_BLK = A_HEADS
_VA_BLK = 2 * A_HEADS
_QB_BLK = 3 * A_HEADS
_FB_BLK = _QB_BLK + B_HEADS
_IB_BLK = _FB_BLK + B_HEADS
_GB_BLK = _IB_BLK + B_HEADS
_GATE_A_COL = 3 * A_WIDTH + 4 * B_WIDTH
_GATE_B_COL = _GATE_A_COL + D_MODEL
IN_WIDTH = _GATE_B_COL + D_MODEL


def _params(*sem):
    return pltpu.CompilerParams(dimension_semantics=sem, vmem_limit_bytes=VMEM_LIMIT)


def _rmsnorm_rows(x, g):
    ms = jnp.mean(x * x, axis=-1, keepdims=True)
    return x * lax.rsqrt(ms + NORM_EPS) * g


def _rmsnorm_body(x_ref, g_ref, o_ref):
    o_ref[...] = _rmsnorm_rows(x_ref[...].astype(F32), g_ref[...]).astype(o_ref.dtype)


def _rmsnorm(x, g, out_dtype, tm=512):
    m, d = x.shape
    return pl.pallas_call(
        _rmsnorm_body,
        grid=(m // tm,),
        in_specs=[pl.BlockSpec((tm, d), lambda i: (i, 0)), pl.BlockSpec((1, d), lambda i: (0, 0))],
        out_specs=pl.BlockSpec((tm, d), lambda i: (i, 0)),
        out_shape=jax.ShapeDtypeStruct((m, d), out_dtype),
        compiler_params=_params("arbitrary"),
        name="rmsnorm",
    )(x, g.reshape(1, d).astype(F32))


def _matmul_body(a_ref, w_ref, o_ref, wb_ref):
    @pl.when(pl.program_id(1) == 0)
    def _():
        wb_ref[...] = w_ref[...].astype(BF16)

    o_ref[...] = jnp.dot(a_ref[...], wb_ref[...], preferred_element_type=F32).astype(o_ref.dtype)


def _matmul(a, w, out_dtype, tm, tn, name):
    m, k = a.shape
    n = w.shape[1]
    return pl.pallas_call(
        _matmul_body,
        grid=(n // tn, m // tm),
        in_specs=[pl.BlockSpec((tm, k), lambda j, i: (i, 0)), pl.BlockSpec((k, tn), lambda j, i: (0, j))],
        out_specs=pl.BlockSpec((tm, tn), lambda j, i: (i, j)),
        out_shape=jax.ShapeDtypeStruct((m, n), out_dtype),
        scratch_shapes=[pltpu.VMEM((k, tn), BF16)],
        compiler_params=_params("arbitrary", "arbitrary"),
        name=name,
    )(a, w)


LOG2_E = 1.4426950408889634
LN_2 = 0.6931471805599453
ATT_BLK = 128
ATT_SUPER = 2048
ATT_DEINT = 4


def _attn_body(q_ref, k_ref, v_ref, o_ref, o_scr, lse_scr, bias_scr, k4, v4, q4, stage):
    sb = pl.program_id(1)
    seq = k_ref.shape[0]
    nd = ATT_DEINT
    sub = ATT_SUPER // nd
    diff = (lax.broadcasted_iota(jnp.int32, (ATT_BLK, 2 * ATT_BLK), 1)
            - lax.broadcasted_iota(jnp.int32, (ATT_BLK, 2 * ATT_BLK), 0))
    neg = jnp.float32(-jnp.inf)
    bias_scr[0] = jnp.where(jnp.logical_and(diff >= 0, diff <= ATT_BLK), 0.0, neg)
    bias_scr[1] = jnp.where(diff <= 0, 0.0, neg)

    @pl.when(sb == 0)
    def _():
        rows = 512

        def split(i, carry):
            for a in range(nd):
                src = pl.ds(i * (rows * nd) + a, rows, stride=nd)
                dst = pl.ds(pl.multiple_of(i * rows, rows), rows)
                k4[a, dst, :] = k_ref[src, :]
                v4[a, dst, :] = v_ref[src, :]
            return carry

        lax.fori_loop(0, seq // nd // rows, split, 0)

    for a in range(nd):
        q4[a] = q_ref[pl.ds(a, sub, stride=nd), :]

    def attend(q, k, v, bias):
        qb = (q * (A_SCALE * LOG2_E)).astype(BF16)
        s = lax.dot_general(qb, k.astype(BF16), (((1,), (1,)), ((), ())), preferred_element_type=F32)
        s = s + bias
        mx = jnp.max(s, axis=-1, keepdims=True)
        p = jnp.exp2(s - mx)
        den = jnp.sum(p, axis=-1, keepdims=True)
        acc = jnp.dot(p.astype(BF16), v.astype(BF16), preferred_element_type=F32)
        return acc / den, jnp.broadcast_to(mx * LN_2 + jnp.log(den), (ATT_BLK, A_HEAD_DIM))

    n_tiles = ATT_SUPER // ATT_BLK
    for c, (window, dil) in enumerate(DILATED_CONFIGS):
        assert window // dil == ATT_BLK and (dil == 1 or dil % nd == 0)
        tiles_per_res = n_tiles // dil
        for t in range(n_tiles):
            n, r = t % tiles_per_res, t // tiles_per_res
            if n == 0:
                first = (sb == 0).astype(jnp.int32)
                back = ATT_BLK * (1 - first)
                bias = bias_scr[first]
            else:
                back, bias = ATT_BLK, bias_scr[0]
            if dil == 1:
                q_rows = pl.ds(n * ATT_BLK, ATT_BLK)
                kv_rows = pl.ds(pl.multiple_of(sb * ATT_SUPER + n * ATT_BLK - back, ATT_BLK), 2 * ATT_BLK)
                o, lse = attend(q_ref[q_rows, :], k_ref[kv_rows, :], v_ref[kv_rows, :], bias)
                out_rows = q_rows
            else:
                st, a, r2 = dil // nd, r % nd, r // nd
                i0 = n * ATT_BLK
                q_rows = pl.ds(st * i0 + r2, ATT_BLK, stride=st) if st > 1 else pl.ds(i0, ATT_BLK)
                kv0 = st * (sb * (ATT_SUPER // dil) + i0 - back) + r2
                kv_rows = pl.ds(kv0, 2 * ATT_BLK, stride=st) if st > 1 else pl.ds(kv0, 2 * ATT_BLK)
                o, lse = attend(q4[a, q_rows, :], k4[a, kv_rows, :], v4[a, kv_rows, :], bias)
                out_rows = (pl.ds(a * sub + st * i0 + r2, ATT_BLK, stride=st) if st > 1
                            else pl.ds(a * sub + i0, ATT_BLK))
            o_scr[c, out_rows, :] = o
            lse_scr[c, out_rows, :] = lse

    rows = 128

    def merge(i, carry):
        for a in range(nd):
            tok = pl.ds(i * (rows * nd) + a, rows, stride=nd)
            cls = pl.ds(a * sub + pl.multiple_of(i * rows, rows), rows)
            l0, l1, l2 = lse_scr[0, tok, :], lse_scr[1, cls, :], lse_scr[2, cls, :]
            m = jnp.maximum(jnp.maximum(l0, l1), l2)
            w0, w1, w2 = jnp.exp(l0 - m), jnp.exp(l1 - m), jnp.exp(l2 - m)
            num = w0 * o_scr[0, tok, :] + w1 * o_scr[1, cls, :] + w2 * o_scr[2, cls, :]
            stage[tok, :] = num / (w0 + w1 + w2)
        return carry

    lax.fori_loop(0, sub // rows, merge, 0)
    o_ref[...] = stage[...].astype(o_ref.dtype)


def _dilated_attention(proj, seq):
    n_super = seq // ATT_SUPER
    blk = lambda off: pl.BlockSpec((seq, A_HEAD_DIM), lambda h, s: (0, off + h))
    return pl.pallas_call(
        _attn_body,
        grid=(A_HEADS, n_super),
        in_specs=[pl.BlockSpec((ATT_SUPER, A_HEAD_DIM), lambda h, s: (s, _QA_BLK + h)), blk(_KA_BLK), blk(_VA_BLK)],
        out_specs=pl.BlockSpec((ATT_SUPER, A_HEAD_DIM), lambda h, s: (s, h)),
        out_shape=jax.ShapeDtypeStruct((seq, A_WIDTH), BF16),
        scratch_shapes=[pltpu.VMEM((len(DILATED_CONFIGS), ATT_SUPER, A_HEAD_DIM), F32),
                        pltpu.VMEM((len(DILATED_CONFIGS), ATT_SUPER, A_HEAD_DIM), F32),
                        pltpu.VMEM((2, ATT_BLK, 2 * ATT_BLK), F32),
                        pltpu.VMEM((ATT_DEINT, seq // ATT_DEINT, A_HEAD_DIM), F32),
                        pltpu.VMEM((ATT_DEINT, seq // ATT_DEINT, A_HEAD_DIM), F32),
                        pltpu.VMEM((ATT_DEINT, ATT_SUPER // ATT_DEINT, A_HEAD_DIM), F32),
                        pltpu.VMEM((ATT_SUPER, A_HEAD_DIM), F32)],
        compiler_params=_params("arbitrary", "arbitrary"),
        name="dilated_attention",
    )(proj, proj, proj)


HG_CHUNK = 64
HG_TB = 512
HG_HEADS_PER_STEP = 2
HG_CHUNKS_PER_STACK = 2
HG_STACKS_PER_ITER = 2


def _hgrn_ref_rows(bc, half):
    n, width = bc.shape
    blk = 2 * half
    if blk >= SUBLANES:
        rows = [jnp.broadcast_to(bc[b0 + half - 1:b0 + half, :], (blk, width)) for b0 in range(0, n, blk)]
        return jnp.concatenate(rows, axis=0) if len(rows) > 1 else rows[0]
    sub = lax.broadcasted_iota(jnp.int32, (SUBLANES, width), 0)
    groups = []
    for g0 in range(0, n, SUBLANES):
        grp = bc[g0:g0 + SUBLANES, :]
        if half == 1:
            groups.append(jnp.where(jnp.bitwise_and(sub, 1) == 1, pltpu.roll(grp, 1, 0), grp))
        else:
            assert half == 2 and SUBLANES == 8
            groups.append(jnp.where(sub < 4, jnp.broadcast_to(grp[1:2, :], grp.shape),
                                    jnp.broadcast_to(grp[5:6, :], grp.shape)))
    return jnp.concatenate(groups, axis=0)


def _hgrn_body(qb_ref, fb_ref, ib_ref, gb_ref, lbl_ref, g_ref, o_ref, st_ref):
    @pl.when(pl.program_id(1) == 0)
    def _():
        st_ref[...] = jnp.zeros_like(st_ref)

    c_ = HG_CHUNK
    lbl = lbl_ref[...]
    e = jnp.exp(lbl - jnp.max(lbl, axis=0, keepdims=True))
    lb_all = e[0:1, :] / jnp.sum(e, axis=0, keepdims=True)
    g_all = g_ref[...]

    nh, nc = HG_HEADS_PER_STEP, HG_CHUNKS_PER_STACK
    rows = nc * c_
    n = nh * rows
    ti = lax.broadcasted_iota(jnp.int32, (n, n), 0)
    si = lax.broadcasted_iota(jnp.int32, (n, n), 1)
    xor = jnp.bitwise_xor(ti, si)
    causal = jnp.logical_and(xor < c_, si <= ti)
    tri = jnp.where(causal, 1.0, 0.0).astype(BF16)

    def stack_of(ref, r0):
        return jnp.concatenate([ref[pl.ds(r0, rows), h * B_KEY_DIM:(h + 1) * B_KEY_DIM] for h in range(nh)], axis=0)

    def per_head_rows(x):
        return jnp.concatenate([jnp.broadcast_to(x[:, h * B_KEY_DIM:(h + 1) * B_KEY_DIM], (rows, B_KEY_DIM))
                                for h in range(nh)], axis=0)

    lb = per_head_rows(lb_all)
    gnorm = per_head_rows(g_all)

    def stack(i):
        r0 = pl.multiple_of(i * rows, rows)
        f = lb + (1.0 - lb) * jax.nn.sigmoid(stack_of(fb_ref, r0))
        logf = jnp.log(f)
        kk = 1.0 - f
        q = jax.nn.silu(stack_of(qb_ref, r0))
        v = stack_of(ib_ref, r0)
        vb = v.astype(BF16)
        hi = logf.astype(BF16)
        rem = logf - hi.astype(F32)
        mid = rem.astype(BF16)
        low = (rem - mid.astype(F32)).astype(BF16)
        parts = jnp.dot(tri, jnp.concatenate([hi, mid, low], axis=1), preferred_element_type=F32)
        bc = (parts[:, :B_KEY_DIM] + (parts[:, B_KEY_DIM:2 * B_KEY_DIM] + parts[:, 2 * B_KEY_DIM:])) * LOG2_E

        attn = lax.dot_general(q.astype(BF16), kk.astype(BF16), (((1,), (1,)), ((), ())),
                               preferred_element_type=F32)
        half = 1
        while half < c_:
            dec = jnp.exp2(-jnp.abs(bc - _hgrn_ref_rows(bc, half)))
            s = lax.dot_general((q * dec).astype(BF16), (kk * dec).astype(BF16), (((1,), (1,)), ((), ())),
                                preferred_element_type=F32)
            attn = jnp.where(xor >= half, s, attn)
            half *= 2
        attn = jnp.where(causal, attn, 0.0)
        o = jnp.dot(attn.astype(BF16), vb, preferred_element_type=F32)

        last = jnp.concatenate([jnp.broadcast_to(bc[a + c_ - 1:a + c_, :], (c_, B_KEY_DIM))
                                for a in range(0, n, c_)], axis=0)
        q_dec = (q * jnp.exp2(bc)).astype(BF16)
        k_dec = (kk * jnp.exp2(last - bc)).astype(BF16)
        st_dec = jnp.exp2(last)
        inter = []
        for h in range(nh):
            st = st_ref[h]
            for c in range(nc):
                a = h * rows + c * c_
                inter.append(lax.dot_general(q_dec[a:a + c_], st.astype(BF16), (((1,), (1,)), ((), ())),
                                             preferred_element_type=F32))
                upd = lax.dot_general(vb[a:a + c_], k_dec[a:a + c_], (((0,), (0,)), ((), ())),
                                      preferred_element_type=F32)
                st = st * st_dec[a:a + 1] + upd
            st_ref[h] = st
        o = o + jnp.concatenate(inter, axis=0)

        o = o * lax.rsqrt(jnp.mean(o * o, axis=-1, keepdims=True) + NORM_EPS)
        res = (o * gnorm * jax.nn.silu(stack_of(gb_ref, r0))).astype(o_ref.dtype)
        for h in range(nh):
            o_ref[pl.ds(r0, rows), h * B_KEY_DIM:(h + 1) * B_KEY_DIM] = res[h * rows:(h + 1) * rows]

    def stacks(i, carry):
        for u in range(HG_STACKS_PER_ITER):
            stack(i * HG_STACKS_PER_ITER + u)
        return carry

    lax.fori_loop(0, HG_TB // rows // HG_STACKS_PER_ITER, stacks, 0)


def _hgrn2(proj, lb_logits, norm_g, seq):
    hp = HG_HEADS_PER_STEP
    width = hp * B_KEY_DIM
    col = lambda off: pl.BlockSpec((HG_TB, width), lambda h, t: (t, off // hp + h))
    n_lb = lb_logits.shape[0]
    return pl.pallas_call(
        _hgrn_body,
        grid=(B_HEADS // hp, seq // HG_TB),
        in_specs=[col(_QB_BLK), col(_FB_BLK), col(_IB_BLK), col(_GB_BLK),
                  pl.BlockSpec((n_lb, width), lambda h, t: (0, h)),
                  pl.BlockSpec((1, width), lambda h, t: (0, h))],
        out_specs=pl.BlockSpec((HG_TB, width), lambda h, t: (t, h)),
        out_shape=jax.ShapeDtypeStruct((seq, B_WIDTH), BF16),
        scratch_shapes=[pltpu.VMEM((hp, B_KEY_DIM, B_KEY_DIM), F32)],
        compiler_params=_params("arbitrary", "arbitrary"),
        name="hgrn2",
    )(proj, proj, proj, proj, lb_logits.astype(F32), norm_g.reshape(1, B_WIDTH).astype(F32))


MERGE_GATE_BLK = 512


def _merge_body(oa_ref, ob_ref, *rest):
    n_g = (len(rest) - 5) // 2
    ga_refs, gb_refs = rest[:n_g], rest[n_g:2 * n_g]
    wa_ref, wb_ref, o_ref, wa_s, wb_s = rest[2 * n_g:]

    @pl.when(pl.program_id(1) == 0)
    def _():
        wa_s[...] = wa_ref[...].astype(BF16)
        wb_s[...] = wb_ref[...].astype(BF16)

    ya = jnp.dot(oa_ref[...], wa_s[...], preferred_element_type=F32)
    yb = jnp.dot(ob_ref[...], wb_s[...], preferred_element_type=F32)
    ga = jnp.concatenate([r[...] for r in ga_refs], axis=1)
    gb = jnp.concatenate([r[...] for r in gb_refs], axis=1)
    o_ref[...] = (jax.nn.sigmoid(ga) * ya + jax.nn.sigmoid(gb) * yb).astype(o_ref.dtype)


def _branch_merge(oa, ob, proj, wa, wb, tm=512, tn=1024):
    m = oa.shape[0]
    gw = MERGE_GATE_BLK
    n_g = tn // gw
    ga0, gb0 = _GATE_A_COL // gw, _GATE_B_COL // gw
    gate = lambda off, u: pl.BlockSpec((tm, gw), lambda j, i: (i, off + j * n_g + u))
    return pl.pallas_call(
        _merge_body,
        grid=(D_MODEL // tn, m // tm),
        in_specs=[pl.BlockSpec((tm, A_WIDTH), lambda j, i: (i, 0)), pl.BlockSpec((tm, B_WIDTH), lambda j, i: (i, 0)),
                  *[gate(ga0, u) for u in range(n_g)], *[gate(gb0, u) for u in range(n_g)],
                  pl.BlockSpec((A_WIDTH, tn), lambda j, i: (0, j)), pl.BlockSpec((B_WIDTH, tn), lambda j, i: (0, j))],
        out_specs=pl.BlockSpec((tm, tn), lambda j, i: (i, j)),
        out_shape=jax.ShapeDtypeStruct((m, D_MODEL), BF16),
        scratch_shapes=[pltpu.VMEM((A_WIDTH, tn), BF16), pltpu.VMEM((B_WIDTH, tn), BF16)],
        compiler_params=_params("arbitrary", "arbitrary"),
        name="branch_merge",
    )(oa, ob, *([proj] * (2 * n_g)), wa, wb)


def _route_rows(h, g, rw, rb):
    hn = _rmsnorm_rows(h, g)
    hn_hi = hn.astype(BF16)
    hn_lo = (hn - hn_hi.astype(F32)).astype(BF16)
    rw_hi = rw.astype(BF16)
    rw_lo = (rw - rw_hi.astype(F32)).astype(BF16)
    logits = (jnp.dot(hn_hi, rw_hi, preferred_element_type=F32)
              + (jnp.dot(hn_lo, rw_hi, preferred_element_type=F32)
                 + jnp.dot(hn_hi, rw_lo, preferred_element_type=F32))) + rb
    lane = lax.broadcasted_iota(jnp.int32, logits.shape, 1)
    neg = jnp.float32(-jnp.inf)
    big = jnp.int32(LANES)

    is_g = lane < N_GROUPS
    lg = jnp.where(is_g, logits, neg)
    mg = jnp.max(lg, axis=-1, keepdims=True)
    g_idx = jnp.min(jnp.where(lg == mg, lane, big), axis=-1, keepdims=True)
    pg_top = 1.0 / jnp.sum(jnp.where(is_g, jnp.exp(lg - mg), 0.0), axis=-1, keepdims=True)

    lo = N_GROUPS + g_idx * EXPERTS_PER_GROUP
    in_grp = jnp.logical_and(lane >= lo, lane < lo + EXPERTS_PER_GROUP)
    le = jnp.where(in_grp, logits, neg)
    v1 = jnp.max(le, axis=-1, keepdims=True)
    i1 = jnp.min(jnp.where(le == v1, lane, big), axis=-1, keepdims=True)
    le2 = jnp.where(lane == i1, neg, le)
    v2 = jnp.max(le2, axis=-1, keepdims=True)
    i2 = jnp.min(jnp.where(le2 == v2, lane, big), axis=-1, keepdims=True)
    e2 = jnp.exp(v2 - v1)
    w1 = pg_top / (1.0 + e2)
    w2 = pg_top * e2 / (1.0 + e2)
    eid = jnp.where(lane == 0, i1 - N_GROUPS, jnp.where(lane == 1, i2 - N_GROUPS, 0))
    wk = jnp.where(lane == 0, w1, jnp.where(lane == 1, w2, 0.0))
    return hn, eid, wk


def _out_router_body(a_ref, w_ref, x_ref, g_ref, rw_ref, rb_ref, h1_ref, hn_ref, eid_ref, wk_ref, h_prev):
    @pl.when(pl.program_id(0) == 0)
    def _():
        h_prev[...] = jnp.zeros_like(h_prev)

    hn_ref[...], eid_ref[...], wk_ref[...] = _route_rows(h_prev[...], g_ref[...], rw_ref[...], rb_ref[...])
    h1 = x_ref[...] + jnp.dot(a_ref[...], w_ref[...], preferred_element_type=F32)
    h1_ref[...] = h1
    h_prev[...] = h1


def _out_proj_router(a, w_bf16, x, g, rw, rb, tm=512):
    m, d = x.shape
    n_blk = m // tm
    cur = lambda width: pl.BlockSpec((tm, width), lambda i: (jnp.minimum(i, n_blk - 1), 0))
    prev = lambda width: pl.BlockSpec((tm, width), lambda i: (jnp.maximum(i - 1, 0), 0))
    const = lambda shape, **kw: pl.BlockSpec(shape, lambda i: (0, 0), **kw)
    return pl.pallas_call(
        _out_router_body,
        grid=(n_blk + 1,),
        in_specs=[cur(a.shape[1]), const(w_bf16.shape, pipeline_mode=pl.Buffered(1)), cur(d),
                  const((1, d)), const((d, LANES)), const((1, LANES))],
        out_specs=[cur(d), prev(d), prev(LANES), prev(LANES)],
        out_shape=[jax.ShapeDtypeStruct((m, d), F32), jax.ShapeDtypeStruct((m, d), F32),
                   jax.ShapeDtypeStruct((m, LANES), jnp.int32), jax.ShapeDtypeStruct((m, LANES), F32)],
        scratch_shapes=[pltpu.VMEM((tm, d), F32)],
        compiler_params=_params("arbitrary"),
        name="out_proj_router",
    )(a, w_bf16, x, g.reshape(1, d).astype(F32), rw, rb)


MOE_TM = 256
ROW_DMA_PRIORITIES = (0, 1)


def _moe_body(tile_e_ref, tile_blk_ref, tile_rows_ref, n_used_ref,
              src0_ref, src_next_ref, dst_prev_ref, hn_ref, w1_ref, w3_ref, w2_ref, y_ref,
              w1_s, w3_s, w2_s, x0, x1, o0, o1, gsem, ssem):
    j = pl.program_id(0)
    n_tiles = pl.num_programs(0)
    n_used = n_used_ref[0]
    xs = (x0, x1)
    os_ = (o0, o1)

    def rows_of(t):
        return jnp.where(t >= 0, tile_rows_ref[jnp.clip(t, 0, n_tiles - 1)], 0)

    def gather_issue(idx_ref, b, n):
        for r in range(MOE_TM):
            @pl.when(r < n)
            def _(r=r):
                pltpu.make_async_copy(hn_ref.at[idx_ref[0, 0, r]], xs[b].at[r],
                                      gsem.at[b]).start(priority=ROW_DMA_PRIORITIES[r % 2])

    def scatter_issue(idx_ref, b, n):
        for r in range(MOE_TM):
            @pl.when(r < n)
            def _(r=r):
                pltpu.make_async_copy(os_[b].at[r], y_ref.at[idx_ref[0, 0, r]],
                                      ssem.at[b]).start(priority=ROW_DMA_PRIORITIES[r % 2])

    def wait_rows(make_copy, n):
        p = MOE_TM
        while p >= 1:
            @pl.when(jnp.bitwise_and(n, p) != 0)
            def _(p=p):
                make_copy(p).wait()
            p //= 2

    def gather_wait(b, n):
        wait_rows(lambda p: pltpu.make_async_copy(hn_ref.at[pl.ds(0, p)], xs[b].at[pl.ds(0, p)], gsem.at[b]), n)

    def scatter_wait(b, n):
        wait_rows(lambda p: pltpu.make_async_copy(os_[b].at[pl.ds(0, p)], y_ref.at[pl.ds(0, p)], ssem.at[b]), n)

    @pl.when(j == 0)
    def _():
        x0[...] = jnp.zeros_like(x0)
        x1[...] = jnp.zeros_like(x1)
        gather_issue(src0_ref, 0, rows_of(0))

    for b in (0, 1):
        @pl.when(jnp.logical_and(j < n_used, j % 2 == b))
        def _(b=b):
            gather_wait(b, rows_of(j))
            scatter_wait(b, rows_of(j - 2))

            prev_e = tile_e_ref[jnp.maximum(j - 1, 0)]

            @pl.when(jnp.logical_or(j == 0, tile_e_ref[j] != prev_e))
            def _():
                w1_s[...] = w1_ref[0].astype(BF16)
                w3_s[...] = w3_ref[0].astype(BF16)
                w2_s[...] = w2_ref[0].astype(BF16)

            gather_issue(src_next_ref, 1 - b, rows_of(j + 1))
            scatter_issue(dst_prev_ref, 1 - b, rows_of(j - 1))
            x = xs[b][...].astype(BF16)
            hid = (jax.nn.silu(jnp.dot(x, w1_s[...], preferred_element_type=F32))
                   * jnp.dot(x, w3_s[...], preferred_element_type=F32))
            os_[b][...] = jnp.dot(hid.astype(BF16), w2_s[...], preferred_element_type=F32)

    for b in (0, 1):
        @pl.when(jnp.logical_and(j == n_used, j % 2 == b))
        def _(b=b):
            scatter_wait(b, rows_of(j - 2))
            scatter_issue(dst_prev_ref, 1 - b, rows_of(j - 1))
            scatter_wait(1 - b, rows_of(j - 1))


def _moe_experts(hn, w1, w3, w2, tile_e, tile_blk, tile_rows, n_used, src, dst, n_tokens):
    d = hn.shape[1]
    n_tiles = tile_e.shape[0]
    ff = w1.shape[-1]
    idx_block = lambda fn: pl.BlockSpec((1, 1, MOE_TM), fn, memory_space=pltpu.SMEM)
    grid_spec = pltpu.PrefetchScalarGridSpec(
        num_scalar_prefetch=4,
        grid=(n_tiles,),
        in_specs=[
            idx_block(lambda j, te, tb, tr, nu: (0, 0, 0)),
            idx_block(lambda j, te, tb, tr, nu: (tb[jnp.minimum(j + 1, n_tiles - 1)], 0, 0)),
            idx_block(lambda j, te, tb, tr, nu: (tb[jnp.maximum(j - 1, 0)], 0, 0)),
            pl.BlockSpec(memory_space=pl.ANY),
            pl.BlockSpec((1, d, ff), lambda j, te, tb, tr, nu: (te[j], 0, 0)),
            pl.BlockSpec((1, d, ff), lambda j, te, tb, tr, nu: (te[j], 0, 0)),
            pl.BlockSpec((1, ff, d), lambda j, te, tb, tr, nu: (te[j], 0, 0)),
        ],
        out_specs=pl.BlockSpec(memory_space=pl.ANY),
        scratch_shapes=[pltpu.VMEM((d, ff), BF16), pltpu.VMEM((d, ff), BF16), pltpu.VMEM((ff, d), BF16),
                        pltpu.VMEM((MOE_TM, d), F32), pltpu.VMEM((MOE_TM, d), F32),
                        pltpu.VMEM((MOE_TM, d), F32), pltpu.VMEM((MOE_TM, d), F32),
                        pltpu.SemaphoreType.DMA((2,)), pltpu.SemaphoreType.DMA((2,))],
    )
    src3 = src.reshape(n_tiles, 1, MOE_TM)
    return pl.pallas_call(
        _moe_body,
        grid_spec=grid_spec,
        out_shape=jax.ShapeDtypeStruct((2 * n_tokens, d), F32),
        compiler_params=_params("arbitrary"),
        name="moe_experts",
    )(tile_e, tile_blk, tile_rows, n_used, src3, src3, dst.reshape(n_tiles, 1, MOE_TM), hn, w1, w3, w2)


def _combine_body(h_ref, y0_ref, y1_ref, wk_ref, g_ref, o_ref):
    wk = wk_ref[...]
    y = h_ref[...] + wk[:, 0:1] * y0_ref[...] + wk[:, 1:2] * y1_ref[...]
    o_ref[...] = _rmsnorm_rows(y, g_ref[...]).astype(o_ref.dtype)


def _combine(h, y2, wk, g, out_dtype, tm=256):
    m, d = h.shape
    nblk = m // tm
    return pl.pallas_call(
        _combine_body,
        grid=(nblk,),
        in_specs=[pl.BlockSpec((tm, d), lambda i: (i, 0)), pl.BlockSpec((tm, d), lambda i: (i, 0)),
                  pl.BlockSpec((tm, d), lambda i: (nblk + i, 0)), pl.BlockSpec((tm, LANES), lambda i: (i, 0)),
                  pl.BlockSpec((1, d), lambda i: (0, 0))],
        out_specs=pl.BlockSpec((tm, d), lambda i: (i, 0)),
        out_shape=jax.ShapeDtypeStruct((m, d), out_dtype),
        compiler_params=_params("arbitrary"),
        name="combine_final_norm",
    )(h, y2, y2, wk, g.reshape(1, d).astype(F32))


def _routing_tables(eid, n_tokens):
    n_pairs = 2 * n_tokens
    n_tiles = n_pairs // MOE_TM + N_EXPERTS + 1
    n_rows = n_tiles * MOE_TM
    experts = jnp.arange(N_EXPERTS, dtype=jnp.int32)[None, :]
    e_flat = eid.reshape(n_pairs).astype(jnp.int32)
    counts = jnp.sum(e_flat[:, None] == experts, axis=0, dtype=jnp.int32)
    tiles_per_e = (counts + MOE_TM - 1) // MOE_TM
    tile_end = jnp.cumsum(tiles_per_e)
    tile_start = tile_end - tiles_per_e
    n_used = tile_end[-1]
    filler_end = jnp.cumsum(tiles_per_e * MOE_TM - counts)
    filler = jnp.arange(n_rows - n_pairs, dtype=jnp.int32)[:, None]
    filler_e = jnp.sum(filler_end[None, :] <= filler, axis=1, dtype=jnp.int32)
    keys = jnp.concatenate([2 * e_flat, 2 * filler_e + 1])
    order = jnp.argsort(keys, stable=True).astype(jnp.int32)
    valid = order < n_pairs
    pair = jnp.where(valid, order, 0)
    src_tok = pair // 2
    dst_row = (pair % 2) * n_tokens + pair // 2
    j = jnp.arange(n_tiles, dtype=jnp.int32)
    tile_blk = jnp.minimum(j, n_used - 1)
    tile_e = jnp.sum(tile_end[None, :] <= tile_blk[:, None], axis=1, dtype=jnp.int32)
    own = tile_e[:, None] == experts
    tile_row0 = (tile_blk - jnp.sum(jnp.where(own, tile_start[None, :], 0), axis=1)) * MOE_TM
    own_count = jnp.sum(jnp.where(own, counts[None, :], 0), axis=1)
    tile_rows = jnp.where(j == tile_blk, jnp.clip(own_count - tile_row0, 0, MOE_TM), 0).astype(jnp.int32)
    return (tile_e, tile_blk.astype(jnp.int32), tile_rows, n_used.reshape(1).astype(jnp.int32),
            src_tok.astype(jnp.int32), dst_row.astype(jnp.int32))


def kernel(x, mix_norm_g, w_in, hgrn_lb_logits, hgrn_norm_g, w_branch_a, w_branch_b, w_out, ffn_norm_g,
           router_w_group, router_b_group, router_w_expert, router_b_expert, expert_w1, expert_w3, expert_w2,
           final_norm_g):
    b, seq, d = x.shape
    assert b == 1 and d == D_MODEL and seq % ATT_SUPER == 0 and w_in.shape == (1, D_MODEL, IN_WIDTH)
    h0 = x.reshape(seq, d).astype(F32)

    xn = _rmsnorm(h0, mix_norm_g[0], BF16)
    proj = _matmul(xn, w_in[0], F32, tm=1024, tn=1280, name="in_proj")
    oa = _dilated_attention(proj, seq)
    ob = _hgrn2(proj, hgrn_lb_logits, hgrn_norm_g[0], seq)
    merged = _branch_merge(oa, ob, proj, w_branch_a[0], w_branch_b[0])
    pad = LANES - N_GROUPS - N_EXPERTS
    rw = jnp.concatenate([router_w_group[0], router_w_expert[0], jnp.zeros((d, pad), F32)], axis=1).astype(F32)
    rb = jnp.concatenate([router_b_group[0], router_b_expert[0], jnp.zeros((pad,), F32)]).reshape(1, LANES).astype(F32)
    h1, hn, eid, wk = _out_proj_router(merged, w_out[0].astype(BF16), h0, ffn_norm_g[0], rw, rb)

    tile_e, tile_blk, tile_rows, n_used, src_tok, dst_row = _routing_tables(eid[:, :2], seq)
    w1 = expert_w1[0].reshape(N_EXPERTS, D_MODEL, EXPERT_FF)
    w3 = expert_w3[0].reshape(N_EXPERTS, D_MODEL, EXPERT_FF)
    w2 = expert_w2[0].reshape(N_EXPERTS, EXPERT_FF, D_MODEL)
    y2 = _moe_experts(hn, w1, w3, w2, tile_e, tile_blk, tile_rows, n_used, src_tok, dst_row, seq)

    out = _combine(h1, y2, wk, final_norm_g, x.dtype)
    return out.reshape(b, seq, d)
```

```python
import functools

import jax
import jax.numpy as jnp
from jax import lax
from jax.experimental import pallas as pl
from jax.experimental.pallas import tpu as pltpu

F32 = jnp.float32
BF16 = jnp.bfloat16

D_MODEL = 2048
A_HEADS = 12
A_HEAD_DIM = 128
A_WIDTH = A_HEADS * A_HEAD_DIM
A_SCALE = A_HEAD_DIM ** -0.5
DILATED_CONFIGS = ((128, 1), (512, 4), (2048, 16))
B_HEADS = 8
B_KEY_DIM = 128
B_WIDTH = B_HEADS * B_KEY_DIM
N_GROUPS = 4
EXPERTS_PER_GROUP = 8
N_EXPERTS = N_GROUPS * EXPERTS_PER_GROUP
EXPERT_FF = 512
NORM_EPS = 1e-6

LANES = 128
SUBLANES = 8
VMEM_LIMIT = 56 * 1024 * 1024

_QA_BLK = 0
_KA_BLK = A_HEADS
_VA_BLK = 2 * A_HEADS
_QB_BLK = 3 * A_HEADS
_FB_BLK = _QB_BLK + B_HEADS
_IB_BLK = _FB_BLK + B_HEADS
_GB_BLK = _IB_BLK + B_HEADS
_GATE_A_COL = 3 * A_WIDTH + 4 * B_WIDTH
_GATE_B_COL = _GATE_A_COL + D_MODEL
IN_WIDTH = _GATE_B_COL + D_MODEL
_MIX_COLS = _GB_BLK * 128
_GATE_COLS = IN_WIDTH - _MIX_COLS
_GATES_A_OFF = _GATE_A_COL - _MIX_COLS


def _params(*sem):
    return pltpu.CompilerParams(dimension_semantics=sem, vmem_limit_bytes=VMEM_LIMIT)


def _rmsnorm_rows(x, g):
    ms = jnp.mean(x * x, axis=-1, keepdims=True)
    return x * lax.rsqrt(ms + NORM_EPS) * g


def _rmsnorm_body(x_ref, g_ref, o_ref):
    o_ref[...] = _rmsnorm_rows(x_ref[...].astype(F32), g_ref[...]).astype(o_ref.dtype)


def _rmsnorm(x, g, out_dtype, tm=512):
    m, d = x.shape
    return pl.pallas_call(
        _rmsnorm_body,
        grid=(m // tm,),
        in_specs=[pl.BlockSpec((tm, d), lambda i: (i, 0)), pl.BlockSpec((1, d), lambda i: (0, 0))],
        out_specs=pl.BlockSpec((tm, d), lambda i: (i, 0)),
        out_shape=jax.ShapeDtypeStruct((m, d), out_dtype),
        compiler_params=_params("arbitrary"),
        name="rmsnorm",
    )(x, g.reshape(1, d).astype(F32))


def _matmul_body(a_ref, w_ref, o_ref, wb_ref, *, gate_cols):
    @pl.when(pl.program_id(1) == 0)
    def _():
        wb_ref[...] = w_ref[...].astype(BF16)

    y = jnp.dot(a_ref[...], wb_ref[...], preferred_element_type=F32)
    if gate_cols is not None:
        silu_end = gate_cols
        col = lax.broadcasted_iota(jnp.int32, y.shape, 1) + pl.program_id(0) * y.shape[1]
        sg = jax.nn.sigmoid(y)
        y = jnp.where(col < silu_end, y * sg, sg)
    o_ref[...] = y.astype(o_ref.dtype)


def _matmul(a, w, col0, n, out_dtype, tm, tn, name, gate_cols=None):
    m, k = a.shape
    j0 = col0 // tn
    return pl.pallas_call(
        functools.partial(_matmul_body, gate_cols=gate_cols),
        grid=(n // tn, m // tm),
        in_specs=[pl.BlockSpec((tm, k), lambda j, i: (i, 0)), pl.BlockSpec((k, tn), lambda j, i: (0, j0 + j))],
        out_specs=pl.BlockSpec((tm, tn), lambda j, i: (i, j)),
        out_shape=jax.ShapeDtypeStruct((m, n), out_dtype),
        scratch_shapes=[pltpu.VMEM((k, tn), BF16)],
        compiler_params=_params("arbitrary", "arbitrary"),
        name=name,
    )(a, w)


LOG2_E = 1.4426950408889634
LN_2 = 0.6931471805599453
ATT_BLK = 128
ATT_SUPER = 2048
ATT_DEINT = 4


def _attn_body(q_ref, k_ref, v_ref, o_ref, o_scr, lse_scr, bias_scr, k4, v4, q4, stage):
    sb = pl.program_id(1)
    seq = k_ref.shape[0]
    nd = ATT_DEINT
    sub = ATT_SUPER // nd
    diff = (lax.broadcasted_iota(jnp.int32, (ATT_BLK, 2 * ATT_BLK), 1)
            - lax.broadcasted_iota(jnp.int32, (ATT_BLK, 2 * ATT_BLK), 0))
    neg = jnp.float32(-jnp.inf)
    bias_scr[0] = jnp.where(jnp.logical_and(diff >= 0, diff <= ATT_BLK), 0.0, neg)
    bias_scr[1] = jnp.where(diff <= 0, 0.0, neg)

    @pl.when(sb == 0)
    def _():
        rows = 512

        def split(i, carry):
            for a in range(nd):
                src = pl.ds(i * (rows * nd) + a, rows, stride=nd)
                dst = pl.ds(pl.multiple_of(i * rows, rows), rows)
                k4[a, dst, :] = k_ref[src, :]
                v4[a, dst, :] = v_ref[src, :]
            return carry

        lax.fori_loop(0, seq // nd // rows, split, 0)

    for a in range(nd):
        q4[a] = q_ref[pl.ds(a, sub, stride=nd), :]

    def attend(q, k, v, bias):
        qb = (q * (A_SCALE * LOG2_E)).astype(BF16)
        s = lax.dot_general(qb, k.astype(BF16), (((1,), (1,)), ((), ())), preferred_element_type=F32)
        s = s + bias
        mx = jnp.max(s, axis=-1, keepdims=True)
        p = jnp.exp2(s - mx)
        den = jnp.sum(p, axis=-1, keepdims=True)
        acc = jnp.dot(p.astype(BF16), v.astype(BF16), preferred_element_type=F32)
        return acc / den, jnp.broadcast_to(mx * LN_2 + jnp.log(den), (ATT_BLK, A_HEAD_DIM))

    n_tiles = ATT_SUPER // ATT_BLK
    for c, (window, dil) in enumerate(DILATED_CONFIGS):
        assert window // dil == ATT_BLK and (dil == 1 or dil % nd == 0)
        tiles_per_res = n_tiles // dil
        for t in range(n_tiles):
            n, r = t % tiles_per_res, t // tiles_per_res
            if n == 0:
                first = (sb == 0).astype(jnp.int32)
                back = ATT_BLK * (1 - first)
                bias = bias_scr[first]
            else:
                back, bias = ATT_BLK, bias_scr[0]
            if dil == 1:
                q_rows = pl.ds(n * ATT_BLK, ATT_BLK)
                kv_rows = pl.ds(pl.multiple_of(sb * ATT_SUPER + n * ATT_BLK - back, ATT_BLK), 2 * ATT_BLK)
                o, lse = attend(q_ref[q_rows, :], k_ref[kv_rows, :], v_ref[kv_rows, :], bias)
                out_rows = q_rows
            else:
                st, a, r2 = dil // nd, r % nd, r // nd
                i0 = n * ATT_BLK
                q_rows = pl.ds(st * i0 + r2, ATT_BLK, stride=st) if st > 1 else pl.ds(i0, ATT_BLK)
                kv0 = st * (sb * (ATT_SUPER // dil) + i0 - back) + r2
                kv_rows = pl.ds(kv0, 2 * ATT_BLK, stride=st) if st > 1 else pl.ds(kv0, 2 * ATT_BLK)
                o, lse = attend(q4[a, q_rows, :], k4[a, kv_rows, :], v4[a, kv_rows, :], bias)
                out_rows = (pl.ds(a * sub + st * i0 + r2, ATT_BLK, stride=st) if st > 1
                            else pl.ds(a * sub + i0, ATT_BLK))
            o_scr[c, out_rows, :] = o
            lse_scr[c, out_rows, :] = lse

    rows = 128

    def merge(i, carry):
        for a in range(nd):
            tok = pl.ds(i * (rows * nd) + a, rows, stride=nd)
            cls = pl.ds(a * sub + pl.multiple_of(i * rows, rows), rows)
            l0, l1, l2 = lse_scr[0, tok, :], lse_scr[1, cls, :], lse_scr[2, cls, :]
            m = jnp.maximum(jnp.maximum(l0, l1), l2)
            w0, w1, w2 = jnp.exp(l0 - m), jnp.exp(l1 - m), jnp.exp(l2 - m)
            num = w0 * o_scr[0, tok, :] + w1 * o_scr[1, cls, :] + w2 * o_scr[2, cls, :]
            stage[tok, :] = num / (w0 + w1 + w2)
        return carry

    lax.fori_loop(0, sub // rows, merge, 0)
    o_ref[...] = stage[...].astype(o_ref.dtype)


def _dilated_attention(proj, seq):
    n_super = seq // ATT_SUPER
    blk = lambda off: pl.BlockSpec((seq, A_HEAD_DIM), lambda h, s: (0, off + h))
    return pl.pallas_call(
        _attn_body,
        grid=(A_HEADS, n_super),
        in_specs=[pl.BlockSpec((ATT_SUPER, A_HEAD_DIM), lambda h, s: (s, _QA_BLK + h)), blk(_KA_BLK), blk(_VA_BLK)],
        out_specs=pl.BlockSpec((ATT_SUPER, A_HEAD_DIM), lambda h, s: (s, h)),
        out_shape=jax.ShapeDtypeStruct((seq, A_WIDTH), BF16),
        scratch_shapes=[pltpu.VMEM((len(DILATED_CONFIGS), ATT_SUPER, A_HEAD_DIM), F32),
                        pltpu.VMEM((len(DILATED_CONFIGS), ATT_SUPER, A_HEAD_DIM), F32),
                        pltpu.VMEM((2, ATT_BLK, 2 * ATT_BLK), F32),
                        pltpu.VMEM((ATT_DEINT, seq // ATT_DEINT, A_HEAD_DIM), F32),
                        pltpu.VMEM((ATT_DEINT, seq // ATT_DEINT, A_HEAD_DIM), F32),
                        pltpu.VMEM((ATT_DEINT, ATT_SUPER // ATT_DEINT, A_HEAD_DIM), F32),
                        pltpu.VMEM((ATT_SUPER, A_HEAD_DIM), F32)],
        compiler_params=_params("arbitrary", "arbitrary"),
        name="dilated_attention",
    )(proj, proj, proj)


HG_CHUNK = 64
HG_TB = 512
HG_HEADS_PER_STEP = 2
HG_CHUNKS_PER_STACK = 2
HG_STACKS_PER_ITER = 2


def _hgrn_ref_rows(bc, half):
    n, width = bc.shape
    blk = 2 * half
    if blk >= SUBLANES:
        rows = [jnp.broadcast_to(bc[b0 + half - 1:b0 + half, :], (blk, width)) for b0 in range(0, n, blk)]
        return jnp.concatenate(rows, axis=0) if len(rows) > 1 else rows[0]
    sub = lax.broadcasted_iota(jnp.int32, (SUBLANES, width), 0)
    groups = []
    for g0 in range(0, n, SUBLANES):
        grp = bc[g0:g0 + SUBLANES, :]
        if half == 1:
            groups.append(jnp.where(jnp.bitwise_and(sub, 1) == 1, pltpu.roll(grp, 1, 0), grp))
        else:
            assert half == 2 and SUBLANES == 8
            groups.append(jnp.where(sub < 4, jnp.broadcast_to(grp[1:2, :], grp.shape),
                                    jnp.broadcast_to(grp[5:6, :], grp.shape)))
    return jnp.concatenate(groups, axis=0)


def _hgrn_body(qb_ref, fb_ref, ib_ref, gb_ref, lbl_ref, g_ref, o_ref, st_ref):
    @pl.when(pl.program_id(1) == 0)
    def _():
        st_ref[...] = jnp.zeros_like(st_ref)

    c_ = HG_CHUNK
    lbl = lbl_ref[...]
    e = jnp.exp(lbl - jnp.max(lbl, axis=0, keepdims=True))
    lb_all = e[0:1, :] / jnp.sum(e, axis=0, keepdims=True)
    g_all = g_ref[...]

    nh, nc = HG_HEADS_PER_STEP, HG_CHUNKS_PER_STACK
    rows = nc * c_
    n = nh * rows
    ti = lax.broadcasted_iota(jnp.int32, (n, n), 0)
    si = lax.broadcasted_iota(jnp.int32, (n, n), 1)
    xor = jnp.bitwise_xor(ti, si)
    causal = jnp.logical_and(xor < c_, si <= ti)
    tri = jnp.where(causal, 1.0, 0.0).astype(BF16)

    def stack_of(ref, r0):
        return jnp.concatenate([ref[pl.ds(r0, rows), h * B_KEY_DIM:(h + 1) * B_KEY_DIM] for h in range(nh)], axis=0)

    def per_head_rows(x):
        return jnp.concatenate([jnp.broadcast_to(x[:, h * B_KEY_DIM:(h + 1) * B_KEY_DIM], (rows, B_KEY_DIM))
                                for h in range(nh)], axis=0)

    lb = per_head_rows(lb_all)
    gnorm = per_head_rows(g_all)

    def stack(i):
        r0 = pl.multiple_of(i * rows, rows)
        f = lb + (1.0 - lb) * jax.nn.sigmoid(stack_of(fb_ref, r0))
        logf = jnp.log(f)
        kk = 1.0 - f
        q = jax.nn.silu(stack_of(qb_ref, r0))
        v = stack_of(ib_ref, r0)
        vb = v.astype(BF16)
        hi = logf.astype(BF16)
        rem = logf - hi.astype(F32)
        mid = rem.astype(BF16)
        low = (rem - mid.astype(F32)).astype(BF16)
        parts = jnp.dot(tri, jnp.concatenate([hi, mid, low], axis=1), preferred_element_type=F32)
        bc = (parts[:, :B_KEY_DIM] + (parts[:, B_KEY_DIM:2 * B_KEY_DIM] + parts[:, 2 * B_KEY_DIM:])) * LOG2_E

        attn = lax.dot_general(q.astype(BF16), kk.astype(BF16), (((1,), (1,)), ((), ())),
                               preferred_element_type=F32)
        half = 1
        while half < c_:
            dec = jnp.exp2(-jnp.abs(bc - _hgrn_ref_rows(bc, half)))
            s = lax.dot_general((q * dec).astype(BF16), (kk * dec).astype(BF16), (((1,), (1,)), ((), ())),
                                preferred_element_type=F32)
            attn = jnp.where(xor >= half, s, attn)
            half *= 2
        attn = jnp.where(causal, attn, 0.0)
        o = jnp.dot(attn.astype(BF16), vb, preferred_element_type=F32)

        last = jnp.concatenate([jnp.broadcast_to(bc[a + c_ - 1:a + c_, :], (c_, B_KEY_DIM))
                                for a in range(0, n, c_)], axis=0)
        q_dec = (q * jnp.exp2(bc)).astype(BF16)
        k_dec = (kk * jnp.exp2(last - bc)).astype(BF16)
        st_dec = jnp.exp2(last)
        inter = []
        for h in range(nh):
            st = st_ref[h]
            for c in range(nc):
                a = h * rows + c * c_
                inter.append(lax.dot_general(q_dec[a:a + c_], st.astype(BF16), (((1,), (1,)), ((), ())),
                                             preferred_element_type=F32))
                upd = lax.dot_general(vb[a:a + c_], k_dec[a:a + c_], (((0,), (0,)), ((), ())),
                                      preferred_element_type=F32)
                st = st * st_dec[a:a + 1] + upd
            st_ref[h] = st
        o = o + jnp.concatenate(inter, axis=0)

        o = o * lax.rsqrt(jnp.mean(o * o, axis=-1, keepdims=True) + NORM_EPS)
        res = (o * gnorm * stack_of(gb_ref, r0).astype(F32)).astype(o_ref.dtype)
        for h in range(nh):
            o_ref[pl.ds(r0, rows), h * B_KEY_DIM:(h + 1) * B_KEY_DIM] = res[h * rows:(h + 1) * rows]

    def stacks(i, carry):
        for u in range(HG_STACKS_PER_ITER):
            stack(i * HG_STACKS_PER_ITER + u)
        return carry

    lax.fori_loop(0, HG_TB // rows // HG_STACKS_PER_ITER, stacks, 0)


def _hgrn2(proj, gates, lb_logits, norm_g, seq):
    hp = HG_HEADS_PER_STEP
    width = hp * B_KEY_DIM
    col = lambda off: pl.BlockSpec((HG_TB, width), lambda h, t: (t, off // hp + h))
    n_lb = lb_logits.shape[0]
    return pl.pallas_call(
        _hgrn_body,
        grid=(B_HEADS // hp, seq // HG_TB),
        in_specs=[col(_QB_BLK), col(_FB_BLK), col(_IB_BLK), col(0),
                  pl.BlockSpec((n_lb, width), lambda h, t: (0, h)),
                  pl.BlockSpec((1, width), lambda h, t: (0, h))],
        out_specs=pl.BlockSpec((HG_TB, width), lambda h, t: (t, h)),
        out_shape=jax.ShapeDtypeStruct((seq, B_WIDTH), BF16),
        scratch_shapes=[pltpu.VMEM((hp, B_KEY_DIM, B_KEY_DIM), F32)],
        compiler_params=_params("arbitrary", "arbitrary"),
        name="hgrn2",
    )(proj, proj, proj, gates, lb_logits.astype(F32), norm_g.reshape(1, B_WIDTH).astype(F32))


def _merge_body(oa_ref, ob_ref, ga_ref, gb_ref, wa_ref, wb_ref, o_ref, wa_s, wb_s):
    @pl.when(pl.program_id(1) == 0)
    def _():
        wa_s[...] = wa_ref[...].astype(BF16)
        wb_s[...] = wb_ref[...].astype(BF16)

    ya = jnp.dot(oa_ref[...], wa_s[...], preferred_element_type=F32)
    yb = jnp.dot(ob_ref[...], wb_s[...], preferred_element_type=F32)
    o_ref[...] = (ga_ref[...].astype(F32) * ya + gb_ref[...].astype(F32) * yb).astype(o_ref.dtype)


def _branch_merge(oa, ob, gates, wa, wb, tm=512, tn=1024):
    m = oa.shape[0]
    ga0, gb0 = _GATES_A_OFF // tn, (_GATES_A_OFF + D_MODEL) // tn
    return pl.pallas_call(
        _merge_body,
        grid=(D_MODEL // tn, m // tm),
        in_specs=[pl.BlockSpec((tm, A_WIDTH), lambda j, i: (i, 0)), pl.BlockSpec((tm, B_WIDTH), lambda j, i: (i, 0)),
                  pl.BlockSpec((tm, tn), lambda j, i: (i, ga0 + j)), pl.BlockSpec((tm, tn), lambda j, i: (i, gb0 + j)),
                  pl.BlockSpec((A_WIDTH, tn), lambda j, i: (0, j)), pl.BlockSpec((B_WIDTH, tn), lambda j, i: (0, j))],
        out_specs=pl.BlockSpec((tm, tn), lambda j, i: (i, j)),
        out_shape=jax.ShapeDtypeStruct((m, D_MODEL), BF16),
        scratch_shapes=[pltpu.VMEM((A_WIDTH, tn), BF16), pltpu.VMEM((B_WIDTH, tn), BF16)],
        compiler_params=_params("arbitrary", "arbitrary"),
        name="branch_merge",
    )(oa, ob, gates, gates, wa, wb)


def _route_rows(h, g, rw, rb):
    hn = _rmsnorm_rows(h, g)
    hn_hi = hn.astype(BF16)
    hn_lo = (hn - hn_hi.astype(F32)).astype(BF16)
    rw_hi = rw.astype(BF16)
    rw_lo = (rw - rw_hi.astype(F32)).astype(BF16)
    logits = (jnp.dot(hn_hi, rw_hi, preferred_element_type=F32)
              + (jnp.dot(hn_lo, rw_hi, preferred_element_type=F32)
                 + jnp.dot(hn_hi, rw_lo, preferred_element_type=F32))) + rb
    lane = lax.broadcasted_iota(jnp.int32, logits.shape, 1)
    neg = jnp.float32(-jnp.inf)
    big = jnp.int32(LANES)

    is_g = lane < N_GROUPS
    lg = jnp.where(is_g, logits, neg)
    mg = jnp.max(lg, axis=-1, keepdims=True)
    g_idx = jnp.min(jnp.where(lg == mg, lane, big), axis=-1, keepdims=True)
    pg_top = 1.0 / jnp.sum(jnp.where(is_g, jnp.exp(lg - mg), 0.0), axis=-1, keepdims=True)

    lo = N_GROUPS + g_idx * EXPERTS_PER_GROUP
    in_grp = jnp.logical_and(lane >= lo, lane < lo + EXPERTS_PER_GROUP)
    le = jnp.where(in_grp, logits, neg)
    v1 = jnp.max(le, axis=-1, keepdims=True)
    i1 = jnp.min(jnp.where(le == v1, lane, big), axis=-1, keepdims=True)
    le2 = jnp.where(lane == i1, neg, le)
    v2 = jnp.max(le2, axis=-1, keepdims=True)
    i2 = jnp.min(jnp.where(le2 == v2, lane, big), axis=-1, keepdims=True)
    e2 = jnp.exp(v2 - v1)
    w1 = pg_top / (1.0 + e2)
    w2 = pg_top * e2 / (1.0 + e2)
    eid = jnp.where(lane == 0, i1 - N_GROUPS, jnp.where(lane == 1, i2 - N_GROUPS, 0))
    wk = jnp.where(lane == 0, w1, jnp.where(lane == 1, w2, 0.0))
    return hn, eid, wk


def _out_router_body(a_ref, w_ref, x_ref, g_ref, rw_ref, rb_ref, h1_ref, hn_ref, eid_ref, wk_ref, h_prev):
    @pl.when(pl.program_id(0) == 0)
    def _():
        h_prev[...] = jnp.zeros_like(h_prev)

    hn_ref[...], eid_ref[...], wk_ref[...] = _route_rows(h_prev[...], g_ref[...], rw_ref[...], rb_ref[...])
    h1 = x_ref[...] + jnp.dot(a_ref[...], w_ref[...], preferred_element_type=F32)
    h1_ref[...] = h1
    h_prev[...] = h1


def _out_proj_router(a, w_bf16, x, g, rw, rb, tm=512):
    m, d = x.shape
    n_blk = m // tm
    cur = lambda width: pl.BlockSpec((tm, width), lambda i: (jnp.minimum(i, n_blk - 1), 0))
    prev = lambda width: pl.BlockSpec((tm, width), lambda i: (jnp.maximum(i - 1, 0), 0))
    const = lambda shape, **kw: pl.BlockSpec(shape, lambda i: (0, 0), **kw)
    return pl.pallas_call(
        _out_router_body,
        grid=(n_blk + 1,),
        in_specs=[cur(a.shape[1]), const(w_bf16.shape, pipeline_mode=pl.Buffered(1)), cur(d),
                  const((1, d)), const((d, LANES)), const((1, LANES))],
        out_specs=[cur(d), prev(d), prev(LANES), prev(LANES)],
        out_shape=[jax.ShapeDtypeStruct((m, d), F32), jax.ShapeDtypeStruct((m, d), F32),
                   jax.ShapeDtypeStruct((m, LANES), jnp.int32), jax.ShapeDtypeStruct((m, LANES), F32)],
        scratch_shapes=[pltpu.VMEM((tm, d), F32)],
        compiler_params=_params("arbitrary"),
        name="out_proj_router",
    )(a, w_bf16, x, g.reshape(1, d).astype(F32), rw, rb)


MOE_TM = 256
ROW_DMA_PRIORITIES = (0, 1)


def _moe_body(tile_e_ref, tile_blk_ref, tile_rows_ref, n_used_ref,
              src0_ref, src_next_ref, dst_prev_ref, hn_ref, w1_ref, w3_ref, w2_ref, y_ref,
              w1_s, w3_s, w2_s, x0, x1, o0, o1, gsem, ssem):
    j = pl.program_id(0)
    n_tiles = pl.num_programs(0)
    n_used = n_used_ref[0]
    xs = (x0, x1)
    os_ = (o0, o1)

    def rows_of(t):
        return jnp.where(t >= 0, tile_rows_ref[jnp.clip(t, 0, n_tiles - 1)], 0)

    def gather_issue(idx_ref, b, n):
        for r in range(MOE_TM):
            @pl.when(r < n)
            def _(r=r):
                pltpu.make_async_copy(hn_ref.at[idx_ref[0, 0, r]], xs[b].at[r],
                                      gsem.at[b]).start(priority=ROW_DMA_PRIORITIES[r % 2])

    def scatter_issue(idx_ref, b, n):
        for r in range(MOE_TM):
            @pl.when(r < n)
            def _(r=r):
                pltpu.make_async_copy(os_[b].at[r], y_ref.at[idx_ref[0, 0, r]],
                                      ssem.at[b]).start(priority=ROW_DMA_PRIORITIES[r % 2])

    def wait_rows(make_copy, n):
        p = MOE_TM
        while p >= 1:
            @pl.when(jnp.bitwise_and(n, p) != 0)
            def _(p=p):
                make_copy(p).wait()
            p //= 2

    def gather_wait(b, n):
        wait_rows(lambda p: pltpu.make_async_copy(hn_ref.at[pl.ds(0, p)], xs[b].at[pl.ds(0, p)], gsem.at[b]), n)

    def scatter_wait(b, n):
        wait_rows(lambda p: pltpu.make_async_copy(os_[b].at[pl.ds(0, p)], y_ref.at[pl.ds(0, p)], ssem.at[b]), n)

    @pl.when(j == 0)
    def _():
        x0[...] = jnp.zeros_like(x0)
        x1[...] = jnp.zeros_like(x1)
        gather_issue(src0_ref, 0, rows_of(0))

    for b in (0, 1):
        @pl.when(jnp.logical_and(j < n_used, j % 2 == b))
        def _(b=b):
            gather_wait(b, rows_of(j))
            scatter_wait(b, rows_of(j - 2))

            prev_e = tile_e_ref[jnp.maximum(j - 1, 0)]

            @pl.when(jnp.logical_or(j == 0, tile_e_ref[j] != prev_e))
            def _():
                w1_s[...] = w1_ref[0].astype(BF16)
                w3_s[...] = w3_ref[0].astype(BF16)
                w2_s[...] = w2_ref[0].astype(BF16)

            gather_issue(src_next_ref, 1 - b, rows_of(j + 1))
            scatter_issue(dst_prev_ref, 1 - b, rows_of(j - 1))
            x = xs[b][...].astype(BF16)
            hid = (jax.nn.silu(jnp.dot(x, w1_s[...], preferred_element_type=F32))
                   * jnp.dot(x, w3_s[...], preferred_element_type=F32))
            os_[b][...] = jnp.dot(hid.astype(BF16), w2_s[...], preferred_element_type=F32)

    for b in (0, 1):
        @pl.when(jnp.logical_and(j == n_used, j % 2 == b))
        def _(b=b):
            scatter_wait(b, rows_of(j - 2))
            scatter_issue(dst_prev_ref, 1 - b, rows_of(j - 1))
            scatter_wait(1 - b, rows_of(j - 1))


def _moe_experts(hn, w1, w3, w2, tile_e, tile_blk, tile_rows, n_used, src, dst, n_tokens):
    d = hn.shape[1]
    n_tiles = tile_e.shape[0]
    ff = w1.shape[-1]
    idx_block = lambda fn: pl.BlockSpec((1, 1, MOE_TM), fn, memory_space=pltpu.SMEM)
    grid_spec = pltpu.PrefetchScalarGridSpec(
        num_scalar_prefetch=4,
        grid=(n_tiles,),
        in_specs=[
            idx_block(lambda j, te, tb, tr, nu: (0, 0, 0)),
            idx_block(lambda j, te, tb, tr, nu: (tb[jnp.minimum(j + 1, n_tiles - 1)], 0, 0)),
            idx_block(lambda j, te, tb, tr, nu: (tb[jnp.maximum(j - 1, 0)], 0, 0)),
            pl.BlockSpec(memory_space=pl.ANY),
            pl.BlockSpec((1, d, ff), lambda j, te, tb, tr, nu: (te[j], 0, 0)),
            pl.BlockSpec((1, d, ff), lambda j, te, tb, tr, nu: (te[j], 0, 0)),
            pl.BlockSpec((1, ff, d), lambda j, te, tb, tr, nu: (te[j], 0, 0)),
        ],
        out_specs=pl.BlockSpec(memory_space=pl.ANY),
        scratch_shapes=[pltpu.VMEM((d, ff), BF16), pltpu.VMEM((d, ff), BF16), pltpu.VMEM((ff, d), BF16),
                        pltpu.VMEM((MOE_TM, d), F32), pltpu.VMEM((MOE_TM, d), F32),
                        pltpu.VMEM((MOE_TM, d), F32), pltpu.VMEM((MOE_TM, d), F32),
                        pltpu.SemaphoreType.DMA((2,)), pltpu.SemaphoreType.DMA((2,))],
    )
    src3 = src.reshape(n_tiles, 1, MOE_TM)
    return pl.pallas_call(
        _moe_body,
        grid_spec=grid_spec,
        out_shape=jax.ShapeDtypeStruct((2 * n_tokens, d), F32),
        compiler_params=_params("arbitrary"),
        name="moe_experts",
    )(tile_e, tile_blk, tile_rows, n_used, src3, src3, dst.reshape(n_tiles, 1, MOE_TM), hn, w1, w3, w2)


def _combine_body(h_ref, y0_ref, y1_ref, wk_ref, g_ref, o_ref):
    wk = wk_ref[...]
    y = h_ref[...] + wk[:, 0:1] * y0_ref[...] + wk[:, 1:2] * y1_ref[...]
    o_ref[...] = _rmsnorm_rows(y, g_ref[...]).astype(o_ref.dtype)


def _combine(h, y2, wk, g, out_dtype, tm=256):
    m, d = h.shape
    nblk = m // tm
    return pl.pallas_call(
        _combine_body,
        grid=(nblk,),
        in_specs=[pl.BlockSpec((tm, d), lambda i: (i, 0)), pl.BlockSpec((tm, d), lambda i: (i, 0)),
                  pl.BlockSpec((tm, d), lambda i: (nblk + i, 0)), pl.BlockSpec((tm, LANES), lambda i: (i, 0)),
                  pl.BlockSpec((1, d), lambda i: (0, 0))],
        out_specs=pl.BlockSpec((tm, d), lambda i: (i, 0)),
        out_shape=jax.ShapeDtypeStruct((m, d), out_dtype),
        compiler_params=_params("arbitrary"),
        name="combine_final_norm",
    )(h, y2, y2, wk, g.reshape(1, d).astype(F32))


def _routing_tables(eid, n_tokens):
    n_pairs = 2 * n_tokens
    n_tiles = n_pairs // MOE_TM + N_EXPERTS + 1
    n_rows = n_tiles * MOE_TM
    experts = jnp.arange(N_EXPERTS, dtype=jnp.int32)[None, :]
    e_flat = eid.reshape(n_pairs).astype(jnp.int32)
    counts = jnp.sum(e_flat[:, None] == experts, axis=0, dtype=jnp.int32)
    tiles_per_e = (counts + MOE_TM - 1) // MOE_TM
    tile_end = jnp.cumsum(tiles_per_e)
    tile_start = tile_end - tiles_per_e
    n_used = tile_end[-1]
    filler_end = jnp.cumsum(tiles_per_e * MOE_TM - counts)
    filler = jnp.arange(n_rows - n_pairs, dtype=jnp.int32)[:, None]
    filler_e = jnp.sum(filler_end[None, :] <= filler, axis=1, dtype=jnp.int32)
    keys = jnp.concatenate([2 * e_flat, 2 * filler_e + 1])
    order = jnp.argsort(keys, stable=True).astype(jnp.int32)
    valid = order < n_pairs
    pair = jnp.where(valid, order, 0)
    src_tok = pair // 2
    dst_row = (pair % 2) * n_tokens + pair // 2
    j = jnp.arange(n_tiles, dtype=jnp.int32)
    tile_blk = jnp.minimum(j, n_used - 1)
    tile_e = jnp.sum(tile_end[None, :] <= tile_blk[:, None], axis=1, dtype=jnp.int32)
    own = tile_e[:, None] == experts
    tile_row0 = (tile_blk - jnp.sum(jnp.where(own, tile_start[None, :], 0), axis=1)) * MOE_TM
    own_count = jnp.sum(jnp.where(own, counts[None, :], 0), axis=1)
    tile_rows = jnp.where(j == tile_blk, jnp.clip(own_count - tile_row0, 0, MOE_TM), 0).astype(jnp.int32)
    return (tile_e, tile_blk.astype(jnp.int32), tile_rows, n_used.reshape(1).astype(jnp.int32),
            src_tok.astype(jnp.int32), dst_row.astype(jnp.int32))


def kernel(x, mix_norm_g, w_in, hgrn_lb_logits, hgrn_norm_g, w_branch_a, w_branch_b, w_out, ffn_norm_g,
           router_w_group, router_b_group, router_w_expert, router_b_expert, expert_w1, expert_w3, expert_w2,
           final_norm_g):
    b, seq, d = x.shape
    assert b == 1 and d == D_MODEL and seq % ATT_SUPER == 0 and w_in.shape == (1, D_MODEL, IN_WIDTH)
    h0 = x.reshape(seq, d).astype(F32)

    xn = _rmsnorm(h0, mix_norm_g[0], BF16)
    proj = _matmul(xn, w_in[0], 0, _MIX_COLS, F32, tm=1024, tn=1280, name="in_proj_mix")
    gates = _matmul(xn, w_in[0], _MIX_COLS, _GATE_COLS, BF16, tm=1024, tn=1280, name="in_proj_gates",
                    gate_cols=_GATES_A_OFF)
    oa = _dilated_attention(proj, seq)
    ob = _hgrn2(proj, gates, hgrn_lb_logits, hgrn_norm_g[0], seq)
    merged = _branch_merge(oa, ob, gates, w_branch_a[0], w_branch_b[0])
    pad = LANES - N_GROUPS - N_EXPERTS
    rw = jnp.concatenate([router_w_group[0], router_w_expert[0], jnp.zeros((d, pad), F32)], axis=1).astype(F32)
    rb = jnp.concatenate([router_b_group[0], router_b_expert[0], jnp.zeros((pad,), F32)]).reshape(1, LANES).astype(F32)
    h1, hn, eid, wk = _out_proj_router(merged, w_out[0].astype(BF16), h0, ffn_norm_g[0], rw, rb)

    tile_e, tile_blk, tile_rows, n_used, src_tok, dst_row = _routing_tables(eid[:, :2], seq)
    w1 = expert_w1[0].reshape(N_EXPERTS, D_MODEL, EXPERT_FF)
    w3 = expert_w3[0].reshape(N_EXPERTS, D_MODEL, EXPERT_FF)
    w2 = expert_w2[0].reshape(N_EXPERTS, EXPERT_FF, D_MODEL)
    y2 = _moe_experts(hn, w1, w3, w2, tile_e, tile_blk, tile_rows, n_used, src_tok, dst_row, seq)

    out = _combine(h1, y2, wk, final_norm_g, x.dtype)
    return out.reshape(b, seq, d)
```

```python
import functools

import jax
import jax.numpy as jnp
from jax import lax
from jax.experimental import pallas as pl
from jax.experimental.pallas import tpu as pltpu

F32 = jnp.float32
BF16 = jnp.bfloat16

D_MODEL = 2048
A_HEADS = 12
A_HEAD_DIM = 128
A_WIDTH = A_HEADS * A_HEAD_DIM
A_SCALE = A_HEAD_DIM ** -0.5
DILATED_CONFIGS = ((128, 1), (512, 4), (2048, 16))
B_HEADS = 8
B_KEY_DIM = 128
B_WIDTH = B_HEADS * B_KEY_DIM
N_GROUPS = 4
EXPERTS_PER_GROUP = 8
N_EXPERTS = N_GROUPS * EXPERTS_PER_GROUP
EXPERT_FF = 512
NORM_EPS = 1e-6

LANES = 128
SUBLANES = 8
VMEM_LIMIT = 56 * 1024 * 1024

_QA_BLK = 0
_KA_BLK = A_HEADS
_VA_BLK = 2 * A_HEADS
_QB_BLK = 3 * A_HEADS
_FB_BLK = _QB_BLK + B_HEADS
_IB_BLK = _FB_BLK + B_HEADS
_GB_BLK = _IB_BLK + B_HEADS
_GATE_A_COL = 3 * A_WIDTH + 4 * B_WIDTH
_GATE_B_COL = _GATE_A_COL + D_MODEL
IN_WIDTH = _GATE_B_COL + D_MODEL


def _params(*sem):
    return pltpu.CompilerParams(dimension_semantics=sem, vmem_limit_bytes=VMEM_LIMIT)


def _rmsnorm_rows(x, g):
    ms = jnp.mean(x * x, axis=-1, keepdims=True)
    return x * lax.rsqrt(ms + NORM_EPS) * g


def _rmsnorm_body(x_ref, g_ref, o_ref):
    o_ref[...] = _rmsnorm_rows(x_ref[...].astype(F32), g_ref[...]).astype(o_ref.dtype)


def _rmsnorm(x, g, out_dtype, tm=512):
    m, d = x.shape
    return pl.pallas_call(
        _rmsnorm_body,
        grid=(m // tm,),
        in_specs=[pl.BlockSpec((tm, d), lambda i: (i, 0)), pl.BlockSpec((1, d), lambda i: (0, 0))],
        out_specs=pl.BlockSpec((tm, d), lambda i: (i, 0)),
        out_shape=jax.ShapeDtypeStruct((m, d), out_dtype),
        compiler_params=_params("arbitrary"),
        name="rmsnorm",
    )(x, g.reshape(1, d).astype(F32))


def _matmul_body(a_ref, w_ref, o_ref, wb_ref):
    @pl.when(pl.program_id(1) == 0)
    def _():
        wb_ref[...] = w_ref[...].astype(BF16)

    o_ref[...] = jnp.dot(a_ref[...], wb_ref[...], preferred_element_type=F32).astype(o_ref.dtype)


def _matmul(a, w, out_dtype, tm, tn, name):
    m, k = a.shape
    n = w.shape[1]
    return pl.pallas_call(
        _matmul_body,
        grid=(n // tn, m // tm),
        in_specs=[pl.BlockSpec((tm, k), lambda j, i: (i, 0)), pl.BlockSpec((k, tn), lambda j, i: (0, j))],
        out_specs=pl.BlockSpec((tm, tn), lambda j, i: (i, j)),
        out_shape=jax.ShapeDtypeStruct((m, n), out_dtype),
        scratch_shapes=[pltpu.VMEM((k, tn), BF16)],
        compiler_params=_params("arbitrary", "arbitrary"),
        name=name,
    )(a, w)


LOG2_E = 1.4426950408889634
LN_2 = 0.6931471805599453
ATT_BLK = 128
ATT_SUPER = 2048
ATT_DEINT = 4


def _attn_body(q_ref, k_ref, v_ref, o_ref, o_scr, lse_scr, bias_scr, k4, v4, q4, stage):
    sb = pl.program_id(1)
    seq = k_ref.shape[0]
    nd = ATT_DEINT
    sub = ATT_SUPER // nd
    diff = (lax.broadcasted_iota(jnp.int32, (ATT_BLK, 2 * ATT_BLK), 1)
            - lax.broadcasted_iota(jnp.int32, (ATT_BLK, 2 * ATT_BLK), 0))
    neg = jnp.float32(-jnp.inf)
    bias_scr[0] = jnp.where(jnp.logical_and(diff >= 0, diff <= ATT_BLK), 0.0, neg)
    bias_scr[1] = jnp.where(diff <= 0, 0.0, neg)

    @pl.when(sb == 0)
    def _():
        rows = 512

        def split(i, carry):
            for a in range(nd):
                src = pl.ds(i * (rows * nd) + a, rows, stride=nd)
                dst = pl.ds(pl.multiple_of(i * rows, rows), rows)
                k4[a, dst, :] = k_ref[src, :]
                v4[a, dst, :] = v_ref[src, :]
            return carry

        lax.fori_loop(0, seq // nd // rows, split, 0)

    for a in range(nd):
        q4[a] = q_ref[pl.ds(a, sub, stride=nd), :]

    def attend(q, k, v, bias):
        qb = (q * (A_SCALE * LOG2_E)).astype(BF16)
        s = lax.dot_general(qb, k.astype(BF16), (((1,), (1,)), ((), ())), preferred_element_type=F32)
        s = s + bias
        mx = jnp.max(s, axis=-1, keepdims=True)
        p = jnp.exp2(s - mx)
        den = jnp.sum(p, axis=-1, keepdims=True)
        acc = jnp.dot(p.astype(BF16), v.astype(BF16), preferred_element_type=F32)
        return acc / den, jnp.broadcast_to(mx * LN_2 + jnp.log(den), (ATT_BLK, A_HEAD_DIM))

    n_tiles = ATT_SUPER // ATT_BLK
    for c, (window, dil) in enumerate(DILATED_CONFIGS):
        assert window // dil == ATT_BLK and (dil == 1 or dil % nd == 0)
        tiles_per_res = n_tiles // dil
        for t in range(n_tiles):
            n, r = t % tiles_per_res, t // tiles_per_res
            if n == 0:
                first = (sb == 0).astype(jnp.int32)
                back = ATT_BLK * (1 - first)
                bias = bias_scr[first]
            else:
                back, bias = ATT_BLK, bias_scr[0]
            if dil == 1:
                q_rows = pl.ds(n * ATT_BLK, ATT_BLK)
                kv_rows = pl.ds(pl.multiple_of(sb * ATT_SUPER + n * ATT_BLK - back, ATT_BLK), 2 * ATT_BLK)
                o, lse = attend(q_ref[q_rows, :], k_ref[kv_rows, :], v_ref[kv_rows, :], bias)
                out_rows = q_rows
            else:
                st, a, r2 = dil // nd, r % nd, r // nd
                i0 = n * ATT_BLK
                q_rows = pl.ds(st * i0 + r2, ATT_BLK, stride=st) if st > 1 else pl.ds(i0, ATT_BLK)
                kv0 = st * (sb * (ATT_SUPER // dil) + i0 - back) + r2
                kv_rows = pl.ds(kv0, 2 * ATT_BLK, stride=st) if st > 1 else pl.ds(kv0, 2 * ATT_BLK)
                o, lse = attend(q4[a, q_rows, :], k4[a, kv_rows, :], v4[a, kv_rows, :], bias)
                out_rows = (pl.ds(a * sub + st * i0 + r2, ATT_BLK, stride=st) if st > 1
                            else pl.ds(a * sub + i0, ATT_BLK))
            o_scr[c, out_rows, :] = o
            lse_scr[c, out_rows, :] = lse

    rows = 128

    def merge(i, carry):
        for a in range(nd):
            tok = pl.ds(i * (rows * nd) + a, rows, stride=nd)
            cls = pl.ds(a * sub + pl.multiple_of(i * rows, rows), rows)
            l0, l1, l2 = lse_scr[0, tok, :], lse_scr[1, cls, :], lse_scr[2, cls, :]
            m = jnp.maximum(jnp.maximum(l0, l1), l2)
            w0, w1, w2 = jnp.exp(l0 - m), jnp.exp(l1 - m), jnp.exp(l2 - m)
            num = w0 * o_scr[0, tok, :] + w1 * o_scr[1, cls, :] + w2 * o_scr[2, cls, :]
            stage[tok, :] = num / (w0 + w1 + w2)
        return carry

    lax.fori_loop(0, sub // rows, merge, 0)
    o_ref[...] = stage[...].astype(o_ref.dtype)


def _dilated_attention(proj, seq):
    n_super = seq // ATT_SUPER
    blk = lambda off: pl.BlockSpec((seq, A_HEAD_DIM), lambda h, s: (0, off + h))
    return pl.pallas_call(
        _attn_body,
        grid=(A_HEADS, n_super),
        in_specs=[pl.BlockSpec((ATT_SUPER, A_HEAD_DIM), lambda h, s: (s, _QA_BLK + h)), blk(_KA_BLK), blk(_VA_BLK)],
        out_specs=pl.BlockSpec((ATT_SUPER, A_HEAD_DIM), lambda h, s: (s, h)),
        out_shape=jax.ShapeDtypeStruct((seq, A_WIDTH), BF16),
        scratch_shapes=[pltpu.VMEM((len(DILATED_CONFIGS), ATT_SUPER, A_HEAD_DIM), F32),
                        pltpu.VMEM((len(DILATED_CONFIGS), ATT_SUPER, A_HEAD_DIM), F32),
                        pltpu.VMEM((2, ATT_BLK, 2 * ATT_BLK), F32),
                        pltpu.VMEM((ATT_DEINT, seq // ATT_DEINT, A_HEAD_DIM), F32),
                        pltpu.VMEM((ATT_DEINT, seq // ATT_DEINT, A_HEAD_DIM), F32),
                        pltpu.VMEM((ATT_DEINT, ATT_SUPER // ATT_DEINT, A_HEAD_DIM), F32),
                        pltpu.VMEM((ATT_SUPER, A_HEAD_DIM), F32)],
        compiler_params=_params("arbitrary", "arbitrary"),
        name="dilated_attention",
    )(proj, proj, proj)


HG_CHUNK = 64
HG_TB = 512
HG_HEADS_PER_STEP = 2
HG_CHUNKS_PER_STACK = 2
HG_STACKS_PER_ITER = 2


def _hgrn_ref_rows(bc, half):
    n, width = bc.shape
    blk = 2 * half
    if blk >= SUBLANES:
        rows = [jnp.broadcast_to(bc[b0 + half - 1:b0 + half, :], (blk, width)) for b0 in range(0, n, blk)]
        return jnp.concatenate(rows, axis=0) if len(rows) > 1 else rows[0]
    sub = lax.broadcasted_iota(jnp.int32, (SUBLANES, width), 0)
    groups = []
    for g0 in range(0, n, SUBLANES):
        grp = bc[g0:g0 + SUBLANES, :]
        if half == 1:
            groups.append(jnp.where(jnp.bitwise_and(sub, 1) == 1, pltpu.roll(grp, 1, 0), grp))
        else:
            assert half == 2 and SUBLANES == 8
            groups.append(jnp.where(sub < 4, jnp.broadcast_to(grp[1:2, :], grp.shape),
                                    jnp.broadcast_to(grp[5:6, :], grp.shape)))
    return jnp.concatenate(groups, axis=0)


def _hgrn_body(qb_ref, fb_ref, ib_ref, gb_ref, lbl_ref, g_ref, o_ref, st_ref):
    @pl.when(pl.program_id(1) == 0)
    def _():
        st_ref[...] = jnp.zeros_like(st_ref)

    c_ = HG_CHUNK
    lbl = lbl_ref[...]
    e = jnp.exp(lbl - jnp.max(lbl, axis=0, keepdims=True))
    lb_all = e[0:1, :] / jnp.sum(e, axis=0, keepdims=True)
    g_all = g_ref[...]

    nh, nc = HG_HEADS_PER_STEP, HG_CHUNKS_PER_STACK
    rows = nc * c_
    n = nh * rows
    ti = lax.broadcasted_iota(jnp.int32, (n, n), 0)
    si = lax.broadcasted_iota(jnp.int32, (n, n), 1)
    xor = jnp.bitwise_xor(ti, si)
    causal = jnp.logical_and(xor < c_, si <= ti)
    tri = jnp.where(causal, 1.0, 0.0).astype(BF16)

    def stack_of(ref, r0):
        return jnp.concatenate([ref[pl.ds(r0, rows), h * B_KEY_DIM:(h + 1) * B_KEY_DIM] for h in range(nh)], axis=0)

    def per_head_rows(x):
        return jnp.concatenate([jnp.broadcast_to(x[:, h * B_KEY_DIM:(h + 1) * B_KEY_DIM], (rows, B_KEY_DIM))
                                for h in range(nh)], axis=0)

    lb = per_head_rows(lb_all)
    gnorm = per_head_rows(g_all)

    def stack(i):
        r0 = pl.multiple_of(i * rows, rows)
        f = lb + (1.0 - lb) * jax.nn.sigmoid(stack_of(fb_ref, r0))
        logf = jnp.log(f)
        kk = 1.0 - f
        q = jax.nn.silu(stack_of(qb_ref, r0))
        v = stack_of(ib_ref, r0)
        vb = v.astype(BF16)
        hi = logf.astype(BF16)
        rem = logf - hi.astype(F32)
        mid = rem.astype(BF16)
        low = (rem - mid.astype(F32)).astype(BF16)
        parts = jnp.dot(tri, jnp.concatenate([hi, mid, low], axis=1), preferred_element_type=F32)
        bc = (parts[:, :B_KEY_DIM] + (parts[:, B_KEY_DIM:2 * B_KEY_DIM] + parts[:, 2 * B_KEY_DIM:])) * LOG2_E

        attn = lax.dot_general(q.astype(BF16), kk.astype(BF16), (((1,), (1,)), ((), ())),
                               preferred_element_type=F32)
        half = 1
        while half < c_:
            dec = jnp.exp2(-jnp.abs(bc - _hgrn_ref_rows(bc, half)))
            s = lax.dot_general((q * dec).astype(BF16), (kk * dec).astype(BF16), (((1,), (1,)), ((), ())),
                                preferred_element_type=F32)
            attn = jnp.where(xor >= half, s, attn)
            half *= 2
        attn = jnp.where(causal, attn, 0.0)
        o = jnp.dot(attn.astype(BF16), vb, preferred_element_type=F32)

        last = jnp.concatenate([jnp.broadcast_to(bc[a + c_ - 1:a + c_, :], (c_, B_KEY_DIM))
                                for a in range(0, n, c_)], axis=0)
        q_dec = (q * jnp.exp2(bc)).astype(BF16)
        k_dec = (kk * jnp.exp2(last - bc)).astype(BF16)
        st_dec = jnp.exp2(last)
        inter = []
        for h in range(nh):
            st = st_ref[h]
            for c in range(nc):
                a = h * rows + c * c_
                inter.append(lax.dot_general(q_dec[a:a + c_], st.astype(BF16), (((1,), (1,)), ((), ())),
                                             preferred_element_type=F32))
                upd = lax.dot_general(vb[a:a + c_], k_dec[a:a + c_], (((0,), (0,)), ((), ())),
                                      preferred_element_type=F32)
                st = st * st_dec[a:a + 1] + upd
            st_ref[h] = st
        o = o + jnp.concatenate(inter, axis=0)

        o = o * lax.rsqrt(jnp.mean(o * o, axis=-1, keepdims=True) + NORM_EPS)
        res = (o * gnorm * jax.nn.silu(stack_of(gb_ref, r0))).astype(o_ref.dtype)
        for h in range(nh):
            o_ref[pl.ds(r0, rows), h * B_KEY_DIM:(h + 1) * B_KEY_DIM] = res[h * rows:(h + 1) * rows]

    def stacks(i, carry):
        for u in range(HG_STACKS_PER_ITER):
            stack(i * HG_STACKS_PER_ITER + u)
        return carry

    lax.fori_loop(0, HG_TB // rows // HG_STACKS_PER_ITER, stacks, 0)


def _hgrn2(proj, lb_logits, norm_g, seq):
    hp = HG_HEADS_PER_STEP
    width = hp * B_KEY_DIM
    col = lambda off: pl.BlockSpec((HG_TB, width), lambda h, t: (t, off // hp + h))
    n_lb = lb_logits.shape[0]
    return pl.pallas_call(
        _hgrn_body,
        grid=(B_HEADS // hp, seq // HG_TB),
        in_specs=[col(_QB_BLK), col(_FB_BLK), col(_IB_BLK), col(_GB_BLK),
                  pl.BlockSpec((n_lb, width), lambda h, t: (0, h)),
                  pl.BlockSpec((1, width), lambda h, t: (0, h))],
        out_specs=pl.BlockSpec((HG_TB, width), lambda h, t: (t, h)),
        out_shape=jax.ShapeDtypeStruct((seq, B_WIDTH), BF16),
        scratch_shapes=[pltpu.VMEM((hp, B_KEY_DIM, B_KEY_DIM), F32)],
        compiler_params=_params("arbitrary", "arbitrary"),
        name="hgrn2",
    )(proj, proj, proj, proj, lb_logits.astype(F32), norm_g.reshape(1, B_WIDTH).astype(F32))


MERGE_GATE_BLK = 512


def _merge_body(oa_ref, ob_ref, *rest):
    n_g = (len(rest) - 5) // 2
    ga_refs, gb_refs = rest[:n_g], rest[n_g:2 * n_g]
    wa_ref, wb_ref, o_ref, wa_s, wb_s = rest[2 * n_g:]

    @pl.when(pl.program_id(1) == 0)
    def _():
        wa_s[...] = wa_ref[...].astype(BF16)
        wb_s[...] = wb_ref[...].astype(BF16)

    ya = jnp.dot(oa_ref[...], wa_s[...], preferred_element_type=F32)
    yb = jnp.dot(ob_ref[...], wb_s[...], preferred_element_type=F32)
    ga = jnp.concatenate([r[...] for r in ga_refs], axis=1)
    gb = jnp.concatenate([r[...] for r in gb_refs], axis=1)
    o_ref[...] = (jax.nn.sigmoid(ga) * ya + jax.nn.sigmoid(gb) * yb).astype(o_ref.dtype)


def _branch_merge(oa, ob, proj, wa, wb, tm=512, tn=1024):
    m = oa.shape[0]
    gw = MERGE_GATE_BLK
    n_g = tn // gw
    ga0, gb0 = _GATE_A_COL // gw, _GATE_B_COL // gw
    gate = lambda off, u: pl.BlockSpec((tm, gw), lambda j, i: (i, off + j * n_g + u))
    return pl.pallas_call(
        _merge_body,
        grid=(D_MODEL // tn, m // tm),
        in_specs=[pl.BlockSpec((tm, A_WIDTH), lambda j, i: (i, 0)), pl.BlockSpec((tm, B_WIDTH), lambda j, i: (i, 0)),
                  *[gate(ga0, u) for u in range(n_g)], *[gate(gb0, u) for u in range(n_g)],
                  pl.BlockSpec((A_WIDTH, tn), lambda j, i: (0, j)), pl.BlockSpec((B_WIDTH, tn), lambda j, i: (0, j))],
        out_specs=pl.BlockSpec((tm, tn), lambda j, i: (i, j)),
        out_shape=jax.ShapeDtypeStruct((m, D_MODEL), BF16),
        scratch_shapes=[pltpu.VMEM((A_WIDTH, tn), BF16), pltpu.VMEM((B_WIDTH, tn), BF16)],
        compiler_params=_params("arbitrary", "arbitrary"),
        name="branch_merge",
    )(oa, ob, *([proj] * (2 * n_g)), wa, wb)


def _route_rows(h, g, rw, rb):
    hn = _rmsnorm_rows(h, g)
    hn_hi = hn.astype(BF16)
    hn_lo = (hn - hn_hi.astype(F32)).astype(BF16)
    rw_hi = rw.astype(BF16)
    rw_lo = (rw - rw_hi.astype(F32)).astype(BF16)
    logits = (jnp.dot(hn_hi, rw_hi, preferred_element_type=F32)
              + (jnp.dot(hn_lo, rw_hi, preferred_element_type=F32)
                 + jnp.dot(hn_hi, rw_lo, preferred_element_type=F32))) + rb
    lane = lax.broadcasted_iota(jnp.int32, logits.shape, 1)
    neg = jnp.float32(-jnp.inf)
    big = jnp.int32(LANES)

    is_g = lane < N_GROUPS
    lg = jnp.where(is_g, logits, neg)
    mg = jnp.max(lg, axis=-1, keepdims=True)
    g_idx = jnp.min(jnp.where(lg == mg, lane, big), axis=-1, keepdims=True)
    pg_top = 1.0 / jnp.sum(jnp.where(is_g, jnp.exp(lg - mg), 0.0), axis=-1, keepdims=True)

    lo = N_GROUPS + g_idx * EXPERTS_PER_GROUP
    in_grp = jnp.logical_and(lane >= lo, lane < lo + EXPERTS_PER_GROUP)
    le = jnp.where(in_grp, logits, neg)
    v1 = jnp.max(le, axis=-1, keepdims=True)
    i1 = jnp.min(jnp.where(le == v1, lane, big), axis=-1, keepdims=True)
    le2 = jnp.where(lane == i1, neg, le)
    v2 = jnp.max(le2, axis=-1, keepdims=True)
    i2 = jnp.min(jnp.where(le2 == v2, lane, big), axis=-1, keepdims=True)
    e2 = jnp.exp(v2 - v1)
    w1 = pg_top / (1.0 + e2)
    w2 = pg_top * e2 / (1.0 + e2)
    eid = jnp.where(lane == 0, i1 - N_GROUPS, jnp.where(lane == 1, i2 - N_GROUPS, 0))
    wk = jnp.where(lane == 0, w1, jnp.where(lane == 1, w2, 0.0))
    return hn, eid, wk


def _out_router_body(a_ref, w_ref, x_ref, g_ref, rw_ref, rb_ref, h1_ref, hn_ref, eid_ref, wk_ref, h_prev):
    @pl.when(pl.program_id(0) == 0)
    def _():
        h_prev[...] = jnp.zeros_like(h_prev)

    hn_ref[...], eid_ref[...], wk_ref[...] = _route_rows(h_prev[...], g_ref[...], rw_ref[...], rb_ref[...])
    h1 = x_ref[...] + jnp.dot(a_ref[...], w_ref[...], preferred_element_type=F32)
    h1_ref[...] = h1
    h_prev[...] = h1


def _out_proj_router(a, w_bf16, x, g, rw, rb, tm=512):
    m, d = x.shape
    n_blk = m // tm
    cur = lambda width: pl.BlockSpec((tm, width), lambda i: (jnp.minimum(i, n_blk - 1), 0))
    prev = lambda width: pl.BlockSpec((tm, width), lambda i: (jnp.maximum(i - 1, 0), 0))
    const = lambda shape, **kw: pl.BlockSpec(shape, lambda i: (0, 0), **kw)
    return pl.pallas_call(
        _out_router_body,
        grid=(n_blk + 1,),
        in_specs=[cur(a.shape[1]), const(w_bf16.shape, pipeline_mode=pl.Buffered(1)), cur(d),
                  const((1, d)), const((d, LANES)), const((1, LANES))],
        out_specs=[cur(d), prev(d), prev(LANES), prev(LANES)],
        out_shape=[jax.ShapeDtypeStruct((m, d), F32), jax.ShapeDtypeStruct((m, d), F32),
                   jax.ShapeDtypeStruct((m, LANES), jnp.int32), jax.ShapeDtypeStruct((m, LANES), F32)],
        scratch_shapes=[pltpu.VMEM((tm, d), F32)],
        compiler_params=_params("arbitrary"),
        name="out_proj_router",
    )(a, w_bf16, x, g.reshape(1, d).astype(F32), rw, rb)


MOE_TM = 256
ROW_DMA_PRIORITIES = (0, 1)


def _moe_body(tile_e_ref, tile_blk_ref, tile_rows_ref, n_used_ref,
              src0_ref, src_next_ref, hn_ref, w1_ref, w3_ref, w2_ref, o_ref,
              w1_s, w3_s, w2_s, x0, x1, gsem):
    j = pl.program_id(0)
    n_tiles = pl.num_programs(0)
    n_used = n_used_ref[0]
    xs = (x0, x1)

    def rows_of(t):
        return jnp.where(t >= 0, tile_rows_ref[jnp.clip(t, 0, n_tiles - 1)], 0)

    def gather_issue(idx_ref, b, n):
        for r in range(MOE_TM):
            @pl.when(r < n)
            def _(r=r):
                pltpu.make_async_copy(hn_ref.at[idx_ref[0, 0, r]], xs[b].at[r],
                                      gsem.at[b]).start(priority=ROW_DMA_PRIORITIES[r % 2])

    def gather_wait(b, n):
        p = MOE_TM
        while p >= 1:
            @pl.when(jnp.bitwise_and(n, p) != 0)
            def _(p=p):
                pltpu.make_async_copy(hn_ref.at[pl.ds(0, p)], xs[b].at[pl.ds(0, p)], gsem.at[b]).wait()
            p //= 2

    @pl.when(j == 0)
    def _():
        x0[...] = jnp.zeros_like(x0)
        x1[...] = jnp.zeros_like(x1)
        gather_issue(src0_ref, 0, rows_of(0))

    @pl.when(j >= n_used)
    def _():
        o_ref[...] = jnp.zeros_like(o_ref)

    for b in (0, 1):
        @pl.when(jnp.logical_and(j < n_used, j % 2 == b))
        def _(b=b):
            gather_wait(b, rows_of(j))

            prev_e = tile_e_ref[jnp.maximum(j - 1, 0)]

            @pl.when(jnp.logical_or(j == 0, tile_e_ref[j] != prev_e))
            def _():
                w1_s[...] = w1_ref[0].astype(BF16)
                w3_s[...] = w3_ref[0].astype(BF16)
                w2_s[...] = w2_ref[0].astype(BF16)

            gather_issue(src_next_ref, 1 - b, rows_of(j + 1))
            x = xs[b][...].astype(BF16)
            hid = (jax.nn.silu(jnp.dot(x, w1_s[...], preferred_element_type=F32))
                   * jnp.dot(x, w3_s[...], preferred_element_type=F32))
            o_ref[...] = jnp.dot(hid.astype(BF16), w2_s[...], preferred_element_type=F32)


def _moe_experts(hn, w1, w3, w2, tile_e, tile_blk, tile_rows, n_used, src):
    d = hn.shape[1]
    n_tiles = tile_e.shape[0]
    ff = w1.shape[-1]
    idx_block = lambda fn: pl.BlockSpec((1, 1, MOE_TM), fn, memory_space=pltpu.SMEM)
    grid_spec = pltpu.PrefetchScalarGridSpec(
        num_scalar_prefetch=4,
        grid=(n_tiles,),
        in_specs=[
            idx_block(lambda j, te, tb, tr, nu: (0, 0, 0)),
            idx_block(lambda j, te, tb, tr, nu: (tb[jnp.minimum(j + 1, n_tiles - 1)], 0, 0)),
            pl.BlockSpec(memory_space=pl.ANY),
            pl.BlockSpec((1, d, ff), lambda j, te, tb, tr, nu: (te[j], 0, 0)),
            pl.BlockSpec((1, d, ff), lambda j, te, tb, tr, nu: (te[j], 0, 0)),
            pl.BlockSpec((1, ff, d), lambda j, te, tb, tr, nu: (te[j], 0, 0)),
        ],
        out_specs=pl.BlockSpec((MOE_TM, d), lambda j, te, tb, tr, nu: (j, 0)),
        scratch_shapes=[pltpu.VMEM((d, ff), BF16), pltpu.VMEM((d, ff), BF16), pltpu.VMEM((ff, d), BF16),
                        pltpu.VMEM((MOE_TM, d), F32), pltpu.VMEM((MOE_TM, d), F32),
                        pltpu.SemaphoreType.DMA((2,))],
    )
    src3 = src.reshape(n_tiles, 1, MOE_TM)
    return pl.pallas_call(
        _moe_body,
        grid_spec=grid_spec,
        out_shape=jax.ShapeDtypeStruct((n_tiles * MOE_TM, d), F32),
        compiler_params=_params("arbitrary"),
        name="moe_experts",
    )(tile_e, tile_blk, tile_rows, n_used, src3, src3, hn, w1, w3, w2)


COMBINE_TM = 256


def _combine_body(idx0_ref, idx_next_ref, h_ref, wk_ref, g_ref, ys_ref, o_ref, ya0, ya1, yb0, yb1, sem):
    i = pl.program_id(0)
    n = pl.num_programs(0)
    bufs = ((ya0, ya1), (yb0, yb1))
    tm = COMBINE_TM

    def gather_issue(idx_ref, b, n_rows):
        for r in range(tm):
            for k in range(2):
                @pl.when(r < n_rows)
                def _(r=r, k=k):
                    pltpu.make_async_copy(ys_ref.at[idx_ref[0, 0, 2 * r + k]], bufs[b][k].at[r],
                                          sem.at[b]).start(priority=ROW_DMA_PRIORITIES[k])

    def gather_wait(b):
        for k in range(2):
            pltpu.make_async_copy(ys_ref.at[pl.ds(0, tm)], bufs[b][k], sem.at[b]).wait()

    @pl.when(i == 0)
    def _():
        gather_issue(idx0_ref, 0, jnp.int32(tm))

    for b in (0, 1):
        @pl.when(i % 2 == b)
        def _(b=b):
            gather_wait(b)
            gather_issue(idx_next_ref, 1 - b, jnp.where(i + 1 < n, tm, 0))
            wk = wk_ref[...]
            y = h_ref[...] + wk[:, 0:1] * bufs[b][0][...] + wk[:, 1:2] * bufs[b][1][...]
            o_ref[...] = _rmsnorm_rows(y, g_ref[...]).astype(o_ref.dtype)


def _combine(h, ys, pos, wk, g, out_dtype):
    m, d = h.shape
    tm = COMBINE_TM
    nblk = m // tm
    idx_block = lambda fn: pl.BlockSpec((1, 1, 2 * tm), fn, memory_space=pltpu.SMEM)
    pos3 = pos.reshape(nblk, 1, 2 * tm)
    return pl.pallas_call(
        _combine_body,
        grid=(nblk,),
        in_specs=[idx_block(lambda i: (0, 0, 0)), idx_block(lambda i: (jnp.minimum(i + 1, nblk - 1), 0, 0)),
                  pl.BlockSpec((tm, d), lambda i: (i, 0)), pl.BlockSpec((tm, LANES), lambda i: (i, 0)),
                  pl.BlockSpec((1, d), lambda i: (0, 0)), pl.BlockSpec(memory_space=pl.ANY)],
        out_specs=pl.BlockSpec((tm, d), lambda i: (i, 0)),
        out_shape=jax.ShapeDtypeStruct((m, d), out_dtype),
        scratch_shapes=[pltpu.VMEM((tm, d), F32), pltpu.VMEM((tm, d), F32), pltpu.VMEM((tm, d), F32),
                        pltpu.VMEM((tm, d), F32), pltpu.SemaphoreType.DMA((2,))],
        compiler_params=_params("arbitrary"),
        name="combine_final_norm",
    )(pos3, pos3, h, wk, g.reshape(1, d).astype(F32), ys)


def _routing_tables(eid, n_tokens):
    n_pairs = 2 * n_tokens
    n_tiles = n_pairs // MOE_TM + N_EXPERTS + 1
    n_rows = n_tiles * MOE_TM
    experts = jnp.arange(N_EXPERTS, dtype=jnp.int32)[None, :]
    e_flat = eid.reshape(n_pairs).astype(jnp.int32)
    counts = jnp.sum(e_flat[:, None] == experts, axis=0, dtype=jnp.int32)
    tiles_per_e = (counts + MOE_TM - 1) // MOE_TM
    tile_end = jnp.cumsum(tiles_per_e)
    tile_start = tile_end - tiles_per_e
    n_used = tile_end[-1]
    filler_end = jnp.cumsum(tiles_per_e * MOE_TM - counts)
    filler = jnp.arange(n_rows - n_pairs, dtype=jnp.int32)[:, None]
    filler_e = jnp.sum(filler_end[None, :] <= filler, axis=1, dtype=jnp.int32)
    keys = jnp.concatenate([2 * e_flat, 2 * filler_e + 1])
    order = jnp.argsort(keys, stable=True).astype(jnp.int32)
    src_tok = jnp.where(order < n_pairs, order, 0) // 2
    pair_row = jnp.argsort(order)[:n_pairs].reshape(n_tokens, 2)
    j = jnp.arange(n_tiles, dtype=jnp.int32)
    tile_blk = jnp.minimum(j, n_used - 1)
    tile_e = jnp.sum(tile_end[None, :] <= tile_blk[:, None], axis=1, dtype=jnp.int32)
    own = tile_e[:, None] == experts
    tile_row0 = (tile_blk - jnp.sum(jnp.where(own, tile_start[None, :], 0), axis=1)) * MOE_TM
    own_count = jnp.sum(jnp.where(own, counts[None, :], 0), axis=1)
    tile_rows = jnp.where(j == tile_blk, jnp.clip(own_count - tile_row0, 0, MOE_TM), 0).astype(jnp.int32)
    return (tile_e, tile_blk.astype(jnp.int32), tile_rows, n_used.reshape(1).astype(jnp.int32),
            src_tok.astype(jnp.int32), pair_row.astype(jnp.int32))


def kernel(x, mix_norm_g, w_in, hgrn_lb_logits, hgrn_norm_g, w_branch_a, w_branch_b, w_out, ffn_norm_g,
           router_w_group, router_b_group, router_w_expert, router_b_expert, expert_w1, expert_w3, expert_w2,
           final_norm_g):
    b, seq, d = x.shape
    assert b == 1 and d == D_MODEL and seq % ATT_SUPER == 0 and w_in.shape == (1, D_MODEL, IN_WIDTH)
    h0 = x.reshape(seq, d).astype(F32)

    xn = _rmsnorm(h0, mix_norm_g[0], BF16)
    proj = _matmul(xn, w_in[0], F32, tm=1024, tn=1280, name="in_proj")
    oa = _dilated_attention(proj, seq)
    ob = _hgrn2(proj, hgrn_lb_logits, hgrn_norm_g[0], seq)
    merged = _branch_merge(oa, ob, proj, w_branch_a[0], w_branch_b[0])
    pad = LANES - N_GROUPS - N_EXPERTS
    rw = jnp.concatenate([router_w_group[0], router_w_expert[0], jnp.zeros((d, pad), F32)], axis=1).astype(F32)
    rb = jnp.concatenate([router_b_group[0], router_b_expert[0], jnp.zeros((pad,), F32)]).reshape(1, LANES).astype(F32)
    h1, hn, eid, wk = _out_proj_router(merged, w_out[0].astype(BF16), h0, ffn_norm_g[0], rw, rb)

    tile_e, tile_blk, tile_rows, n_used, src_tok, pair_row = _routing_tables(eid[:, :2], seq)
    w1 = expert_w1[0].reshape(N_EXPERTS, D_MODEL, EXPERT_FF)
    w3 = expert_w3[0].reshape(N_EXPERTS, D_MODEL, EXPERT_FF)
    w2 = expert_w2[0].reshape(N_EXPERTS, EXPERT_FF, D_MODEL)
    ys = _moe_experts(hn, w1, w3, w2, tile_e, tile_blk, tile_rows, n_used, src_tok)

    out = _combine(h1, ys, pair_row, wk, final_norm_g, x.dtype)
    return out.reshape(b, seq, d)
```

```python
import functools

import jax
import jax.numpy as jnp
from jax import lax
from jax.experimental import pallas as pl
from jax.experimental.pallas import tpu as pltpu

F32 = jnp.float32
BF16 = jnp.bfloat16

D_MODEL = 2048
A_HEADS = 12
A_HEAD_DIM = 128
A_WIDTH = A_HEADS * A_HEAD_DIM
A_SCALE = A_HEAD_DIM ** -0.5
DILATED_CONFIGS = ((128, 1), (512, 4), (2048, 16))
B_HEADS = 8
B_KEY_DIM = 128
B_WIDTH = B_HEADS * B_KEY_DIM
N_GROUPS = 4
EXPERTS_PER_GROUP = 8
N_EXPERTS = N_GROUPS * EXPERTS_PER_GROUP
EXPERT_FF = 512
NORM_EPS = 1e-6

LANES = 128
SUBLANES = 8
VMEM_LIMIT = 56 * 1024 * 1024

_QA_BLK = 0
_KA_BLK = A_HEADS
_VA_BLK = 2 * A_HEADS
_QB_BLK = 3 * A_HEADS
_FB_BLK = _QB_BLK + B_HEADS
_IB_BLK = _FB_BLK + B_HEADS
_GB_BLK = _IB_BLK + B_HEADS
_GATE_A_COL = 3 * A_WIDTH + 4 * B_WIDTH
_GATE_B_COL = _GATE_A_COL + D_MODEL
IN_WIDTH = _GATE_B_COL + D_MODEL


def _params(*sem):
    return pltpu.CompilerParams(dimension_semantics=sem, vmem_limit_bytes=VMEM_LIMIT)


def _rmsnorm_rows(x, g):
    ms = jnp.mean(x * x, axis=-1, keepdims=True)
    return x * lax.rsqrt(ms + NORM_EPS) * g


def _rmsnorm_body(x_ref, g_ref, o_ref):
    o_ref[...] = _rmsnorm_rows(x_ref[...].astype(F32), g_ref[...]).astype(o_ref.dtype)


def _rmsnorm(x, g, out_dtype, tm=512):
    m, d = x.shape
    return pl.pallas_call(
        _rmsnorm_body,
        grid=(m // tm,),
        in_specs=[pl.BlockSpec((tm, d), lambda i: (i, 0)), pl.BlockSpec((1, d), lambda i: (0, 0))],
        out_specs=pl.BlockSpec((tm, d), lambda i: (i, 0)),
        out_shape=jax.ShapeDtypeStruct((m, d), out_dtype),
        compiler_params=_params("arbitrary"),
        name="rmsnorm",
    )(x, g.reshape(1, d).astype(F32))


def _matmul_body(a_ref, w_ref, o_ref, wb_ref):
    @pl.when(pl.program_id(1) == 0)
    def _():
        wb_ref[...] = w_ref[...].astype(BF16)

    o_ref[...] = jnp.dot(a_ref[...], wb_ref[...], preferred_element_type=F32).astype(o_ref.dtype)


def _matmul(a, w, out_dtype, tm, tn, name):
    m, k = a.shape
    n = w.shape[1]
    return pl.pallas_call(
        _matmul_body,
        grid=(n // tn, m // tm),
        in_specs=[pl.BlockSpec((tm, k), lambda j, i: (i, 0)), pl.BlockSpec((k, tn), lambda j, i: (0, j))],
        out_specs=pl.BlockSpec((tm, tn), lambda j, i: (i, j)),
        out_shape=jax.ShapeDtypeStruct((m, n), out_dtype),
        scratch_shapes=[pltpu.VMEM((k, tn), BF16)],
        compiler_params=_params("arbitrary", "arbitrary"),
        name=name,
    )(a, w)


LOG2_E = 1.4426950408889634
LN_2 = 0.6931471805599453
ATT_BLK = 128
ATT_SUPER = 2048
ATT_DEINT = 4


def _attn_body(q_ref, k_ref, v_ref, o_ref, o_scr, lse_scr, bias_scr, k4, v4, q4, stage):
    sb = pl.program_id(1)
    seq = k_ref.shape[0]
    nd = ATT_DEINT
    sub = ATT_SUPER // nd
    diff = (lax.broadcasted_iota(jnp.int32, (ATT_BLK, 2 * ATT_BLK), 1)
            - lax.broadcasted_iota(jnp.int32, (ATT_BLK, 2 * ATT_BLK), 0))
    neg = jnp.float32(-jnp.inf)
    bias_scr[0] = jnp.where(jnp.logical_and(diff >= 0, diff <= ATT_BLK), 0.0, neg)
    bias_scr[1] = jnp.where(diff <= 0, 0.0, neg)

    @pl.when(sb == 0)
    def _():
        rows = 512

        def split(i, carry):
            for a in range(nd):
                src = pl.ds(i * (rows * nd) + a, rows, stride=nd)
                dst = pl.ds(pl.multiple_of(i * rows, rows), rows)
                k4[a, dst, :] = k_ref[src, :]
                v4[a, dst, :] = v_ref[src, :]
            return carry

        lax.fori_loop(0, seq // nd // rows, split, 0)

    for a in range(nd):
        q4[a] = q_ref[pl.ds(a, sub, stride=nd), :]

    def attend(q, k, v, bias):
        qb = (q * (A_SCALE * LOG2_E)).astype(BF16)
        s = lax.dot_general(qb, k.astype(BF16), (((1,), (1,)), ((), ())), preferred_element_type=F32)
        s = s + bias
        mx = jnp.max(s, axis=-1, keepdims=True)
        p = jnp.exp2(s - mx)
        den = jnp.sum(p, axis=-1, keepdims=True)
        acc = jnp.dot(p.astype(BF16), v.astype(BF16), preferred_element_type=F32)
        return acc / den, jnp.broadcast_to(mx * LN_2 + jnp.log(den), (ATT_BLK, A_HEAD_DIM))

    n_tiles = ATT_SUPER // ATT_BLK
    for c, (window, dil) in enumerate(DILATED_CONFIGS):
        assert window // dil == ATT_BLK and (dil == 1 or dil % nd == 0)
        tiles_per_res = n_tiles // dil
        for t in range(n_tiles):
            n, r = t % tiles_per_res, t // tiles_per_res
            if n == 0:
                first = (sb == 0).astype(jnp.int32)
                back = ATT_BLK * (1 - first)
                bias = bias_scr[first]
            else:
                back, bias = ATT_BLK, bias_scr[0]
            if dil == 1:
                q_rows = pl.ds(n * ATT_BLK, ATT_BLK)
                kv_rows = pl.ds(pl.multiple_of(sb * ATT_SUPER + n * ATT_BLK - back, ATT_BLK), 2 * ATT_BLK)
                o, lse = attend(q_ref[q_rows, :], k_ref[kv_rows, :], v_ref[kv_rows, :], bias)
                out_rows = q_rows
            else:
                st, a, r2 = dil // nd, r % nd, r // nd
                i0 = n * ATT_BLK
                q_rows = pl.ds(st * i0 + r2, ATT_BLK, stride=st) if st > 1 else pl.ds(i0, ATT_BLK)
                kv0 = st * (sb * (ATT_SUPER // dil) + i0 - back) + r2
                kv_rows = pl.ds(kv0, 2 * ATT_BLK, stride=st) if st > 1 else pl.ds(kv0, 2 * ATT_BLK)
                o, lse = attend(q4[a, q_rows, :], k4[a, kv_rows, :], v4[a, kv_rows, :], bias)
                out_rows = (pl.ds(a * sub + st * i0 + r2, ATT_BLK, stride=st) if st > 1
                            else pl.ds(a * sub + i0, ATT_BLK))
            o_scr[c, out_rows, :] = o
            lse_scr[c, out_rows, :] = lse

    rows = 128

    def merge(i, carry):
        for a in range(nd):
            tok = pl.ds(i * (rows * nd) + a, rows, stride=nd)
            cls = pl.ds(a * sub + pl.multiple_of(i * rows, rows), rows)
            l0, l1, l2 = lse_scr[0, tok, :], lse_scr[1, cls, :], lse_scr[2, cls, :]
            m = jnp.maximum(jnp.maximum(l0, l1), l2)
            w0, w1, w2 = jnp.exp(l0 - m), jnp.exp(l1 - m), jnp.exp(l2 - m)
            num = w0 * o_scr[0, tok, :] + w1 * o_scr[1, cls, :] + w2 * o_scr[2, cls, :]
            stage[tok, :] = num / (w0 + w1 + w2)
        return carry

    lax.fori_loop(0, sub // rows, merge, 0)
    o_ref[...] = stage[...].astype(o_ref.dtype)


def _dilated_attention(proj, seq):
    n_super = seq // ATT_SUPER
    blk = lambda off: pl.BlockSpec((seq, A_HEAD_DIM), lambda h, s: (0, off + h))
    return pl.pallas_call(
        _attn_body,
        grid=(A_HEADS, n_super),
        in_specs=[pl.BlockSpec((ATT_SUPER, A_HEAD_DIM), lambda h, s: (s, _QA_BLK + h)), blk(_KA_BLK), blk(_VA_BLK)],
        out_specs=pl.BlockSpec((ATT_SUPER, A_HEAD_DIM), lambda h, s: (s, h)),
        out_shape=jax.ShapeDtypeStruct((seq, A_WIDTH), BF16),
        scratch_shapes=[pltpu.VMEM((len(DILATED_CONFIGS), ATT_SUPER, A_HEAD_DIM), F32),
                        pltpu.VMEM((len(DILATED_CONFIGS), ATT_SUPER, A_HEAD_DIM), F32),
                        pltpu.VMEM((2, ATT_BLK, 2 * ATT_BLK), F32),
                        pltpu.VMEM((ATT_DEINT, seq // ATT_DEINT, A_HEAD_DIM), F32),
                        pltpu.VMEM((ATT_DEINT, seq // ATT_DEINT, A_HEAD_DIM), F32),
                        pltpu.VMEM((ATT_DEINT, ATT_SUPER // ATT_DEINT, A_HEAD_DIM), F32),
                        pltpu.VMEM((ATT_SUPER, A_HEAD_DIM), F32)],
        compiler_params=_params("arbitrary", "arbitrary"),
        name="dilated_attention",
    )(proj, proj, proj)


HG_CHUNK = 64
HG_TB = 512
HG_HEADS_PER_STEP = 2
HG_CHUNKS_PER_STACK = 2
HG_STACKS_PER_ITER = 2


def _hgrn_ref_rows(bc, half):
    n, width = bc.shape
    blk = 2 * half
    if blk >= SUBLANES:
        rows = [jnp.broadcast_to(bc[b0 + half - 1:b0 + half, :], (blk, width)) for b0 in range(0, n, blk)]
        return jnp.concatenate(rows, axis=0) if len(rows) > 1 else rows[0]
    sub = lax.broadcasted_iota(jnp.int32, (SUBLANES, width), 0)
    groups = []
    for g0 in range(0, n, SUBLANES):
        grp = bc[g0:g0 + SUBLANES, :]
        if half == 1:
            groups.append(jnp.where(jnp.bitwise_and(sub, 1) == 1, pltpu.roll(grp, 1, 0), grp))
        else:
            assert half == 2 and SUBLANES == 8
            groups.append(jnp.where(sub < 4, jnp.broadcast_to(grp[1:2, :], grp.shape),
                                    jnp.broadcast_to(grp[5:6, :], grp.shape)))
    return jnp.concatenate(groups, axis=0)


def _hgrn_body(qb_ref, fb_ref, ib_ref, gb_ref, lbl_ref, g_ref, o_ref, st_ref):
    @pl.when(pl.program_id(1) == 0)
    def _():
        st_ref[...] = jnp.zeros_like(st_ref)

    c_ = HG_CHUNK
    lbl = lbl_ref[...]
    e = jnp.exp(lbl - jnp.max(lbl, axis=0, keepdims=True))
    lb_all = e[0:1, :] / jnp.sum(e, axis=0, keepdims=True)
    g_all = g_ref[...]

    nh, nc = HG_HEADS_PER_STEP, HG_CHUNKS_PER_STACK
    rows = nc * c_
    n = nh * rows
    ti = lax.broadcasted_iota(jnp.int32, (n, n), 0)
    si = lax.broadcasted_iota(jnp.int32, (n, n), 1)
    xor = jnp.bitwise_xor(ti, si)
    causal = jnp.logical_and(xor < c_, si <= ti)
    tri = jnp.where(causal, 1.0, 0.0).astype(BF16)

    def stack_of(ref, r0):
        return jnp.concatenate([ref[pl.ds(r0, rows), h * B_KEY_DIM:(h + 1) * B_KEY_DIM] for h in range(nh)], axis=0)

    def per_head_rows(x):
        return jnp.concatenate([jnp.broadcast_to(x[:, h * B_KEY_DIM:(h + 1) * B_KEY_DIM], (rows, B_KEY_DIM))
                                for h in range(nh)], axis=0)

    lb = per_head_rows(lb_all)
    gnorm = per_head_rows(g_all)

    def stack(i):
        r0 = pl.multiple_of(i * rows, rows)
        f = lb + (1.0 - lb) * jax.nn.sigmoid(stack_of(fb_ref, r0))
        logf = jnp.log(f)
        kk = 1.0 - f
        q = jax.nn.silu(stack_of(qb_ref, r0))
        v = stack_of(ib_ref, r0)
        vb = v.astype(BF16)
        hi = logf.astype(BF16)
        rem = logf - hi.astype(F32)
        mid = rem.astype(BF16)
        low = (rem - mid.astype(F32)).astype(BF16)
        parts = jnp.dot(tri, jnp.concatenate([hi, mid, low], axis=1), preferred_element_type=F32)
        bc = (parts[:, :B_KEY_DIM] + (parts[:, B_KEY_DIM:2 * B_KEY_DIM] + parts[:, 2 * B_KEY_DIM:])) * LOG2_E

        attn = lax.dot_general(q.astype(BF16), kk.astype(BF16), (((1,), (1,)), ((), ())),
                               preferred_element_type=F32)
        half = 1
        while half < c_:
            dec = jnp.exp2(-jnp.abs(bc - _hgrn_ref_rows(bc, half)))
            s = lax.dot_general((q * dec).astype(BF16), (kk * dec).astype(BF16), (((1,), (1,)), ((), ())),
                                preferred_element_type=F32)
            attn = jnp.where(xor >= half, s, attn)
            half *= 2
        attn = jnp.where(causal, attn, 0.0)
        o = jnp.dot(attn.astype(BF16), vb, preferred_element_type=F32)

        last = jnp.concatenate([jnp.broadcast_to(bc[a + c_ - 1:a + c_, :], (c_, B_KEY_DIM))
                                for a in range(0, n, c_)], axis=0)
        q_dec = (q * jnp.exp2(bc)).astype(BF16)
        k_dec = (kk * jnp.exp2(last - bc)).astype(BF16)
        st_dec = jnp.exp2(last)
        inter = []
        for h in range(nh):
            st = st_ref[h]
            for c in range(nc):
                a = h * rows + c * c_
                inter.append(lax.dot_general(q_dec[a:a + c_], st.astype(BF16), (((1,), (1,)), ((), ())),
                                             preferred_element_type=F32))
                upd = lax.dot_general(vb[a:a + c_], k_dec[a:a + c_], (((0,), (0,)), ((), ())),
                                      preferred_element_type=F32)
                st = st * st_dec[a:a + 1] + upd
            st_ref[h] = st
        o = o + jnp.concatenate(inter, axis=0)

        o = o * lax.rsqrt(jnp.mean(o * o, axis=-1, keepdims=True) + NORM_EPS)
        res = (o * gnorm * jax.nn.silu(stack_of(gb_ref, r0))).astype(o_ref.dtype)
        for h in range(nh):
            o_ref[pl.ds(r0, rows), h * B_KEY_DIM:(h + 1) * B_KEY_DIM] = res[h * rows:(h + 1) * rows]

    def stacks(i, carry):
        for u in range(HG_STACKS_PER_ITER):
            stack(i * HG_STACKS_PER_ITER + u)
        return carry

    lax.fori_loop(0, HG_TB // rows // HG_STACKS_PER_ITER, stacks, 0)


def _hgrn2(proj, lb_logits, norm_g, seq):
    hp = HG_HEADS_PER_STEP
    width = hp * B_KEY_DIM
    col = lambda off: pl.BlockSpec((HG_TB, width), lambda h, t: (t, off // hp + h))
    n_lb = lb_logits.shape[0]
    return pl.pallas_call(
        _hgrn_body,
        grid=(B_HEADS // hp, seq // HG_TB),
        in_specs=[col(_QB_BLK), col(_FB_BLK), col(_IB_BLK), col(_GB_BLK),
                  pl.BlockSpec((n_lb, width), lambda h, t: (0, h)),
                  pl.BlockSpec((1, width), lambda h, t: (0, h))],
        out_specs=pl.BlockSpec((HG_TB, width), lambda h, t: (t, h)),
        out_shape=jax.ShapeDtypeStruct((seq, B_WIDTH), BF16),
        scratch_shapes=[pltpu.VMEM((hp, B_KEY_DIM, B_KEY_DIM), F32)],
        compiler_params=_params("arbitrary", "arbitrary"),
        name="hgrn2",
    )(proj, proj, proj, proj, lb_logits.astype(F32), norm_g.reshape(1, B_WIDTH).astype(F32))


MERGE_GATE_BLK = 512


def _merge_body(oa_ref, ob_ref, *rest):
    n_g = (len(rest) - 5) // 2
    ga_refs, gb_refs = rest[:n_g], rest[n_g:2 * n_g]
    wa_ref, wb_ref, o_ref, wa_s, wb_s = rest[2 * n_g:]

    @pl.when(pl.program_id(1) == 0)
    def _():
        wa_s[...] = wa_ref[...].astype(BF16)
        wb_s[...] = wb_ref[...].astype(BF16)

    ya = jnp.dot(oa_ref[...], wa_s[...], preferred_element_type=F32)
    yb = jnp.dot(ob_ref[...], wb_s[...], preferred_element_type=F32)
    ga = jnp.concatenate([r[...] for r in ga_refs], axis=1)
    gb = jnp.concatenate([r[...] for r in gb_refs], axis=1)
    o_ref[...] = (jax.nn.sigmoid(ga) * ya + jax.nn.sigmoid(gb) * yb).astype(o_ref.dtype)


def _branch_merge(oa, ob, proj, wa, wb, tm=512, tn=1024):
    m = oa.shape[0]
    gw = MERGE_GATE_BLK
    n_g = tn // gw
    ga0, gb0 = _GATE_A_COL // gw, _GATE_B_COL // gw
    gate = lambda off, u: pl.BlockSpec((tm, gw), lambda j, i: (i, off + j * n_g + u))
    return pl.pallas_call(
        _merge_body,
        grid=(D_MODEL // tn, m // tm),
        in_specs=[pl.BlockSpec((tm, A_WIDTH), lambda j, i: (i, 0)), pl.BlockSpec((tm, B_WIDTH), lambda j, i: (i, 0)),
                  *[gate(ga0, u) for u in range(n_g)], *[gate(gb0, u) for u in range(n_g)],
                  pl.BlockSpec((A_WIDTH, tn), lambda j, i: (0, j)), pl.BlockSpec((B_WIDTH, tn), lambda j, i: (0, j))],
        out_specs=pl.BlockSpec((tm, tn), lambda j, i: (i, j)),
        out_shape=jax.ShapeDtypeStruct((m, D_MODEL), BF16),
        scratch_shapes=[pltpu.VMEM((A_WIDTH, tn), BF16), pltpu.VMEM((B_WIDTH, tn), BF16)],
        compiler_params=_params("arbitrary", "arbitrary"),
        name="branch_merge",
    )(oa, ob, *([proj] * (2 * n_g)), wa, wb)


def _route_rows(h, g, rw, rb):
    hn = _rmsnorm_rows(h, g)
    hn_hi = hn.astype(BF16)
    hn_lo = (hn - hn_hi.astype(F32)).astype(BF16)
    rw_hi = rw.astype(BF16)
    rw_lo = (rw - rw_hi.astype(F32)).astype(BF16)
    logits = (jnp.dot(hn_hi, rw_hi, preferred_element_type=F32)
              + (jnp.dot(hn_lo, rw_hi, preferred_element_type=F32)
                 + jnp.dot(hn_hi, rw_lo, preferred_element_type=F32))) + rb
    lane = lax.broadcasted_iota(jnp.int32, logits.shape, 1)
    neg = jnp.float32(-jnp.inf)
    big = jnp.int32(LANES)

    is_g = lane < N_GROUPS
    lg = jnp.where(is_g, logits, neg)
    mg = jnp.max(lg, axis=-1, keepdims=True)
    g_idx = jnp.min(jnp.where(lg == mg, lane, big), axis=-1, keepdims=True)
    pg_top = 1.0 / jnp.sum(jnp.where(is_g, jnp.exp(lg - mg), 0.0), axis=-1, keepdims=True)

    lo = N_GROUPS + g_idx * EXPERTS_PER_GROUP
    in_grp = jnp.logical_and(lane >= lo, lane < lo + EXPERTS_PER_GROUP)
    le = jnp.where(in_grp, logits, neg)
    v1 = jnp.max(le, axis=-1, keepdims=True)
    i1 = jnp.min(jnp.where(le == v1, lane, big), axis=-1, keepdims=True)
    le2 = jnp.where(lane == i1, neg, le)
    v2 = jnp.max(le2, axis=-1, keepdims=True)
    i2 = jnp.min(jnp.where(le2 == v2, lane, big), axis=-1, keepdims=True)
    e2 = jnp.exp(v2 - v1)
    w1 = pg_top / (1.0 + e2)
    w2 = pg_top * e2 / (1.0 + e2)
    eid = jnp.where(lane == 0, i1 - N_GROUPS, jnp.where(lane == 1, i2 - N_GROUPS, 0))
    wk = jnp.where(lane == 0, w1, jnp.where(lane == 1, w2, 0.0))
    return hn, eid, wk


def _out_router_body(a_ref, w_ref, x_ref, g_ref, rw_ref, rb_ref, h1_ref, hn_ref, eid_ref, wk_ref, h_prev):
    @pl.when(pl.program_id(0) == 0)
    def _():
        h_prev[...] = jnp.zeros_like(h_prev)

    hn_ref[...], eid_ref[...], wk_ref[...] = _route_rows(h_prev[...], g_ref[...], rw_ref[...], rb_ref[...])
    h1 = x_ref[...] + jnp.dot(a_ref[...], w_ref[...], preferred_element_type=F32)
    h1_ref[...] = h1
    h_prev[...] = h1


def _out_proj_router(a, w_bf16, x, g, rw, rb, tm=512):
    m, d = x.shape
    n_blk = m // tm
    cur = lambda width: pl.BlockSpec((tm, width), lambda i: (jnp.minimum(i, n_blk - 1), 0))
    prev = lambda width: pl.BlockSpec((tm, width), lambda i: (jnp.maximum(i - 1, 0), 0))
    const = lambda shape, **kw: pl.BlockSpec(shape, lambda i: (0, 0), **kw)
    return pl.pallas_call(
        _out_router_body,
        grid=(n_blk + 1,),
        in_specs=[cur(a.shape[1]), const(w_bf16.shape, pipeline_mode=pl.Buffered(1)), cur(d),
                  const((1, d)), const((d, LANES)), const((1, LANES))],
        out_specs=[cur(d), prev(d), prev(LANES), prev(LANES)],
        out_shape=[jax.ShapeDtypeStruct((m, d), F32), jax.ShapeDtypeStruct((m, d), F32),
                   jax.ShapeDtypeStruct((m, LANES), jnp.int32), jax.ShapeDtypeStruct((m, LANES), F32)],
        scratch_shapes=[pltpu.VMEM((tm, d), F32)],
        compiler_params=_params("arbitrary"),
        name="out_proj_router",
    )(a, w_bf16, x, g.reshape(1, d).astype(F32), rw, rb)


MOE_TM = 256
ROW_DMA_PRIORITIES = (0, 1)


def _moe_body(pref_e_ref, w_slot_ref, tile_rows_ref, n_used_ref,
              src_ref, hn_ref, w1_ref, w3_ref, w2_ref, o_ref,
              w1_s, w3_s, w2_s, x0, x1, gsem):
    s = pl.program_id(0)
    n_tiles = pl.num_programs(0) - 1
    t = s - 1
    n_used = n_used_ref[0]
    xs = (x0, x1)

    def rows_of(tt):
        inside = jnp.logical_and(tt >= 0, tt < n_tiles)
        return jnp.where(inside, tile_rows_ref[jnp.clip(tt, 0, n_tiles - 1)], 0)

    def gather_issue(b, n):
        for r in range(MOE_TM):
            @pl.when(r < n)
            def _(r=r):
                pltpu.make_async_copy(hn_ref.at[src_ref[0, 0, r]], xs[b].at[r],
                                      gsem.at[b]).start(priority=ROW_DMA_PRIORITIES[r % 2])

    def gather_wait(b, n):
        p = MOE_TM
        while p >= 1:
            @pl.when(jnp.bitwise_and(n, p) != 0)
            def _(p=p):
                pltpu.make_async_copy(hn_ref.at[pl.ds(0, p)], xs[b].at[pl.ds(0, p)], gsem.at[b]).wait()
            p //= 2

    @pl.when(s == 0)
    def _():
        x0[...] = jnp.zeros_like(x0)
        x1[...] = jnp.zeros_like(x1)

    arrived = jnp.logical_or(s == 0, pref_e_ref[s] != pref_e_ref[jnp.maximum(s - 1, 0)])
    cur_slot = w_slot_ref[jnp.clip(t, 0, n_tiles - 1)]
    new_slot = jnp.where(s == 0, 0, 1 - cur_slot)

    @pl.when(arrived)
    def _():
        w1_s[new_slot] = w1_ref[0].astype(BF16)
        w3_s[new_slot] = w3_ref[0].astype(BF16)
        w2_s[new_slot] = w2_ref[0].astype(BF16)

    used = jnp.logical_and(t >= 0, t < n_used)
    for b in (0, 1):
        @pl.when(jnp.logical_and(s % 2 == b, used))
        def _(b=b):
            gather_wait(1 - b, rows_of(t))
            gather_issue(b, rows_of(s))
            x = xs[1 - b][...].astype(BF16)
            hid = (jax.nn.silu(jnp.dot(x, w1_s[cur_slot], preferred_element_type=F32))
                   * jnp.dot(x, w3_s[cur_slot], preferred_element_type=F32))
            o_ref[...] = jnp.dot(hid.astype(BF16), w2_s[cur_slot], preferred_element_type=F32)

        @pl.when(jnp.logical_and(s % 2 == b, jnp.logical_not(used)))
        def _(b=b):
            gather_issue(b, rows_of(s))

    @pl.when(t >= n_used)
    def _():
        o_ref[...] = jnp.zeros_like(o_ref)


def _moe_experts(hn, w1, w3, w2, pref_e, w_slot, tile_rows, n_used, src):
    d = hn.shape[1]
    n_tiles = tile_rows.shape[0]
    ff = w1.shape[-1]
    grid_spec = pltpu.PrefetchScalarGridSpec(
        num_scalar_prefetch=4,
        grid=(n_tiles + 1,),
        in_specs=[
            pl.BlockSpec((1, 1, MOE_TM), lambda s, pe, ws, tr, nu: (jnp.minimum(s, n_tiles - 1), 0, 0),
                         memory_space=pltpu.SMEM),
            pl.BlockSpec(memory_space=pl.ANY),
            pl.BlockSpec((1, d, ff), lambda s, pe, ws, tr, nu: (pe[s], 0, 0)),
            pl.BlockSpec((1, d, ff), lambda s, pe, ws, tr, nu: (pe[s], 0, 0)),
            pl.BlockSpec((1, ff, d), lambda s, pe, ws, tr, nu: (pe[s], 0, 0)),
        ],
        out_specs=pl.BlockSpec((MOE_TM, d), lambda s, pe, ws, tr, nu: (jnp.maximum(s - 1, 0), 0)),
        scratch_shapes=[pltpu.VMEM((2, d, ff), BF16), pltpu.VMEM((2, d, ff), BF16), pltpu.VMEM((2, ff, d), BF16),
                        pltpu.VMEM((MOE_TM, d), F32), pltpu.VMEM((MOE_TM, d), F32),
                        pltpu.SemaphoreType.DMA((2,))],
    )
    return pl.pallas_call(
        _moe_body,
        grid_spec=grid_spec,
        out_shape=jax.ShapeDtypeStruct((n_tiles * MOE_TM, d), F32),
        compiler_params=_params("arbitrary"),
        name="moe_experts",
    )(pref_e, w_slot, tile_rows, n_used, src.reshape(n_tiles, 1, MOE_TM), hn, w1, w3, w2)


COMBINE_TM = 256


def _combine_body(idx0_ref, idx_next_ref, h_ref, wk_ref, g_ref, ys_ref, o_ref, ya0, ya1, yb0, yb1, sem):
    i = pl.program_id(0)
    n = pl.num_programs(0)
    bufs = ((ya0, ya1), (yb0, yb1))
    tm = COMBINE_TM

    def gather_issue(idx_ref, b, n_rows):
        for r in range(tm):
            for k in range(2):
                @pl.when(r < n_rows)
                def _(r=r, k=k):
                    pltpu.make_async_copy(ys_ref.at[idx_ref[0, 0, 2 * r + k]], bufs[b][k].at[r],
                                          sem.at[b]).start(priority=ROW_DMA_PRIORITIES[k])

    def gather_wait(b):
        for k in range(2):
            pltpu.make_async_copy(ys_ref.at[pl.ds(0, tm)], bufs[b][k], sem.at[b]).wait()

    @pl.when(i == 0)
    def _():
        gather_issue(idx0_ref, 0, jnp.int32(tm))

    for b in (0, 1):
        @pl.when(i % 2 == b)
        def _(b=b):
            gather_wait(b)
            gather_issue(idx_next_ref, 1 - b, jnp.where(i + 1 < n, tm, 0))
            wk = wk_ref[...]
            y = h_ref[...] + wk[:, 0:1] * bufs[b][0][...] + wk[:, 1:2] * bufs[b][1][...]
            o_ref[...] = _rmsnorm_rows(y, g_ref[...]).astype(o_ref.dtype)


def _combine(h, ys, pos, wk, g, out_dtype):
    m, d = h.shape
    tm = COMBINE_TM
    nblk = m // tm
    idx_block = lambda fn: pl.BlockSpec((1, 1, 2 * tm), fn, memory_space=pltpu.SMEM)
    pos3 = pos.reshape(nblk, 1, 2 * tm)
    return pl.pallas_call(
        _combine_body,
        grid=(nblk,),
        in_specs=[idx_block(lambda i: (0, 0, 0)), idx_block(lambda i: (jnp.minimum(i + 1, nblk - 1), 0, 0)),
                  pl.BlockSpec((tm, d), lambda i: (i, 0)), pl.BlockSpec((tm, LANES), lambda i: (i, 0)),
                  pl.BlockSpec((1, d), lambda i: (0, 0)), pl.BlockSpec(memory_space=pl.ANY)],
        out_specs=pl.BlockSpec((tm, d), lambda i: (i, 0)),
        out_shape=jax.ShapeDtypeStruct((m, d), out_dtype),
        scratch_shapes=[pltpu.VMEM((tm, d), F32), pltpu.VMEM((tm, d), F32), pltpu.VMEM((tm, d), F32),
                        pltpu.VMEM((tm, d), F32), pltpu.SemaphoreType.DMA((2,))],
        compiler_params=_params("arbitrary"),
        name="combine_final_norm",
    )(pos3, pos3, h, wk, g.reshape(1, d).astype(F32), ys)


def _routing_tables(eid, n_tokens):
    n_pairs = 2 * n_tokens
    n_tiles = n_pairs // MOE_TM + N_EXPERTS + 1
    n_rows = n_tiles * MOE_TM
    experts = jnp.arange(N_EXPERTS, dtype=jnp.int32)[None, :]
    e_flat = eid.reshape(n_pairs).astype(jnp.int32)
    counts = jnp.sum(e_flat[:, None] == experts, axis=0, dtype=jnp.int32)
    tiles_per_e = (counts + MOE_TM - 1) // MOE_TM
    tile_end = jnp.cumsum(tiles_per_e)
    tile_start = tile_end - tiles_per_e
    n_used = tile_end[-1]
    filler_end = jnp.cumsum(tiles_per_e * MOE_TM - counts)
    filler = jnp.arange(n_rows - n_pairs, dtype=jnp.int32)[:, None]
    filler_e = jnp.sum(filler_end[None, :] <= filler, axis=1, dtype=jnp.int32)
    keys = jnp.concatenate([2 * e_flat, 2 * filler_e + 1])
    order = jnp.argsort(keys, stable=True).astype(jnp.int32)
    src_tok = jnp.where(order < n_pairs, order, 0) // 2
    pair_row = jnp.argsort(order)[:n_pairs].reshape(n_tokens, 2)
    j = jnp.arange(n_tiles, dtype=jnp.int32)
    tile_blk = jnp.minimum(j, n_used - 1)
    tile_e = jnp.sum(tile_end[None, :] <= tile_blk[:, None], axis=1, dtype=jnp.int32)
    own = tile_e[:, None] == experts
    tile_row0 = (tile_blk - jnp.sum(jnp.where(own, tile_start[None, :], 0), axis=1)) * MOE_TM
    own_count = jnp.sum(jnp.where(own, counts[None, :], 0), axis=1)
    tile_rows = jnp.where(j == tile_blk, jnp.clip(own_count - tile_row0, 0, MOE_TM), 0).astype(jnp.int32)
    next_first = jnp.minimum(jnp.sum(jnp.where(own, tile_end[None, :], 0), axis=1), n_used - 1)
    next_e = jnp.sum(jnp.where(next_first[:, None] == j[None, :], tile_e[None, :], 0), axis=1, dtype=jnp.int32)
    pref_e = jnp.concatenate([tile_e[:1], next_e])
    first_of_e = jnp.logical_or(j == 0, tile_e != jnp.concatenate([tile_e[:1], tile_e[:-1]]))
    w_slot = ((jnp.cumsum(first_of_e.astype(jnp.int32)) - 1) % 2).astype(jnp.int32)
    return (pref_e.astype(jnp.int32), w_slot, tile_rows, n_used.reshape(1).astype(jnp.int32),
            src_tok.astype(jnp.int32), pair_row.astype(jnp.int32))


def kernel(x, mix_norm_g, w_in, hgrn_lb_logits, hgrn_norm_g, w_branch_a, w_branch_b, w_out, ffn_norm_g,
           router_w_group, router_b_group, router_w_expert, router_b_expert, expert_w1, expert_w3, expert_w2,
           final_norm_g):
    b, seq, d = x.shape
    assert b == 1 and d == D_MODEL and seq % ATT_SUPER == 0 and w_in.shape == (1, D_MODEL, IN_WIDTH)
    h0 = x.reshape(seq, d).astype(F32)

    xn = _rmsnorm(h0, mix_norm_g[0], BF16)
    proj = _matmul(xn, w_in[0], F32, tm=1024, tn=1280, name="in_proj")
    oa = _dilated_attention(proj, seq)
    ob = _hgrn2(proj, hgrn_lb_logits, hgrn_norm_g[0], seq)
    merged = _branch_merge(oa, ob, proj, w_branch_a[0], w_branch_b[0])
    pad = LANES - N_GROUPS - N_EXPERTS
    rw = jnp.concatenate([router_w_group[0], router_w_expert[0], jnp.zeros((d, pad), F32)], axis=1).astype(F32)
    rb = jnp.concatenate([router_b_group[0], router_b_expert[0], jnp.zeros((pad,), F32)]).reshape(1, LANES).astype(F32)
    h1, hn, eid, wk = _out_proj_router(merged, w_out[0].astype(BF16), h0, ffn_norm_g[0], rw, rb)

    pref_e, w_slot, tile_rows, n_used, src_tok, pair_row = _routing_tables(eid[:, :2], seq)
    w1 = expert_w1[0].reshape(N_EXPERTS, D_MODEL, EXPERT_FF)
    w3 = expert_w3[0].reshape(N_EXPERTS, D_MODEL, EXPERT_FF)
    w2 = expert_w2[0].reshape(N_EXPERTS, EXPERT_FF, D_MODEL)
    ys = _moe_experts(hn, w1, w3, w2, pref_e, w_slot, tile_rows, n_used, src_tok)

    out = _combine(h1, ys, pair_row, wk, final_norm_g, x.dtype)
    return out.reshape(b, seq, d)
```

```python
import functools

import jax
import jax.numpy as jnp
from jax import lax
from jax.experimental import pallas as pl
from jax.experimental.pallas import tpu as pltpu

F32 = jnp.float32
BF16 = jnp.bfloat16

D_MODEL = 2048
A_HEADS = 12
A_HEAD_DIM = 128
A_WIDTH = A_HEADS * A_HEAD_DIM
A_SCALE = A_HEAD_DIM ** -0.5
DILATED_CONFIGS = ((128, 1), (512, 4), (2048, 16))
B_HEADS = 8
B_KEY_DIM = 128
B_WIDTH = B_HEADS * B_KEY_DIM
N_GROUPS = 4
EXPERTS_PER_GROUP = 8
N_EXPERTS = N_GROUPS * EXPERTS_PER_GROUP
EXPERT_FF = 512
NORM_EPS = 1e-6

LANES = 128
SUBLANES = 8
VMEM_LIMIT = 56 * 1024 * 1024

_QA_BLK = 0
_KA_BLK = A_HEADS
_VA_BLK = 2 * A_HEADS
_QB_BLK = 3 * A_HEADS
_FB_BLK = _QB_BLK + B_HEADS
_IB_BLK = _FB_BLK + B_HEADS
_GB_BLK = _IB_BLK + B_HEADS
_GATE_A_COL = 3 * A_WIDTH + 4 * B_WIDTH
_GATE_B_COL = _GATE_A_COL + D_MODEL
IN_WIDTH = _GATE_B_COL + D_MODEL


def _params(*sem):
    return pltpu.CompilerParams(dimension_semantics=sem, vmem_limit_bytes=VMEM_LIMIT)


def _rmsnorm_rows(x, g):
    ms = jnp.mean(x * x, axis=-1, keepdims=True)
    return x * lax.rsqrt(ms + NORM_EPS) * g


def _rmsnorm_body(x_ref, g_ref, o_ref):
    o_ref[...] = _rmsnorm_rows(x_ref[...].astype(F32), g_ref[...]).astype(o_ref.dtype)


def _rmsnorm(x, g, out_dtype, tm=512):
    m, d = x.shape
    return pl.pallas_call(
        _rmsnorm_body,
        grid=(m // tm,),
        in_specs=[pl.BlockSpec((tm, d), lambda i: (i, 0)), pl.BlockSpec((1, d), lambda i: (0, 0))],
        out_specs=pl.BlockSpec((tm, d), lambda i: (i, 0)),
        out_shape=jax.ShapeDtypeStruct((m, d), out_dtype),
        compiler_params=_params("arbitrary"),
        name="rmsnorm",
    )(x, g.reshape(1, d).astype(F32))


def _matmul_body(a_ref, w_ref, o_ref, wb_ref):
    @pl.when(pl.program_id(1) == 0)
    def _():
        wb_ref[...] = w_ref[...].astype(BF16)

    o_ref[...] = jnp.dot(a_ref[...], wb_ref[...], preferred_element_type=F32).astype(o_ref.dtype)


def _matmul(a, w, out_dtype, tm, tn, name):
    m, k = a.shape
    n = w.shape[1]
    return pl.pallas_call(
        _matmul_body,
        grid=(n // tn, m // tm),
        in_specs=[pl.BlockSpec((tm, k), lambda j, i: (i, 0)), pl.BlockSpec((k, tn), lambda j, i: (0, j))],
        out_specs=pl.BlockSpec((tm, tn), lambda j, i: (i, j)),
        out_shape=jax.ShapeDtypeStruct((m, n), out_dtype),
        scratch_shapes=[pltpu.VMEM((k, tn), BF16)],
        compiler_params=_params("arbitrary", "arbitrary"),
        name=name,
    )(a, w)


LOG2_E = 1.4426950408889634
LN_2 = 0.6931471805599453
ATT_BLK = 128
ATT_SUPER = 2048
ATT_DEINT = 4


def _attn_body(q_ref, k_ref, v_ref, o_ref, o_scr, lse_scr, bias_scr, k4, v4, q4, stage):
    sb = pl.program_id(1)
    seq = k_ref.shape[0]
    nd = ATT_DEINT
    sub = ATT_SUPER // nd
    diff = (lax.broadcasted_iota(jnp.int32, (ATT_BLK, 2 * ATT_BLK), 1)
            - lax.broadcasted_iota(jnp.int32, (ATT_BLK, 2 * ATT_BLK), 0))
    neg = jnp.float32(-jnp.inf)
    bias_scr[0] = jnp.where(jnp.logical_and(diff >= 0, diff <= ATT_BLK), 0.0, neg)
    bias_scr[1] = jnp.where(diff <= 0, 0.0, neg)

    @pl.when(sb == 0)
    def _():
        rows = 512

        def split(i, carry):
            for a in range(nd):
                src = pl.ds(i * (rows * nd) + a, rows, stride=nd)
                dst = pl.ds(pl.multiple_of(i * rows, rows), rows)
                k4[a, dst, :] = k_ref[src, :]
                v4[a, dst, :] = v_ref[src, :]
            return carry

        lax.fori_loop(0, seq // nd // rows, split, 0)

    for a in range(nd):
        q4[a] = q_ref[pl.ds(a, sub, stride=nd), :]

    def attend(q, k, v, bias):
        qb = (q * (A_SCALE * LOG2_E)).astype(BF16)
        s = lax.dot_general(qb, k.astype(BF16), (((1,), (1,)), ((), ())), preferred_element_type=F32)
        s = s + bias
        mx = jnp.max(s, axis=-1, keepdims=True)
        p = jnp.exp2(s - mx)
        den = jnp.sum(p, axis=-1, keepdims=True)
        acc = jnp.dot(p.astype(BF16), v.astype(BF16), preferred_element_type=F32)
        return acc / den, jnp.broadcast_to(mx * LN_2 + jnp.log(den), (ATT_BLK, A_HEAD_DIM))

    n_tiles = ATT_SUPER // ATT_BLK
    for c, (window, dil) in enumerate(DILATED_CONFIGS):
        assert window // dil == ATT_BLK and (dil == 1 or dil % nd == 0)
        tiles_per_res = n_tiles // dil
        for t in range(n_tiles):
            n, r = t % tiles_per_res, t // tiles_per_res
            if n == 0:
                first = (sb == 0).astype(jnp.int32)
                back = ATT_BLK * (1 - first)
                bias = bias_scr[first]
            else:
                back, bias = ATT_BLK, bias_scr[0]
            if dil == 1:
                q_rows = pl.ds(n * ATT_BLK, ATT_BLK)
                kv_rows = pl.ds(pl.multiple_of(sb * ATT_SUPER + n * ATT_BLK - back, ATT_BLK), 2 * ATT_BLK)
                o, lse = attend(q_ref[q_rows, :], k_ref[kv_rows, :], v_ref[kv_rows, :], bias)
                out_rows = q_rows
            else:
                st, a, r2 = dil // nd, r % nd, r // nd
                i0 = n * ATT_BLK
                q_rows = pl.ds(st * i0 + r2, ATT_BLK, stride=st) if st > 1 else pl.ds(i0, ATT_BLK)
                kv0 = st * (sb * (ATT_SUPER // dil) + i0 - back) + r2
                kv_rows = pl.ds(kv0, 2 * ATT_BLK, stride=st) if st > 1 else pl.ds(kv0, 2 * ATT_BLK)
                o, lse = attend(q4[a, q_rows, :], k4[a, kv_rows, :], v4[a, kv_rows, :], bias)
                out_rows = (pl.ds(a * sub + st * i0 + r2, ATT_BLK, stride=st) if st > 1
                            else pl.ds(a * sub + i0, ATT_BLK))
            o_scr[c, out_rows, :] = o
            lse_scr[c, out_rows, :] = lse

    rows = 128

    def merge(i, carry):
        for a in range(nd):
            tok = pl.ds(i * (rows * nd) + a, rows, stride=nd)
            cls = pl.ds(a * sub + pl.multiple_of(i * rows, rows), rows)
            l0, l1, l2 = lse_scr[0, tok, :], lse_scr[1, cls, :], lse_scr[2, cls, :]
            m = jnp.maximum(jnp.maximum(l0, l1), l2)
            w0, w1, w2 = jnp.exp(l0 - m), jnp.exp(l1 - m), jnp.exp(l2 - m)
            num = w0 * o_scr[0, tok, :] + w1 * o_scr[1, cls, :] + w2 * o_scr[2, cls, :]
            stage[tok, :] = num / (w0 + w1 + w2)
        return carry

    lax.fori_loop(0, sub // rows, merge, 0)
    o_ref[...] = stage[...].astype(o_ref.dtype)


def _dilated_attention(proj, seq):
    n_super = seq // ATT_SUPER
    blk = lambda off: pl.BlockSpec((seq, A_HEAD_DIM), lambda h, s: (0, off + h))
    return pl.pallas_call(
        _attn_body,
        grid=(A_HEADS, n_super),
        in_specs=[pl.BlockSpec((ATT_SUPER, A_HEAD_DIM), lambda h, s: (s, _QA_BLK + h)), blk(_KA_BLK), blk(_VA_BLK)],
        out_specs=pl.BlockSpec((ATT_SUPER, A_HEAD_DIM), lambda h, s: (s, h)),
        out_shape=jax.ShapeDtypeStruct((seq, A_WIDTH), BF16),
        scratch_shapes=[pltpu.VMEM((len(DILATED_CONFIGS), ATT_SUPER, A_HEAD_DIM), F32),
                        pltpu.VMEM((len(DILATED_CONFIGS), ATT_SUPER, A_HEAD_DIM), F32),
                        pltpu.VMEM((2, ATT_BLK, 2 * ATT_BLK), F32),
                        pltpu.VMEM((ATT_DEINT, seq // ATT_DEINT, A_HEAD_DIM), F32),
                        pltpu.VMEM((ATT_DEINT, seq // ATT_DEINT, A_HEAD_DIM), F32),
                        pltpu.VMEM((ATT_DEINT, ATT_SUPER // ATT_DEINT, A_HEAD_DIM), F32),
                        pltpu.VMEM((ATT_SUPER, A_HEAD_DIM), F32)],
        compiler_params=_params("arbitrary", "arbitrary"),
        name="dilated_attention",
    )(proj, proj, proj)


HG_CHUNK = 64
HG_TB = 1024
HG_HEADS_PER_STEP = 2
HG_CHUNKS_PER_STACK = 2
HG_STACKS_PER_ITER = 2


def _hgrn_ref_rows(bc, half):
    n, width = bc.shape
    blk = 2 * half
    if blk >= SUBLANES:
        rows = [jnp.broadcast_to(bc[b0 + half - 1:b0 + half, :], (blk, width)) for b0 in range(0, n, blk)]
        return jnp.concatenate(rows, axis=0) if len(rows) > 1 else rows[0]
    sub = lax.broadcasted_iota(jnp.int32, (SUBLANES, width), 0)
    groups = []
    for g0 in range(0, n, SUBLANES):
        grp = bc[g0:g0 + SUBLANES, :]
        if half == 1:
            groups.append(jnp.where(jnp.bitwise_and(sub, 1) == 1, pltpu.roll(grp, 1, 0), grp))
        else:
            assert half == 2 and SUBLANES == 8
            groups.append(jnp.where(sub < 4, jnp.broadcast_to(grp[1:2, :], grp.shape),
                                    jnp.broadcast_to(grp[5:6, :], grp.shape)))
    return jnp.concatenate(groups, axis=0)


def _hgrn_body(qb_ref, fb_ref, ib_ref, gb_ref, lbl_ref, g_ref, o_ref, st_ref):
    @pl.when(pl.program_id(1) == 0)
    def _():
        st_ref[...] = jnp.zeros_like(st_ref)

    c_ = HG_CHUNK
    lbl = lbl_ref[...]
    e = jnp.exp(lbl - jnp.max(lbl, axis=0, keepdims=True))
    lb_all = e[0:1, :] / jnp.sum(e, axis=0, keepdims=True)
    g_all = g_ref[...]

    nh, nc = HG_HEADS_PER_STEP, HG_CHUNKS_PER_STACK
    rows = nc * c_
    n = nh * rows
    ti = lax.broadcasted_iota(jnp.int32, (n, n), 0)
    si = lax.broadcasted_iota(jnp.int32, (n, n), 1)
    xor = jnp.bitwise_xor(ti, si)
    causal = jnp.logical_and(xor < c_, si <= ti)
    tri = jnp.where(causal, 1.0, 0.0).astype(BF16)

    def stack_of(ref, r0):
        return jnp.concatenate([ref[pl.ds(r0, rows), h * B_KEY_DIM:(h + 1) * B_KEY_DIM] for h in range(nh)], axis=0)

    def per_head_rows(x):
        return jnp.concatenate([jnp.broadcast_to(x[:, h * B_KEY_DIM:(h + 1) * B_KEY_DIM], (rows, B_KEY_DIM))
                                for h in range(nh)], axis=0)

    lb = per_head_rows(lb_all)
    gnorm = per_head_rows(g_all)

    def stack(i):
        r0 = pl.multiple_of(i * rows, rows)
        f = lb + (1.0 - lb) * jax.nn.sigmoid(stack_of(fb_ref, r0))
        logf = jnp.log(f)
        kk = 1.0 - f
        q = jax.nn.silu(stack_of(qb_ref, r0))
        v = stack_of(ib_ref, r0)
        vb = v.astype(BF16)
        hi = logf.astype(BF16)
        rem = logf - hi.astype(F32)
        mid = rem.astype(BF16)
        low = (rem - mid.astype(F32)).astype(BF16)
        parts = jnp.dot(tri, jnp.concatenate([hi, mid, low], axis=1), preferred_element_type=F32)
        bc = (parts[:, :B_KEY_DIM] + (parts[:, B_KEY_DIM:2 * B_KEY_DIM] + parts[:, 2 * B_KEY_DIM:])) * LOG2_E

        attn = lax.dot_general(q.astype(BF16), kk.astype(BF16), (((1,), (1,)), ((), ())),
                               preferred_element_type=F32)
        half = 1
        while half < c_:
            dec = jnp.exp2(-jnp.abs(bc - _hgrn_ref_rows(bc, half)))
            s = lax.dot_general((q * dec).astype(BF16), (kk * dec).astype(BF16), (((1,), (1,)), ((), ())),
                                preferred_element_type=F32)
            attn = jnp.where(xor >= half, s, attn)
            half *= 2
        attn = jnp.where(causal, attn, 0.0)
        o = jnp.dot(attn.astype(BF16), vb, preferred_element_type=F32)

        last = jnp.concatenate([jnp.broadcast_to(bc[a + c_ - 1:a + c_, :], (c_, B_KEY_DIM))
                                for a in range(0, n, c_)], axis=0)
        q_dec = (q * jnp.exp2(bc)).astype(BF16)
        k_dec = (kk * jnp.exp2(last - bc)).astype(BF16)
        st_dec = jnp.exp2(last)
        inter = []
        for h in range(nh):
            st = st_ref[h]
            for c in range(nc):
                a = h * rows + c * c_
                inter.append(lax.dot_general(q_dec[a:a + c_], st.astype(BF16), (((1,), (1,)), ((), ())),
                                             preferred_element_type=F32))
                upd = lax.dot_general(vb[a:a + c_], k_dec[a:a + c_], (((0,), (0,)), ((), ())),
                                      preferred_element_type=F32)
                st = st * st_dec[a:a + 1] + upd
            st_ref[h] = st
        o = o + jnp.concatenate(inter, axis=0)

        o = o * lax.rsqrt(jnp.mean(o * o, axis=-1, keepdims=True) + NORM_EPS)
        res = (o * gnorm * jax.nn.silu(stack_of(gb_ref, r0))).astype(o_ref.dtype)
        for h in range(nh):
            o_ref[pl.ds(r0, rows), h * B_KEY_DIM:(h + 1) * B_KEY_DIM] = res[h * rows:(h + 1) * rows]

    def stacks(i, carry):
        for u in range(HG_STACKS_PER_ITER):
            stack(i * HG_STACKS_PER_ITER + u)
        return carry

    lax.fori_loop(0, HG_TB // rows // HG_STACKS_PER_ITER, stacks, 0)


def _hgrn2(proj, lb_logits, norm_g, seq):
    hp = HG_HEADS_PER_STEP
    width = hp * B_KEY_DIM
    col = lambda off: pl.BlockSpec((HG_TB, width), lambda h, t: (t, off // hp + h))
    n_lb = lb_logits.shape[0]
    return pl.pallas_call(
        _hgrn_body,
        grid=(B_HEADS // hp, seq // HG_TB),
        in_specs=[col(_QB_BLK), col(_FB_BLK), col(_IB_BLK), col(_GB_BLK),
                  pl.BlockSpec((n_lb, width), lambda h, t: (0, h)),
                  pl.BlockSpec((1, width), lambda h, t: (0, h))],
        out_specs=pl.BlockSpec((HG_TB, width), lambda h, t: (t, h)),
        out_shape=jax.ShapeDtypeStruct((seq, B_WIDTH), BF16),
        scratch_shapes=[pltpu.VMEM((hp, B_KEY_DIM, B_KEY_DIM), F32)],
        compiler_params=_params("arbitrary", "arbitrary"),
        name="hgrn2",
    )(proj, proj, proj, proj, lb_logits.astype(F32), norm_g.reshape(1, B_WIDTH).astype(F32))


MERGE_GATE_BLK = 512


def _merge_body(oa_ref, ob_ref, *rest):
    n_g = (len(rest) - 5) // 2
    ga_refs, gb_refs = rest[:n_g], rest[n_g:2 * n_g]
    wa_ref, wb_ref, o_ref, wa_s, wb_s = rest[2 * n_g:]

    @pl.when(pl.program_id(1) == 0)
    def _():
        wa_s[...] = wa_ref[...].astype(BF16)
        wb_s[...] = wb_ref[...].astype(BF16)

    ya = jnp.dot(oa_ref[...], wa_s[...], preferred_element_type=F32)
    yb = jnp.dot(ob_ref[...], wb_s[...], preferred_element_type=F32)
    ga = jnp.concatenate([r[...] for r in ga_refs], axis=1)
    gb = jnp.concatenate([r[...] for r in gb_refs], axis=1)
    o_ref[...] = (jax.nn.sigmoid(ga) * ya + jax.nn.sigmoid(gb) * yb).astype(o_ref.dtype)


def _branch_merge(oa, ob, proj, wa, wb, tm=512, tn=1024):
    m = oa.shape[0]
    gw = MERGE_GATE_BLK
    n_g = tn // gw
    ga0, gb0 = _GATE_A_COL // gw, _GATE_B_COL // gw
    gate = lambda off, u: pl.BlockSpec((tm, gw), lambda j, i: (i, off + j * n_g + u))
    return pl.pallas_call(
        _merge_body,
        grid=(D_MODEL // tn, m // tm),
        in_specs=[pl.BlockSpec((tm, A_WIDTH), lambda j, i: (i, 0)), pl.BlockSpec((tm, B_WIDTH), lambda j, i: (i, 0)),
                  *[gate(ga0, u) for u in range(n_g)], *[gate(gb0, u) for u in range(n_g)],
                  pl.BlockSpec((A_WIDTH, tn), lambda j, i: (0, j)), pl.BlockSpec((B_WIDTH, tn), lambda j, i: (0, j))],
        out_specs=pl.BlockSpec((tm, tn), lambda j, i: (i, j)),
        out_shape=jax.ShapeDtypeStruct((m, D_MODEL), BF16),
        scratch_shapes=[pltpu.VMEM((A_WIDTH, tn), BF16), pltpu.VMEM((B_WIDTH, tn), BF16)],
        compiler_params=_params("arbitrary", "arbitrary"),
        name="branch_merge",
    )(oa, ob, *([proj] * (2 * n_g)), wa, wb)


def _route_rows(h, g, rw, rb):
    hn = _rmsnorm_rows(h, g)
    hn_hi = hn.astype(BF16)
    hn_lo = (hn - hn_hi.astype(F32)).astype(BF16)
    rw_hi = rw.astype(BF16)
    rw_lo = (rw - rw_hi.astype(F32)).astype(BF16)
    logits = (jnp.dot(hn_hi, rw_hi, preferred_element_type=F32)
              + (jnp.dot(hn_lo, rw_hi, preferred_element_type=F32)
                 + jnp.dot(hn_hi, rw_lo, preferred_element_type=F32))) + rb
    lane = lax.broadcasted_iota(jnp.int32, logits.shape, 1)
    neg = jnp.float32(-jnp.inf)
    big = jnp.int32(LANES)

    is_g = lane < N_GROUPS
    lg = jnp.where(is_g, logits, neg)
    mg = jnp.max(lg, axis=-1, keepdims=True)
    g_idx = jnp.min(jnp.where(lg == mg, lane, big), axis=-1, keepdims=True)
    pg_top = 1.0 / jnp.sum(jnp.where(is_g, jnp.exp(lg - mg), 0.0), axis=-1, keepdims=True)

    lo = N_GROUPS + g_idx * EXPERTS_PER_GROUP
    in_grp = jnp.logical_and(lane >= lo, lane < lo + EXPERTS_PER_GROUP)
    le = jnp.where(in_grp, logits, neg)
    v1 = jnp.max(le, axis=-1, keepdims=True)
    i1 = jnp.min(jnp.where(le == v1, lane, big), axis=-1, keepdims=True)
    le2 = jnp.where(lane == i1, neg, le)
    v2 = jnp.max(le2, axis=-1, keepdims=True)
    i2 = jnp.min(jnp.where(le2 == v2, lane, big), axis=-1, keepdims=True)
    e2 = jnp.exp(v2 - v1)
    w1 = pg_top / (1.0 + e2)
    w2 = pg_top * e2 / (1.0 + e2)
    eid = jnp.where(lane == 0, i1 - N_GROUPS, jnp.where(lane == 1, i2 - N_GROUPS, 0))
    wk = jnp.where(lane == 0, w1, jnp.where(lane == 1, w2, 0.0))
    return hn, eid, wk


def _out_router_body(a_ref, w_ref, x_ref, g_ref, rw_ref, rb_ref, h1_ref, hn_ref, eid_ref, wk_ref, h_prev):
    @pl.when(pl.program_id(0) == 0)
    def _():
        h_prev[...] = jnp.zeros_like(h_prev)

    hn_ref[...], eid_ref[...], wk_ref[...] = _route_rows(h_prev[...], g_ref[...], rw_ref[...], rb_ref[...])
    h1 = x_ref[...] + jnp.dot(a_ref[...], w_ref[...], preferred_element_type=F32)
    h1_ref[...] = h1
    h_prev[...] = h1


def _out_proj_router(a, w_bf16, x, g, rw, rb, tm=512):
    m, d = x.shape
    n_blk = m // tm
    cur = lambda width: pl.BlockSpec((tm, width), lambda i: (jnp.minimum(i, n_blk - 1), 0))
    prev = lambda width: pl.BlockSpec((tm, width), lambda i: (jnp.maximum(i - 1, 0), 0))
    const = lambda shape, **kw: pl.BlockSpec(shape, lambda i: (0, 0), **kw)
    return pl.pallas_call(
        _out_router_body,
        grid=(n_blk + 1,),
        in_specs=[cur(a.shape[1]), const(w_bf16.shape, pipeline_mode=pl.Buffered(1)), cur(d),
                  const((1, d)), const((d, LANES)), const((1, LANES))],
        out_specs=[cur(d), prev(d), prev(LANES), prev(LANES)],
        out_shape=[jax.ShapeDtypeStruct((m, d), F32), jax.ShapeDtypeStruct((m, d), F32),
                   jax.ShapeDtypeStruct((m, LANES), jnp.int32), jax.ShapeDtypeStruct((m, LANES), F32)],
        scratch_shapes=[pltpu.VMEM((tm, d), F32)],
        compiler_params=_params("arbitrary"),
        name="out_proj_router",
    )(a, w_bf16, x, g.reshape(1, d).astype(F32), rw, rb)


MOE_TM = 256
ROW_DMA_PRIORITIES = (0, 1)
MOE_GATHER_GROUP = 8


def _moe_body(tile_e_ref, tile_blk_ref, tile_rows_ref, n_used_ref,
              src0_ref, src_next_ref, dst_prev_ref, hn_ref, w1_ref, w3_ref, w2_ref, y_ref,
              w1_s, w3_s, w2_s, x0, x1, o0, o1, gsem, ssem):
    j = pl.program_id(0)
    n_tiles = pl.num_programs(0)
    n_used = n_used_ref[0]
    xs = (x0, x1)
    os_ = (o0, o1)

    def rows_of(t):
        return jnp.where(t >= 0, tile_rows_ref[jnp.clip(t, 0, n_tiles - 1)], 0)

    def gather_rows_of(t):
        return (rows_of(t) + MOE_GATHER_GROUP - 1) // MOE_GATHER_GROUP * MOE_GATHER_GROUP

    def gather_issue(idx_ref, b, n):
        for r in range(MOE_TM):
            @pl.when(r - r % MOE_GATHER_GROUP < n)
            def _(r=r):
                pltpu.make_async_copy(hn_ref.at[idx_ref[0, 0, r]], xs[b].at[r],
                                      gsem.at[b]).start(priority=ROW_DMA_PRIORITIES[r % 2])

    def scatter_issue(idx_ref, b, n):
        for r in range(MOE_TM):
            @pl.when(r < n)
            def _(r=r):
                pltpu.make_async_copy(os_[b].at[r], y_ref.at[idx_ref[0, 0, r]],
                                      ssem.at[b]).start(priority=ROW_DMA_PRIORITIES[r % 2])

    def wait_rows(make_copy, n):
        p = MOE_TM
        while p >= 1:
            @pl.when(jnp.bitwise_and(n, p) != 0)
            def _(p=p):
                make_copy(p).wait()
            p //= 2

    def gather_wait(b, n):
        wait_rows(lambda p: pltpu.make_async_copy(hn_ref.at[pl.ds(0, p)], xs[b].at[pl.ds(0, p)], gsem.at[b]), n)

    def scatter_wait(b, n):
        wait_rows(lambda p: pltpu.make_async_copy(os_[b].at[pl.ds(0, p)], y_ref.at[pl.ds(0, p)], ssem.at[b]), n)

    @pl.when(j == 0)
    def _():
        x0[...] = jnp.zeros_like(x0)
        x1[...] = jnp.zeros_like(x1)
        gather_issue(src0_ref, 0, gather_rows_of(0))

    for b in (0, 1):
        @pl.when(jnp.logical_and(j < n_used, j % 2 == b))
        def _(b=b):
            gather_wait(b, gather_rows_of(j))
            scatter_wait(b, rows_of(j - 2))

            prev_e = tile_e_ref[jnp.maximum(j - 1, 0)]

            @pl.when(jnp.logical_or(j == 0, tile_e_ref[j] != prev_e))
            def _():
                w1_s[...] = w1_ref[0].astype(BF16)
                w3_s[...] = w3_ref[0].astype(BF16)
                w2_s[...] = w2_ref[0].astype(BF16)

            gather_issue(src_next_ref, 1 - b, gather_rows_of(j + 1))
            scatter_issue(dst_prev_ref, 1 - b, rows_of(j - 1))
            x = xs[b][...].astype(BF16)
            hid = (jax.nn.silu(jnp.dot(x, w1_s[...], preferred_element_type=F32))
                   * jnp.dot(x, w3_s[...], preferred_element_type=F32))
            os_[b][...] = jnp.dot(hid.astype(BF16), w2_s[...], preferred_element_type=F32)

    for b in (0, 1):
        @pl.when(jnp.logical_and(j == n_used, j % 2 == b))
        def _(b=b):
            scatter_wait(b, rows_of(j - 2))
            scatter_issue(dst_prev_ref, 1 - b, rows_of(j - 1))
            scatter_wait(1 - b, rows_of(j - 1))


def _moe_experts(hn, w1, w3, w2, tile_e, tile_blk, tile_rows, n_used, src, dst, n_tokens):
    d = hn.shape[1]
    n_tiles = tile_e.shape[0]
    ff = w1.shape[-1]
    idx_block = lambda fn: pl.BlockSpec((1, 1, MOE_TM), fn, memory_space=pltpu.SMEM)
    grid_spec = pltpu.PrefetchScalarGridSpec(
        num_scalar_prefetch=4,
        grid=(n_tiles,),
        in_specs=[
            idx_block(lambda j, te, tb, tr, nu: (0, 0, 0)),
            idx_block(lambda j, te, tb, tr, nu: (tb[jnp.minimum(j + 1, n_tiles - 1)], 0, 0)),
            idx_block(lambda j, te, tb, tr, nu: (tb[jnp.maximum(j - 1, 0)], 0, 0)),
            pl.BlockSpec(memory_space=pl.ANY),
            pl.BlockSpec((1, d, ff), lambda j, te, tb, tr, nu: (te[j], 0, 0)),
            pl.BlockSpec((1, d, ff), lambda j, te, tb, tr, nu: (te[j], 0, 0)),
            pl.BlockSpec((1, ff, d), lambda j, te, tb, tr, nu: (te[j], 0, 0)),
        ],
        out_specs=pl.BlockSpec(memory_space=pl.ANY),
        scratch_shapes=[pltpu.VMEM((d, ff), BF16), pltpu.VMEM((d, ff), BF16), pltpu.VMEM((ff, d), BF16),
                        pltpu.VMEM((MOE_TM, d), F32), pltpu.VMEM((MOE_TM, d), F32),
                        pltpu.VMEM((MOE_TM, d), F32), pltpu.VMEM((MOE_TM, d), F32),
                        pltpu.SemaphoreType.DMA((2,)), pltpu.SemaphoreType.DMA((2,))],
    )
    src3 = src.reshape(n_tiles, 1, MOE_TM)
    return pl.pallas_call(
        _moe_body,
        grid_spec=grid_spec,
        out_shape=jax.ShapeDtypeStruct((2 * n_tokens, d), F32),
        compiler_params=_params("arbitrary"),
        name="moe_experts",
    )(tile_e, tile_blk, tile_rows, n_used, src3, src3, dst.reshape(n_tiles, 1, MOE_TM), hn, w1, w3, w2)


def _combine_body(h_ref, y0_ref, y1_ref, wk_ref, g_ref, o_ref):
    wk = wk_ref[...]
    y = h_ref[...] + wk[:, 0:1] * y0_ref[...] + wk[:, 1:2] * y1_ref[...]
    o_ref[...] = _rmsnorm_rows(y, g_ref[...]).astype(o_ref.dtype)


def _combine(h, y2, wk, g, out_dtype, tm=256):
    m, d = h.shape
    nblk = m // tm
    return pl.pallas_call(
        _combine_body,
        grid=(nblk,),
        in_specs=[pl.BlockSpec((tm, d), lambda i: (i, 0)), pl.BlockSpec((tm, d), lambda i: (i, 0)),
                  pl.BlockSpec((tm, d), lambda i: (nblk + i, 0)), pl.BlockSpec((tm, LANES), lambda i: (i, 0)),
                  pl.BlockSpec((1, d), lambda i: (0, 0))],
        out_specs=pl.BlockSpec((tm, d), lambda i: (i, 0)),
        out_shape=jax.ShapeDtypeStruct((m, d), out_dtype),
        compiler_params=_params("arbitrary"),
        name="combine_final_norm",
    )(h, y2, y2, wk, g.reshape(1, d).astype(F32))


def _routing_tables(eid, n_tokens):
    n_pairs = 2 * n_tokens
    n_tiles = n_pairs // MOE_TM + N_EXPERTS + 1
    n_rows = n_tiles * MOE_TM
    experts = jnp.arange(N_EXPERTS, dtype=jnp.int32)[None, :]
    e_flat = eid.reshape(n_pairs).astype(jnp.int32)
    counts = jnp.sum(e_flat[:, None] == experts, axis=0, dtype=jnp.int32)
    tiles_per_e = (counts + MOE_TM - 1) // MOE_TM
    tile_end = jnp.cumsum(tiles_per_e)
    tile_start = tile_end - tiles_per_e
    n_used = tile_end[-1]
    filler_end = jnp.cumsum(tiles_per_e * MOE_TM - counts)
    filler = jnp.arange(n_rows - n_pairs, dtype=jnp.int32)[:, None]
    filler_e = jnp.sum(filler_end[None, :] <= filler, axis=1, dtype=jnp.int32)
    keys = jnp.concatenate([2 * e_flat, 2 * filler_e + 1])
    order = jnp.argsort(keys, stable=True).astype(jnp.int32)
    valid = order < n_pairs
    pair = jnp.where(valid, order, 0)
    src_tok = pair // 2
    dst_row = (pair % 2) * n_tokens + pair // 2
    j = jnp.arange(n_tiles, dtype=jnp.int32)
    tile_blk = jnp.minimum(j, n_used - 1)
    tile_e = jnp.sum(tile_end[None, :] <= tile_blk[:, None], axis=1, dtype=jnp.int32)
    own = tile_e[:, None] == experts
    tile_row0 = (tile_blk - jnp.sum(jnp.where(own, tile_start[None, :], 0), axis=1)) * MOE_TM
    own_count = jnp.sum(jnp.where(own, counts[None, :], 0), axis=1)
    tile_rows = jnp.where(j == tile_blk, jnp.clip(own_count - tile_row0, 0, MOE_TM), 0).astype(jnp.int32)
    return (tile_e, tile_blk.astype(jnp.int32), tile_rows, n_used.reshape(1).astype(jnp.int32),
            src_tok.astype(jnp.int32), dst_row.astype(jnp.int32))


def kernel(x, mix_norm_g, w_in, hgrn_lb_logits, hgrn_norm_g, w_branch_a, w_branch_b, w_out, ffn_norm_g,
           router_w_group, router_b_group, router_w_expert, router_b_expert, expert_w1, expert_w3, expert_w2,
           final_norm_g):
    b, seq, d = x.shape
    assert b == 1 and d == D_MODEL and seq % ATT_SUPER == 0 and w_in.shape == (1, D_MODEL, IN_WIDTH)
    h0 = x.reshape(seq, d).astype(F32)

    xn = _rmsnorm(h0, mix_norm_g[0], BF16)
    proj = _matmul(xn, w_in[0], F32, tm=1024, tn=1280, name="in_proj")
    oa = _dilated_attention(proj, seq)
    ob = _hgrn2(proj, hgrn_lb_logits, hgrn_norm_g[0], seq)
    merged = _branch_merge(oa, ob, proj, w_branch_a[0], w_branch_b[0])
    pad = LANES - N_GROUPS - N_EXPERTS
    rw = jnp.concatenate([router_w_group[0], router_w_expert[0], jnp.zeros((d, pad), F32)], axis=1).astype(F32)
    rb = jnp.concatenate([router_b_group[0], router_b_expert[0], jnp.zeros((pad,), F32)]).reshape(1, LANES).astype(F32)
    h1, hn, eid, wk = _out_proj_router(merged, w_out[0].astype(BF16), h0, ffn_norm_g[0], rw, rb)

    tile_e, tile_blk, tile_rows, n_used, src_tok, dst_row = _routing_tables(eid[:, :2], seq)
    w1 = expert_w1[0].reshape(N_EXPERTS, D_MODEL, EXPERT_FF)
    w3 = expert_w3[0].reshape(N_EXPERTS, D_MODEL, EXPERT_FF)
    w2 = expert_w2[0].reshape(N_EXPERTS, EXPERT_FF, D_MODEL)
    y2 = _moe_experts(hn, w1, w3, w2, tile_e, tile_blk, tile_rows, n_used, src_tok, dst_row, seq)

    out = _combine(h1, y2, wk, final_norm_g, x.dtype)
    return out.reshape(b, seq, d)
```

```python
import functools

import jax
import jax.numpy as jnp
from jax import lax
from jax.experimental import pallas as pl
from jax.experimental.pallas import tpu as pltpu

F32 = jnp.float32
BF16 = jnp.bfloat16

D_MODEL = 2048
A_HEADS = 12
A_HEAD_DIM = 128
A_WIDTH = A_HEADS * A_HEAD_DIM
A_SCALE = A_HEAD_DIM ** -0.5
DILATED_CONFIGS = ((128, 1), (512, 4), (2048, 16))
B_HEADS = 8
B_KEY_DIM = 128
B_WIDTH = B_HEADS * B_KEY_DIM
N_GROUPS = 4
EXPERTS_PER_GROUP = 8
N_EXPERTS = N_GROUPS * EXPERTS_PER_GROUP
EXPERT_FF = 512
NORM_EPS = 1e-6

LANES = 128
SUBLANES = 8
VMEM_LIMIT = 56 * 1024 * 1024

_QA_BLK = 0
_KA_BLK = A_HEADS
_VA_BLK = 2 * A_HEADS
_QB_BLK = 3 * A_HEADS
_FB_BLK = _QB_BLK + B_HEADS
_IB_BLK = _FB_BLK + B_HEADS
_GB_BLK = _IB_BLK + B_HEADS
_GATE_A_COL = 3 * A_WIDTH + 4 * B_WIDTH
_GATE_B_COL = _GATE_A_COL + D_MODEL
IN_WIDTH = _GATE_B_COL + D_MODEL


def _params(*sem):
    return pltpu.CompilerParams(dimension_semantics=sem, vmem_limit_bytes=VMEM_LIMIT)


def _rmsnorm_rows(x, g):
    ms = jnp.mean(x * x, axis=-1, keepdims=True)
    return x * lax.rsqrt(ms + NORM_EPS) * g


def _rmsnorm_body(x_ref, g_ref, o_ref):
    o_ref[...] = _rmsnorm_rows(x_ref[...].astype(F32), g_ref[...]).astype(o_ref.dtype)


def _rmsnorm(x, g, out_dtype, tm=512):
    m, d = x.shape
    return pl.pallas_call(
        _rmsnorm_body,
        grid=(m // tm,),
        in_specs=[pl.BlockSpec((tm, d), lambda i: (i, 0)), pl.BlockSpec((1, d), lambda i: (0, 0))],
        out_specs=pl.BlockSpec((tm, d), lambda i: (i, 0)),
        out_shape=jax.ShapeDtypeStruct((m, d), out_dtype),
        compiler_params=_params("arbitrary"),
        name="rmsnorm",
    )(x, g.reshape(1, d).astype(F32))


def _matmul_body(a_ref, w_ref, o_ref, wb_ref):
    @pl.when(pl.program_id(1) == 0)
    def _():
        wb_ref[...] = w_ref[...].astype(BF16)

    o_ref[...] = jnp.dot(a_ref[...], wb_ref[...], preferred_element_type=F32).astype(o_ref.dtype)


def _matmul(a, w, out_dtype, tm, tn, name):
    m, k = a.shape
    n = w.shape[1]
    return pl.pallas_call(
        _matmul_body,
        grid=(n // tn, m // tm),
        in_specs=[pl.BlockSpec((tm, k), lambda j, i: (i, 0)), pl.BlockSpec((k, tn), lambda j, i: (0, j))],
        out_specs=pl.BlockSpec((tm, tn), lambda j, i: (i, j)),
        out_shape=jax.ShapeDtypeStruct((m, n), out_dtype),
        scratch_shapes=[pltpu.VMEM((k, tn), BF16)],
        compiler_params=_params("arbitrary", "arbitrary"),
        name=name,
    )(a, w)


LOG2_E = 1.4426950408889634
LN_2 = 0.6931471805599453
ATT_BLK = 128
ATT_SUPER = 2048
ATT_DEINT = 4


def _attn_body(q_ref, k_ref, v_ref, o_ref, o_scr, lse_scr, bias_scr, k4, v4, q4, stage):
    sb = pl.program_id(1)
    seq = k_ref.shape[0]
    nd = ATT_DEINT
    sub = ATT_SUPER // nd
    diff = (lax.broadcasted_iota(jnp.int32, (ATT_BLK, 2 * ATT_BLK), 1)
            - lax.broadcasted_iota(jnp.int32, (ATT_BLK, 2 * ATT_BLK), 0))
    neg = jnp.float32(-jnp.inf)
    bias_scr[0] = jnp.where(jnp.logical_and(diff >= 0, diff <= ATT_BLK), 0.0, neg)
    bias_scr[1] = jnp.where(diff <= 0, 0.0, neg)

    @pl.when(sb == 0)
    def _():
        rows = 512

        def split(i, carry):
            for a in range(nd):
                src = pl.ds(i * (rows * nd) + a, rows, stride=nd)
                dst = pl.ds(pl.multiple_of(i * rows, rows), rows)
                k4[a, dst, :] = k_ref[src, :]
                v4[a, dst, :] = v_ref[src, :]
            return carry

        lax.fori_loop(0, seq // nd // rows, split, 0)

    for a in range(nd):
        q4[a] = q_ref[pl.ds(a, sub, stride=nd), :]

    def attend(q, k, v, bias):
        qb = (q * (A_SCALE * LOG2_E)).astype(BF16)
        s = lax.dot_general(qb, k.astype(BF16), (((1,), (1,)), ((), ())), preferred_element_type=F32)
        s = s + bias
        mx = jnp.max(s, axis=-1, keepdims=True)
        p = jnp.exp2(s - mx)
        den = jnp.sum(p, axis=-1, keepdims=True)
        acc = jnp.dot(p.astype(BF16), v.astype(BF16), preferred_element_type=F32)
        return acc / den, jnp.broadcast_to(mx * LN_2 + jnp.log(den), (ATT_BLK, A_HEAD_DIM))

    n_tiles = ATT_SUPER // ATT_BLK
    for c, (window, dil) in enumerate(DILATED_CONFIGS):
        assert window // dil == ATT_BLK and (dil == 1 or dil % nd == 0)
        tiles_per_res = n_tiles // dil
        for t in range(n_tiles):
            n, r = t % tiles_per_res, t // tiles_per_res
            if n == 0:
                first = (sb == 0).astype(jnp.int32)
                back = ATT_BLK * (1 - first)
                bias = bias_scr[first]
            else:
                back, bias = ATT_BLK, bias_scr[0]
            if dil == 1:
                q_rows = pl.ds(n * ATT_BLK, ATT_BLK)
                kv_rows = pl.ds(pl.multiple_of(sb * ATT_SUPER + n * ATT_BLK - back, ATT_BLK), 2 * ATT_BLK)
                o, lse = attend(q_ref[q_rows, :], k_ref[kv_rows, :], v_ref[kv_rows, :], bias)
                out_rows = q_rows
            else:
                st, a, r2 = dil // nd, r % nd, r // nd
                i0 = n * ATT_BLK
                q_rows = pl.ds(st * i0 + r2, ATT_BLK, stride=st) if st > 1 else pl.ds(i0, ATT_BLK)
                kv0 = st * (sb * (ATT_SUPER // dil) + i0 - back) + r2
                kv_rows = pl.ds(kv0, 2 * ATT_BLK, stride=st) if st > 1 else pl.ds(kv0, 2 * ATT_BLK)
                o, lse = attend(q4[a, q_rows, :], k4[a, kv_rows, :], v4[a, kv_rows, :], bias)
                out_rows = (pl.ds(a * sub + st * i0 + r2, ATT_BLK, stride=st) if st > 1
                            else pl.ds(a * sub + i0, ATT_BLK))
            o_scr[c, out_rows, :] = o
            lse_scr[c, out_rows, :] = lse

    rows = 128

    def merge(i, carry):
        for a in range(nd):
            tok = pl.ds(i * (rows * nd) + a, rows, stride=nd)
            cls = pl.ds(a * sub + pl.multiple_of(i * rows, rows), rows)
            l0, l1, l2 = lse_scr[0, tok, :], lse_scr[1, cls, :], lse_scr[2, cls, :]
            m = jnp.maximum(jnp.maximum(l0, l1), l2)
            w0, w1, w2 = jnp.exp(l0 - m), jnp.exp(l1 - m), jnp.exp(l2 - m)
            num = w0 * o_scr[0, tok, :] + w1 * o_scr[1, cls, :] + w2 * o_scr[2, cls, :]
            stage[tok, :] = num / (w0 + w1 + w2)
        return carry

    lax.fori_loop(0, sub // rows, merge, 0)
    o_ref[...] = stage[...].astype(o_ref.dtype)


def _dilated_attention(proj, seq):
    n_super = seq // ATT_SUPER
    blk = lambda off: pl.BlockSpec((seq, A_HEAD_DIM), lambda h, s: (0, off + h))
    return pl.pallas_call(
        _attn_body,
        grid=(A_HEADS, n_super),
        in_specs=[pl.BlockSpec((ATT_SUPER, A_HEAD_DIM), lambda h, s: (s, _QA_BLK + h)), blk(_KA_BLK), blk(_VA_BLK)],
        out_specs=pl.BlockSpec((ATT_SUPER, A_HEAD_DIM), lambda h, s: (s, h)),
        out_shape=jax.ShapeDtypeStruct((seq, A_WIDTH), BF16),
        scratch_shapes=[pltpu.VMEM((len(DILATED_CONFIGS), ATT_SUPER, A_HEAD_DIM), F32),
                        pltpu.VMEM((len(DILATED_CONFIGS), ATT_SUPER, A_HEAD_DIM), F32),
                        pltpu.VMEM((2, ATT_BLK, 2 * ATT_BLK), F32),
                        pltpu.VMEM((ATT_DEINT, seq // ATT_DEINT, A_HEAD_DIM), F32),
                        pltpu.VMEM((ATT_DEINT, seq // ATT_DEINT, A_HEAD_DIM), F32),
                        pltpu.VMEM((ATT_DEINT, ATT_SUPER // ATT_DEINT, A_HEAD_DIM), F32),
                        pltpu.VMEM((ATT_SUPER, A_HEAD_DIM), F32)],
        compiler_params=_params("arbitrary", "arbitrary"),
        name="dilated_attention",
    )(proj, proj, proj)


HG_CHUNK = 64
HG_TB = 1024
HG_HEADS_PER_STEP = 4
HG_CHUNKS_PER_STACK = 1
HG_STACKS_PER_ITER = 4


def _hgrn_ref_rows(bc, half):
    n, width = bc.shape
    blk = 2 * half
    if blk >= SUBLANES:
        rows = [jnp.broadcast_to(bc[b0 + half - 1:b0 + half, :], (blk, width)) for b0 in range(0, n, blk)]
        return jnp.concatenate(rows, axis=0) if len(rows) > 1 else rows[0]
    sub = lax.broadcasted_iota(jnp.int32, (SUBLANES, width), 0)
    groups = []
    for g0 in range(0, n, SUBLANES):
        grp = bc[g0:g0 + SUBLANES, :]
        if half == 1:
            groups.append(jnp.where(jnp.bitwise_and(sub, 1) == 1, pltpu.roll(grp, 1, 0), grp))
        else:
            assert half == 2 and SUBLANES == 8
            groups.append(jnp.where(sub < 4, jnp.broadcast_to(grp[1:2, :], grp.shape),
                                    jnp.broadcast_to(grp[5:6, :], grp.shape)))
    return jnp.concatenate(groups, axis=0)


def _hgrn_body(qb_ref, fb_ref, ib_ref, gb_ref, lbl_ref, g_ref, o_ref, st_ref):
    @pl.when(pl.program_id(1) == 0)
    def _():
        st_ref[...] = jnp.zeros_like(st_ref)

    c_ = HG_CHUNK
    lbl = lbl_ref[...]
    e = jnp.exp(lbl - jnp.max(lbl, axis=0, keepdims=True))
    lb_all = e[0:1, :] / jnp.sum(e, axis=0, keepdims=True)
    g_all = g_ref[...]

    nh, nc = HG_HEADS_PER_STEP, HG_CHUNKS_PER_STACK
    rows = nc * c_
    n = nh * rows
    ti = lax.broadcasted_iota(jnp.int32, (n, n), 0)
    si = lax.broadcasted_iota(jnp.int32, (n, n), 1)
    xor = jnp.bitwise_xor(ti, si)
    causal = jnp.logical_and(xor < c_, si <= ti)
    tri = jnp.where(causal, 1.0, 0.0).astype(BF16)

    def stack_of(ref, r0):
        return jnp.concatenate([ref[pl.ds(r0, rows), h * B_KEY_DIM:(h + 1) * B_KEY_DIM] for h in range(nh)], axis=0)

    def per_head_rows(x):
        return jnp.concatenate([jnp.broadcast_to(x[:, h * B_KEY_DIM:(h + 1) * B_KEY_DIM], (rows, B_KEY_DIM))
                                for h in range(nh)], axis=0)

    lb = per_head_rows(lb_all)
    gnorm = per_head_rows(g_all)

    def stack(i):
        r0 = pl.multiple_of(i * rows, rows)
        f = lb + (1.0 - lb) * jax.nn.sigmoid(stack_of(fb_ref, r0))
        logf = jnp.log(f)
        kk = 1.0 - f
        q = jax.nn.silu(stack_of(qb_ref, r0))
        v = stack_of(ib_ref, r0)
        vb = v.astype(BF16)
        hi = logf.astype(BF16)
        rem = logf - hi.astype(F32)
        mid = rem.astype(BF16)
        low = (rem - mid.astype(F32)).astype(BF16)
        parts = jnp.dot(tri, jnp.concatenate([hi, mid, low], axis=1), preferred_element_type=F32)
        bc = (parts[:, :B_KEY_DIM] + (parts[:, B_KEY_DIM:2 * B_KEY_DIM] + parts[:, 2 * B_KEY_DIM:])) * LOG2_E

        attn = lax.dot_general(q.astype(BF16), kk.astype(BF16), (((1,), (1,)), ((), ())),
                               preferred_element_type=F32)
        half = 1
        while half < c_:
            dec = jnp.exp2(-jnp.abs(bc - _hgrn_ref_rows(bc, half)))
            s = lax.dot_general((q * dec).astype(BF16), (kk * dec).astype(BF16), (((1,), (1,)), ((), ())),
                                preferred_element_type=F32)
            attn = jnp.where(xor >= half, s, attn)
            half *= 2
        attn = jnp.where(causal, attn, 0.0)
        o = jnp.dot(attn.astype(BF16), vb, preferred_element_type=F32)

        last = jnp.concatenate([jnp.broadcast_to(bc[a + c_ - 1:a + c_, :], (c_, B_KEY_DIM))
                                for a in range(0, n, c_)], axis=0)
        q_dec = (q * jnp.exp2(bc)).astype(BF16)
        k_dec = (kk * jnp.exp2(last - bc)).astype(BF16)
        st_dec = jnp.exp2(last)
        inter = []
        for h in range(nh):
            st = st_ref[h]
            for c in range(nc):
                a = h * rows + c * c_
                inter.append(lax.dot_general(q_dec[a:a + c_], st.astype(BF16), (((1,), (1,)), ((), ())),
                                             preferred_element_type=F32))
                upd = lax.dot_general(vb[a:a + c_], k_dec[a:a + c_], (((0,), (0,)), ((), ())),
                                      preferred_element_type=F32)
                st = st * st_dec[a:a + 1] + upd
            st_ref[h] = st
        o = o + jnp.concatenate(inter, axis=0)

        o = o * lax.rsqrt(jnp.mean(o * o, axis=-1, keepdims=True) + NORM_EPS)
        res = (o * gnorm * jax.nn.silu(stack_of(gb_ref, r0))).astype(o_ref.dtype)
        for h in range(nh):
            o_ref[pl.ds(r0, rows), h * B_KEY_DIM:(h + 1) * B_KEY_DIM] = res[h * rows:(h + 1) * rows]

    def stacks(i, carry):
        for u in range(HG_STACKS_PER_ITER):
            stack(i * HG_STACKS_PER_ITER + u)
        return carry

    lax.fori_loop(0, HG_TB // rows // HG_STACKS_PER_ITER, stacks, 0)


def _hgrn2(proj, lb_logits, norm_g, seq):
    hp = HG_HEADS_PER_STEP
    width = hp * B_KEY_DIM
    col = lambda off: pl.BlockSpec((HG_TB, width), lambda h, t: (t, off // hp + h))
    n_lb = lb_logits.shape[0]
    return pl.pallas_call(
        _hgrn_body,
        grid=(B_HEADS // hp, seq // HG_TB),
        in_specs=[col(_QB_BLK), col(_FB_BLK), col(_IB_BLK), col(_GB_BLK),
                  pl.BlockSpec((n_lb, width), lambda h, t: (0, h)),
                  pl.BlockSpec((1, width), lambda h, t: (0, h))],
        out_specs=pl.BlockSpec((HG_TB, width), lambda h, t: (t, h)),
        out_shape=jax.ShapeDtypeStruct((seq, B_WIDTH), BF16),
        scratch_shapes=[pltpu.VMEM((hp, B_KEY_DIM, B_KEY_DIM), F32)],
        compiler_params=_params("arbitrary", "arbitrary"),
        name="hgrn2",
    )(proj, proj, proj, proj, lb_logits.astype(F32), norm_g.reshape(1, B_WIDTH).astype(F32))


MERGE_GATE_BLK = 512


def _merge_body(oa_ref, ob_ref, *rest):
    n_g = (len(rest) - 5) // 2
    ga_refs, gb_refs = rest[:n_g], rest[n_g:2 * n_g]
    wa_ref, wb_ref, o_ref, wa_s, wb_s = rest[2 * n_g:]

    @pl.when(pl.program_id(1) == 0)
    def _():
        wa_s[...] = wa_ref[...].astype(BF16)
        wb_s[...] = wb_ref[...].astype(BF16)

    ya = jnp.dot(oa_ref[...], wa_s[...], preferred_element_type=F32)
    yb = jnp.dot(ob_ref[...], wb_s[...], preferred_element_type=F32)
    ga = jnp.concatenate([r[...] for r in ga_refs], axis=1)
    gb = jnp.concatenate([r[...] for r in gb_refs], axis=1)
    o_ref[...] = (jax.nn.sigmoid(ga) * ya + jax.nn.sigmoid(gb) * yb).astype(o_ref.dtype)


def _branch_merge(oa, ob, proj, wa, wb, tm=512, tn=1024):
    m = oa.shape[0]
    gw = MERGE_GATE_BLK
    n_g = tn // gw
    ga0, gb0 = _GATE_A_COL // gw, _GATE_B_COL // gw
    gate = lambda off, u: pl.BlockSpec((tm, gw), lambda j, i: (i, off + j * n_g + u))
    return pl.pallas_call(
        _merge_body,
        grid=(D_MODEL // tn, m // tm),
        in_specs=[pl.BlockSpec((tm, A_WIDTH), lambda j, i: (i, 0)), pl.BlockSpec((tm, B_WIDTH), lambda j, i: (i, 0)),
                  *[gate(ga0, u) for u in range(n_g)], *[gate(gb0, u) for u in range(n_g)],
                  pl.BlockSpec((A_WIDTH, tn), lambda j, i: (0, j)), pl.BlockSpec((B_WIDTH, tn), lambda j, i: (0, j))],
        out_specs=pl.BlockSpec((tm, tn), lambda j, i: (i, j)),
        out_shape=jax.ShapeDtypeStruct((m, D_MODEL), BF16),
        scratch_shapes=[pltpu.VMEM((A_WIDTH, tn), BF16), pltpu.VMEM((B_WIDTH, tn), BF16)],
        compiler_params=_params("arbitrary", "arbitrary"),
        name="branch_merge",
    )(oa, ob, *([proj] * (2 * n_g)), wa, wb)


def _route_rows(h, g, rw, rb):
    hn = _rmsnorm_rows(h, g)
    hn_hi = hn.astype(BF16)
    hn_lo = (hn - hn_hi.astype(F32)).astype(BF16)
    rw_hi = rw.astype(BF16)
    rw_lo = (rw - rw_hi.astype(F32)).astype(BF16)
    logits = (jnp.dot(hn_hi, rw_hi, preferred_element_type=F32)
              + (jnp.dot(hn_lo, rw_hi, preferred_element_type=F32)
                 + jnp.dot(hn_hi, rw_lo, preferred_element_type=F32))) + rb
    lane = lax.broadcasted_iota(jnp.int32, logits.shape, 1)
    neg = jnp.float32(-jnp.inf)
    big = jnp.int32(LANES)

    is_g = lane < N_GROUPS
    lg = jnp.where(is_g, logits, neg)
    mg = jnp.max(lg, axis=-1, keepdims=True)
    g_idx = jnp.min(jnp.where(lg == mg, lane, big), axis=-1, keepdims=True)
    pg_top = 1.0 / jnp.sum(jnp.where(is_g, jnp.exp(lg - mg), 0.0), axis=-1, keepdims=True)

    lo = N_GROUPS + g_idx * EXPERTS_PER_GROUP
    in_grp = jnp.logical_and(lane >= lo, lane < lo + EXPERTS_PER_GROUP)
    le = jnp.where(in_grp, logits, neg)
    v1 = jnp.max(le, axis=-1, keepdims=True)
    i1 = jnp.min(jnp.where(le == v1, lane, big), axis=-1, keepdims=True)
    le2 = jnp.where(lane == i1, neg, le)
    v2 = jnp.max(le2, axis=-1, keepdims=True)
    i2 = jnp.min(jnp.where(le2 == v2, lane, big), axis=-1, keepdims=True)
    e2 = jnp.exp(v2 - v1)
    w1 = pg_top / (1.0 + e2)
    w2 = pg_top * e2 / (1.0 + e2)
    eid = jnp.where(lane == 0, i1 - N_GROUPS, jnp.where(lane == 1, i2 - N_GROUPS, 0))
    wk = jnp.where(lane == 0, w1, jnp.where(lane == 1, w2, 0.0))
    return hn, eid, wk


def _out_router_body(a_ref, w_ref, x_ref, g_ref, rw_ref, rb_ref, h1_ref, hn_ref, eid_ref, wk_ref, h_prev):
    @pl.when(pl.program_id(0) == 0)
    def _():
        h_prev[...] = jnp.zeros_like(h_prev)

    hn_ref[...], eid_ref[...], wk_ref[...] = _route_rows(h_prev[...], g_ref[...], rw_ref[...], rb_ref[...])
    h1 = x_ref[...] + jnp.dot(a_ref[...], w_ref[...], preferred_element_type=F32)
    h1_ref[...] = h1
    h_prev[...] = h1


def _out_proj_router(a, w_bf16, x, g, rw, rb, tm=512):
    m, d = x.shape
    n_blk = m // tm
    cur = lambda width: pl.BlockSpec((tm, width), lambda i: (jnp.minimum(i, n_blk - 1), 0))
    prev = lambda width: pl.BlockSpec((tm, width), lambda i: (jnp.maximum(i - 1, 0), 0))
    const = lambda shape, **kw: pl.BlockSpec(shape, lambda i: (0, 0), **kw)
    return pl.pallas_call(
        _out_router_body,
        grid=(n_blk + 1,),
        in_specs=[cur(a.shape[1]), const(w_bf16.shape, pipeline_mode=pl.Buffered(1)), cur(d),
                  const((1, d)), const((d, LANES)), const((1, LANES))],
        out_specs=[cur(d), prev(d), prev(LANES), prev(LANES)],
        out_shape=[jax.ShapeDtypeStruct((m, d), F32), jax.ShapeDtypeStruct((m, d), F32),
                   jax.ShapeDtypeStruct((m, LANES), jnp.int32), jax.ShapeDtypeStruct((m, LANES), F32)],
        scratch_shapes=[pltpu.VMEM((tm, d), F32)],
        compiler_params=_params("arbitrary"),
        name="out_proj_router",
    )(a, w_bf16, x, g.reshape(1, d).astype(F32), rw, rb)


MOE_TM = 256
ROW_DMA_PRIORITIES = (0, 1)
MOE_GATHER_GROUP = 8


def _moe_body(tile_e_ref, tile_blk_ref, tile_rows_ref, n_used_ref,
              src0_ref, src_next_ref, dst_prev_ref, hn_ref, w1_ref, w3_ref, w2_ref, y_ref,
              w1_s, w3_s, w2_s, x0, x1, o0, o1, gsem, ssem):
    j = pl.program_id(0)
    n_tiles = pl.num_programs(0)
    n_used = n_used_ref[0]
    xs = (x0, x1)
    os_ = (o0, o1)

    def rows_of(t):
        return jnp.where(t >= 0, tile_rows_ref[jnp.clip(t, 0, n_tiles - 1)], 0)

    def gather_rows_of(t):
        return (rows_of(t) + MOE_GATHER_GROUP - 1) // MOE_GATHER_GROUP * MOE_GATHER_GROUP

    def gather_issue(idx_ref, b, n):
        for r in range(MOE_TM):
            @pl.when(r - r % MOE_GATHER_GROUP < n)
            def _(r=r):
                pltpu.make_async_copy(hn_ref.at[idx_ref[0, 0, r]], xs[b].at[r],
                                      gsem.at[b]).start(priority=ROW_DMA_PRIORITIES[r % 2])

    def scatter_issue(idx_ref, b, n):
        for r in range(MOE_TM):
            @pl.when(r < n)
            def _(r=r):
                pltpu.make_async_copy(os_[b].at[r], y_ref.at[idx_ref[0, 0, r]],
                                      ssem.at[b]).start(priority=ROW_DMA_PRIORITIES[r % 2])

    def wait_rows(make_copy, n):
        p = MOE_TM
        while p >= 1:
            @pl.when(jnp.bitwise_and(n, p) != 0)
            def _(p=p):
                make_copy(p).wait()
            p //= 2

    def gather_wait(b, n):
        wait_rows(lambda p: pltpu.make_async_copy(hn_ref.at[pl.ds(0, p)], xs[b].at[pl.ds(0, p)], gsem.at[b]), n)

    def scatter_wait(b, n):
        wait_rows(lambda p: pltpu.make_async_copy(os_[b].at[pl.ds(0, p)], y_ref.at[pl.ds(0, p)], ssem.at[b]), n)

    @pl.when(j == 0)
    def _():
        x0[...] = jnp.zeros_like(x0)
        x1[...] = jnp.zeros_like(x1)
        gather_issue(src0_ref, 0, gather_rows_of(0))

    for b in (0, 1):
        @pl.when(jnp.logical_and(j < n_used, j % 2 == b))
        def _(b=b):
            gather_wait(b, gather_rows_of(j))
            scatter_wait(b, rows_of(j - 2))

            prev_e = tile_e_ref[jnp.maximum(j - 1, 0)]

            @pl.when(jnp.logical_or(j == 0, tile_e_ref[j] != prev_e))
            def _():
                w1_s[...] = w1_ref[0].astype(BF16)
                w3_s[...] = w3_ref[0].astype(BF16)
                w2_s[...] = w2_ref[0].astype(BF16)

            gather_issue(src_next_ref, 1 - b, gather_rows_of(j + 1))
            scatter_issue(dst_prev_ref, 1 - b, rows_of(j - 1))
            x = xs[b][...].astype(BF16)
            hid = (jax.nn.silu(jnp.dot(x, w1_s[...], preferred_element_type=F32))
                   * jnp.dot(x, w3_s[...], preferred_element_type=F32))
            os_[b][...] = jnp.dot(hid.astype(BF16), w2_s[...], preferred_element_type=F32)

    for b in (0, 1):
        @pl.when(jnp.logical_and(j == n_used, j % 2 == b))
        def _(b=b):
            scatter_wait(b, rows_of(j - 2))
            scatter_issue(dst_prev_ref, 1 - b, rows_of(j - 1))
            scatter_wait(1 - b, rows_of(j - 1))


def _moe_experts(hn, w1, w3, w2, tile_e, tile_blk, tile_rows, n_used, src, dst, n_tokens):
    d = hn.shape[1]
    n_tiles = tile_e.shape[0]
    ff = w1.shape[-1]
    idx_block = lambda fn: pl.BlockSpec((1, 1, MOE_TM), fn, memory_space=pltpu.SMEM)
    grid_spec = pltpu.PrefetchScalarGridSpec(
        num_scalar_prefetch=4,
        grid=(n_tiles,),
        in_specs=[
            idx_block(lambda j, te, tb, tr, nu: (0, 0, 0)),
            idx_block(lambda j, te, tb, tr, nu: (tb[jnp.minimum(j + 1, n_tiles - 1)], 0, 0)),
            idx_block(lambda j, te, tb, tr, nu: (tb[jnp.maximum(j - 1, 0)], 0, 0)),
            pl.BlockSpec(memory_space=pl.ANY),
            pl.BlockSpec((1, d, ff), lambda j, te, tb, tr, nu: (te[j], 0, 0)),
            pl.BlockSpec((1, d, ff), lambda j, te, tb, tr, nu: (te[j], 0, 0)),
            pl.BlockSpec((1, ff, d), lambda j, te, tb, tr, nu: (te[j], 0, 0)),
        ],
        out_specs=pl.BlockSpec(memory_space=pl.ANY),
        scratch_shapes=[pltpu.VMEM((d, ff), BF16), pltpu.VMEM((d, ff), BF16), pltpu.VMEM((ff, d), BF16),
                        pltpu.VMEM((MOE_TM, d), F32), pltpu.VMEM((MOE_TM, d), F32),
                        pltpu.VMEM((MOE_TM, d), F32), pltpu.VMEM((MOE_TM, d), F32),
                        pltpu.SemaphoreType.DMA((2,)), pltpu.SemaphoreType.DMA((2,))],
    )
    src3 = src.reshape(n_tiles, 1, MOE_TM)
    return pl.pallas_call(
        _moe_body,
        grid_spec=grid_spec,
        out_shape=jax.ShapeDtypeStruct((2 * n_tokens, d), F32),
        compiler_params=_params("arbitrary"),
        name="moe_experts",
    )(tile_e, tile_blk, tile_rows, n_used, src3, src3, dst.reshape(n_tiles, 1, MOE_TM), hn, w1, w3, w2)


def _combine_body(h_ref, y0_ref, y1_ref, wk_ref, g_ref, o_ref):
    wk = wk_ref[...]
    y = h_ref[...] + wk[:, 0:1] * y0_ref[...] + wk[:, 1:2] * y1_ref[...]
    o_ref[...] = _rmsnorm_rows(y, g_ref[...]).astype(o_ref.dtype)


def _combine(h, y2, wk, g, out_dtype, tm=256):
    m, d = h.shape
    nblk = m // tm
    return pl.pallas_call(
        _combine_body,
        grid=(nblk,),
        in_specs=[pl.BlockSpec((tm, d), lambda i: (i, 0)), pl.BlockSpec((tm, d), lambda i: (i, 0)),
                  pl.BlockSpec((tm, d), lambda i: (nblk + i, 0)), pl.BlockSpec((tm, LANES), lambda i: (i, 0)),
                  pl.BlockSpec((1, d), lambda i: (0, 0))],
        out_specs=pl.BlockSpec((tm, d), lambda i: (i, 0)),
        out_shape=jax.ShapeDtypeStruct((m, d), out_dtype),
        compiler_params=_params("arbitrary"),
        name="combine_final_norm",
    )(h, y2, y2, wk, g.reshape(1, d).astype(F32))


def _routing_tables(eid, n_tokens):
    n_pairs = 2 * n_tokens
    n_tiles = n_pairs // MOE_TM + N_EXPERTS + 1
    n_rows = n_tiles * MOE_TM
    experts = jnp.arange(N_EXPERTS, dtype=jnp.int32)[None, :]
    e_flat = eid.reshape(n_pairs).astype(jnp.int32)
    counts = jnp.sum(e_flat[:, None] == experts, axis=0, dtype=jnp.int32)
    tiles_per_e = (counts + MOE_TM - 1) // MOE_TM
    tile_end = jnp.cumsum(tiles_per_e)
    tile_start = tile_end - tiles_per_e
    n_used = tile_end[-1]
    filler_end = jnp.cumsum(tiles_per_e * MOE_TM - counts)
    filler = jnp.arange(n_rows - n_pairs, dtype=jnp.int32)[:, None]
    filler_e = jnp.sum(filler_end[None, :] <= filler, axis=1, dtype=jnp.int32)
    keys = jnp.concatenate([2 * e_flat, 2 * filler_e + 1])
    order = jnp.argsort(keys, stable=True).astype(jnp.int32)
    valid = order < n_pairs
    pair = jnp.where(valid, order, 0)
    src_tok = pair // 2
    dst_row = (pair % 2) * n_tokens + pair // 2
    j = jnp.arange(n_tiles, dtype=jnp.int32)
    tile_blk = jnp.minimum(j, n_used - 1)
    tile_e = jnp.sum(tile_end[None, :] <= tile_blk[:, None], axis=1, dtype=jnp.int32)
    own = tile_e[:, None] == experts
    tile_row0 = (tile_blk - jnp.sum(jnp.where(own, tile_start[None, :], 0), axis=1)) * MOE_TM
    own_count = jnp.sum(jnp.where(own, counts[None, :], 0), axis=1)
    tile_rows = jnp.where(j == tile_blk, jnp.clip(own_count - tile_row0, 0, MOE_TM), 0).astype(jnp.int32)
    return (tile_e, tile_blk.astype(jnp.int32), tile_rows, n_used.reshape(1).astype(jnp.int32),
            src_tok.astype(jnp.int32), dst_row.astype(jnp.int32))


def kernel(x, mix_norm_g, w_in, hgrn_lb_logits, hgrn_norm_g, w_branch_a, w_branch_b, w_out, ffn_norm_g,
           router_w_group, router_b_group, router_w_expert, router_b_expert, expert_w1, expert_w3, expert_w2,
           final_norm_g):
    b, seq, d = x.shape
    assert b == 1 and d == D_MODEL and seq % ATT_SUPER == 0 and w_in.shape == (1, D_MODEL, IN_WIDTH)
    h0 = x.reshape(seq, d).astype(F32)

    xn = _rmsnorm(h0, mix_norm_g[0], BF16)
    proj = _matmul(xn, w_in[0], F32, tm=1024, tn=1280, name="in_proj")
    oa = _dilated_attention(proj, seq)
    ob = _hgrn2(proj, hgrn_lb_logits, hgrn_norm_g[0], seq)
    merged = _branch_merge(oa, ob, proj, w_branch_a[0], w_branch_b[0])
    pad = LANES - N_GROUPS - N_EXPERTS
    rw = jnp.concatenate([router_w_group[0], router_w_expert[0], jnp.zeros((d, pad), F32)], axis=1).astype(F32)
    rb = jnp.concatenate([router_b_group[0], router_b_expert[0], jnp.zeros((pad,), F32)]).reshape(1, LANES).astype(F32)
    h1, hn, eid, wk = _out_proj_router(merged, w_out[0].astype(BF16), h0, ffn_norm_g[0], rw, rb)

    tile_e, tile_blk, tile_rows, n_used, src_tok, dst_row = _routing_tables(eid[:, :2], seq)
    w1 = expert_w1[0].reshape(N_EXPERTS, D_MODEL, EXPERT_FF)
    w3 = expert_w3[0].reshape(N_EXPERTS, D_MODEL, EXPERT_FF)
    w2 = expert_w2[0].reshape(N_EXPERTS, EXPERT_FF, D_MODEL)
    y2 = _moe_experts(hn, w1, w3, w2, tile_e, tile_blk, tile_rows, n_used, src_tok, dst_row, seq)

    out = _combine(h1, y2, wk, final_norm_g, x.dtype)
    return out.reshape(b, seq, d)
```

```python
import functools

import jax
import jax.numpy as jnp
from jax import lax
from jax.experimental import pallas as pl
from jax.experimental.pallas import tpu as pltpu

F32 = jnp.float32
BF16 = jnp.bfloat16

D_MODEL = 2048
A_HEADS = 12
A_HEAD_DIM = 128
A_WIDTH = A_HEADS * A_HEAD_DIM
A_SCALE = A_HEAD_DIM ** -0.5
DILATED_CONFIGS = ((128, 1), (512, 4), (2048, 16))
B_HEADS = 8
B_KEY_DIM = 128
B_WIDTH = B_HEADS * B_KEY_DIM
N_GROUPS = 4
EXPERTS_PER_GROUP = 8
N_EXPERTS = N_GROUPS * EXPERTS_PER_GROUP
EXPERT_FF = 512
NORM_EPS = 1e-6

LANES = 128
SUBLANES = 8
VMEM_LIMIT = 56 * 1024 * 1024

_QA_BLK = 0
_KA_BLK = A_HEADS
_VA_BLK = 2 * A_HEADS
_QB_BLK = 3 * A_HEADS
_FB_BLK = _QB_BLK + B_HEADS
_IB_BLK = _FB_BLK + B_HEADS
_GB_BLK = _IB_BLK + B_HEADS
_GATE_A_COL = 3 * A_WIDTH + 4 * B_WIDTH
_GATE_B_COL = _GATE_A_COL + D_MODEL
IN_WIDTH = _GATE_B_COL + D_MODEL


def _params(*sem):
    return pltpu.CompilerParams(dimension_semantics=sem, vmem_limit_bytes=VMEM_LIMIT)


def _rmsnorm_rows(x, g):
    ms = jnp.mean(x * x, axis=-1, keepdims=True)
    return x * lax.rsqrt(ms + NORM_EPS) * g


def _rmsnorm_body(x_ref, g_ref, o_ref):
    o_ref[...] = _rmsnorm_rows(x_ref[...].astype(F32), g_ref[...]).astype(o_ref.dtype)


def _rmsnorm(x, g, out_dtype, tm=1024):
    m, d = x.shape
    return pl.pallas_call(
        _rmsnorm_body,
        grid=(m // tm,),
        in_specs=[pl.BlockSpec((tm, d), lambda i: (i, 0)), pl.BlockSpec((1, d), lambda i: (0, 0))],
        out_specs=pl.BlockSpec((tm, d), lambda i: (i, 0)),
        out_shape=jax.ShapeDtypeStruct((m, d), out_dtype),
        compiler_params=_params("arbitrary"),
        name="rmsnorm",
    )(x, g.reshape(1, d).astype(F32))


def _matmul_body(a_ref, w_ref, o_ref, wb_ref):
    @pl.when(pl.program_id(1) == 0)
    def _():
        wb_ref[...] = w_ref[...].astype(BF16)

    o_ref[...] = jnp.dot(a_ref[...], wb_ref[...], preferred_element_type=F32).astype(o_ref.dtype)


def _matmul(a, w, out_dtype, tm, tn, name):
    m, k = a.shape
    n = w.shape[1]
    return pl.pallas_call(
        _matmul_body,
        grid=(n // tn, m // tm),
        in_specs=[pl.BlockSpec((tm, k), lambda j, i: (i, 0)), pl.BlockSpec((k, tn), lambda j, i: (0, j))],
        out_specs=pl.BlockSpec((tm, tn), lambda j, i: (i, j)),
        out_shape=jax.ShapeDtypeStruct((m, n), out_dtype),
        scratch_shapes=[pltpu.VMEM((k, tn), BF16)],
        compiler_params=_params("arbitrary", "arbitrary"),
        name=name,
    )(a, w)


LOG2_E = 1.4426950408889634
LN_2 = 0.6931471805599453
ATT_BLK = 128
ATT_SUPER = 2048
ATT_DEINT = 4


def _attn_body(q_ref, k_ref, v_ref, o_ref, o_scr, lse_scr, bias_scr, k4, v4, q4, stage):
    sb = pl.program_id(1)
    seq = k_ref.shape[0]
    nd = ATT_DEINT
    sub = ATT_SUPER // nd
    diff = (lax.broadcasted_iota(jnp.int32, (ATT_BLK, 2 * ATT_BLK), 1)
            - lax.broadcasted_iota(jnp.int32, (ATT_BLK, 2 * ATT_BLK), 0))
    neg = jnp.float32(-jnp.inf)
    bias_scr[0] = jnp.where(jnp.logical_and(diff >= 0, diff <= ATT_BLK), 0.0, neg)
    bias_scr[1] = jnp.where(diff <= 0, 0.0, neg)

    @pl.when(sb == 0)
    def _():
        rows = 512

        def split(i, carry):
            for a in range(nd):
                src = pl.ds(i * (rows * nd) + a, rows, stride=nd)
                dst = pl.ds(pl.multiple_of(i * rows, rows), rows)
                k4[a, dst, :] = k_ref[src, :]
                v4[a, dst, :] = v_ref[src, :]
            return carry

        lax.fori_loop(0, seq // nd // rows, split, 0)

    for a in range(nd):
        q4[a] = q_ref[pl.ds(a, sub, stride=nd), :]

    def attend(q, k, v, bias):
        qb = (q * (A_SCALE * LOG2_E)).astype(BF16)
        s = lax.dot_general(qb, k.astype(BF16), (((1,), (1,)), ((), ())), preferred_element_type=F32)
        s = s + bias
        mx = jnp.max(s, axis=-1, keepdims=True)
        p = jnp.exp2(s - mx)
        den = jnp.sum(p, axis=-1, keepdims=True)
        acc = jnp.dot(p.astype(BF16), v.astype(BF16), preferred_element_type=F32)
        return acc / den, jnp.broadcast_to(mx * LN_2 + jnp.log(den), (ATT_BLK, A_HEAD_DIM))

    n_tiles = ATT_SUPER // ATT_BLK
    for c, (window, dil) in enumerate(DILATED_CONFIGS):
        assert window // dil == ATT_BLK and (dil == 1 or dil % nd == 0)
        tiles_per_res = n_tiles // dil
        for t in range(n_tiles):
            n, r = t % tiles_per_res, t // tiles_per_res
            if n == 0:
                first = (sb == 0).astype(jnp.int32)
                back = ATT_BLK * (1 - first)
                bias = bias_scr[first]
            else:
                back, bias = ATT_BLK, bias_scr[0]
            if dil == 1:
                q_rows = pl.ds(n * ATT_BLK, ATT_BLK)
                kv_rows = pl.ds(pl.multiple_of(sb * ATT_SUPER + n * ATT_BLK - back, ATT_BLK), 2 * ATT_BLK)
                o, lse = attend(q_ref[q_rows, :], k_ref[kv_rows, :], v_ref[kv_rows, :], bias)
                out_rows = q_rows
            else:
                st, a, r2 = dil // nd, r % nd, r // nd
                i0 = n * ATT_BLK
                q_rows = pl.ds(st * i0 + r2, ATT_BLK, stride=st) if st > 1 else pl.ds(i0, ATT_BLK)
                kv0 = st * (sb * (ATT_SUPER // dil) + i0 - back) + r2
                kv_rows = pl.ds(kv0, 2 * ATT_BLK, stride=st) if st > 1 else pl.ds(kv0, 2 * ATT_BLK)
                o, lse = attend(q4[a, q_rows, :], k4[a, kv_rows, :], v4[a, kv_rows, :], bias)
                out_rows = (pl.ds(a * sub + st * i0 + r2, ATT_BLK, stride=st) if st > 1
                            else pl.ds(a * sub + i0, ATT_BLK))
            o_scr[c, out_rows, :] = o
            lse_scr[c, out_rows, :] = lse

    rows = 128

    def merge(i, carry):
        for a in range(nd):
            tok = pl.ds(i * (rows * nd) + a, rows, stride=nd)
            cls = pl.ds(a * sub + pl.multiple_of(i * rows, rows), rows)
            l0, l1, l2 = lse_scr[0, tok, :], lse_scr[1, cls, :], lse_scr[2, cls, :]
            m = jnp.maximum(jnp.maximum(l0, l1), l2)
            w0, w1, w2 = jnp.exp(l0 - m), jnp.exp(l1 - m), jnp.exp(l2 - m)
            num = w0 * o_scr[0, tok, :] + w1 * o_scr[1, cls, :] + w2 * o_scr[2, cls, :]
            stage[tok, :] = num / (w0 + w1 + w2)
        return carry

    lax.fori_loop(0, sub // rows, merge, 0)
    o_ref[...] = stage[...].astype(o_ref.dtype)


def _dilated_attention(proj, seq):
    n_super = seq // ATT_SUPER
    blk = lambda off: pl.BlockSpec((seq, A_HEAD_DIM), lambda h, s: (0, off + h))
    return pl.pallas_call(
        _attn_body,
        grid=(A_HEADS, n_super),
        in_specs=[pl.BlockSpec((ATT_SUPER, A_HEAD_DIM), lambda h, s: (s, _QA_BLK + h)), blk(_KA_BLK), blk(_VA_BLK)],
        out_specs=pl.BlockSpec((ATT_SUPER, A_HEAD_DIM), lambda h, s: (s, h)),
        out_shape=jax.ShapeDtypeStruct((seq, A_WIDTH), BF16),
        scratch_shapes=[pltpu.VMEM((len(DILATED_CONFIGS), ATT_SUPER, A_HEAD_DIM), F32),
                        pltpu.VMEM((len(DILATED_CONFIGS), ATT_SUPER, A_HEAD_DIM), F32),
                        pltpu.VMEM((2, ATT_BLK, 2 * ATT_BLK), F32),
                        pltpu.VMEM((ATT_DEINT, seq // ATT_DEINT, A_HEAD_DIM), F32),
                        pltpu.VMEM((ATT_DEINT, seq // ATT_DEINT, A_HEAD_DIM), F32),
                        pltpu.VMEM((ATT_DEINT, ATT_SUPER // ATT_DEINT, A_HEAD_DIM), F32),
                        pltpu.VMEM((ATT_SUPER, A_HEAD_DIM), F32)],
        compiler_params=_params("arbitrary", "arbitrary"),
        name="dilated_attention",
    )(proj, proj, proj)


HG_CHUNK = 64
HG_TB = 1024
HG_HEADS_PER_STEP = 4
HG_CHUNKS_PER_STACK = 1
HG_STACKS_PER_ITER = 8


def _hgrn_ref_rows(bc, half):
    n, width = bc.shape
    blk = 2 * half
    if blk >= SUBLANES:
        rows = [jnp.broadcast_to(bc[b0 + half - 1:b0 + half, :], (blk, width)) for b0 in range(0, n, blk)]
        return jnp.concatenate(rows, axis=0) if len(rows) > 1 else rows[0]
    sub = lax.broadcasted_iota(jnp.int32, (SUBLANES, width), 0)
    groups = []
    for g0 in range(0, n, SUBLANES):
        grp = bc[g0:g0 + SUBLANES, :]
        if half == 1:
            groups.append(jnp.where(jnp.bitwise_and(sub, 1) == 1, pltpu.roll(grp, 1, 0), grp))
        else:
            assert half == 2 and SUBLANES == 8
            groups.append(jnp.where(sub < 4, jnp.broadcast_to(grp[1:2, :], grp.shape),
                                    jnp.broadcast_to(grp[5:6, :], grp.shape)))
    return jnp.concatenate(groups, axis=0)


def _hgrn_body(qb_ref, fb_ref, ib_ref, gb_ref, lbl_ref, g_ref, o_ref, st_ref):
    @pl.when(pl.program_id(1) == 0)
    def _():
        st_ref[...] = jnp.zeros_like(st_ref)

    c_ = HG_CHUNK
    lbl = lbl_ref[...]
    e = jnp.exp(lbl - jnp.max(lbl, axis=0, keepdims=True))
    lb_all = e[0:1, :] / jnp.sum(e, axis=0, keepdims=True)
    g_all = g_ref[...]

    nh, nc = HG_HEADS_PER_STEP, HG_CHUNKS_PER_STACK
    rows = nc * c_
    n = nh * rows
    ti = lax.broadcasted_iota(jnp.int32, (n, n), 0)
    si = lax.broadcasted_iota(jnp.int32, (n, n), 1)
    xor = jnp.bitwise_xor(ti, si)
    causal = jnp.logical_and(xor < c_, si <= ti)
    tri = jnp.where(causal, 1.0, 0.0).astype(BF16)

    def stack_of(ref, r0):
        return jnp.concatenate([ref[pl.ds(r0, rows), h * B_KEY_DIM:(h + 1) * B_KEY_DIM] for h in range(nh)], axis=0)

    def per_head_rows(x):
        return jnp.concatenate([jnp.broadcast_to(x[:, h * B_KEY_DIM:(h + 1) * B_KEY_DIM], (rows, B_KEY_DIM))
                                for h in range(nh)], axis=0)

    lb = per_head_rows(lb_all)
    gnorm = per_head_rows(g_all)

    def stack(i):
        r0 = pl.multiple_of(i * rows, rows)
        f = lb + (1.0 - lb) * jax.nn.sigmoid(stack_of(fb_ref, r0))
        logf = jnp.log(f)
        kk = 1.0 - f
        q = jax.nn.silu(stack_of(qb_ref, r0))
        v = stack_of(ib_ref, r0)
        vb = v.astype(BF16)
        hi = logf.astype(BF16)
        rem = logf - hi.astype(F32)
        mid = rem.astype(BF16)
        low = (rem - mid.astype(F32)).astype(BF16)
        parts = jnp.dot(tri, jnp.concatenate([hi, mid, low], axis=1), preferred_element_type=F32)
        bc = (parts[:, :B_KEY_DIM] + (parts[:, B_KEY_DIM:2 * B_KEY_DIM] + parts[:, 2 * B_KEY_DIM:])) * LOG2_E

        attn = lax.dot_general(q.astype(BF16), kk.astype(BF16), (((1,), (1,)), ((), ())),
                               preferred_element_type=F32)
        half = 1
        while half < c_:
            dec = jnp.exp2(-jnp.abs(bc - _hgrn_ref_rows(bc, half)))
            s = lax.dot_general((q * dec).astype(BF16), (kk * dec).astype(BF16), (((1,), (1,)), ((), ())),
                                preferred_element_type=F32)
            attn = jnp.where(xor >= half, s, attn)
            half *= 2
        attn = jnp.where(causal, attn, 0.0)
        o = jnp.dot(attn.astype(BF16), vb, preferred_element_type=F32)

        last = jnp.concatenate([jnp.broadcast_to(bc[a + c_ - 1:a + c_, :], (c_, B_KEY_DIM))
                                for a in range(0, n, c_)], axis=0)
        q_dec = (q * jnp.exp2(bc)).astype(BF16)
        k_dec = (kk * jnp.exp2(last - bc)).astype(BF16)
        st_dec = jnp.exp2(last)
        inter = []
        for h in range(nh):
            st = st_ref[h]
            for c in range(nc):
                a = h * rows + c * c_
                inter.append(lax.dot_general(q_dec[a:a + c_], st.astype(BF16), (((1,), (1,)), ((), ())),
                                             preferred_element_type=F32))
                upd = lax.dot_general(vb[a:a + c_], k_dec[a:a + c_], (((0,), (0,)), ((), ())),
                                      preferred_element_type=F32)
                st = st * st_dec[a:a + 1] + upd
            st_ref[h] = st
        o = o + jnp.concatenate(inter, axis=0)

        o = o * lax.rsqrt(jnp.mean(o * o, axis=-1, keepdims=True) + NORM_EPS)
        res = (o * gnorm * jax.nn.silu(stack_of(gb_ref, r0))).astype(o_ref.dtype)
        for h in range(nh):
            o_ref[pl.ds(r0, rows), h * B_KEY_DIM:(h + 1) * B_KEY_DIM] = res[h * rows:(h + 1) * rows]

    def stacks(i, carry):
        for u in range(HG_STACKS_PER_ITER):
            stack(i * HG_STACKS_PER_ITER + u)
        return carry

    lax.fori_loop(0, HG_TB // rows // HG_STACKS_PER_ITER, stacks, 0)


def _hgrn2(proj, lb_logits, norm_g, seq):
    hp = HG_HEADS_PER_STEP
    width = hp * B_KEY_DIM
    col = lambda off: pl.BlockSpec((HG_TB, width), lambda h, t: (t, off // hp + h))
    n_lb = lb_logits.shape[0]
    return pl.pallas_call(
        _hgrn_body,
        grid=(B_HEADS // hp, seq // HG_TB),
        in_specs=[col(_QB_BLK), col(_FB_BLK), col(_IB_BLK), col(_GB_BLK),
                  pl.BlockSpec((n_lb, width), lambda h, t: (0, h)),
                  pl.BlockSpec((1, width), lambda h, t: (0, h))],
        out_specs=pl.BlockSpec((HG_TB, width), lambda h, t: (t, h)),
        out_shape=jax.ShapeDtypeStruct((seq, B_WIDTH), BF16),
        scratch_shapes=[pltpu.VMEM((hp, B_KEY_DIM, B_KEY_DIM), F32)],
        compiler_params=_params("arbitrary", "arbitrary"),
        name="hgrn2",
    )(proj, proj, proj, proj, lb_logits.astype(F32), norm_g.reshape(1, B_WIDTH).astype(F32))


MERGE_GATE_BLK = 512


def _merge_body(oa_ref, ob_ref, *rest):
    n_g = (len(rest) - 5) // 2
    ga_refs, gb_refs = rest[:n_g], rest[n_g:2 * n_g]
    wa_ref, wb_ref, o_ref, wa_s, wb_s = rest[2 * n_g:]

    @pl.when(pl.program_id(1) == 0)
    def _():
        wa_s[...] = wa_ref[...].astype(BF16)
        wb_s[...] = wb_ref[...].astype(BF16)

    ya = jnp.dot(oa_ref[...], wa_s[...], preferred_element_type=F32)
    yb = jnp.dot(ob_ref[...], wb_s[...], preferred_element_type=F32)
    ga = jnp.concatenate([r[...] for r in ga_refs], axis=1)
    gb = jnp.concatenate([r[...] for r in gb_refs], axis=1)
    o_ref[...] = (jax.nn.sigmoid(ga) * ya + jax.nn.sigmoid(gb) * yb).astype(o_ref.dtype)


def _branch_merge(oa, ob, proj, wa, wb, tm=512, tn=1024):
    m = oa.shape[0]
    gw = MERGE_GATE_BLK
    n_g = tn // gw
    ga0, gb0 = _GATE_A_COL // gw, _GATE_B_COL // gw
    gate = lambda off, u: pl.BlockSpec((tm, gw), lambda j, i: (i, off + j * n_g + u))
    return pl.pallas_call(
        _merge_body,
        grid=(D_MODEL // tn, m // tm),
        in_specs=[pl.BlockSpec((tm, A_WIDTH), lambda j, i: (i, 0)), pl.BlockSpec((tm, B_WIDTH), lambda j, i: (i, 0)),
                  *[gate(ga0, u) for u in range(n_g)], *[gate(gb0, u) for u in range(n_g)],
                  pl.BlockSpec((A_WIDTH, tn), lambda j, i: (0, j)), pl.BlockSpec((B_WIDTH, tn), lambda j, i: (0, j))],
        out_specs=pl.BlockSpec((tm, tn), lambda j, i: (i, j)),
        out_shape=jax.ShapeDtypeStruct((m, D_MODEL), BF16),
        scratch_shapes=[pltpu.VMEM((A_WIDTH, tn), BF16), pltpu.VMEM((B_WIDTH, tn), BF16)],
        compiler_params=_params("arbitrary", "arbitrary"),
        name="branch_merge",
    )(oa, ob, *([proj] * (2 * n_g)), wa, wb)


def _route_rows(h, g, rw, rb):
    hn = _rmsnorm_rows(h, g)
    hn_hi = hn.astype(BF16)
    hn_lo = (hn - hn_hi.astype(F32)).astype(BF16)
    rw_hi = rw.astype(BF16)
    rw_lo = (rw - rw_hi.astype(F32)).astype(BF16)
    logits = (jnp.dot(hn_hi, rw_hi, preferred_element_type=F32)
              + (jnp.dot(hn_lo, rw_hi, preferred_element_type=F32)
                 + jnp.dot(hn_hi, rw_lo, preferred_element_type=F32))) + rb
    lane = lax.broadcasted_iota(jnp.int32, logits.shape, 1)
    neg = jnp.float32(-jnp.inf)
    big = jnp.int32(LANES)

    is_g = lane < N_GROUPS
    lg = jnp.where(is_g, logits, neg)
    mg = jnp.max(lg, axis=-1, keepdims=True)
    g_idx = jnp.min(jnp.where(lg == mg, lane, big), axis=-1, keepdims=True)
    pg_top = 1.0 / jnp.sum(jnp.where(is_g, jnp.exp(lg - mg), 0.0), axis=-1, keepdims=True)

    lo = N_GROUPS + g_idx * EXPERTS_PER_GROUP
    in_grp = jnp.logical_and(lane >= lo, lane < lo + EXPERTS_PER_GROUP)
    le = jnp.where(in_grp, logits, neg)
    v1 = jnp.max(le, axis=-1, keepdims=True)
    i1 = jnp.min(jnp.where(le == v1, lane, big), axis=-1, keepdims=True)
    le2 = jnp.where(lane == i1, neg, le)
    v2 = jnp.max(le2, axis=-1, keepdims=True)
    i2 = jnp.min(jnp.where(le2 == v2, lane, big), axis=-1, keepdims=True)
    e2 = jnp.exp(v2 - v1)
    w1 = pg_top / (1.0 + e2)
    w2 = pg_top * e2 / (1.0 + e2)
    eid = jnp.where(lane == 0, i1 - N_GROUPS, jnp.where(lane == 1, i2 - N_GROUPS, 0))
    wk = jnp.where(lane == 0, w1, jnp.where(lane == 1, w2, 0.0))
    return hn, eid, wk


def _out_router_body(a_ref, w_ref, x_ref, g_ref, rw_ref, rb_ref, h1_ref, hn_ref, eid_ref, wk_ref, h_prev):
    @pl.when(pl.program_id(0) == 0)
    def _():
        h_prev[...] = jnp.zeros_like(h_prev)

    hn_ref[...], eid_ref[...], wk_ref[...] = _route_rows(h_prev[...], g_ref[...], rw_ref[...], rb_ref[...])
    h1 = x_ref[...] + jnp.dot(a_ref[...], w_ref[...], preferred_element_type=F32)
    h1_ref[...] = h1
    h_prev[...] = h1


def _out_proj_router(a, w_bf16, x, g, rw, rb, tm=512):
    m, d = x.shape
    n_blk = m // tm
    cur = lambda width: pl.BlockSpec((tm, width), lambda i: (jnp.minimum(i, n_blk - 1), 0))
    prev = lambda width: pl.BlockSpec((tm, width), lambda i: (jnp.maximum(i - 1, 0), 0))
    const = lambda shape, **kw: pl.BlockSpec(shape, lambda i: (0, 0), **kw)
    return pl.pallas_call(
        _out_router_body,
        grid=(n_blk + 1,),
        in_specs=[cur(a.shape[1]), const(w_bf16.shape, pipeline_mode=pl.Buffered(1)), cur(d),
                  const((1, d)), const((d, LANES)), const((1, LANES))],
        out_specs=[cur(d), prev(d), prev(LANES), prev(LANES)],
        out_shape=[jax.ShapeDtypeStruct((m, d), F32), jax.ShapeDtypeStruct((m, d), F32),
                   jax.ShapeDtypeStruct((m, LANES), jnp.int32), jax.ShapeDtypeStruct((m, LANES), F32)],
        scratch_shapes=[pltpu.VMEM((tm, d), F32)],
        compiler_params=_params("arbitrary"),
        name="out_proj_router",
    )(a, w_bf16, x, g.reshape(1, d).astype(F32), rw, rb)


MOE_TM = 256
ROW_DMA_PRIORITIES = (0, 1)
MOE_GATHER_GROUP = 8


def _moe_body(tile_e_ref, tile_blk_ref, tile_rows_ref, n_used_ref,
              src0_ref, src_next_ref, dst_prev_ref, hn_ref, w1_ref, w3_ref, w2_ref, y_ref,
              w1_s, w3_s, w2_s, x0, x1, o0, o1, gsem, ssem):
    j = pl.program_id(0)
    n_tiles = pl.num_programs(0)
    n_used = n_used_ref[0]
    xs = (x0, x1)
    os_ = (o0, o1)

    def rows_of(t):
        return jnp.where(t >= 0, tile_rows_ref[jnp.clip(t, 0, n_tiles - 1)], 0)

    def gather_rows_of(t):
        return (rows_of(t) + MOE_GATHER_GROUP - 1) // MOE_GATHER_GROUP * MOE_GATHER_GROUP

    def gather_issue(idx_ref, b, n):
        for r in range(MOE_TM):
            @pl.when(r - r % MOE_GATHER_GROUP < n)
            def _(r=r):
                pltpu.make_async_copy(hn_ref.at[idx_ref[0, 0, r]], xs[b].at[r],
                                      gsem.at[b]).start(priority=ROW_DMA_PRIORITIES[r % 2])

    def scatter_issue(idx_ref, b, n):
        for r in range(MOE_TM):
            @pl.when(r < n)
            def _(r=r):
                pltpu.make_async_copy(os_[b].at[r], y_ref.at[idx_ref[0, 0, r]],
                                      ssem.at[b]).start(priority=ROW_DMA_PRIORITIES[r % 2])

    def wait_rows(make_copy, n):
        p = MOE_TM
        while p >= 1:
            @pl.when(jnp.bitwise_and(n, p) != 0)
            def _(p=p):
                make_copy(p).wait()
            p //= 2

    def gather_wait(b, n):
        wait_rows(lambda p: pltpu.make_async_copy(hn_ref.at[pl.ds(0, p)], xs[b].at[pl.ds(0, p)], gsem.at[b]), n)

    def scatter_wait(b, n):
        wait_rows(lambda p: pltpu.make_async_copy(os_[b].at[pl.ds(0, p)], y_ref.at[pl.ds(0, p)], ssem.at[b]), n)

    @pl.when(j == 0)
    def _():
        x0[...] = jnp.zeros_like(x0)
        x1[...] = jnp.zeros_like(x1)
        gather_issue(src0_ref, 0, gather_rows_of(0))

    for b in (0, 1):
        @pl.when(jnp.logical_and(j < n_used, j % 2 == b))
        def _(b=b):
            gather_wait(b, gather_rows_of(j))
            scatter_wait(b, rows_of(j - 2))

            prev_e = tile_e_ref[jnp.maximum(j - 1, 0)]

            @pl.when(jnp.logical_or(j == 0, tile_e_ref[j] != prev_e))
            def _():
                w1_s[...] = w1_ref[0].astype(BF16)
                w3_s[...] = w3_ref[0].astype(BF16)
                w2_s[...] = w2_ref[0].astype(BF16)

            gather_issue(src_next_ref, 1 - b, gather_rows_of(j + 1))
            scatter_issue(dst_prev_ref, 1 - b, rows_of(j - 1))
            x = xs[b][...].astype(BF16)
            hid = (jax.nn.silu(jnp.dot(x, w1_s[...], preferred_element_type=F32))
                   * jnp.dot(x, w3_s[...], preferred_element_type=F32))
            os_[b][...] = jnp.dot(hid.astype(BF16), w2_s[...], preferred_element_type=F32)

    for b in (0, 1):
        @pl.when(jnp.logical_and(j == n_used, j % 2 == b))
        def _(b=b):
            scatter_wait(b, rows_of(j - 2))
            scatter_issue(dst_prev_ref, 1 - b, rows_of(j - 1))
            scatter_wait(1 - b, rows_of(j - 1))


def _moe_experts(hn, w1, w3, w2, tile_e, tile_blk, tile_rows, n_used, src, dst, n_tokens):
    d = hn.shape[1]
    n_tiles = tile_e.shape[0]
    ff = w1.shape[-1]
    idx_block = lambda fn: pl.BlockSpec((1, 1, MOE_TM), fn, memory_space=pltpu.SMEM)
    grid_spec = pltpu.PrefetchScalarGridSpec(
        num_scalar_prefetch=4,
        grid=(n_tiles,),
        in_specs=[
            idx_block(lambda j, te, tb, tr, nu: (0, 0, 0)),
            idx_block(lambda j, te, tb, tr, nu: (tb[jnp.minimum(j + 1, n_tiles - 1)], 0, 0)),
            idx_block(lambda j, te, tb, tr, nu: (tb[jnp.maximum(j - 1, 0)], 0, 0)),
            pl.BlockSpec(memory_space=pl.ANY),
            pl.BlockSpec((1, d, ff), lambda j, te, tb, tr, nu: (te[j], 0, 0)),
            pl.BlockSpec((1, d, ff), lambda j, te, tb, tr, nu: (te[j], 0, 0)),
            pl.BlockSpec((1, ff, d), lambda j, te, tb, tr, nu: (te[j], 0, 0)),
        ],
        out_specs=pl.BlockSpec(memory_space=pl.ANY),
        scratch_shapes=[pltpu.VMEM((d, ff), BF16), pltpu.VMEM((d, ff), BF16), pltpu.VMEM((ff, d), BF16),
                        pltpu.VMEM((MOE_TM, d), F32), pltpu.VMEM((MOE_TM, d), F32),
                        pltpu.VMEM((MOE_TM, d), F32), pltpu.VMEM((MOE_TM, d), F32),
                        pltpu.SemaphoreType.DMA((2,)), pltpu.SemaphoreType.DMA((2,))],
    )
    src3 = src.reshape(n_tiles, 1, MOE_TM)
    return pl.pallas_call(
        _moe_body,
        grid_spec=grid_spec,
        out_shape=jax.ShapeDtypeStruct((2 * n_tokens, d), F32),
        compiler_params=_params("arbitrary"),
        name="moe_experts",
    )(tile_e, tile_blk, tile_rows, n_used, src3, src3, dst.reshape(n_tiles, 1, MOE_TM), hn, w1, w3, w2)


def _combine_body(h_ref, y0_ref, y1_ref, wk_ref, g_ref, o_ref):
    wk = wk_ref[...]
    y = h_ref[...] + wk[:, 0:1] * y0_ref[...] + wk[:, 1:2] * y1_ref[...]
    o_ref[...] = _rmsnorm_rows(y, g_ref[...]).astype(o_ref.dtype)


def _combine(h, y2, wk, g, out_dtype, tm=512):
    m, d = h.shape
    nblk = m // tm
    return pl.pallas_call(
        _combine_body,
        grid=(nblk,),
        in_specs=[pl.BlockSpec((tm, d), lambda i: (i, 0)), pl.BlockSpec((tm, d), lambda i: (i, 0)),
                  pl.BlockSpec((tm, d), lambda i: (nblk + i, 0)), pl.BlockSpec((tm, LANES), lambda i: (i, 0)),
                  pl.BlockSpec((1, d), lambda i: (0, 0))],
        out_specs=pl.BlockSpec((tm, d), lambda i: (i, 0)),
        out_shape=jax.ShapeDtypeStruct((m, d), out_dtype),
        compiler_params=_params("arbitrary"),
        name="combine_final_norm",
    )(h, y2, y2, wk, g.reshape(1, d).astype(F32))


def _routing_tables(eid, n_tokens):
    n_pairs = 2 * n_tokens
    n_tiles = n_pairs // MOE_TM + N_EXPERTS + 1
    n_rows = n_tiles * MOE_TM
    experts = jnp.arange(N_EXPERTS, dtype=jnp.int32)[None, :]
    e_flat = eid.reshape(n_pairs).astype(jnp.int32)
    counts = jnp.sum(e_flat[:, None] == experts, axis=0, dtype=jnp.int32)
    tiles_per_e = (counts + MOE_TM - 1) // MOE_TM
    tile_end = jnp.cumsum(tiles_per_e)
    tile_start = tile_end - tiles_per_e
    n_used = tile_end[-1]
    filler_end = jnp.cumsum(tiles_per_e * MOE_TM - counts)
    filler = jnp.arange(n_rows - n_pairs, dtype=jnp.int32)[:, None]
    filler_e = jnp.sum(filler_end[None, :] <= filler, axis=1, dtype=jnp.int32)
    keys = jnp.concatenate([2 * e_flat, 2 * filler_e + 1])
    order = jnp.argsort(keys, stable=True).astype(jnp.int32)
    valid = order < n_pairs
    pair = jnp.where(valid, order, 0)
    src_tok = pair // 2
    dst_row = (pair % 2) * n_tokens + pair // 2
    j = jnp.arange(n_tiles, dtype=jnp.int32)
    tile_blk = jnp.minimum(j, n_used - 1)
    tile_e = jnp.sum(tile_end[None, :] <= tile_blk[:, None], axis=1, dtype=jnp.int32)
    own = tile_e[:, None] == experts
    tile_row0 = (tile_blk - jnp.sum(jnp.where(own, tile_start[None, :], 0), axis=1)) * MOE_TM
    own_count = jnp.sum(jnp.where(own, counts[None, :], 0), axis=1)
    tile_rows = jnp.where(j == tile_blk, jnp.clip(own_count - tile_row0, 0, MOE_TM), 0).astype(jnp.int32)
    return (tile_e, tile_blk.astype(jnp.int32), tile_rows, n_used.reshape(1).astype(jnp.int32),
            src_tok.astype(jnp.int32), dst_row.astype(jnp.int32))


def kernel(x, mix_norm_g, w_in, hgrn_lb_logits, hgrn_norm_g, w_branch_a, w_branch_b, w_out, ffn_norm_g,
           router_w_group, router_b_group, router_w_expert, router_b_expert, expert_w1, expert_w3, expert_w2,
           final_norm_g):
    b, seq, d = x.shape
    assert b == 1 and d == D_MODEL and seq % ATT_SUPER == 0 and w_in.shape == (1, D_MODEL, IN_WIDTH)
    h0 = x.reshape(seq, d).astype(F32)

    xn = _rmsnorm(h0, mix_norm_g[0], BF16)
    proj = _matmul(xn, w_in[0], F32, tm=1024, tn=1280, name="in_proj")
    oa = _dilated_attention(proj, seq)
    ob = _hgrn2(proj, hgrn_lb_logits, hgrn_norm_g[0], seq)
    merged = _branch_merge(oa, ob, proj, w_branch_a[0], w_branch_b[0])
    pad = LANES - N_GROUPS - N_EXPERTS
    rw = jnp.concatenate([router_w_group[0], router_w_expert[0], jnp.zeros((d, pad), F32)], axis=1).astype(F32)
    rb = jnp.concatenate([router_b_group[0], router_b_expert[0], jnp.zeros((pad,), F32)]).reshape(1, LANES).astype(F32)
    h1, hn, eid, wk = _out_proj_router(merged, w_out[0].astype(BF16), h0, ffn_norm_g[0], rw, rb)

    tile_e, tile_blk, tile_rows, n_used, src_tok, dst_row = _routing_tables(eid[:, :2], seq)
    w1 = expert_w1[0].reshape(N_EXPERTS, D_MODEL, EXPERT_FF)
    w3 = expert_w3[0].reshape(N_EXPERTS, D_MODEL, EXPERT_FF)
    w2 = expert_w2[0].reshape(N_EXPERTS, EXPERT_FF, D_MODEL)
    y2 = _moe_experts(hn, w1, w3, w2, tile_e, tile_blk, tile_rows, n_used, src_tok, dst_row, seq)

    out = _combine(h1, y2, wk, final_norm_g, x.dtype)
    return out.reshape(b, seq, d)
```

```python
import functools

import jax
import jax.numpy as jnp
from jax import lax
from jax.experimental import pallas as pl
from jax.experimental.pallas import tpu as pltpu

F32 = jnp.float32
BF16 = jnp.bfloat16

D_MODEL = 2048
A_HEADS = 12
A_HEAD_DIM = 128
A_WIDTH = A_HEADS * A_HEAD_DIM
A_SCALE = A_HEAD_DIM ** -0.5
DILATED_CONFIGS = ((128, 1), (512, 4), (2048, 16))
B_HEADS = 8
B_KEY_DIM = 128
B_WIDTH = B_HEADS * B_KEY_DIM
N_GROUPS = 4
EXPERTS_PER_GROUP = 8
N_EXPERTS = N_GROUPS * EXPERTS_PER_GROUP
EXPERT_FF = 512
NORM_EPS = 1e-6

LANES = 128
SUBLANES = 8
VMEM_LIMIT = 56 * 1024 * 1024

_QA_BLK = 0
_KA_BLK = A_HEADS
_VA_BLK = 2 * A_HEADS
_QB_BLK = 3 * A_HEADS
_FB_BLK = _QB_BLK + B_HEADS
_IB_BLK = _FB_BLK + B_HEADS
_GB_BLK = _IB_BLK + B_HEADS
_GATE_A_COL = 3 * A_WIDTH + 4 * B_WIDTH
_GATE_B_COL = _GATE_A_COL + D_MODEL
IN_WIDTH = _GATE_B_COL + D_MODEL


def _params(*sem):
    return pltpu.CompilerParams(dimension_semantics=sem, vmem_limit_bytes=VMEM_LIMIT)


def _rmsnorm_rows(x, g):
    ms = jnp.mean(x * x, axis=-1, keepdims=True)
    return x * lax.rsqrt(ms + NORM_EPS) * g


def _rmsnorm_body(x_ref, g_ref, o_ref):
    o_ref[...] = _rmsnorm_rows(x_ref[...].astype(F32), g_ref[...]).astype(o_ref.dtype)


def _rmsnorm(x, g, out_dtype, tm=1024):
    m, d = x.shape
    return pl.pallas_call(
        _rmsnorm_body,
        grid=(m // tm,),
        in_specs=[pl.BlockSpec((tm, d), lambda i: (i, 0)), pl.BlockSpec((1, d), lambda i: (0, 0))],
        out_specs=pl.BlockSpec((tm, d), lambda i: (i, 0)),
        out_shape=jax.ShapeDtypeStruct((m, d), out_dtype),
        compiler_params=_params("arbitrary"),
        name="rmsnorm",
    )(x, g.reshape(1, d).astype(F32))


def _matmul_body(a_ref, w_ref, o_ref, wb_ref):
    @pl.when(pl.program_id(1) == 0)
    def _():
        wb_ref[...] = w_ref[...].astype(BF16)

    o_ref[...] = jnp.dot(a_ref[...], wb_ref[...], preferred_element_type=F32).astype(o_ref.dtype)


def _matmul(a, w, out_dtype, tm, tn, name):
    m, k = a.shape
    n = w.shape[1]
    return pl.pallas_call(
        _matmul_body,
        grid=(n // tn, m // tm),
        in_specs=[pl.BlockSpec((tm, k), lambda j, i: (i, 0)), pl.BlockSpec((k, tn), lambda j, i: (0, j))],
        out_specs=pl.BlockSpec((tm, tn), lambda j, i: (i, j)),
        out_shape=jax.ShapeDtypeStruct((m, n), out_dtype),
        scratch_shapes=[pltpu.VMEM((k, tn), BF16)],
        compiler_params=_params("arbitrary", "arbitrary"),
        name=name,
    )(a, w)


LOG2_E = 1.4426950408889634
LN_2 = 0.6931471805599453
ATT_BLK = 128
ATT_SUPER = 2048
ATT_DEINT = 4


def _attn_body(q_ref, k_ref, v_ref, o_ref, o_scr, lse_scr, bias_scr, k4, v4, q4, stage):
    sb = pl.program_id(1)
    seq = k_ref.shape[0]
    nd = ATT_DEINT
    sub = ATT_SUPER // nd
    diff = (lax.broadcasted_iota(jnp.int32, (ATT_BLK, 2 * ATT_BLK), 1)
            - lax.broadcasted_iota(jnp.int32, (ATT_BLK, 2 * ATT_BLK), 0))
    neg = jnp.float32(-jnp.inf)
    bias_scr[0] = jnp.where(jnp.logical_and(diff >= 0, diff <= ATT_BLK), 0.0, neg)
    bias_scr[1] = jnp.where(diff <= 0, 0.0, neg)

    @pl.when(sb == 0)
    def _():
        rows = 512

        def split(i, carry):
            for a in range(nd):
                src = pl.ds(i * (rows * nd) + a, rows, stride=nd)
                dst = pl.ds(pl.multiple_of(i * rows, rows), rows)
                k4[a, dst, :] = k_ref[src, :]
                v4[a, dst, :] = v_ref[src, :]
            return carry

        lax.fori_loop(0, seq // nd // rows, split, 0)

    for a in range(nd):
        q4[a] = q_ref[pl.ds(a, sub, stride=nd), :]

    def attend(q, k, v, bias):
        qb = (q * (A_SCALE * LOG2_E)).astype(BF16)
        s = lax.dot_general(qb, k.astype(BF16), (((1,), (1,)), ((), ())), preferred_element_type=F32)
        s = s + bias
        mx = jnp.max(s, axis=-1, keepdims=True)
        p = jnp.exp2(s - mx)
        den = jnp.sum(p, axis=-1, keepdims=True)
        acc = jnp.dot(p.astype(BF16), v.astype(BF16), preferred_element_type=F32)
        return acc / den, jnp.broadcast_to(mx * LN_2 + jnp.log(den), (ATT_BLK, A_HEAD_DIM))

    n_tiles = ATT_SUPER // ATT_BLK
    for c, (window, dil) in enumerate(DILATED_CONFIGS):
        assert window // dil == ATT_BLK and (dil == 1 or dil % nd == 0)
        tiles_per_res = n_tiles // dil
        for t in range(n_tiles):
            n, r = t % tiles_per_res, t // tiles_per_res
            if n == 0:
                first = (sb == 0).astype(jnp.int32)
                back = ATT_BLK * (1 - first)
                bias = bias_scr[first]
            else:
                back, bias = ATT_BLK, bias_scr[0]
            if dil == 1:
                q_rows = pl.ds(n * ATT_BLK, ATT_BLK)
                kv_rows = pl.ds(pl.multiple_of(sb * ATT_SUPER + n * ATT_BLK - back, ATT_BLK), 2 * ATT_BLK)
                o, lse = attend(q_ref[q_rows, :], k_ref[kv_rows, :], v_ref[kv_rows, :], bias)
                out_rows = q_rows
            else:
                st, a, r2 = dil // nd, r % nd, r // nd
                i0 = n * ATT_BLK
                q_rows = pl.ds(st * i0 + r2, ATT_BLK, stride=st) if st > 1 else pl.ds(i0, ATT_BLK)
                kv0 = st * (sb * (ATT_SUPER // dil) + i0 - back) + r2
                kv_rows = pl.ds(kv0, 2 * ATT_BLK, stride=st) if st > 1 else pl.ds(kv0, 2 * ATT_BLK)
                o, lse = attend(q4[a, q_rows, :], k4[a, kv_rows, :], v4[a, kv_rows, :], bias)
                out_rows = (pl.ds(a * sub + st * i0 + r2, ATT_BLK, stride=st) if st > 1
                            else pl.ds(a * sub + i0, ATT_BLK))
            o_scr[c, out_rows, :] = o
            lse_scr[c, out_rows, :] = lse

    rows = 128

    def merge(i, carry):
        for a in range(nd):
            tok = pl.ds(i * (rows * nd) + a, rows, stride=nd)
            cls = pl.ds(a * sub + pl.multiple_of(i * rows, rows), rows)
            l0, l1, l2 = lse_scr[0, tok, :], lse_scr[1, cls, :], lse_scr[2, cls, :]
            m = jnp.maximum(jnp.maximum(l0, l1), l2)
            w0, w1, w2 = jnp.exp(l0 - m), jnp.exp(l1 - m), jnp.exp(l2 - m)
            num = w0 * o_scr[0, tok, :] + w1 * o_scr[1, cls, :] + w2 * o_scr[2, cls, :]
            stage[tok, :] = num / (w0 + w1 + w2)
        return carry

    lax.fori_loop(0, sub // rows, merge, 0)
    o_ref[...] = stage[...].astype(o_ref.dtype)


def _dilated_attention(proj, seq):
    n_super = seq // ATT_SUPER
    blk = lambda off: pl.BlockSpec((seq, A_HEAD_DIM), lambda h, s: (0, off + h))
    return pl.pallas_call(
        _attn_body,
        grid=(A_HEADS, n_super),
        in_specs=[pl.BlockSpec((ATT_SUPER, A_HEAD_DIM), lambda h, s: (s, _QA_BLK + h)), blk(_KA_BLK), blk(_VA_BLK)],
        out_specs=pl.BlockSpec((ATT_SUPER, A_HEAD_DIM), lambda h, s: (s, h)),
        out_shape=jax.ShapeDtypeStruct((seq, A_WIDTH), BF16),
        scratch_shapes=[pltpu.VMEM((len(DILATED_CONFIGS), ATT_SUPER, A_HEAD_DIM), F32),
                        pltpu.VMEM((len(DILATED_CONFIGS), ATT_SUPER, A_HEAD_DIM), F32),
                        pltpu.VMEM((2, ATT_BLK, 2 * ATT_BLK), F32),
                        pltpu.VMEM((ATT_DEINT, seq // ATT_DEINT, A_HEAD_DIM), F32),
                        pltpu.VMEM((ATT_DEINT, seq // ATT_DEINT, A_HEAD_DIM), F32),
                        pltpu.VMEM((ATT_DEINT, ATT_SUPER // ATT_DEINT, A_HEAD_DIM), F32),
                        pltpu.VMEM((ATT_SUPER, A_HEAD_DIM), F32)],
        compiler_params=_params("arbitrary", "arbitrary"),
        name="dilated_attention",
    )(proj, proj, proj)


HG_CHUNK = 64
HG_TB = 1024
HG_HEADS_PER_STEP = 4
HG_CHUNKS_PER_STACK = 1
HG_STACKS_PER_ITER = 8


def _hgrn_ref_rows(bc, half):
    n, width = bc.shape
    blk = 2 * half
    if blk >= SUBLANES:
        rows = [jnp.broadcast_to(bc[b0 + half - 1:b0 + half, :], (blk, width)) for b0 in range(0, n, blk)]
        return jnp.concatenate(rows, axis=0) if len(rows) > 1 else rows[0]
    sub = lax.broadcasted_iota(jnp.int32, (SUBLANES, width), 0)
    groups = []
    for g0 in range(0, n, SUBLANES):
        grp = bc[g0:g0 + SUBLANES, :]
        if half == 1:
            groups.append(jnp.where(jnp.bitwise_and(sub, 1) == 1, pltpu.roll(grp, 1, 0), grp))
        else:
            assert half == 2 and SUBLANES == 8
            groups.append(jnp.where(sub < 4, jnp.broadcast_to(grp[1:2, :], grp.shape),
                                    jnp.broadcast_to(grp[5:6, :], grp.shape)))
    return jnp.concatenate(groups, axis=0)


def _hgrn_body(qb_ref, fb_ref, ib_ref, gb_ref, lbl_ref, g_ref, o_ref, st_ref):
    @pl.when(pl.program_id(1) == 0)
    def _():
        st_ref[...] = jnp.zeros_like(st_ref)

    c_ = HG_CHUNK
    lbl = lbl_ref[...]
    e = jnp.exp(lbl - jnp.max(lbl, axis=0, keepdims=True))
    lb_all = e[0:1, :] / jnp.sum(e, axis=0, keepdims=True)
    g_all = g_ref[...]

    nh, nc = HG_HEADS_PER_STEP, HG_CHUNKS_PER_STACK
    rows = nc * c_
    n = nh * rows
    ti = lax.broadcasted_iota(jnp.int32, (n, n), 0)
    si = lax.broadcasted_iota(jnp.int32, (n, n), 1)
    xor = jnp.bitwise_xor(ti, si)
    causal = jnp.logical_and(xor < c_, si <= ti)
    tri = jnp.where(causal, 1.0, 0.0).astype(BF16)

    def stack_of(ref, r0):
        return jnp.concatenate([ref[pl.ds(r0, rows), h * B_KEY_DIM:(h + 1) * B_KEY_DIM] for h in range(nh)], axis=0)

    def per_head_rows(x):
        return jnp.concatenate([jnp.broadcast_to(x[:, h * B_KEY_DIM:(h + 1) * B_KEY_DIM], (rows, B_KEY_DIM))
                                for h in range(nh)], axis=0)

    lb = per_head_rows(lb_all)
    gnorm = per_head_rows(g_all)

    def stack(i):
        r0 = pl.multiple_of(i * rows, rows)
        f = lb + (1.0 - lb) * jax.nn.sigmoid(stack_of(fb_ref, r0))
        logf = jnp.log(f)
        kk = 1.0 - f
        q = jax.nn.silu(stack_of(qb_ref, r0))
        v = stack_of(ib_ref, r0)
        vb = v.astype(BF16)
        hi = logf.astype(BF16)
        rem = logf - hi.astype(F32)
        mid = rem.astype(BF16)
        low = (rem - mid.astype(F32)).astype(BF16)
        parts = jnp.dot(tri, jnp.concatenate([hi, mid, low], axis=1), preferred_element_type=F32)
        bc = (parts[:, :B_KEY_DIM] + (parts[:, B_KEY_DIM:2 * B_KEY_DIM] + parts[:, 2 * B_KEY_DIM:])) * LOG2_E

        attn = lax.dot_general(q.astype(BF16), kk.astype(BF16), (((1,), (1,)), ((), ())),
                               preferred_element_type=F32)
        half = 1
        while half < c_:
            dec = jnp.exp2(-jnp.abs(bc - _hgrn_ref_rows(bc, half)))
            s = lax.dot_general((q * dec).astype(BF16), (kk * dec).astype(BF16), (((1,), (1,)), ((), ())),
                                preferred_element_type=F32)
            attn = jnp.where(xor >= half, s, attn)
            half *= 2
        attn = jnp.where(causal, attn, 0.0)
        o = jnp.dot(attn.astype(BF16), vb, preferred_element_type=F32)

        last = jnp.concatenate([jnp.broadcast_to(bc[a + c_ - 1:a + c_, :], (c_, B_KEY_DIM))
                                for a in range(0, n, c_)], axis=0)
        q_dec = (q * jnp.exp2(bc)).astype(BF16)
        k_dec = (kk * jnp.exp2(last - bc)).astype(BF16)
        st_dec = jnp.exp2(last)
        inter = []
        for h in range(nh):
            st = st_ref[h]
            for c in range(nc):
                a = h * rows + c * c_
                inter.append(lax.dot_general(q_dec[a:a + c_], st.astype(BF16), (((1,), (1,)), ((), ())),
                                             preferred_element_type=F32))
                upd = lax.dot_general(vb[a:a + c_], k_dec[a:a + c_], (((0,), (0,)), ((), ())),
                                      preferred_element_type=F32)
                st = st * st_dec[a:a + 1] + upd
            st_ref[h] = st
        o = o + jnp.concatenate(inter, axis=0)

        o = o * lax.rsqrt(jnp.mean(o * o, axis=-1, keepdims=True) + NORM_EPS)
        res = (o * gnorm * jax.nn.silu(stack_of(gb_ref, r0))).astype(o_ref.dtype)
        for h in range(nh):
            o_ref[pl.ds(r0, rows), h * B_KEY_DIM:(h + 1) * B_KEY_DIM] = res[h * rows:(h + 1) * rows]

    def stacks(i, carry):
        for u in range(HG_STACKS_PER_ITER):
            stack(i * HG_STACKS_PER_ITER + u)
        return carry

    lax.fori_loop(0, HG_TB // rows // HG_STACKS_PER_ITER, stacks, 0)


def _hgrn2(proj, lb_logits, norm_g, seq):
    hp = HG_HEADS_PER_STEP
    width = hp * B_KEY_DIM
    col = lambda off: pl.BlockSpec((HG_TB, width), lambda h, t: (t, off // hp + h))
    n_lb = lb_logits.shape[0]
    return pl.pallas_call(
        _hgrn_body,
        grid=(B_HEADS // hp, seq // HG_TB),
        in_specs=[col(_QB_BLK), col(_FB_BLK), col(_IB_BLK), col(_GB_BLK),
                  pl.BlockSpec((n_lb, width), lambda h, t: (0, h)),
                  pl.BlockSpec((1, width), lambda h, t: (0, h))],
        out_specs=pl.BlockSpec((HG_TB, width), lambda h, t: (t, h)),
        out_shape=jax.ShapeDtypeStruct((seq, B_WIDTH), BF16),
        scratch_shapes=[pltpu.VMEM((hp, B_KEY_DIM, B_KEY_DIM), F32)],
        compiler_params=_params("arbitrary", "arbitrary"),
        name="hgrn2",
    )(proj, proj, proj, proj, lb_logits.astype(F32), norm_g.reshape(1, B_WIDTH).astype(F32))


MERGE_GATE_BLK = 512


def _merge_body(oa_ref, ob_ref, *rest):
    n_g = (len(rest) - 5) // 2
    ga_refs, gb_refs = rest[:n_g], rest[n_g:2 * n_g]
    wa_ref, wb_ref, o_ref, wa_s, wb_s = rest[2 * n_g:]

    @pl.when(pl.program_id(1) == 0)
    def _():
        wa_s[...] = wa_ref[...].astype(BF16)
        wb_s[...] = wb_ref[...].astype(BF16)

    ya = jnp.dot(oa_ref[...], wa_s[...], preferred_element_type=F32)
    yb = jnp.dot(ob_ref[...], wb_s[...], preferred_element_type=F32)
    ga = jnp.concatenate([r[...] for r in ga_refs], axis=1)
    gb = jnp.concatenate([r[...] for r in gb_refs], axis=1)
    o_ref[...] = (jax.nn.sigmoid(ga) * ya + jax.nn.sigmoid(gb) * yb).astype(o_ref.dtype)


def _branch_merge(oa, ob, proj, wa, wb, tm=512, tn=1024):
    m = oa.shape[0]
    gw = MERGE_GATE_BLK
    n_g = tn // gw
    ga0, gb0 = _GATE_A_COL // gw, _GATE_B_COL // gw
    gate = lambda off, u: pl.BlockSpec((tm, gw), lambda j, i: (i, off + j * n_g + u))
    return pl.pallas_call(
        _merge_body,
        grid=(D_MODEL // tn, m // tm),
        in_specs=[pl.BlockSpec((tm, A_WIDTH), lambda j, i: (i, 0)), pl.BlockSpec((tm, B_WIDTH), lambda j, i: (i, 0)),
                  *[gate(ga0, u) for u in range(n_g)], *[gate(gb0, u) for u in range(n_g)],
                  pl.BlockSpec((A_WIDTH, tn), lambda j, i: (0, j)), pl.BlockSpec((B_WIDTH, tn), lambda j, i: (0, j))],
        out_specs=pl.BlockSpec((tm, tn), lambda j, i: (i, j)),
        out_shape=jax.ShapeDtypeStruct((m, D_MODEL), BF16),
        scratch_shapes=[pltpu.VMEM((A_WIDTH, tn), BF16), pltpu.VMEM((B_WIDTH, tn), BF16)],
        compiler_params=_params("arbitrary", "arbitrary"),
        name="branch_merge",
    )(oa, ob, *([proj] * (2 * n_g)), wa, wb)


def _route_rows(h, g, rw, rb):
    hn = _rmsnorm_rows(h, g)
    hn_hi = hn.astype(BF16)
    hn_lo = (hn - hn_hi.astype(F32)).astype(BF16)
    rw_hi = rw.astype(BF16)
    rw_lo = (rw - rw_hi.astype(F32)).astype(BF16)
    logits = (jnp.dot(hn_hi, rw_hi, preferred_element_type=F32)
              + (jnp.dot(hn_lo, rw_hi, preferred_element_type=F32)
                 + jnp.dot(hn_hi, rw_lo, preferred_element_type=F32))) + rb
    lane = lax.broadcasted_iota(jnp.int32, logits.shape, 1)
    neg = jnp.float32(-jnp.inf)
    big = jnp.int32(LANES)

    is_g = lane < N_GROUPS
    lg = jnp.where(is_g, logits, neg)
    mg = jnp.max(lg, axis=-1, keepdims=True)
    g_idx = jnp.min(jnp.where(lg == mg, lane, big), axis=-1, keepdims=True)
    pg_top = 1.0 / jnp.sum(jnp.where(is_g, jnp.exp(lg - mg), 0.0), axis=-1, keepdims=True)

    lo = N_GROUPS + g_idx * EXPERTS_PER_GROUP
    in_grp = jnp.logical_and(lane >= lo, lane < lo + EXPERTS_PER_GROUP)
    le = jnp.where(in_grp, logits, neg)
    v1 = jnp.max(le, axis=-1, keepdims=True)
    i1 = jnp.min(jnp.where(le == v1, lane, big), axis=-1, keepdims=True)
    le2 = jnp.where(lane == i1, neg, le)
    v2 = jnp.max(le2, axis=-1, keepdims=True)
    i2 = jnp.min(jnp.where(le2 == v2, lane, big), axis=-1, keepdims=True)
    e2 = jnp.exp(v2 - v1)
    w1 = pg_top / (1.0 + e2)
    w2 = pg_top * e2 / (1.0 + e2)
    eid = jnp.where(lane == 0, i1 - N_GROUPS, jnp.where(lane == 1, i2 - N_GROUPS, 0))
    wk = jnp.where(lane == 0, w1, jnp.where(lane == 1, w2, 0.0))
    return hn, eid, wk


def _out_router_body(a_ref, w_ref, x_ref, g_ref, rw_ref, rb_ref, h1_ref, hn_ref, eid_ref, wk_ref, h_prev):
    @pl.when(pl.program_id(0) == 0)
    def _():
        h_prev[...] = jnp.zeros_like(h_prev)

    hn_ref[...], eid_ref[...], wk_ref[...] = _route_rows(h_prev[...], g_ref[...], rw_ref[...], rb_ref[...])
    h1 = x_ref[...] + jnp.dot(a_ref[...], w_ref[...], preferred_element_type=F32)
    h1_ref[...] = h1
    h_prev[...] = h1


def _out_proj_router(a, w_bf16, x, g, rw, rb, tm=512):
    m, d = x.shape
    n_blk = m // tm
    cur = lambda width: pl.BlockSpec((tm, width), lambda i: (jnp.minimum(i, n_blk - 1), 0))
    prev = lambda width: pl.BlockSpec((tm, width), lambda i: (jnp.maximum(i - 1, 0), 0))
    const = lambda shape, **kw: pl.BlockSpec(shape, lambda i: (0, 0), **kw)
    return pl.pallas_call(
        _out_router_body,
        grid=(n_blk + 1,),
        in_specs=[cur(a.shape[1]), const(w_bf16.shape, pipeline_mode=pl.Buffered(1)), cur(d),
                  const((1, d)), const((d, LANES)), const((1, LANES))],
        out_specs=[cur(d), prev(d), prev(LANES), prev(LANES)],
        out_shape=[jax.ShapeDtypeStruct((m, d), F32), jax.ShapeDtypeStruct((m, d), F32),
                   jax.ShapeDtypeStruct((m, LANES), jnp.int32), jax.ShapeDtypeStruct((m, LANES), F32)],
        scratch_shapes=[pltpu.VMEM((tm, d), F32)],
        compiler_params=_params("arbitrary"),
        name="out_proj_router",
    )(a, w_bf16, x, g.reshape(1, d).astype(F32), rw, rb)


MOE_TM = 256
ROW_DMA_PRIORITIES = (0, 1)
MOE_GATHER_GROUP = 8


def _moe_body(tile_e_ref, tile_blk_ref, tile_rows_ref, n_used_ref,
              src0_ref, src_next_ref, dst_prev_ref, hn_ref, w1_ref, w3_ref, w2_ref, y_ref,
              w1_s, w3_s, w2_s, x0, x1, o0, o1, gsem, ssem):
    j = pl.program_id(0)
    n_tiles = pl.num_programs(0)
    n_used = n_used_ref[0]
    xs = (x0, x1)
    os_ = (o0, o1)

    def rows_of(t):
        return jnp.where(t >= 0, tile_rows_ref[jnp.clip(t, 0, n_tiles - 1)], 0)

    def gather_rows_of(t):
        return (rows_of(t) + MOE_GATHER_GROUP - 1) // MOE_GATHER_GROUP * MOE_GATHER_GROUP

    def gather_issue(idx_ref, b, n):
        for r in range(MOE_TM):
            row = idx_ref[0, 0, r]

            @pl.when(r - r % MOE_GATHER_GROUP < n)
            def _(r=r, row=row):
                pltpu.make_async_copy(hn_ref.at[row], xs[b].at[r],
                                      gsem.at[b]).start(priority=ROW_DMA_PRIORITIES[r % 2])

    def scatter_issue(idx_ref, b, n):
        for r in range(MOE_TM):
            row = idx_ref[0, 0, r]

            @pl.when(r < n)
            def _(r=r, row=row):
                pltpu.make_async_copy(os_[b].at[r], y_ref.at[row],
                                      ssem.at[b]).start(priority=ROW_DMA_PRIORITIES[r % 2])

    def wait_rows(make_copy, n):
        p = MOE_TM
        while p >= 1:
            @pl.when(jnp.bitwise_and(n, p) != 0)
            def _(p=p):
                make_copy(p).wait()
            p //= 2

    def gather_wait(b, n):
        wait_rows(lambda p: pltpu.make_async_copy(hn_ref.at[pl.ds(0, p)], xs[b].at[pl.ds(0, p)], gsem.at[b]), n)

    def scatter_wait(b, n):
        wait_rows(lambda p: pltpu.make_async_copy(os_[b].at[pl.ds(0, p)], y_ref.at[pl.ds(0, p)], ssem.at[b]), n)

    @pl.when(j == 0)
    def _():
        x0[...] = jnp.zeros_like(x0)
        x1[...] = jnp.zeros_like(x1)
        gather_issue(src0_ref, 0, gather_rows_of(0))

    for b in (0, 1):
        @pl.when(jnp.logical_and(j < n_used, j % 2 == b))
        def _(b=b):
            gather_wait(b, gather_rows_of(j))
            scatter_wait(b, rows_of(j - 2))

            prev_e = tile_e_ref[jnp.maximum(j - 1, 0)]

            @pl.when(jnp.logical_or(j == 0, tile_e_ref[j] != prev_e))
            def _():
                w1_s[...] = w1_ref[0].astype(BF16)
                w3_s[...] = w3_ref[0].astype(BF16)
                w2_s[...] = w2_ref[0].astype(BF16)

            gather_issue(src_next_ref, 1 - b, gather_rows_of(j + 1))
            scatter_issue(dst_prev_ref, 1 - b, rows_of(j - 1))
            x = xs[b][...].astype(BF16)
            hid = (jax.nn.silu(jnp.dot(x, w1_s[...], preferred_element_type=F32))
                   * jnp.dot(x, w3_s[...], preferred_element_type=F32))
            os_[b][...] = jnp.dot(hid.astype(BF16), w2_s[...], preferred_element_type=F32)

    for b in (0, 1):
        @pl.when(jnp.logical_and(j == n_used, j % 2 == b))
        def _(b=b):
            scatter_wait(b, rows_of(j - 2))
            scatter_issue(dst_prev_ref, 1 - b, rows_of(j - 1))
            scatter_wait(1 - b, rows_of(j - 1))


def _moe_experts(hn, w1, w3, w2, tile_e, tile_blk, tile_rows, n_used, src, dst, n_tokens):
    d = hn.shape[1]
    n_tiles = tile_e.shape[0]
    ff = w1.shape[-1]
    idx_block = lambda fn: pl.BlockSpec((1, 1, MOE_TM), fn, memory_space=pltpu.SMEM)
    grid_spec = pltpu.PrefetchScalarGridSpec(
        num_scalar_prefetch=4,
        grid=(n_tiles,),
        in_specs=[
            idx_block(lambda j, te, tb, tr, nu: (0, 0, 0)),
            idx_block(lambda j, te, tb, tr, nu: (tb[jnp.minimum(j + 1, n_tiles - 1)], 0, 0)),
            idx_block(lambda j, te, tb, tr, nu: (tb[jnp.maximum(j - 1, 0)], 0, 0)),
            pl.BlockSpec(memory_space=pl.ANY),
            pl.BlockSpec((1, d, ff), lambda j, te, tb, tr, nu: (te[j], 0, 0)),
            pl.BlockSpec((1, d, ff), lambda j, te, tb, tr, nu: (te[j], 0, 0)),
            pl.BlockSpec((1, ff, d), lambda j, te, tb, tr, nu: (te[j], 0, 0)),
        ],
        out_specs=pl.BlockSpec(memory_space=pl.ANY),
        scratch_shapes=[pltpu.VMEM((d, ff), BF16), pltpu.VMEM((d, ff), BF16), pltpu.VMEM((ff, d), BF16),
                        pltpu.VMEM((MOE_TM, d), F32), pltpu.VMEM((MOE_TM, d), F32),
                        pltpu.VMEM((MOE_TM, d), F32), pltpu.VMEM((MOE_TM, d), F32),
                        pltpu.SemaphoreType.DMA((2,)), pltpu.SemaphoreType.DMA((2,))],
    )
    src3 = src.reshape(n_tiles, 1, MOE_TM)
    return pl.pallas_call(
        _moe_body,
        grid_spec=grid_spec,
        out_shape=jax.ShapeDtypeStruct((2 * n_tokens, d), F32),
        compiler_params=_params("arbitrary"),
        name="moe_experts",
    )(tile_e, tile_blk, tile_rows, n_used, src3, src3, dst.reshape(n_tiles, 1, MOE_TM), hn, w1, w3, w2)


def _combine_body(h_ref, y0_ref, y1_ref, wk_ref, g_ref, o_ref):
    wk = wk_ref[...]
    y = h_ref[...] + wk[:, 0:1] * y0_ref[...] + wk[:, 1:2] * y1_ref[...]
    o_ref[...] = _rmsnorm_rows(y, g_ref[...]).astype(o_ref.dtype)


def _combine(h, y2, wk, g, out_dtype, tm=512):
    m, d = h.shape
    nblk = m // tm
    return pl.pallas_call(
        _combine_body,
        grid=(nblk,),
        in_specs=[pl.BlockSpec((tm, d), lambda i: (i, 0)), pl.BlockSpec((tm, d), lambda i: (i, 0)),
                  pl.BlockSpec((tm, d), lambda i: (nblk + i, 0)), pl.BlockSpec((tm, LANES), lambda i: (i, 0)),
                  pl.BlockSpec((1, d), lambda i: (0, 0))],
        out_specs=pl.BlockSpec((tm, d), lambda i: (i, 0)),
        out_shape=jax.ShapeDtypeStruct((m, d), out_dtype),
        compiler_params=_params("arbitrary"),
        name="combine_final_norm",
    )(h, y2, y2, wk, g.reshape(1, d).astype(F32))


def _routing_tables(eid, n_tokens):
    n_pairs = 2 * n_tokens
    n_tiles = n_pairs // MOE_TM + N_EXPERTS + 1
    n_rows = n_tiles * MOE_TM
    experts = jnp.arange(N_EXPERTS, dtype=jnp.int32)[None, :]
    e_flat = eid.reshape(n_pairs).astype(jnp.int32)
    counts = jnp.sum(e_flat[:, None] == experts, axis=0, dtype=jnp.int32)
    tiles_per_e = (counts + MOE_TM - 1) // MOE_TM
    tile_end = jnp.cumsum(tiles_per_e)
    tile_start = tile_end - tiles_per_e
    n_used = tile_end[-1]
    filler_end = jnp.cumsum(tiles_per_e * MOE_TM - counts)
    filler = jnp.arange(n_rows - n_pairs, dtype=jnp.int32)[:, None]
    filler_e = jnp.sum(filler_end[None, :] <= filler, axis=1, dtype=jnp.int32)
    keys = jnp.concatenate([2 * e_flat, 2 * filler_e + 1])
    order = jnp.argsort(keys, stable=True).astype(jnp.int32)
    valid = order < n_pairs
    pair = jnp.where(valid, order, 0)
    src_tok = pair // 2
    dst_row = (pair % 2) * n_tokens + pair // 2
    j = jnp.arange(n_tiles, dtype=jnp.int32)
    tile_blk = jnp.minimum(j, n_used - 1)
    tile_e = jnp.sum(tile_end[None, :] <= tile_blk[:, None], axis=1, dtype=jnp.int32)
    own = tile_e[:, None] == experts
    tile_row0 = (tile_blk - jnp.sum(jnp.where(own, tile_start[None, :], 0), axis=1)) * MOE_TM
    own_count = jnp.sum(jnp.where(own, counts[None, :], 0), axis=1)
    tile_rows = jnp.where(j == tile_blk, jnp.clip(own_count - tile_row0, 0, MOE_TM), 0).astype(jnp.int32)
    return (tile_e, tile_blk.astype(jnp.int32), tile_rows, n_used.reshape(1).astype(jnp.int32),
            src_tok.astype(jnp.int32), dst_row.astype(jnp.int32))


def kernel(x, mix_norm_g, w_in, hgrn_lb_logits, hgrn_norm_g, w_branch_a, w_branch_b, w_out, ffn_norm_g,
           router_w_group, router_b_group, router_w_expert, router_b_expert, expert_w1, expert_w3, expert_w2,
           final_norm_g):
    b, seq, d = x.shape
    assert b == 1 and d == D_MODEL and seq % ATT_SUPER == 0 and w_in.shape == (1, D_MODEL, IN_WIDTH)
    h0 = x.reshape(seq, d).astype(F32)

    xn = _rmsnorm(h0, mix_norm_g[0], BF16)
    proj = _matmul(xn, w_in[0], F32, tm=1024, tn=1280, name="in_proj")
    oa = _dilated_attention(proj, seq)
    ob = _hgrn2(proj, hgrn_lb_logits, hgrn_norm_g[0], seq)
    merged = _branch_merge(oa, ob, proj, w_branch_a[0], w_branch_b[0])
    pad = LANES - N_GROUPS - N_EXPERTS
    rw = jnp.concatenate([router_w_group[0], router_w_expert[0], jnp.zeros((d, pad), F32)], axis=1).astype(F32)
    rb = jnp.concatenate([router_b_group[0], router_b_expert[0], jnp.zeros((pad,), F32)]).reshape(1, LANES).astype(F32)
    h1, hn, eid, wk = _out_proj_router(merged, w_out[0].astype(BF16), h0, ffn_norm_g[0], rw, rb)

    tile_e, tile_blk, tile_rows, n_used, src_tok, dst_row = _routing_tables(eid[:, :2], seq)
    w1 = expert_w1[0].reshape(N_EXPERTS, D_MODEL, EXPERT_FF)
    w3 = expert_w3[0].reshape(N_EXPERTS, D_MODEL, EXPERT_FF)
    w2 = expert_w2[0].reshape(N_EXPERTS, EXPERT_FF, D_MODEL)
    y2 = _moe_experts(hn, w1, w3, w2, tile_e, tile_blk, tile_rows, n_used, src_tok, dst_row, seq)

    out = _combine(h1, y2, wk, final_norm_g, x.dtype)
    return out.reshape(b, seq, d)
```

```python
import functools

import jax
import jax.numpy as jnp
from jax import lax
from jax.experimental import pallas as pl
from jax.experimental.pallas import tpu as pltpu

F32 = jnp.float32
BF16 = jnp.bfloat16

D_MODEL = 2048
A_HEADS = 12
A_HEAD_DIM = 128
A_WIDTH = A_HEADS * A_HEAD_DIM
A_SCALE = A_HEAD_DIM ** -0.5
DILATED_CONFIGS = ((128, 1), (512, 4), (2048, 16))
B_HEADS = 8
B_KEY_DIM = 128
B_WIDTH = B_HEADS * B_KEY_DIM
N_GROUPS = 4
EXPERTS_PER_GROUP = 8
N_EXPERTS = N_GROUPS * EXPERTS_PER_GROUP
EXPERT_FF = 512
NORM_EPS = 1e-6

LANES = 128
SUBLANES = 8
VMEM_LIMIT = 56 * 1024 * 1024

_QA_BLK = 0
_KA_BLK = A_HEADS
_VA_BLK = 2 * A_HEADS
_QB_BLK = 3 * A_HEADS
_FB_BLK = _QB_BLK + B_HEADS
_IB_BLK = _FB_BLK + B_HEADS
_GB_BLK = _IB_BLK + B_HEADS
_GATE_A_COL = 3 * A_WIDTH + 4 * B_WIDTH
_GATE_B_COL = _GATE_A_COL + D_MODEL
IN_WIDTH = _GATE_B_COL + D_MODEL


def _params(*sem):
    return pltpu.CompilerParams(dimension_semantics=sem, vmem_limit_bytes=VMEM_LIMIT)


def _rmsnorm_rows(x, g):
    ms = jnp.mean(x * x, axis=-1, keepdims=True)
    return x * lax.rsqrt(ms + NORM_EPS) * g


def _rmsnorm_body(x_ref, g_ref, o_ref):
    o_ref[...] = _rmsnorm_rows(x_ref[...].astype(F32), g_ref[...]).astype(o_ref.dtype)


def _rmsnorm(x, g, out_dtype, tm=1024):
    m, d = x.shape
    return pl.pallas_call(
        _rmsnorm_body,
        grid=(m // tm,),
        in_specs=[pl.BlockSpec((tm, d), lambda i: (i, 0)), pl.BlockSpec((1, d), lambda i: (0, 0))],
        out_specs=pl.BlockSpec((tm, d), lambda i: (i, 0)),
        out_shape=jax.ShapeDtypeStruct((m, d), out_dtype),
        compiler_params=_params("arbitrary"),
        name="rmsnorm",
    )(x, g.reshape(1, d).astype(F32))


def _matmul_body(a_ref, w_ref, o_ref, wb_ref):
    @pl.when(pl.program_id(1) == 0)
    def _():
        wb_ref[...] = w_ref[...].astype(BF16)

    o_ref[...] = jnp.dot(a_ref[...], wb_ref[...], preferred_element_type=F32).astype(o_ref.dtype)


def _matmul(a, w, out_dtype, tm, tn, name):
    m, k = a.shape
    n = w.shape[1]
    return pl.pallas_call(
        _matmul_body,
        grid=(n // tn, m // tm),
        in_specs=[pl.BlockSpec((tm, k), lambda j, i: (i, 0)), pl.BlockSpec((k, tn), lambda j, i: (0, j))],
        out_specs=pl.BlockSpec((tm, tn), lambda j, i: (i, j)),
        out_shape=jax.ShapeDtypeStruct((m, n), out_dtype),
        scratch_shapes=[pltpu.VMEM((k, tn), BF16)],
        compiler_params=_params("arbitrary", "arbitrary"),
        name=name,
    )(a, w)


LOG2_E = 1.4426950408889634
LN_2 = 0.6931471805599453
ATT_BLK = 128
ATT_SUPER = 2048
ATT_DEINT = 4


def _attn_body(q_ref, k_ref, v_ref, o_ref, o_scr, lse_scr, bias_scr, k4, v4, q4, stage):
    sb = pl.program_id(1)
    seq = k_ref.shape[0]
    nd = ATT_DEINT
    sub = ATT_SUPER // nd
    diff = (lax.broadcasted_iota(jnp.int32, (ATT_BLK, 2 * ATT_BLK), 1)
            - lax.broadcasted_iota(jnp.int32, (ATT_BLK, 2 * ATT_BLK), 0))
    neg = jnp.float32(-jnp.inf)
    bias_scr[0] = jnp.where(jnp.logical_and(diff >= 0, diff <= ATT_BLK), 0.0, neg)
    bias_scr[1] = jnp.where(diff <= 0, 0.0, neg)

    @pl.when(sb == 0)
    def _():
        rows = 512

        def split(i, carry):
            for a in range(nd):
                src = pl.ds(i * (rows * nd) + a, rows, stride=nd)
                dst = pl.ds(pl.multiple_of(i * rows, rows), rows)
                k4[a, dst, :] = k_ref[src, :]
                v4[a, dst, :] = v_ref[src, :]
            return carry

        lax.fori_loop(0, seq // nd // rows, split, 0)

    for a in range(nd):
        q4[a] = q_ref[pl.ds(a, sub, stride=nd), :]

    def attend(q, k, v, bias):
        qb = (q * (A_SCALE * LOG2_E)).astype(BF16)
        s = lax.dot_general(qb, k.astype(BF16), (((1,), (1,)), ((), ())), preferred_element_type=F32)
        s = s + bias
        mx = jnp.max(s, axis=-1, keepdims=True)
        p = jnp.exp2(s - mx)
        den = jnp.sum(p, axis=-1, keepdims=True)
        acc = jnp.dot(p.astype(BF16), v.astype(BF16), preferred_element_type=F32)
        return acc / den, jnp.broadcast_to(mx * LN_2 + jnp.log(den), (ATT_BLK, A_HEAD_DIM))

    n_tiles = ATT_SUPER // ATT_BLK
    for c, (window, dil) in enumerate(DILATED_CONFIGS):
        assert window // dil == ATT_BLK and (dil == 1 or dil % nd == 0)
        tiles_per_res = n_tiles // dil
        for t in range(n_tiles):
            n, r = t % tiles_per_res, t // tiles_per_res
            if n == 0:
                first = (sb == 0).astype(jnp.int32)
                back = ATT_BLK * (1 - first)
                bias = bias_scr[first]
            else:
                back, bias = ATT_BLK, bias_scr[0]
            if dil == 1:
                q_rows = pl.ds(n * ATT_BLK, ATT_BLK)
                kv_rows = pl.ds(pl.multiple_of(sb * ATT_SUPER + n * ATT_BLK - back, ATT_BLK), 2 * ATT_BLK)
                o, lse = attend(q_ref[q_rows, :], k_ref[kv_rows, :], v_ref[kv_rows, :], bias)
                out_rows = q_rows
            else:
                st, a, r2 = dil // nd, r % nd, r // nd
                i0 = n * ATT_BLK
                q_rows = pl.ds(st * i0 + r2, ATT_BLK, stride=st) if st > 1 else pl.ds(i0, ATT_BLK)
                kv0 = st * (sb * (ATT_SUPER // dil) + i0 - back) + r2
                kv_rows = pl.ds(kv0, 2 * ATT_BLK, stride=st) if st > 1 else pl.ds(kv0, 2 * ATT_BLK)
                o, lse = attend(q4[a, q_rows, :], k4[a, kv_rows, :], v4[a, kv_rows, :], bias)
                out_rows = (pl.ds(a * sub + st * i0 + r2, ATT_BLK, stride=st) if st > 1
                            else pl.ds(a * sub + i0, ATT_BLK))
            o_scr[c, out_rows, :] = o
            lse_scr[c, out_rows, :] = lse

    rows = 128

    def merge(i, carry):
        for a in range(nd):
            tok = pl.ds(i * (rows * nd) + a, rows, stride=nd)
            cls = pl.ds(a * sub + pl.multiple_of(i * rows, rows), rows)
            l0, l1, l2 = lse_scr[0, tok, :], lse_scr[1, cls, :], lse_scr[2, cls, :]
            m = jnp.maximum(jnp.maximum(l0, l1), l2)
            w0, w1, w2 = jnp.exp(l0 - m), jnp.exp(l1 - m), jnp.exp(l2 - m)
            num = w0 * o_scr[0, tok, :] + w1 * o_scr[1, cls, :] + w2 * o_scr[2, cls, :]
            stage[tok, :] = num / (w0 + w1 + w2)
        return carry

    lax.fori_loop(0, sub // rows, merge, 0)
    o_ref[...] = stage[...].astype(o_ref.dtype)


def _dilated_attention(proj, seq):
    n_super = seq // ATT_SUPER
    blk = lambda off: pl.BlockSpec((seq, A_HEAD_DIM), lambda h, s: (0, off + h))
    return pl.pallas_call(
        _attn_body,
        grid=(A_HEADS, n_super),
        in_specs=[pl.BlockSpec((ATT_SUPER, A_HEAD_DIM), lambda h, s: (s, _QA_BLK + h)), blk(_KA_BLK), blk(_VA_BLK)],
        out_specs=pl.BlockSpec((ATT_SUPER, A_HEAD_DIM), lambda h, s: (s, h)),
        out_shape=jax.ShapeDtypeStruct((seq, A_WIDTH), BF16),
        scratch_shapes=[pltpu.VMEM((len(DILATED_CONFIGS), ATT_SUPER, A_HEAD_DIM), F32),
                        pltpu.VMEM((len(DILATED_CONFIGS), ATT_SUPER, A_HEAD_DIM), F32),
                        pltpu.VMEM((2, ATT_BLK, 2 * ATT_BLK), F32),
                        pltpu.VMEM((ATT_DEINT, seq // ATT_DEINT, A_HEAD_DIM), F32),
                        pltpu.VMEM((ATT_DEINT, seq // ATT_DEINT, A_HEAD_DIM), F32),
                        pltpu.VMEM((ATT_DEINT, ATT_SUPER // ATT_DEINT, A_HEAD_DIM), F32),
                        pltpu.VMEM((ATT_SUPER, A_HEAD_DIM), F32)],
        compiler_params=_params("arbitrary", "arbitrary"),
        name="dilated_attention",
    )(proj, proj, proj)


HG_CHUNK = 64
HG_TB = 1024
HG_HEADS_PER_STEP = 4
HG_CHUNKS_PER_STACK = 1
HG_STACKS_PER_ITER = 8


def _hgrn_ref_rows(bc, half):
    n, width = bc.shape
    blk = 2 * half
    if blk >= SUBLANES:
        rows = [jnp.broadcast_to(bc[b0 + half - 1:b0 + half, :], (blk, width)) for b0 in range(0, n, blk)]
        return jnp.concatenate(rows, axis=0) if len(rows) > 1 else rows[0]
    sub = lax.broadcasted_iota(jnp.int32, (SUBLANES, width), 0)
    groups = []
    for g0 in range(0, n, SUBLANES):
        grp = bc[g0:g0 + SUBLANES, :]
        if half == 1:
            groups.append(jnp.where(jnp.bitwise_and(sub, 1) == 1, pltpu.roll(grp, 1, 0), grp))
        else:
            assert half == 2 and SUBLANES == 8
            groups.append(jnp.where(sub < 4, jnp.broadcast_to(grp[1:2, :], grp.shape),
                                    jnp.broadcast_to(grp[5:6, :], grp.shape)))
    return jnp.concatenate(groups, axis=0)


def _hgrn_body(qb_ref, fb_ref, ib_ref, gb_ref, lbl_ref, g_ref, o_ref, st_ref):
    @pl.when(pl.program_id(1) == 0)
    def _():
        st_ref[...] = jnp.zeros_like(st_ref)

    c_ = HG_CHUNK
    lbl = lbl_ref[...]
    e = jnp.exp(lbl - jnp.max(lbl, axis=0, keepdims=True))
    lb_all = e[0:1, :] / jnp.sum(e, axis=0, keepdims=True)
    g_all = g_ref[...]

    nh, nc = HG_HEADS_PER_STEP, HG_CHUNKS_PER_STACK
    rows = nc * c_
    n = nh * rows
    ti = lax.broadcasted_iota(jnp.int32, (n, n), 0)
    si = lax.broadcasted_iota(jnp.int32, (n, n), 1)
    xor = jnp.bitwise_xor(ti, si)
    causal = jnp.logical_and(xor < c_, si <= ti)
    tri = jnp.where(causal, 1.0, 0.0).astype(BF16)

    def stack_of(ref, r0):
        return jnp.concatenate([ref[pl.ds(r0, rows), h * B_KEY_DIM:(h + 1) * B_KEY_DIM] for h in range(nh)], axis=0)

    def per_head_rows(x):
        return jnp.concatenate([jnp.broadcast_to(x[:, h * B_KEY_DIM:(h + 1) * B_KEY_DIM], (rows, B_KEY_DIM))
                                for h in range(nh)], axis=0)

    lb = per_head_rows(lb_all)
    gnorm = per_head_rows(g_all)

    def stack(i):
        r0 = pl.multiple_of(i * rows, rows)
        f = lb + (1.0 - lb) * jax.nn.sigmoid(stack_of(fb_ref, r0))
        logf = jnp.log(f)
        kk = 1.0 - f
        q = jax.nn.silu(stack_of(qb_ref, r0))
        v = stack_of(ib_ref, r0)
        vb = v.astype(BF16)
        hi = logf.astype(BF16)
        rem = logf - hi.astype(F32)
        mid = rem.astype(BF16)
        low = (rem - mid.astype(F32)).astype(BF16)
        parts = jnp.dot(tri, jnp.concatenate([hi, mid, low], axis=1), preferred_element_type=F32)
        bc = (parts[:, :B_KEY_DIM] + (parts[:, B_KEY_DIM:2 * B_KEY_DIM] + parts[:, 2 * B_KEY_DIM:])) * LOG2_E

        attn = lax.dot_general(q.astype(BF16), kk.astype(BF16), (((1,), (1,)), ((), ())),
                               preferred_element_type=F32)
        half = 1
        while half < c_:
            dec = jnp.exp2(-jnp.abs(bc - _hgrn_ref_rows(bc, half)))
            s = lax.dot_general((q * dec).astype(BF16), (kk * dec).astype(BF16), (((1,), (1,)), ((), ())),
                                preferred_element_type=F32)
            attn = jnp.where(xor >= half, s, attn)
            half *= 2
        attn = jnp.where(causal, attn, 0.0)
        o = jnp.dot(attn.astype(BF16), vb, preferred_element_type=F32)

        last = jnp.concatenate([jnp.broadcast_to(bc[a + c_ - 1:a + c_, :], (c_, B_KEY_DIM))
                                for a in range(0, n, c_)], axis=0)
        q_dec = (q * jnp.exp2(bc)).astype(BF16)
        k_dec = (kk * jnp.exp2(last - bc)).astype(BF16)
        st_dec = jnp.exp2(last)
        inter = []
        for h in range(nh):
            st = st_ref[h]
            for c in range(nc):
                a = h * rows + c * c_
                inter.append(lax.dot_general(q_dec[a:a + c_], st.astype(BF16), (((1,), (1,)), ((), ())),
                                             preferred_element_type=F32))
                upd = lax.dot_general(vb[a:a + c_], k_dec[a:a + c_], (((0,), (0,)), ((), ())),
                                      preferred_element_type=F32)
                st = st * st_dec[a:a + 1] + upd
            st_ref[h] = st
        o = o + jnp.concatenate(inter, axis=0)

        o = o * lax.rsqrt(jnp.mean(o * o, axis=-1, keepdims=True) + NORM_EPS)
        res = (o * gnorm * jax.nn.silu(stack_of(gb_ref, r0))).astype(o_ref.dtype)
        for h in range(nh):
            o_ref[pl.ds(r0, rows), h * B_KEY_DIM:(h + 1) * B_KEY_DIM] = res[h * rows:(h + 1) * rows]

    def stacks(i, carry):
        for u in range(HG_STACKS_PER_ITER):
            stack(i * HG_STACKS_PER_ITER + u)
        return carry

    lax.fori_loop(0, HG_TB // rows // HG_STACKS_PER_ITER, stacks, 0)


def _hgrn2(proj, lb_logits, norm_g, seq):
    hp = HG_HEADS_PER_STEP
    width = hp * B_KEY_DIM
    col = lambda off: pl.BlockSpec((HG_TB, width), lambda h, t: (t, off // hp + h))
    n_lb = lb_logits.shape[0]
    return pl.pallas_call(
        _hgrn_body,
        grid=(B_HEADS // hp, seq // HG_TB),
        in_specs=[col(_QB_BLK), col(_FB_BLK), col(_IB_BLK), col(_GB_BLK),
                  pl.BlockSpec((n_lb, width), lambda h, t: (0, h)),
                  pl.BlockSpec((1, width), lambda h, t: (0, h))],
        out_specs=pl.BlockSpec((HG_TB, width), lambda h, t: (t, h)),
        out_shape=jax.ShapeDtypeStruct((seq, B_WIDTH), BF16),
        scratch_shapes=[pltpu.VMEM((hp, B_KEY_DIM, B_KEY_DIM), F32)],
        compiler_params=_params("arbitrary", "arbitrary"),
        name="hgrn2",
    )(proj, proj, proj, proj, lb_logits.astype(F32), norm_g.reshape(1, B_WIDTH).astype(F32))


MERGE_GATE_BLK = 512


def _merge_body(oa_ref, ob_ref, *rest):
    n_g = (len(rest) - 5) // 2
    ga_refs, gb_refs = rest[:n_g], rest[n_g:2 * n_g]
    wa_ref, wb_ref, o_ref, wa_s, wb_s = rest[2 * n_g:]

    @pl.when(pl.program_id(1) == 0)
    def _():
        wa_s[...] = wa_ref[...].astype(BF16)
        wb_s[...] = wb_ref[...].astype(BF16)

    ya = jnp.dot(oa_ref[...], wa_s[...], preferred_element_type=F32)
    yb = jnp.dot(ob_ref[...], wb_s[...], preferred_element_type=F32)
    ga = jnp.concatenate([r[...] for r in ga_refs], axis=1)
    gb = jnp.concatenate([r[...] for r in gb_refs], axis=1)
    o_ref[...] = (jax.nn.sigmoid(ga) * ya + jax.nn.sigmoid(gb) * yb).astype(o_ref.dtype)


def _branch_merge(oa, ob, proj, wa, wb, tm=512, tn=1024):
    m = oa.shape[0]
    gw = MERGE_GATE_BLK
    n_g = tn // gw
    ga0, gb0 = _GATE_A_COL // gw, _GATE_B_COL // gw
    gate = lambda off, u: pl.BlockSpec((tm, gw), lambda j, i: (i, off + j * n_g + u))
    return pl.pallas_call(
        _merge_body,
        grid=(D_MODEL // tn, m // tm),
        in_specs=[pl.BlockSpec((tm, A_WIDTH), lambda j, i: (i, 0)), pl.BlockSpec((tm, B_WIDTH), lambda j, i: (i, 0)),
                  *[gate(ga0, u) for u in range(n_g)], *[gate(gb0, u) for u in range(n_g)],
                  pl.BlockSpec((A_WIDTH, tn), lambda j, i: (0, j)), pl.BlockSpec((B_WIDTH, tn), lambda j, i: (0, j))],
        out_specs=pl.BlockSpec((tm, tn), lambda j, i: (i, j)),
        out_shape=jax.ShapeDtypeStruct((m, D_MODEL), BF16),
        scratch_shapes=[pltpu.VMEM((A_WIDTH, tn), BF16), pltpu.VMEM((B_WIDTH, tn), BF16)],
        compiler_params=_params("arbitrary", "arbitrary"),
        name="branch_merge",
    )(oa, ob, *([proj] * (2 * n_g)), wa, wb)


def _route_rows(h, g, rw, rb):
    hn = _rmsnorm_rows(h, g)
    hn_hi = hn.astype(BF16)
    hn_lo = (hn - hn_hi.astype(F32)).astype(BF16)
    rw_hi = rw.astype(BF16)
    rw_lo = (rw - rw_hi.astype(F32)).astype(BF16)
    logits = (jnp.dot(hn_hi, rw_hi, preferred_element_type=F32)
              + (jnp.dot(hn_lo, rw_hi, preferred_element_type=F32)
                 + jnp.dot(hn_hi, rw_lo, preferred_element_type=F32))) + rb
    lane = lax.broadcasted_iota(jnp.int32, logits.shape, 1)
    neg = jnp.float32(-jnp.inf)
    big = jnp.int32(LANES)

    is_g = lane < N_GROUPS
    lg = jnp.where(is_g, logits, neg)
    mg = jnp.max(lg, axis=-1, keepdims=True)
    g_idx = jnp.min(jnp.where(lg == mg, lane, big), axis=-1, keepdims=True)
    pg_top = 1.0 / jnp.sum(jnp.where(is_g, jnp.exp(lg - mg), 0.0), axis=-1, keepdims=True)

    lo = N_GROUPS + g_idx * EXPERTS_PER_GROUP
    in_grp = jnp.logical_and(lane >= lo, lane < lo + EXPERTS_PER_GROUP)
    le = jnp.where(in_grp, logits, neg)
    v1 = jnp.max(le, axis=-1, keepdims=True)
    i1 = jnp.min(jnp.where(le == v1, lane, big), axis=-1, keepdims=True)
    le2 = jnp.where(lane == i1, neg, le)
    v2 = jnp.max(le2, axis=-1, keepdims=True)
    i2 = jnp.min(jnp.where(le2 == v2, lane, big), axis=-1, keepdims=True)
    e2 = jnp.exp(v2 - v1)
    w1 = pg_top / (1.0 + e2)
    w2 = pg_top * e2 / (1.0 + e2)
    eid = jnp.where(lane == 0, i1 - N_GROUPS, jnp.where(lane == 1, i2 - N_GROUPS, 0))
    wk = jnp.where(lane == 0, w1, jnp.where(lane == 1, w2, 0.0))
    return hn, eid, wk


def _out_router_body(a_ref, w_ref, x_ref, g_ref, rw_ref, rb_ref, h1_ref, hn_ref, eid_ref, wk_ref, h_prev):
    @pl.when(pl.program_id(0) == 0)
    def _():
        h_prev[...] = jnp.zeros_like(h_prev)

    hn_ref[...], eid_ref[...], wk_ref[...] = _route_rows(h_prev[...], g_ref[...], rw_ref[...], rb_ref[...])
    h1 = x_ref[...] + jnp.dot(a_ref[...], w_ref[...], preferred_element_type=F32)
    h1_ref[...] = h1
    h_prev[...] = h1


def _out_proj_router(a, w_bf16, x, g, rw, rb, tm=512):
    m, d = x.shape
    n_blk = m // tm
    cur = lambda width: pl.BlockSpec((tm, width), lambda i: (jnp.minimum(i, n_blk - 1), 0))
    prev = lambda width: pl.BlockSpec((tm, width), lambda i: (jnp.maximum(i - 1, 0), 0))
    const = lambda shape, **kw: pl.BlockSpec(shape, lambda i: (0, 0), **kw)
    return pl.pallas_call(
        _out_router_body,
        grid=(n_blk + 1,),
        in_specs=[cur(a.shape[1]), const(w_bf16.shape, pipeline_mode=pl.Buffered(1)), cur(d),
                  const((1, d)), const((d, LANES)), const((1, LANES))],
        out_specs=[cur(d), prev(d), prev(LANES), prev(LANES)],
        out_shape=[jax.ShapeDtypeStruct((m, d), F32), jax.ShapeDtypeStruct((m, d), F32),
                   jax.ShapeDtypeStruct((m, LANES), jnp.int32), jax.ShapeDtypeStruct((m, LANES), F32)],
        scratch_shapes=[pltpu.VMEM((tm, d), F32)],
        compiler_params=_params("arbitrary"),
        name="out_proj_router",
    )(a, w_bf16, x, g.reshape(1, d).astype(F32), rw, rb)


MOE_TM = 256
ROW_DMA_PRIORITIES = (0, 1)
MOE_GATHER_GROUP = 8


def _moe_body(tile_e_ref, tile_blk_ref, tile_rows_ref, n_used_ref,
              src0_ref, src_next_ref, dst_prev_ref, hn_ref, w1_ref, w3_ref, w2_ref, y_ref,
              w1_s, w3_s, w2_s, x0, x1, o0, o1, gsem, ssem):
    j = pl.program_id(0)
    n_tiles = pl.num_programs(0)
    n_used = n_used_ref[0]
    xs = (x0, x1)
    os_ = (o0, o1)

    def rows_of(t):
        return jnp.where(t >= 0, tile_rows_ref[jnp.clip(t, 0, n_tiles - 1)], 0)

    def gather_rows_of(t):
        return (rows_of(t) + MOE_GATHER_GROUP - 1) // MOE_GATHER_GROUP * MOE_GATHER_GROUP

    def gather_issue(idx_ref, b, n):
        for r in range(MOE_TM):
            row = idx_ref[0, 0, r]

            @pl.when(r - r % MOE_GATHER_GROUP < n)
            def _(r=r, row=row):
                pltpu.make_async_copy(hn_ref.at[row], xs[b].at[r],
                                      gsem.at[b]).start(priority=ROW_DMA_PRIORITIES[r % 2])

    def scatter_issue(idx_ref, b, n):
        for r in range(MOE_TM):
            row = idx_ref[0, 0, r]

            @pl.when(r < n)
            def _(r=r, row=row):
                pltpu.make_async_copy(os_[b].at[r], y_ref.at[row],
                                      ssem.at[b]).start(priority=ROW_DMA_PRIORITIES[r % 2])

    def wait_rows(make_copy, n):
        p = MOE_TM
        while p >= 1:
            @pl.when(jnp.bitwise_and(n, p) != 0)
            def _(p=p):
                make_copy(p).wait()
            p //= 2

    def gather_wait(b, n):
        wait_rows(lambda p: pltpu.make_async_copy(hn_ref.at[pl.ds(0, p)], xs[b].at[pl.ds(0, p)], gsem.at[b]), n)

    def scatter_wait(b, n):
        wait_rows(lambda p: pltpu.make_async_copy(os_[b].at[pl.ds(0, p)], y_ref.at[pl.ds(0, p)], ssem.at[b]), n)

    @pl.when(j == 0)
    def _():
        x0[...] = jnp.zeros_like(x0)
        x1[...] = jnp.zeros_like(x1)
        gather_issue(src0_ref, 0, gather_rows_of(0))

    for b in (0, 1):
        @pl.when(jnp.logical_and(j < n_used, j % 2 == b))
        def _(b=b):
            prev_e = tile_e_ref[jnp.maximum(j - 1, 0)]

            @pl.when(jnp.logical_or(j == 0, tile_e_ref[j] != prev_e))
            def _():
                w1_s[...] = w1_ref[0].astype(BF16)
                w3_s[...] = w3_ref[0].astype(BF16)
                w2_s[...] = w2_ref[0].astype(BF16)

            gather_wait(b, gather_rows_of(j))
            scatter_wait(b, rows_of(j - 2))

            gather_issue(src_next_ref, 1 - b, gather_rows_of(j + 1))
            scatter_issue(dst_prev_ref, 1 - b, rows_of(j - 1))
            x = xs[b][...].astype(BF16)
            hid = (jax.nn.silu(jnp.dot(x, w1_s[...], preferred_element_type=F32))
                   * jnp.dot(x, w3_s[...], preferred_element_type=F32))
            os_[b][...] = jnp.dot(hid.astype(BF16), w2_s[...], preferred_element_type=F32)

    for b in (0, 1):
        @pl.when(jnp.logical_and(j == n_used, j % 2 == b))
        def _(b=b):
            scatter_wait(b, rows_of(j - 2))
            scatter_issue(dst_prev_ref, 1 - b, rows_of(j - 1))
            scatter_wait(1 - b, rows_of(j - 1))


def _moe_experts(hn, w1, w3, w2, tile_e, tile_blk, tile_rows, n_used, src, dst, n_tokens):
    d = hn.shape[1]
    n_tiles = tile_e.shape[0]
    ff = w1.shape[-1]
    idx_block = lambda fn: pl.BlockSpec((1, 1, MOE_TM), fn, memory_space=pltpu.SMEM)
    grid_spec = pltpu.PrefetchScalarGridSpec(
        num_scalar_prefetch=4,
        grid=(n_tiles,),
        in_specs=[
            idx_block(lambda j, te, tb, tr, nu: (0, 0, 0)),
            idx_block(lambda j, te, tb, tr, nu: (tb[jnp.minimum(j + 1, n_tiles - 1)], 0, 0)),
            idx_block(lambda j, te, tb, tr, nu: (tb[jnp.maximum(j - 1, 0)], 0, 0)),
            pl.BlockSpec(memory_space=pl.ANY),
            pl.BlockSpec((1, d, ff), lambda j, te, tb, tr, nu: (te[j], 0, 0)),
            pl.BlockSpec((1, d, ff), lambda j, te, tb, tr, nu: (te[j], 0, 0)),
            pl.BlockSpec((1, ff, d), lambda j, te, tb, tr, nu: (te[j], 0, 0)),
        ],
        out_specs=pl.BlockSpec(memory_space=pl.ANY),
        scratch_shapes=[pltpu.VMEM((d, ff), BF16), pltpu.VMEM((d, ff), BF16), pltpu.VMEM((ff, d), BF16),
                        pltpu.VMEM((MOE_TM, d), F32), pltpu.VMEM((MOE_TM, d), F32),
                        pltpu.VMEM((MOE_TM, d), F32), pltpu.VMEM((MOE_TM, d), F32),
                        pltpu.SemaphoreType.DMA((2,)), pltpu.SemaphoreType.DMA((2,))],
    )
    src3 = src.reshape(n_tiles, 1, MOE_TM)
    return pl.pallas_call(
        _moe_body,
        grid_spec=grid_spec,
        out_shape=jax.ShapeDtypeStruct((2 * n_tokens, d), F32),
        compiler_params=_params("arbitrary"),
        name="moe_experts",
    )(tile_e, tile_blk, tile_rows, n_used, src3, src3, dst.reshape(n_tiles, 1, MOE_TM), hn, w1, w3, w2)


def _combine_body(h_ref, y0_ref, y1_ref, wk_ref, g_ref, o_ref):
    wk = wk_ref[...]
    y = h_ref[...] + wk[:, 0:1] * y0_ref[...] + wk[:, 1:2] * y1_ref[...]
    o_ref[...] = _rmsnorm_rows(y, g_ref[...]).astype(o_ref.dtype)


def _combine(h, y2, wk, g, out_dtype, tm=512):
    m, d = h.shape
    nblk = m // tm
    return pl.pallas_call(
        _combine_body,
        grid=(nblk,),
        in_specs=[pl.BlockSpec((tm, d), lambda i: (i, 0)), pl.BlockSpec((tm, d), lambda i: (i, 0)),
                  pl.BlockSpec((tm, d), lambda i: (nblk + i, 0)), pl.BlockSpec((tm, LANES), lambda i: (i, 0)),
                  pl.BlockSpec((1, d), lambda i: (0, 0))],
        out_specs=pl.BlockSpec((tm, d), lambda i: (i, 0)),
        out_shape=jax.ShapeDtypeStruct((m, d), out_dtype),
        compiler_params=_params("arbitrary"),
        name="combine_final_norm",
    )(h, y2, y2, wk, g.reshape(1, d).astype(F32))


def _routing_tables(eid, n_tokens):
    n_pairs = 2 * n_tokens
    n_tiles = n_pairs // MOE_TM + N_EXPERTS + 1
    n_rows = n_tiles * MOE_TM
    experts = jnp.arange(N_EXPERTS, dtype=jnp.int32)[None, :]
    e_flat = eid.reshape(n_pairs).astype(jnp.int32)
    counts = jnp.sum(e_flat[:, None] == experts, axis=0, dtype=jnp.int32)
    tiles_per_e = (counts + MOE_TM - 1) // MOE_TM
    tile_end = jnp.cumsum(tiles_per_e)
    tile_start = tile_end - tiles_per_e
    n_used = tile_end[-1]
    filler_end = jnp.cumsum(tiles_per_e * MOE_TM - counts)
    filler = jnp.arange(n_rows - n_pairs, dtype=jnp.int32)[:, None]
    filler_e = jnp.sum(filler_end[None, :] <= filler, axis=1, dtype=jnp.int32)
    keys = jnp.concatenate([2 * e_flat, 2 * filler_e + 1])
    order = jnp.argsort(keys, stable=True).astype(jnp.int32)
    valid = order < n_pairs
    pair = jnp.where(valid, order, 0)
    src_tok = pair // 2
    dst_row = (pair % 2) * n_tokens + pair // 2
    j = jnp.arange(n_tiles, dtype=jnp.int32)
    tile_blk = jnp.minimum(j, n_used - 1)
    tile_e = jnp.sum(tile_end[None, :] <= tile_blk[:, None], axis=1, dtype=jnp.int32)
    own = tile_e[:, None] == experts
    tile_row0 = (tile_blk - jnp.sum(jnp.where(own, tile_start[None, :], 0), axis=1)) * MOE_TM
    own_count = jnp.sum(jnp.where(own, counts[None, :], 0), axis=1)
    tile_rows = jnp.where(j == tile_blk, jnp.clip(own_count - tile_row0, 0, MOE_TM), 0).astype(jnp.int32)
    return (tile_e, tile_blk.astype(jnp.int32), tile_rows, n_used.reshape(1).astype(jnp.int32),
            src_tok.astype(jnp.int32), dst_row.astype(jnp.int32))


def kernel(x, mix_norm_g, w_in, hgrn_lb_logits, hgrn_norm_g, w_branch_a, w_branch_b, w_out, ffn_norm_g,
           router_w_group, router_b_group, router_w_expert, router_b_expert, expert_w1, expert_w3, expert_w2,
           final_norm_g):
    b, seq, d = x.shape
    assert b == 1 and d == D_MODEL and seq % ATT_SUPER == 0 and w_in.shape == (1, D_MODEL, IN_WIDTH)
    h0 = x.reshape(seq, d).astype(F32)

    xn = _rmsnorm(h0, mix_norm_g[0], BF16)
    proj = _matmul(xn, w_in[0], F32, tm=1024, tn=1280, name="in_proj")
    oa = _dilated_attention(proj, seq)
    ob = _hgrn2(proj, hgrn_lb_logits, hgrn_norm_g[0], seq)
    merged = _branch_merge(oa, ob, proj, w_branch_a[0], w_branch_b[0])
    pad = LANES - N_GROUPS - N_EXPERTS
    rw = jnp.concatenate([router_w_group[0], router_w_expert[0], jnp.zeros((d, pad), F32)], axis=1).astype(F32)
    rb = jnp.concatenate([router_b_group[0], router_b_expert[0], jnp.zeros((pad,), F32)]).reshape(1, LANES).astype(F32)
    h1, hn, eid, wk = _out_proj_router(merged, w_out[0].astype(BF16), h0, ffn_norm_g[0], rw, rb)

    tile_e, tile_blk, tile_rows, n_used, src_tok, dst_row = _routing_tables(eid[:, :2], seq)
    w1 = expert_w1[0].reshape(N_EXPERTS, D_MODEL, EXPERT_FF)
    w3 = expert_w3[0].reshape(N_EXPERTS, D_MODEL, EXPERT_FF)
    w2 = expert_w2[0].reshape(N_EXPERTS, EXPERT_FF, D_MODEL)
    y2 = _moe_experts(hn, w1, w3, w2, tile_e, tile_blk, tile_rows, n_used, src_tok, dst_row, seq)

    out = _combine(h1, y2, wk, final_norm_g, x.dtype)
    return out.reshape(b, seq, d)
```

```python
import functools

import jax
import jax.numpy as jnp
from jax import lax
from jax.experimental import pallas as pl
from jax.experimental.pallas import tpu as pltpu

F32 = jnp.float32
BF16 = jnp.bfloat16

D_MODEL = 2048
A_HEADS = 12
A_HEAD_DIM = 128
A_WIDTH = A_HEADS * A_HEAD_DIM
A_SCALE = A_HEAD_DIM ** -0.5
DILATED_CONFIGS = ((128, 1), (512, 4), (2048, 16))
B_HEADS = 8
B_KEY_DIM = 128
B_WIDTH = B_HEADS * B_KEY_DIM
N_GROUPS = 4
EXPERTS_PER_GROUP = 8
N_EXPERTS = N_GROUPS * EXPERTS_PER_GROUP
EXPERT_FF = 512
NORM_EPS = 1e-6

LANES = 128
SUBLANES = 8
VMEM_LIMIT = 56 * 1024 * 1024

_QA_BLK = 0
_KA_BLK = A_HEADS
_VA_BLK = 2 * A_HEADS
_QB_BLK = 3 * A_HEADS
_FB_BLK = _QB_BLK + B_HEADS
_IB_BLK = _FB_BLK + B_HEADS
_GB_BLK = _IB_BLK + B_HEADS
_GATE_A_COL = 3 * A_WIDTH + 4 * B_WIDTH
_GATE_B_COL = _GATE_A_COL + D_MODEL
IN_WIDTH = _GATE_B_COL + D_MODEL


def _params(*sem):
    return pltpu.CompilerParams(dimension_semantics=sem, vmem_limit_bytes=VMEM_LIMIT)


def _rmsnorm_rows(x, g):
    ms = jnp.mean(x * x, axis=-1, keepdims=True)
    return x * lax.rsqrt(ms + NORM_EPS) * g


def _rmsnorm_body(x_ref, g_ref, o_ref):
    o_ref[...] = _rmsnorm_rows(x_ref[...].astype(F32), g_ref[...]).astype(o_ref.dtype)


def _rmsnorm(x, g, out_dtype, tm=1024):
    m, d = x.shape
    return pl.pallas_call(
        _rmsnorm_body,
        grid=(m // tm,),
        in_specs=[pl.BlockSpec((tm, d), lambda i: (i, 0)), pl.BlockSpec((1, d), lambda i: (0, 0))],
        out_specs=pl.BlockSpec((tm, d), lambda i: (i, 0)),
        out_shape=jax.ShapeDtypeStruct((m, d), out_dtype),
        compiler_params=_params("arbitrary"),
        name="rmsnorm",
    )(x, g.reshape(1, d).astype(F32))


def _matmul_body(a_ref, w_ref, o_ref, wb_ref):
    @pl.when(pl.program_id(1) == 0)
    def _():
        wb_ref[...] = w_ref[...].astype(BF16)

    o_ref[...] = jnp.dot(a_ref[...], wb_ref[...], preferred_element_type=F32).astype(o_ref.dtype)


def _matmul(a, w, out_dtype, tm, tn, name):
    m, k = a.shape
    n = w.shape[1]
    return pl.pallas_call(
        _matmul_body,
        grid=(n // tn, m // tm),
        in_specs=[pl.BlockSpec((tm, k), lambda j, i: (i, 0)), pl.BlockSpec((k, tn), lambda j, i: (0, j))],
        out_specs=pl.BlockSpec((tm, tn), lambda j, i: (i, j)),
        out_shape=jax.ShapeDtypeStruct((m, n), out_dtype),
        scratch_shapes=[pltpu.VMEM((k, tn), BF16)],
        compiler_params=_params("arbitrary", "arbitrary"),
        name=name,
    )(a, w)


LOG2_E = 1.4426950408889634
LN_2 = 0.6931471805599453
ATT_BLK = 128
ATT_SUPER = 2048
ATT_DEINT = 4


def _attn_body(q_ref, k_ref, v_ref, o_ref, o_scr, lse_scr, bias_scr, k4, v4, q4, stage):
    sb = pl.program_id(1)
    seq = k_ref.shape[0]
    nd = ATT_DEINT
    sub = ATT_SUPER // nd
    diff = (lax.broadcasted_iota(jnp.int32, (ATT_BLK, 2 * ATT_BLK), 1)
            - lax.broadcasted_iota(jnp.int32, (ATT_BLK, 2 * ATT_BLK), 0))
    neg = jnp.float32(-jnp.inf)
    bias_scr[0] = jnp.where(jnp.logical_and(diff >= 0, diff <= ATT_BLK), 0.0, neg)
    bias_scr[1] = jnp.where(diff <= 0, 0.0, neg)

    @pl.when(sb == 0)
    def _():
        rows = 512

        def split(i, carry):
            for a in range(nd):
                src = pl.ds(i * (rows * nd) + a, rows, stride=nd)
                dst = pl.ds(pl.multiple_of(i * rows, rows), rows)
                k4[a, dst, :] = k_ref[src, :]
                v4[a, dst, :] = v_ref[src, :]
            return carry

        lax.fori_loop(0, seq // nd // rows, split, 0)

    for a in range(nd):
        q4[a] = q_ref[pl.ds(a, sub, stride=nd), :]

    def attend(q, k, v, bias):
        qb = (q * (A_SCALE * LOG2_E)).astype(BF16)
        s = lax.dot_general(qb, k.astype(BF16), (((1,), (1,)), ((), ())), preferred_element_type=F32)
        s = s + bias
        mx = jnp.max(s, axis=-1, keepdims=True)
        p = jnp.exp2(s - mx)
        den = jnp.sum(p, axis=-1, keepdims=True)
        acc = jnp.dot(p.astype(BF16), v.astype(BF16), preferred_element_type=F32)
        return acc / den, jnp.broadcast_to(mx * LN_2 + jnp.log(den), (ATT_BLK, A_HEAD_DIM))

    n_tiles = ATT_SUPER // ATT_BLK
    for c, (window, dil) in enumerate(DILATED_CONFIGS):
        assert window // dil == ATT_BLK and (dil == 1 or dil % nd == 0)
        tiles_per_res = n_tiles // dil
        for t in range(n_tiles):
            n, r = t % tiles_per_res, t // tiles_per_res
            if n == 0:
                first = (sb == 0).astype(jnp.int32)
                back = ATT_BLK * (1 - first)
                bias = bias_scr[first]
            else:
                back, bias = ATT_BLK, bias_scr[0]
            if dil == 1:
                q_rows = pl.ds(n * ATT_BLK, ATT_BLK)
                kv_rows = pl.ds(pl.multiple_of(sb * ATT_SUPER + n * ATT_BLK - back, ATT_BLK), 2 * ATT_BLK)
                o, lse = attend(q_ref[q_rows, :], k_ref[kv_rows, :], v_ref[kv_rows, :], bias)
                out_rows = q_rows
            else:
                st, a, r2 = dil // nd, r % nd, r // nd
                i0 = n * ATT_BLK
                q_rows = pl.ds(st * i0 + r2, ATT_BLK, stride=st) if st > 1 else pl.ds(i0, ATT_BLK)
                kv0 = st * (sb * (ATT_SUPER // dil) + i0 - back) + r2
                kv_rows = pl.ds(kv0, 2 * ATT_BLK, stride=st) if st > 1 else pl.ds(kv0, 2 * ATT_BLK)
                o, lse = attend(q4[a, q_rows, :], k4[a, kv_rows, :], v4[a, kv_rows, :], bias)
                out_rows = (pl.ds(a * sub + st * i0 + r2, ATT_BLK, stride=st) if st > 1
                            else pl.ds(a * sub + i0, ATT_BLK))
            o_scr[c, out_rows, :] = o
            lse_scr[c, out_rows, :] = lse

    rows = 128

    def merge(i, carry):
        for a in range(nd):
            tok = pl.ds(i * (rows * nd) + a, rows, stride=nd)
            cls = pl.ds(a * sub + pl.multiple_of(i * rows, rows), rows)
            l0, l1, l2 = lse_scr[0, tok, :], lse_scr[1, cls, :], lse_scr[2, cls, :]
            m = jnp.maximum(jnp.maximum(l0, l1), l2)
            w0, w1, w2 = jnp.exp(l0 - m), jnp.exp(l1 - m), jnp.exp(l2 - m)
            num = w0 * o_scr[0, tok, :] + w1 * o_scr[1, cls, :] + w2 * o_scr[2, cls, :]
            stage[tok, :] = num / (w0 + w1 + w2)
        return carry

    lax.fori_loop(0, sub // rows, merge, 0)
    o_ref[...] = stage[...].astype(o_ref.dtype)


def _dilated_attention(proj, seq):
    n_super = seq // ATT_SUPER
    blk = lambda off: pl.BlockSpec((seq, A_HEAD_DIM), lambda h, s: (0, off + h))
    return pl.pallas_call(
        _attn_body,
        grid=(A_HEADS, n_super),
        in_specs=[pl.BlockSpec((ATT_SUPER, A_HEAD_DIM), lambda h, s: (s, _QA_BLK + h)), blk(_KA_BLK), blk(_VA_BLK)],
        out_specs=pl.BlockSpec((ATT_SUPER, A_HEAD_DIM), lambda h, s: (s, h)),
        out_shape=jax.ShapeDtypeStruct((seq, A_WIDTH), BF16),
        scratch_shapes=[pltpu.VMEM((len(DILATED_CONFIGS), ATT_SUPER, A_HEAD_DIM), F32),
                        pltpu.VMEM((len(DILATED_CONFIGS), ATT_SUPER, A_HEAD_DIM), F32),
                        pltpu.VMEM((2, ATT_BLK, 2 * ATT_BLK), F32),
                        pltpu.VMEM((ATT_DEINT, seq // ATT_DEINT, A_HEAD_DIM), F32),
                        pltpu.VMEM((ATT_DEINT, seq // ATT_DEINT, A_HEAD_DIM), F32),
                        pltpu.VMEM((ATT_DEINT, ATT_SUPER // ATT_DEINT, A_HEAD_DIM), F32),
                        pltpu.VMEM((ATT_SUPER, A_HEAD_DIM), F32)],
        compiler_params=_params("arbitrary", "arbitrary"),
        name="dilated_attention",
    )(proj, proj, proj)


HG_CHUNK = 64
HG_TB = 1024
HG_HEADS_PER_STEP = 4
HG_CHUNKS_PER_STACK = 1
HG_STACKS_PER_ITER = 8


def _hgrn_ref_rows(bc, half):
    n, width = bc.shape
    blk = 2 * half
    if blk >= SUBLANES:
        rows = [jnp.broadcast_to(bc[b0 + half - 1:b0 + half, :], (blk, width)) for b0 in range(0, n, blk)]
        return jnp.concatenate(rows, axis=0) if len(rows) > 1 else rows[0]
    sub = lax.broadcasted_iota(jnp.int32, (SUBLANES, width), 0)
    groups = []
    for g0 in range(0, n, SUBLANES):
        grp = bc[g0:g0 + SUBLANES, :]
        if half == 1:
            groups.append(jnp.where(jnp.bitwise_and(sub, 1) == 1, pltpu.roll(grp, 1, 0), grp))
        else:
            assert half == 2 and SUBLANES == 8
            groups.append(jnp.where(sub < 4, jnp.broadcast_to(grp[1:2, :], grp.shape),
                                    jnp.broadcast_to(grp[5:6, :], grp.shape)))
    return jnp.concatenate(groups, axis=0)


def _hgrn_body(qb_ref, fb_ref, ib_ref, gb_ref, lbl_ref, g_ref, o_ref, st_ref):
    @pl.when(pl.program_id(1) == 0)
    def _():
        st_ref[...] = jnp.zeros_like(st_ref)

    c_ = HG_CHUNK
    lbl = lbl_ref[...]
    e = jnp.exp(lbl - jnp.max(lbl, axis=0, keepdims=True))
    lb_all = e[0:1, :] / jnp.sum(e, axis=0, keepdims=True)
    g_all = g_ref[...]

    nh, nc = HG_HEADS_PER_STEP, HG_CHUNKS_PER_STACK
    rows = nc * c_
    n = nh * rows
    ti = lax.broadcasted_iota(jnp.int32, (n, n), 0)
    si = lax.broadcasted_iota(jnp.int32, (n, n), 1)
    xor = jnp.bitwise_xor(ti, si)
    causal = jnp.logical_and(xor < c_, si <= ti)
    tri = jnp.where(causal, 1.0, 0.0).astype(BF16)

    def stack_of(ref, r0):
        return jnp.concatenate([ref[pl.ds(r0, rows), h * B_KEY_DIM:(h + 1) * B_KEY_DIM] for h in range(nh)], axis=0)

    def per_head_rows(x):
        return jnp.concatenate([jnp.broadcast_to(x[:, h * B_KEY_DIM:(h + 1) * B_KEY_DIM], (rows, B_KEY_DIM))
                                for h in range(nh)], axis=0)

    lb = per_head_rows(lb_all)
    gnorm = per_head_rows(g_all)

    def stack(i):
        r0 = pl.multiple_of(i * rows, rows)
        f = lb + (1.0 - lb) * jax.nn.sigmoid(stack_of(fb_ref, r0))
        logf = jnp.log(f)
        kk = 1.0 - f
        q = jax.nn.silu(stack_of(qb_ref, r0))
        v = stack_of(ib_ref, r0)
        vb = v.astype(BF16)
        hi = logf.astype(BF16)
        rem = logf - hi.astype(F32)
        mid = rem.astype(BF16)
        low = (rem - mid.astype(F32)).astype(BF16)
        parts = jnp.dot(tri, jnp.concatenate([hi, mid, low], axis=1), preferred_element_type=F32)
        bc = (parts[:, :B_KEY_DIM] + (parts[:, B_KEY_DIM:2 * B_KEY_DIM] + parts[:, 2 * B_KEY_DIM:])) * LOG2_E

        attn = lax.dot_general(q.astype(BF16), kk.astype(BF16), (((1,), (1,)), ((), ())),
                               preferred_element_type=F32)
        half = 1
        while half < c_:
            dec = jnp.exp2(-jnp.abs(bc - _hgrn_ref_rows(bc, half)))
            s = lax.dot_general((q * dec).astype(BF16), (kk * dec).astype(BF16), (((1,), (1,)), ((), ())),
                                preferred_element_type=F32)
            attn = jnp.where(xor >= half, s, attn)
            half *= 2
        attn = jnp.where(causal, attn, 0.0)
        o = jnp.dot(attn.astype(BF16), vb, preferred_element_type=F32)

        last = jnp.concatenate([jnp.broadcast_to(bc[a + c_ - 1:a + c_, :], (c_, B_KEY_DIM))
                                for a in range(0, n, c_)], axis=0)
        q_dec = (q * jnp.exp2(bc)).astype(BF16)
        k_dec = (kk * jnp.exp2(last - bc)).astype(BF16)
        st_dec = jnp.exp2(last)
        inter = []
        for h in range(nh):
            st = st_ref[h]
            for c in range(nc):
                a = h * rows + c * c_
                inter.append(lax.dot_general(q_dec[a:a + c_], st.astype(BF16), (((1,), (1,)), ((), ())),
                                             preferred_element_type=F32))
                upd = lax.dot_general(vb[a:a + c_], k_dec[a:a + c_], (((0,), (0,)), ((), ())),
                                      preferred_element_type=F32)
                st = st * st_dec[a:a + 1] + upd
            st_ref[h] = st
        o = o + jnp.concatenate(inter, axis=0)

        o = o * lax.rsqrt(jnp.mean(o * o, axis=-1, keepdims=True) + NORM_EPS)
        res = (o * gnorm * jax.nn.silu(stack_of(gb_ref, r0))).astype(o_ref.dtype)
        for h in range(nh):
            o_ref[pl.ds(r0, rows), h * B_KEY_DIM:(h + 1) * B_KEY_DIM] = res[h * rows:(h + 1) * rows]

    def stacks(i, carry):
        for u in range(HG_STACKS_PER_ITER):
            stack(i * HG_STACKS_PER_ITER + u)
        return carry

    lax.fori_loop(0, HG_TB // rows // HG_STACKS_PER_ITER, stacks, 0)


def _hgrn2(proj, lb_logits, norm_g, seq):
    hp = HG_HEADS_PER_STEP
    width = hp * B_KEY_DIM
    col = lambda off: pl.BlockSpec((HG_TB, width), lambda h, t: (t, off // hp + h))
    n_lb = lb_logits.shape[0]
    return pl.pallas_call(
        _hgrn_body,
        grid=(B_HEADS // hp, seq // HG_TB),
        in_specs=[col(_QB_BLK), col(_FB_BLK), col(_IB_BLK), col(_GB_BLK),
                  pl.BlockSpec((n_lb, width), lambda h, t: (0, h)),
                  pl.BlockSpec((1, width), lambda h, t: (0, h))],
        out_specs=pl.BlockSpec((HG_TB, width), lambda h, t: (t, h)),
        out_shape=jax.ShapeDtypeStruct((seq, B_WIDTH), BF16),
        scratch_shapes=[pltpu.VMEM((hp, B_KEY_DIM, B_KEY_DIM), F32)],
        compiler_params=_params("arbitrary", "arbitrary"),
        name="hgrn2",
    )(proj, proj, proj, proj, lb_logits.astype(F32), norm_g.reshape(1, B_WIDTH).astype(F32))


MERGE_GATE_BLK = 512


def _merge_body(oa_ref, ob_ref, *rest):
    n_g = (len(rest) - 5) // 2
    ga_refs, gb_refs = rest[:n_g], rest[n_g:2 * n_g]
    wa_ref, wb_ref, o_ref, wa_s, wb_s = rest[2 * n_g:]

    @pl.when(pl.program_id(1) == 0)
    def _():
        wa_s[...] = wa_ref[...].astype(BF16)
        wb_s[...] = wb_ref[...].astype(BF16)

    ya = jnp.dot(oa_ref[...], wa_s[...], preferred_element_type=F32)
    yb = jnp.dot(ob_ref[...], wb_s[...], preferred_element_type=F32)
    ga = jnp.concatenate([r[...] for r in ga_refs], axis=1)
    gb = jnp.concatenate([r[...] for r in gb_refs], axis=1)
    o_ref[...] = (jax.nn.sigmoid(ga) * ya + jax.nn.sigmoid(gb) * yb).astype(o_ref.dtype)


def _branch_merge(oa, ob, proj, wa, wb, tm=512, tn=1024):
    m = oa.shape[0]
    gw = MERGE_GATE_BLK
    n_g = tn // gw
    ga0, gb0 = _GATE_A_COL // gw, _GATE_B_COL // gw
    gate = lambda off, u: pl.BlockSpec((tm, gw), lambda j, i: (i, off + j * n_g + u))
    return pl.pallas_call(
        _merge_body,
        grid=(D_MODEL // tn, m // tm),
        in_specs=[pl.BlockSpec((tm, A_WIDTH), lambda j, i: (i, 0)), pl.BlockSpec((tm, B_WIDTH), lambda j, i: (i, 0)),
                  *[gate(ga0, u) for u in range(n_g)], *[gate(gb0, u) for u in range(n_g)],
                  pl.BlockSpec((A_WIDTH, tn), lambda j, i: (0, j)), pl.BlockSpec((B_WIDTH, tn), lambda j, i: (0, j))],
        out_specs=pl.BlockSpec((tm, tn), lambda j, i: (i, j)),
        out_shape=jax.ShapeDtypeStruct((m, D_MODEL), BF16),
        scratch_shapes=[pltpu.VMEM((A_WIDTH, tn), BF16), pltpu.VMEM((B_WIDTH, tn), BF16)],
        compiler_params=_params("arbitrary", "arbitrary"),
        name="branch_merge",
    )(oa, ob, *([proj] * (2 * n_g)), wa, wb)


def _route_rows(h, g, rw, rb):
    hn = _rmsnorm_rows(h, g)
    hn_hi = hn.astype(BF16)
    hn_lo = (hn - hn_hi.astype(F32)).astype(BF16)
    rw_hi = rw.astype(BF16)
    rw_lo = (rw - rw_hi.astype(F32)).astype(BF16)
    logits = (jnp.dot(hn_hi, rw_hi, preferred_element_type=F32)
              + (jnp.dot(hn_lo, rw_hi, preferred_element_type=F32)
                 + jnp.dot(hn_hi, rw_lo, preferred_element_type=F32))) + rb
    lane = lax.broadcasted_iota(jnp.int32, logits.shape, 1)
    neg = jnp.float32(-jnp.inf)
    big = jnp.int32(LANES)

    is_g = lane < N_GROUPS
    lg = jnp.where(is_g, logits, neg)
    mg = jnp.max(lg, axis=-1, keepdims=True)
    g_idx = jnp.min(jnp.where(lg == mg, lane, big), axis=-1, keepdims=True)
    pg_top = 1.0 / jnp.sum(jnp.where(is_g, jnp.exp(lg - mg), 0.0), axis=-1, keepdims=True)

    lo = N_GROUPS + g_idx * EXPERTS_PER_GROUP
    in_grp = jnp.logical_and(lane >= lo, lane < lo + EXPERTS_PER_GROUP)
    le = jnp.where(in_grp, logits, neg)
    v1 = jnp.max(le, axis=-1, keepdims=True)
    i1 = jnp.min(jnp.where(le == v1, lane, big), axis=-1, keepdims=True)
    le2 = jnp.where(lane == i1, neg, le)
    v2 = jnp.max(le2, axis=-1, keepdims=True)
    i2 = jnp.min(jnp.where(le2 == v2, lane, big), axis=-1, keepdims=True)
    e2 = jnp.exp(v2 - v1)
    w1 = pg_top / (1.0 + e2)
    w2 = pg_top * e2 / (1.0 + e2)
    eid = jnp.where(lane == 0, i1 - N_GROUPS, jnp.where(lane == 1, i2 - N_GROUPS, 0))
    wk = jnp.where(lane == 0, w1, jnp.where(lane == 1, w2, 0.0))
    return hn, eid, wk


def _out_router_body(a_ref, w_ref, x_ref, g_ref, rw_ref, rb_ref, h1_ref, hn_ref, eid_ref, wk_ref, h_prev):
    @pl.when(pl.program_id(0) == 0)
    def _():
        h_prev[...] = jnp.zeros_like(h_prev)

    hn_ref[...], eid_ref[...], wk_ref[...] = _route_rows(h_prev[...], g_ref[...], rw_ref[...], rb_ref[...])
    h1 = x_ref[...] + jnp.dot(a_ref[...], w_ref[...], preferred_element_type=F32)
    h1_ref[...] = h1
    h_prev[...] = h1


def _out_proj_router(a, w_bf16, x, g, rw, rb, tm=512):
    m, d = x.shape
    n_blk = m // tm
    cur = lambda width: pl.BlockSpec((tm, width), lambda i: (jnp.minimum(i, n_blk - 1), 0))
    prev = lambda width: pl.BlockSpec((tm, width), lambda i: (jnp.maximum(i - 1, 0), 0))
    const = lambda shape, **kw: pl.BlockSpec(shape, lambda i: (0, 0), **kw)
    return pl.pallas_call(
        _out_router_body,
        grid=(n_blk + 1,),
        in_specs=[cur(a.shape[1]), const(w_bf16.shape, pipeline_mode=pl.Buffered(1)), cur(d),
                  const((1, d)), const((d, LANES)), const((1, LANES))],
        out_specs=[cur(d), prev(d), prev(LANES), prev(LANES)],
        out_shape=[jax.ShapeDtypeStruct((m, d), F32), jax.ShapeDtypeStruct((m, d), F32),
                   jax.ShapeDtypeStruct((m, LANES), jnp.int32), jax.ShapeDtypeStruct((m, LANES), F32)],
        scratch_shapes=[pltpu.VMEM((tm, d), F32)],
        compiler_params=_params("arbitrary"),
        name="out_proj_router",
    )(a, w_bf16, x, g.reshape(1, d).astype(F32), rw, rb)


MOE_TM = 256
ROW_DMA_PRIORITIES = (0, 1)
MOE_GATHER_GROUP = 8


def _moe_body(tile_e_ref, tile_blk_ref, tile_rows_ref, n_used_ref,
              src0_ref, src_next_ref, dst_prev_ref, hn_ref, w1_ref, w3_ref, w2_ref, y_ref,
              w1_s, w3_s, w2_s, x0, x1, o0, o1, hid_s, gsem, ssem):
    j = pl.program_id(0)
    n_tiles = pl.num_programs(0)
    n_used = n_used_ref[0]
    xs = (x0, x1)
    os_ = (o0, o1)

    def rows_of(t):
        return jnp.where(t >= 0, tile_rows_ref[jnp.clip(t, 0, n_tiles - 1)], 0)

    def gather_rows_of(t):
        return (rows_of(t) + MOE_GATHER_GROUP - 1) // MOE_GATHER_GROUP * MOE_GATHER_GROUP

    def gather_issue(idx_ref, b, n):
        for r in range(MOE_TM):
            row = idx_ref[0, 0, r]

            @pl.when(r - r % MOE_GATHER_GROUP < n)
            def _(r=r, row=row):
                pltpu.make_async_copy(hn_ref.at[row], xs[b].at[r],
                                      gsem.at[b]).start(priority=ROW_DMA_PRIORITIES[r % 2])

    def scatter_issue(idx_ref, b, n):
        for r in range(MOE_TM):
            row = idx_ref[0, 0, r]

            @pl.when(r < n)
            def _(r=r, row=row):
                pltpu.make_async_copy(os_[b].at[r], y_ref.at[row],
                                      ssem.at[b]).start(priority=ROW_DMA_PRIORITIES[r % 2])

    def wait_rows(make_copy, n):
        p = MOE_TM
        while p >= 1:
            @pl.when(jnp.bitwise_and(n, p) != 0)
            def _(p=p):
                make_copy(p).wait()
            p //= 2

    def gather_wait(b, n):
        wait_rows(lambda p: pltpu.make_async_copy(hn_ref.at[pl.ds(0, p)], xs[b].at[pl.ds(0, p)], gsem.at[b]), n)

    def scatter_wait(b, n):
        wait_rows(lambda p: pltpu.make_async_copy(os_[b].at[pl.ds(0, p)], y_ref.at[pl.ds(0, p)], ssem.at[b]), n)

    @pl.when(j == 0)
    def _():
        x0[...] = jnp.zeros_like(x0)
        x1[...] = jnp.zeros_like(x1)
        gather_issue(src0_ref, 0, gather_rows_of(0))

    for b in (0, 1):
        @pl.when(jnp.logical_and(j < n_used, j % 2 == b))
        def _(b=b):
            prev_e = tile_e_ref[jnp.maximum(j - 1, 0)]

            @pl.when(jnp.logical_or(j == 0, tile_e_ref[j] != prev_e))
            def _():
                w1_s[...] = w1_ref[0].astype(BF16)
                w3_s[...] = w3_ref[0].astype(BF16)
                w2_s[...] = w2_ref[0].astype(BF16)

            gather_wait(b, gather_rows_of(j))

            gather_issue(src_next_ref, 1 - b, gather_rows_of(j + 1))
            scatter_issue(dst_prev_ref, 1 - b, rows_of(j - 1))
            x = xs[b][...].astype(BF16)
            hid_s[...] = (jax.nn.silu(jnp.dot(x, w1_s[...], preferred_element_type=F32))
                          * jnp.dot(x, w3_s[...], preferred_element_type=F32)).astype(BF16)
            scatter_wait(b, rows_of(j - 2))
            os_[b][...] = jnp.dot(hid_s[...], w2_s[...], preferred_element_type=F32)

    for b in (0, 1):
        @pl.when(jnp.logical_and(j == n_used, j % 2 == b))
        def _(b=b):
            scatter_wait(b, rows_of(j - 2))
            scatter_issue(dst_prev_ref, 1 - b, rows_of(j - 1))
            scatter_wait(1 - b, rows_of(j - 1))


def _moe_experts(hn, w1, w3, w2, tile_e, tile_blk, tile_rows, n_used, src, dst, n_tokens):
    d = hn.shape[1]
    n_tiles = tile_e.shape[0]
    ff = w1.shape[-1]
    idx_block = lambda fn: pl.BlockSpec((1, 1, MOE_TM), fn, memory_space=pltpu.SMEM)
    grid_spec = pltpu.PrefetchScalarGridSpec(
        num_scalar_prefetch=4,
        grid=(n_tiles,),
        in_specs=[
            idx_block(lambda j, te, tb, tr, nu: (0, 0, 0)),
            idx_block(lambda j, te, tb, tr, nu: (tb[jnp.minimum(j + 1, n_tiles - 1)], 0, 0)),
            idx_block(lambda j, te, tb, tr, nu: (tb[jnp.maximum(j - 1, 0)], 0, 0)),
            pl.BlockSpec(memory_space=pl.ANY),
            pl.BlockSpec((1, d, ff), lambda j, te, tb, tr, nu: (te[j], 0, 0)),
            pl.BlockSpec((1, d, ff), lambda j, te, tb, tr, nu: (te[j], 0, 0)),
            pl.BlockSpec((1, ff, d), lambda j, te, tb, tr, nu: (te[j], 0, 0)),
        ],
        out_specs=pl.BlockSpec(memory_space=pl.ANY),
        scratch_shapes=[pltpu.VMEM((d, ff), BF16), pltpu.VMEM((d, ff), BF16), pltpu.VMEM((ff, d), BF16),
                        pltpu.VMEM((MOE_TM, d), F32), pltpu.VMEM((MOE_TM, d), F32),
                        pltpu.VMEM((MOE_TM, d), F32), pltpu.VMEM((MOE_TM, d), F32),
                        pltpu.VMEM((MOE_TM, ff), BF16),
                        pltpu.SemaphoreType.DMA((2,)), pltpu.SemaphoreType.DMA((2,))],
    )
    src3 = src.reshape(n_tiles, 1, MOE_TM)
    return pl.pallas_call(
        _moe_body,
        grid_spec=grid_spec,
        out_shape=jax.ShapeDtypeStruct((2 * n_tokens, d), F32),
        compiler_params=_params("arbitrary"),
        name="moe_experts",
    )(tile_e, tile_blk, tile_rows, n_used, src3, src3, dst.reshape(n_tiles, 1, MOE_TM), hn, w1, w3, w2)


def _combine_body(h_ref, y0_ref, y1_ref, wk_ref, g_ref, o_ref):
    wk = wk_ref[...]
    y = h_ref[...] + wk[:, 0:1] * y0_ref[...] + wk[:, 1:2] * y1_ref[...]
    o_ref[...] = _rmsnorm_rows(y, g_ref[...]).astype(o_ref.dtype)


def _combine(h, y2, wk, g, out_dtype, tm=512):
    m, d = h.shape
    nblk = m // tm
    return pl.pallas_call(
        _combine_body,
        grid=(nblk,),
        in_specs=[pl.BlockSpec((tm, d), lambda i: (i, 0)), pl.BlockSpec((tm, d), lambda i: (i, 0)),
                  pl.BlockSpec((tm, d), lambda i: (nblk + i, 0)), pl.BlockSpec((tm, LANES), lambda i: (i, 0)),
                  pl.BlockSpec((1, d), lambda i: (0, 0))],
        out_specs=pl.BlockSpec((tm, d), lambda i: (i, 0)),
        out_shape=jax.ShapeDtypeStruct((m, d), out_dtype),
        compiler_params=_params("arbitrary"),
        name="combine_final_norm",
    )(h, y2, y2, wk, g.reshape(1, d).astype(F32))


def _routing_tables(eid, n_tokens):
    n_pairs = 2 * n_tokens
    n_tiles = n_pairs // MOE_TM + N_EXPERTS + 1
    n_rows = n_tiles * MOE_TM
    experts = jnp.arange(N_EXPERTS, dtype=jnp.int32)[None, :]
    e_flat = eid.reshape(n_pairs).astype(jnp.int32)
    counts = jnp.sum(e_flat[:, None] == experts, axis=0, dtype=jnp.int32)
    tiles_per_e = (counts + MOE_TM - 1) // MOE_TM
    tile_end = jnp.cumsum(tiles_per_e)
    tile_start = tile_end - tiles_per_e
    n_used = tile_end[-1]
    filler_end = jnp.cumsum(tiles_per_e * MOE_TM - counts)
    filler = jnp.arange(n_rows - n_pairs, dtype=jnp.int32)[:, None]
    filler_e = jnp.sum(filler_end[None, :] <= filler, axis=1, dtype=jnp.int32)
    keys = jnp.concatenate([2 * e_flat, 2 * filler_e + 1])
    order = jnp.argsort(keys, stable=True).astype(jnp.int32)
    valid = order < n_pairs
    pair = jnp.where(valid, order, 0)
    src_tok = pair // 2
    dst_row = (pair % 2) * n_tokens + pair // 2
    j = jnp.arange(n_tiles, dtype=jnp.int32)
    tile_blk = jnp.minimum(j, n_used - 1)
    tile_e = jnp.sum(tile_end[None, :] <= tile_blk[:, None], axis=1, dtype=jnp.int32)
    own = tile_e[:, None] == experts
    tile_row0 = (tile_blk - jnp.sum(jnp.where(own, tile_start[None, :], 0), axis=1)) * MOE_TM
    own_count = jnp.sum(jnp.where(own, counts[None, :], 0), axis=1)
    tile_rows = jnp.where(j == tile_blk, jnp.clip(own_count - tile_row0, 0, MOE_TM), 0).astype(jnp.int32)
    return (tile_e, tile_blk.astype(jnp.int32), tile_rows, n_used.reshape(1).astype(jnp.int32),
            src_tok.astype(jnp.int32), dst_row.astype(jnp.int32))


def kernel(x, mix_norm_g, w_in, hgrn_lb_logits, hgrn_norm_g, w_branch_a, w_branch_b, w_out, ffn_norm_g,
           router_w_group, router_b_group, router_w_expert, router_b_expert, expert_w1, expert_w3, expert_w2,
           final_norm_g):
    b, seq, d = x.shape
    assert b == 1 and d == D_MODEL and seq % ATT_SUPER == 0 and w_in.shape == (1, D_MODEL, IN_WIDTH)
    h0 = x.reshape(seq, d).astype(F32)

    xn = _rmsnorm(h0, mix_norm_g[0], BF16)
    proj = _matmul(xn, w_in[0], F32, tm=1024, tn=1280, name="in_proj")
    oa = _dilated_attention(proj, seq)
    ob = _hgrn2(proj, hgrn_lb_logits, hgrn_norm_g[0], seq)
    merged = _branch_merge(oa, ob, proj, w_branch_a[0], w_branch_b[0])
    pad = LANES - N_GROUPS - N_EXPERTS
    rw = jnp.concatenate([router_w_group[0], router_w_expert[0], jnp.zeros((d, pad), F32)], axis=1).astype(F32)
    rb = jnp.concatenate([router_b_group[0], router_b_expert[0], jnp.zeros((pad,), F32)]).reshape(1, LANES).astype(F32)
    h1, hn, eid, wk = _out_proj_router(merged, w_out[0].astype(BF16), h0, ffn_norm_g[0], rw, rb)

    tile_e, tile_blk, tile_rows, n_used, src_tok, dst_row = _routing_tables(eid[:, :2], seq)
    w1 = expert_w1[0].reshape(N_EXPERTS, D_MODEL, EXPERT_FF)
    w3 = expert_w3[0].reshape(N_EXPERTS, D_MODEL, EXPERT_FF)
    w2 = expert_w2[0].reshape(N_EXPERTS, EXPERT_FF, D_MODEL)
    y2 = _moe_experts(hn, w1, w3, w2, tile_e, tile_blk, tile_rows, n_used, src_tok, dst_row, seq)

    out = _combine(h1, y2, wk, final_norm_g, x.dtype)
    return out.reshape(b, seq, d)
```

```python
import functools

import jax
import jax.numpy as jnp
from jax import lax
from jax.experimental import pallas as pl
from jax.experimental.pallas import tpu as pltpu

F32 = jnp.float32
BF16 = jnp.bfloat16

D_MODEL = 2048
A_HEADS = 12
A_HEAD_DIM = 128
A_WIDTH = A_HEADS * A_HEAD_DIM
A_SCALE = A_HEAD_DIM ** -0.5
DILATED_CONFIGS = ((128, 1), (512, 4), (2048, 16))
B_HEADS = 8
B_KEY_DIM = 128
B_WIDTH = B_HEADS * B_KEY_DIM
N_GROUPS = 4
EXPERTS_PER_GROUP = 8
N_EXPERTS = N_GROUPS * EXPERTS_PER_GROUP
EXPERT_FF = 512
NORM_EPS = 1e-6

LANES = 128
SUBLANES = 8
VMEM_LIMIT = 56 * 1024 * 1024

_QA_BLK = 0
_KA_BLK = A_HEADS
_VA_BLK = 2 * A_HEADS
_QB_BLK = 3 * A_HEADS
_FB_BLK = _QB_BLK + B_HEADS
_IB_BLK = _FB_BLK + B_HEADS
_GB_BLK = _IB_BLK + B_HEADS
_GATE_A_COL = 3 * A_WIDTH + 4 * B_WIDTH
_GATE_B_COL = _GATE_A_COL + D_MODEL
IN_WIDTH = _GATE_B_COL + D_MODEL


def _params(*sem):
    return pltpu.CompilerParams(dimension_semantics=sem, vmem_limit_bytes=VMEM_LIMIT)


def _rmsnorm_rows(x, g):
    ms = jnp.mean(x * x, axis=-1, keepdims=True)
    return x * lax.rsqrt(ms + NORM_EPS) * g


def _rmsnorm_body(x_ref, g_ref, o_ref):
    o_ref[...] = _rmsnorm_rows(x_ref[...].astype(F32), g_ref[...]).astype(o_ref.dtype)


def _rmsnorm(x, g, out_dtype, tm=1024):
    m, d = x.shape
    return pl.pallas_call(
        _rmsnorm_body,
        grid=(m // tm,),
        in_specs=[pl.BlockSpec((tm, d), lambda i: (i, 0)), pl.BlockSpec((1, d), lambda i: (0, 0))],
        out_specs=pl.BlockSpec((tm, d), lambda i: (i, 0)),
        out_shape=jax.ShapeDtypeStruct((m, d), out_dtype),
        compiler_params=_params("arbitrary"),
        name="rmsnorm",
    )(x, g.reshape(1, d).astype(F32))


def _matmul_body(a_ref, w_ref, o_ref, wb_ref):
    @pl.when(pl.program_id(1) == 0)
    def _():
        wb_ref[...] = w_ref[...].astype(BF16)

    o_ref[...] = jnp.dot(a_ref[...], wb_ref[...], preferred_element_type=F32).astype(o_ref.dtype)


def _matmul(a, w, out_dtype, tm, tn, name):
    m, k = a.shape
    n = w.shape[1]
    return pl.pallas_call(
        _matmul_body,
        grid=(n // tn, m // tm),
        in_specs=[pl.BlockSpec((tm, k), lambda j, i: (i, 0)), pl.BlockSpec((k, tn), lambda j, i: (0, j))],
        out_specs=pl.BlockSpec((tm, tn), lambda j, i: (i, j)),
        out_shape=jax.ShapeDtypeStruct((m, n), out_dtype),
        scratch_shapes=[pltpu.VMEM((k, tn), BF16)],
        compiler_params=_params("arbitrary", "arbitrary"),
        name=name,
    )(a, w)


LOG2_E = 1.4426950408889634
LN_2 = 0.6931471805599453
ATT_BLK = 128
ATT_SUPER = 2048
ATT_DEINT = 4


def _attn_body(q_ref, k_ref, v_ref, o_ref, o_scr, lse_scr, bias_scr, k4, v4, q4, stage):
    sb = pl.program_id(1)
    seq = k_ref.shape[0]
    nd = ATT_DEINT
    sub = ATT_SUPER // nd
    diff = (lax.broadcasted_iota(jnp.int32, (ATT_BLK, 2 * ATT_BLK), 1)
            - lax.broadcasted_iota(jnp.int32, (ATT_BLK, 2 * ATT_BLK), 0))
    neg = jnp.float32(-jnp.inf)
    bias_scr[0] = jnp.where(jnp.logical_and(diff >= 0, diff <= ATT_BLK), 0.0, neg)
    bias_scr[1] = jnp.where(diff <= 0, 0.0, neg)

    @pl.when(sb == 0)
    def _():
        rows = 512

        def split(i, carry):
            for a in range(nd):
                src = pl.ds(i * (rows * nd) + a, rows, stride=nd)
                dst = pl.ds(pl.multiple_of(i * rows, rows), rows)
                k4[a, dst, :] = k_ref[src, :]
                v4[a, dst, :] = v_ref[src, :]
            return carry

        lax.fori_loop(0, seq // nd // rows, split, 0)

    for a in range(nd):
        q4[a] = q_ref[pl.ds(a, sub, stride=nd), :]

    def attend(q, k, v, bias):
        qb = (q * (A_SCALE * LOG2_E)).astype(BF16)
        s = lax.dot_general(qb, k.astype(BF16), (((1,), (1,)), ((), ())), preferred_element_type=F32)
        s = s + bias
        mx = jnp.max(s, axis=-1, keepdims=True)
        p = jnp.exp2(s - mx)
        den = jnp.sum(p, axis=-1, keepdims=True)
        acc = jnp.dot(p.astype(BF16), v.astype(BF16), preferred_element_type=F32)
        return acc / den, jnp.broadcast_to(mx * LN_2 + jnp.log(den), (ATT_BLK, A_HEAD_DIM))

    n_tiles = ATT_SUPER // ATT_BLK
    for c, (window, dil) in enumerate(DILATED_CONFIGS):
        assert window // dil == ATT_BLK and (dil == 1 or dil % nd == 0)
        tiles_per_res = n_tiles // dil
        for t in range(n_tiles):
            n, r = t % tiles_per_res, t // tiles_per_res
            if n == 0:
                first = (sb == 0).astype(jnp.int32)
                back = ATT_BLK * (1 - first)
                bias = bias_scr[first]
            else:
                back, bias = ATT_BLK, bias_scr[0]
            if dil == 1:
                q_rows = pl.ds(n * ATT_BLK, ATT_BLK)
                kv_rows = pl.ds(pl.multiple_of(sb * ATT_SUPER + n * ATT_BLK - back, ATT_BLK), 2 * ATT_BLK)
                o, lse = attend(q_ref[q_rows, :], k_ref[kv_rows, :], v_ref[kv_rows, :], bias)
                out_rows = q_rows
            else:
                st, a, r2 = dil // nd, r % nd, r // nd
                i0 = n * ATT_BLK
                q_rows = pl.ds(st * i0 + r2, ATT_BLK, stride=st) if st > 1 else pl.ds(i0, ATT_BLK)
                kv0 = st * (sb * (ATT_SUPER // dil) + i0 - back) + r2
                kv_rows = pl.ds(kv0, 2 * ATT_BLK, stride=st) if st > 1 else pl.ds(kv0, 2 * ATT_BLK)
                o, lse = attend(q4[a, q_rows, :], k4[a, kv_rows, :], v4[a, kv_rows, :], bias)
                out_rows = (pl.ds(a * sub + st * i0 + r2, ATT_BLK, stride=st) if st > 1
                            else pl.ds(a * sub + i0, ATT_BLK))
            o_scr[c, out_rows, :] = o
            lse_scr[c, out_rows, :] = lse

    rows = 128

    def merge(i, carry):
        for a in range(nd):
            tok = pl.ds(i * (rows * nd) + a, rows, stride=nd)
            cls = pl.ds(a * sub + pl.multiple_of(i * rows, rows), rows)
            l0, l1, l2 = lse_scr[0, tok, :], lse_scr[1, cls, :], lse_scr[2, cls, :]
            m = jnp.maximum(jnp.maximum(l0, l1), l2)
            w0, w1, w2 = jnp.exp(l0 - m), jnp.exp(l1 - m), jnp.exp(l2 - m)
            num = w0 * o_scr[0, tok, :] + w1 * o_scr[1, cls, :] + w2 * o_scr[2, cls, :]
            stage[tok, :] = num / (w0 + w1 + w2)
        return carry

    lax.fori_loop(0, sub // rows, merge, 0)
    o_ref[...] = stage[...].astype(o_ref.dtype)


def _dilated_attention(proj, seq):
    n_super = seq // ATT_SUPER
    blk = lambda off: pl.BlockSpec((seq, A_HEAD_DIM), lambda h, s: (0, off + h))
    return pl.pallas_call(
        _attn_body,
        grid=(A_HEADS, n_super),
        in_specs=[pl.BlockSpec((ATT_SUPER, A_HEAD_DIM), lambda h, s: (s, _QA_BLK + h)), blk(_KA_BLK), blk(_VA_BLK)],
        out_specs=pl.BlockSpec((ATT_SUPER, A_HEAD_DIM), lambda h, s: (s, h)),
        out_shape=jax.ShapeDtypeStruct((seq, A_WIDTH), BF16),
        scratch_shapes=[pltpu.VMEM((len(DILATED_CONFIGS), ATT_SUPER, A_HEAD_DIM), F32),
                        pltpu.VMEM((len(DILATED_CONFIGS), ATT_SUPER, A_HEAD_DIM), F32),
                        pltpu.VMEM((2, ATT_BLK, 2 * ATT_BLK), F32),
                        pltpu.VMEM((ATT_DEINT, seq // ATT_DEINT, A_HEAD_DIM), F32),
                        pltpu.VMEM((ATT_DEINT, seq // ATT_DEINT, A_HEAD_DIM), F32),
                        pltpu.VMEM((ATT_DEINT, ATT_SUPER // ATT_DEINT, A_HEAD_DIM), F32),
                        pltpu.VMEM((ATT_SUPER, A_HEAD_DIM), F32)],
        compiler_params=_params("arbitrary", "arbitrary"),
        name="dilated_attention",
    )(proj, proj, proj)


HG_CHUNK = 64
HG_TB = 1024
HG_HEADS_PER_STEP = 4
HG_CHUNKS_PER_STACK = 1
HG_STACKS_PER_ITER = 8


def _hgrn_ref_rows(bc, half):
    n, width = bc.shape
    blk = 2 * half
    if blk >= SUBLANES:
        rows = [jnp.broadcast_to(bc[b0 + half - 1:b0 + half, :], (blk, width)) for b0 in range(0, n, blk)]
        return jnp.concatenate(rows, axis=0) if len(rows) > 1 else rows[0]
    sub = lax.broadcasted_iota(jnp.int32, (SUBLANES, width), 0)
    groups = []
    for g0 in range(0, n, SUBLANES):
        grp = bc[g0:g0 + SUBLANES, :]
        if half == 1:
            groups.append(jnp.where(jnp.bitwise_and(sub, 1) == 1, pltpu.roll(grp, 1, 0), grp))
        else:
            assert half == 2 and SUBLANES == 8
            groups.append(jnp.where(sub < 4, jnp.broadcast_to(grp[1:2, :], grp.shape),
                                    jnp.broadcast_to(grp[5:6, :], grp.shape)))
    return jnp.concatenate(groups, axis=0)


def _hgrn_body(qb_ref, fb_ref, ib_ref, gb_ref, lbl_ref, g_ref, o_ref, st_ref):
    @pl.when(pl.program_id(1) == 0)
    def _():
        st_ref[...] = jnp.zeros_like(st_ref)

    c_ = HG_CHUNK
    lbl = lbl_ref[...]
    e = jnp.exp(lbl - jnp.max(lbl, axis=0, keepdims=True))
    lb_all = e[0:1, :] / jnp.sum(e, axis=0, keepdims=True)
    g_all = g_ref[...]

    nh, nc = HG_HEADS_PER_STEP, HG_CHUNKS_PER_STACK
    rows = nc * c_
    n = nh * rows
    ti = lax.broadcasted_iota(jnp.int32, (n, n), 0)
    si = lax.broadcasted_iota(jnp.int32, (n, n), 1)
    xor = jnp.bitwise_xor(ti, si)
    causal = jnp.logical_and(xor < c_, si <= ti)
    tri = jnp.where(causal, 1.0, 0.0).astype(BF16)

    def stack_of(ref, r0):
        return jnp.concatenate([ref[pl.ds(r0, rows), h * B_KEY_DIM:(h + 1) * B_KEY_DIM] for h in range(nh)], axis=0)

    def per_head_rows(x):
        return jnp.concatenate([jnp.broadcast_to(x[:, h * B_KEY_DIM:(h + 1) * B_KEY_DIM], (rows, B_KEY_DIM))
                                for h in range(nh)], axis=0)

    lb = per_head_rows(lb_all)
    gnorm = per_head_rows(g_all)

    def stack(i):
        r0 = pl.multiple_of(i * rows, rows)
        f = lb + (1.0 - lb) * jax.nn.sigmoid(stack_of(fb_ref, r0))
        logf = jnp.log(f)
        kk = 1.0 - f
        q = jax.nn.silu(stack_of(qb_ref, r0))
        v = stack_of(ib_ref, r0)
        vb = v.astype(BF16)
        hi = logf.astype(BF16)
        rem = logf - hi.astype(F32)
        mid = rem.astype(BF16)
        low = (rem - mid.astype(F32)).astype(BF16)
        parts = jnp.dot(tri, jnp.concatenate([hi, mid, low], axis=1), preferred_element_type=F32)
        bc = (parts[:, :B_KEY_DIM] + (parts[:, B_KEY_DIM:2 * B_KEY_DIM] + parts[:, 2 * B_KEY_DIM:])) * LOG2_E

        attn = lax.dot_general(q.astype(BF16), kk.astype(BF16), (((1,), (1,)), ((), ())),
                               preferred_element_type=F32)
        half = 1
        while half < c_:
            dec = jnp.exp2(-jnp.abs(bc - _hgrn_ref_rows(bc, half)))
            s = lax.dot_general((q * dec).astype(BF16), (kk * dec).astype(BF16), (((1,), (1,)), ((), ())),
                                preferred_element_type=F32)
            attn = jnp.where(xor >= half, s, attn)
            half *= 2
        attn = jnp.where(causal, attn, 0.0)
        o = jnp.dot(attn.astype(BF16), vb, preferred_element_type=F32)

        last = jnp.concatenate([jnp.broadcast_to(bc[a + c_ - 1:a + c_, :], (c_, B_KEY_DIM))
                                for a in range(0, n, c_)], axis=0)
        q_dec = (q * jnp.exp2(bc)).astype(BF16)
        k_dec = (kk * jnp.exp2(last - bc)).astype(BF16)
        st_dec = jnp.exp2(last)
        inter = []
        for h in range(nh):
            st = st_ref[h]
            for c in range(nc):
                a = h * rows + c * c_
                inter.append(lax.dot_general(q_dec[a:a + c_], st.astype(BF16), (((1,), (1,)), ((), ())),
                                             preferred_element_type=F32))
                upd = lax.dot_general(vb[a:a + c_], k_dec[a:a + c_], (((0,), (0,)), ((), ())),
                                      preferred_element_type=F32)
                st = st * st_dec[a:a + 1] + upd
            st_ref[h] = st
        o = o + jnp.concatenate(inter, axis=0)

        o = o * lax.rsqrt(jnp.mean(o * o, axis=-1, keepdims=True) + NORM_EPS)
        res = (o * gnorm * jax.nn.silu(stack_of(gb_ref, r0))).astype(o_ref.dtype)
        for h in range(nh):
            o_ref[pl.ds(r0, rows), h * B_KEY_DIM:(h + 1) * B_KEY_DIM] = res[h * rows:(h + 1) * rows]

    def stacks(i, carry):
        for u in range(HG_STACKS_PER_ITER):
            stack(i * HG_STACKS_PER_ITER + u)
        return carry

    lax.fori_loop(0, HG_TB // rows // HG_STACKS_PER_ITER, stacks, 0)


def _hgrn2(proj, lb_logits, norm_g, seq):
    hp = HG_HEADS_PER_STEP
    width = hp * B_KEY_DIM
    col = lambda off: pl.BlockSpec((HG_TB, width), lambda h, t: (t, off // hp + h))
    n_lb = lb_logits.shape[0]
    return pl.pallas_call(
        _hgrn_body,
        grid=(B_HEADS // hp, seq // HG_TB),
        in_specs=[col(_QB_BLK), col(_FB_BLK), col(_IB_BLK), col(_GB_BLK),
                  pl.BlockSpec((n_lb, width), lambda h, t: (0, h)),
                  pl.BlockSpec((1, width), lambda h, t: (0, h))],
        out_specs=pl.BlockSpec((HG_TB, width), lambda h, t: (t, h)),
        out_shape=jax.ShapeDtypeStruct((seq, B_WIDTH), BF16),
        scratch_shapes=[pltpu.VMEM((hp, B_KEY_DIM, B_KEY_DIM), F32)],
        compiler_params=_params("arbitrary", "arbitrary"),
        name="hgrn2",
    )(proj, proj, proj, proj, lb_logits.astype(F32), norm_g.reshape(1, B_WIDTH).astype(F32))


MERGE_GATE_BLK = 512


def _merge_body(oa_ref, ob_ref, *rest):
    n_g = (len(rest) - 5) // 2
    ga_refs, gb_refs = rest[:n_g], rest[n_g:2 * n_g]
    wa_ref, wb_ref, o_ref, wa_s, wb_s = rest[2 * n_g:]

    @pl.when(pl.program_id(1) == 0)
    def _():
        wa_s[...] = wa_ref[...].astype(BF16)
        wb_s[...] = wb_ref[...].astype(BF16)

    ya = jnp.dot(oa_ref[...], wa_s[...], preferred_element_type=F32)
    yb = jnp.dot(ob_ref[...], wb_s[...], preferred_element_type=F32)
    ga = jnp.concatenate([r[...] for r in ga_refs], axis=1)
    gb = jnp.concatenate([r[...] for r in gb_refs], axis=1)
    o_ref[...] = (jax.nn.sigmoid(ga) * ya + jax.nn.sigmoid(gb) * yb).astype(o_ref.dtype)


def _branch_merge(oa, ob, proj, wa, wb, tm=512, tn=1024):
    m = oa.shape[0]
    gw = MERGE_GATE_BLK
    n_g = tn // gw
    ga0, gb0 = _GATE_A_COL // gw, _GATE_B_COL // gw
    gate = lambda off, u: pl.BlockSpec((tm, gw), lambda j, i: (i, off + j * n_g + u))
    return pl.pallas_call(
        _merge_body,
        grid=(D_MODEL // tn, m // tm),
        in_specs=[pl.BlockSpec((tm, A_WIDTH), lambda j, i: (i, 0)), pl.BlockSpec((tm, B_WIDTH), lambda j, i: (i, 0)),
                  *[gate(ga0, u) for u in range(n_g)], *[gate(gb0, u) for u in range(n_g)],
                  pl.BlockSpec((A_WIDTH, tn), lambda j, i: (0, j)), pl.BlockSpec((B_WIDTH, tn), lambda j, i: (0, j))],
        out_specs=pl.BlockSpec((tm, tn), lambda j, i: (i, j)),
        out_shape=jax.ShapeDtypeStruct((m, D_MODEL), BF16),
        scratch_shapes=[pltpu.VMEM((A_WIDTH, tn), BF16), pltpu.VMEM((B_WIDTH, tn), BF16)],
        compiler_params=_params("arbitrary", "arbitrary"),
        name="branch_merge",
    )(oa, ob, *([proj] * (2 * n_g)), wa, wb)


def _route_rows(h, g, rw, rb):
    hn = _rmsnorm_rows(h, g)
    hn_hi = hn.astype(BF16)
    hn_lo = (hn - hn_hi.astype(F32)).astype(BF16)
    rw_hi = rw.astype(BF16)
    rw_lo = (rw - rw_hi.astype(F32)).astype(BF16)
    logits = (jnp.dot(hn_hi, rw_hi, preferred_element_type=F32)
              + (jnp.dot(hn_lo, rw_hi, preferred_element_type=F32)
                 + jnp.dot(hn_hi, rw_lo, preferred_element_type=F32))) + rb
    lane = lax.broadcasted_iota(jnp.int32, logits.shape, 1)
    neg = jnp.float32(-jnp.inf)
    big = jnp.int32(LANES)

    is_g = lane < N_GROUPS
    lg = jnp.where(is_g, logits, neg)
    mg = jnp.max(lg, axis=-1, keepdims=True)
    g_idx = jnp.min(jnp.where(lg == mg, lane, big), axis=-1, keepdims=True)
    pg_top = 1.0 / jnp.sum(jnp.where(is_g, jnp.exp(lg - mg), 0.0), axis=-1, keepdims=True)

    lo = N_GROUPS + g_idx * EXPERTS_PER_GROUP
    in_grp = jnp.logical_and(lane >= lo, lane < lo + EXPERTS_PER_GROUP)
    le = jnp.where(in_grp, logits, neg)
    v1 = jnp.max(le, axis=-1, keepdims=True)
    i1 = jnp.min(jnp.where(le == v1, lane, big), axis=-1, keepdims=True)
    le2 = jnp.where(lane == i1, neg, le)
    v2 = jnp.max(le2, axis=-1, keepdims=True)
    i2 = jnp.min(jnp.where(le2 == v2, lane, big), axis=-1, keepdims=True)
    e2 = jnp.exp(v2 - v1)
    w1 = pg_top / (1.0 + e2)
    w2 = pg_top * e2 / (1.0 + e2)
    eid = jnp.where(lane == 0, i1 - N_GROUPS, jnp.where(lane == 1, i2 - N_GROUPS, 0))
    wk = jnp.where(lane == 0, w1, jnp.where(lane == 1, w2, 0.0))
    return hn, eid, wk


def _out_router_body(a_ref, w_ref, x_ref, g_ref, rw_ref, rb_ref, h1_ref, hn_ref, eid_ref, wk_ref, h_prev):
    @pl.when(pl.program_id(0) == 0)
    def _():
        h_prev[...] = jnp.zeros_like(h_prev)

    hn_ref[...], eid_ref[...], wk_ref[...] = _route_rows(h_prev[...], g_ref[...], rw_ref[...], rb_ref[...])
    h1 = x_ref[...] + jnp.dot(a_ref[...], w_ref[...], preferred_element_type=F32)
    h1_ref[...] = h1
    h_prev[...] = h1


def _out_proj_router(a, w_bf16, x, g, rw, rb, tm=512):
    m, d = x.shape
    n_blk = m // tm
    cur = lambda width: pl.BlockSpec((tm, width), lambda i: (jnp.minimum(i, n_blk - 1), 0))
    prev = lambda width: pl.BlockSpec((tm, width), lambda i: (jnp.maximum(i - 1, 0), 0))
    const = lambda shape, **kw: pl.BlockSpec(shape, lambda i: (0, 0), **kw)
    return pl.pallas_call(
        _out_router_body,
        grid=(n_blk + 1,),
        in_specs=[cur(a.shape[1]), const(w_bf16.shape, pipeline_mode=pl.Buffered(1)), cur(d),
                  const((1, d)), const((d, LANES)), const((1, LANES))],
        out_specs=[cur(d), prev(d), prev(LANES), prev(LANES)],
        out_shape=[jax.ShapeDtypeStruct((m, d), F32), jax.ShapeDtypeStruct((m, d), F32),
                   jax.ShapeDtypeStruct((m, LANES), jnp.int32), jax.ShapeDtypeStruct((m, LANES), F32)],
        scratch_shapes=[pltpu.VMEM((tm, d), F32)],
        compiler_params=_params("arbitrary"),
        name="out_proj_router",
    )(a, w_bf16, x, g.reshape(1, d).astype(F32), rw, rb)


MOE_TM = 256
ROW_DMA_PRIORITIES = (0, 1)
MOE_GATHER_GROUP = 8


def _moe_body(tile_e_ref, tile_blk_ref, tile_rows_ref, n_used_ref,
              src0_ref, src_next_ref, dst_prev_ref, hn_ref, w1_ref, w3_ref, w2_ref, y_ref,
              w1_s, w3_s, w2_s, w1_f, w3_f, w2_f, x0, x1, o0, o1, gsem, ssem, wsem):
    j = pl.program_id(0)
    n_tiles = pl.num_programs(0)
    n_used = n_used_ref[0]
    xs = (x0, x1)
    os_ = (o0, o1)

    def rows_of(t):
        return jnp.where(t >= 0, tile_rows_ref[jnp.clip(t, 0, n_tiles - 1)], 0)

    def gather_rows_of(t):
        return (rows_of(t) + MOE_GATHER_GROUP - 1) // MOE_GATHER_GROUP * MOE_GATHER_GROUP

    def gather_issue(idx_ref, b, n):
        for r in range(MOE_TM):
            row = idx_ref[0, 0, r]

            @pl.when(r - r % MOE_GATHER_GROUP < n)
            def _(r=r, row=row):
                pltpu.make_async_copy(hn_ref.at[row], xs[b].at[r],
                                      gsem.at[b]).start(priority=ROW_DMA_PRIORITIES[r % 2])

    def scatter_issue(idx_ref, b, n):
        for r in range(MOE_TM):
            row = idx_ref[0, 0, r]

            @pl.when(r < n)
            def _(r=r, row=row):
                pltpu.make_async_copy(os_[b].at[r], y_ref.at[row],
                                      ssem.at[b]).start(priority=ROW_DMA_PRIORITIES[r % 2])

    def wait_rows(make_copy, n):
        p = MOE_TM
        while p >= 1:
            @pl.when(jnp.bitwise_and(n, p) != 0)
            def _(p=p):
                make_copy(p).wait()
            p //= 2

    def gather_wait(b, n):
        wait_rows(lambda p: pltpu.make_async_copy(hn_ref.at[pl.ds(0, p)], xs[b].at[pl.ds(0, p)], gsem.at[b]), n)

    def scatter_wait(b, n):
        wait_rows(lambda p: pltpu.make_async_copy(os_[b].at[pl.ds(0, p)], y_ref.at[pl.ds(0, p)], ssem.at[b]), n)

    @pl.when(j == 0)
    def _():
        x0[...] = jnp.zeros_like(x0)
        x1[...] = jnp.zeros_like(x1)
        gather_issue(src0_ref, 0, gather_rows_of(0))

    for b in (0, 1):
        @pl.when(jnp.logical_and(j < n_used, j % 2 == b))
        def _(b=b):
            cur_e = tile_e_ref[j]
            prev_e = tile_e_ref[jnp.maximum(j - 1, 0)]
            next_e = tile_e_ref[n_tiles + j]

            def weight_copies(e):
                return [pltpu.make_async_copy(src.at[e], dst, wsem.at[k])
                        for k, (src, dst) in enumerate(((w1_ref, w1_f), (w3_ref, w3_f), (w2_ref, w2_f)))]

            @pl.when(jnp.logical_or(j == 0, cur_e != prev_e))
            def _():
                @pl.when(j == 0)
                def _():
                    for c in weight_copies(cur_e):
                        c.start()

                for c in weight_copies(cur_e):
                    c.wait()
                w1_s[...] = w1_f[...].astype(BF16)
                w3_s[...] = w3_f[...].astype(BF16)
                w2_s[...] = w2_f[...].astype(BF16)

                @pl.when(next_e != cur_e)
                def _():
                    for c in weight_copies(next_e):
                        c.start()

            gather_wait(b, gather_rows_of(j))
            scatter_wait(b, rows_of(j - 2))

            gather_issue(src_next_ref, 1 - b, gather_rows_of(j + 1))
            scatter_issue(dst_prev_ref, 1 - b, rows_of(j - 1))
            x = xs[b][...].astype(BF16)
            hid = (jax.nn.silu(jnp.dot(x, w1_s[...], preferred_element_type=F32))
                   * jnp.dot(x, w3_s[...], preferred_element_type=F32))
            os_[b][...] = jnp.dot(hid.astype(BF16), w2_s[...], preferred_element_type=F32)

    for b in (0, 1):
        @pl.when(jnp.logical_and(j == n_used, j % 2 == b))
        def _(b=b):
            scatter_wait(b, rows_of(j - 2))
            scatter_issue(dst_prev_ref, 1 - b, rows_of(j - 1))
            scatter_wait(1 - b, rows_of(j - 1))


def _moe_experts(hn, w1, w3, w2, tile_e, tile_blk, tile_rows, n_used, src, dst, n_tokens):
    d = hn.shape[1]
    n_tiles = tile_rows.shape[0]
    ff = w1.shape[-1]
    idx_block = lambda fn: pl.BlockSpec((1, 1, MOE_TM), fn, memory_space=pltpu.SMEM)
    grid_spec = pltpu.PrefetchScalarGridSpec(
        num_scalar_prefetch=4,
        grid=(n_tiles,),
        in_specs=[
            idx_block(lambda j, te, tb, tr, nu: (0, 0, 0)),
            idx_block(lambda j, te, tb, tr, nu: (tb[jnp.minimum(j + 1, n_tiles - 1)], 0, 0)),
            idx_block(lambda j, te, tb, tr, nu: (tb[jnp.maximum(j - 1, 0)], 0, 0)),
            pl.BlockSpec(memory_space=pl.ANY),
            pl.BlockSpec(memory_space=pl.ANY),
            pl.BlockSpec(memory_space=pl.ANY),
            pl.BlockSpec(memory_space=pl.ANY),
        ],
        out_specs=pl.BlockSpec(memory_space=pl.ANY),
        scratch_shapes=[pltpu.VMEM((d, ff), BF16), pltpu.VMEM((d, ff), BF16), pltpu.VMEM((ff, d), BF16),
                        pltpu.VMEM((d, ff), F32), pltpu.VMEM((d, ff), F32), pltpu.VMEM((ff, d), F32),
                        pltpu.VMEM((MOE_TM, d), F32), pltpu.VMEM((MOE_TM, d), F32),
                        pltpu.VMEM((MOE_TM, d), F32), pltpu.VMEM((MOE_TM, d), F32),
                        pltpu.SemaphoreType.DMA((2,)), pltpu.SemaphoreType.DMA((2,)),
                        pltpu.SemaphoreType.DMA((3,))],
    )
    src3 = src.reshape(n_tiles, 1, MOE_TM)
    return pl.pallas_call(
        _moe_body,
        grid_spec=grid_spec,
        out_shape=jax.ShapeDtypeStruct((2 * n_tokens, d), F32),
        compiler_params=_params("arbitrary"),
        name="moe_experts",
    )(tile_e, tile_blk, tile_rows, n_used, src3, src3, dst.reshape(n_tiles, 1, MOE_TM), hn, w1, w3, w2)


def _combine_body(h_ref, y0_ref, y1_ref, wk_ref, g_ref, o_ref):
    wk = wk_ref[...]
    y = h_ref[...] + wk[:, 0:1] * y0_ref[...] + wk[:, 1:2] * y1_ref[...]
    o_ref[...] = _rmsnorm_rows(y, g_ref[...]).astype(o_ref.dtype)


def _combine(h, y2, wk, g, out_dtype, tm=512):
    m, d = h.shape
    nblk = m // tm
    return pl.pallas_call(
        _combine_body,
        grid=(nblk,),
        in_specs=[pl.BlockSpec((tm, d), lambda i: (i, 0)), pl.BlockSpec((tm, d), lambda i: (i, 0)),
                  pl.BlockSpec((tm, d), lambda i: (nblk + i, 0)), pl.BlockSpec((tm, LANES), lambda i: (i, 0)),
                  pl.BlockSpec((1, d), lambda i: (0, 0))],
        out_specs=pl.BlockSpec((tm, d), lambda i: (i, 0)),
        out_shape=jax.ShapeDtypeStruct((m, d), out_dtype),
        compiler_params=_params("arbitrary"),
        name="combine_final_norm",
    )(h, y2, y2, wk, g.reshape(1, d).astype(F32))


def _routing_tables(eid, n_tokens):
    n_pairs = 2 * n_tokens
    n_tiles = n_pairs // MOE_TM + N_EXPERTS + 1
    n_rows = n_tiles * MOE_TM
    experts = jnp.arange(N_EXPERTS, dtype=jnp.int32)[None, :]
    e_flat = eid.reshape(n_pairs).astype(jnp.int32)
    counts = jnp.sum(e_flat[:, None] == experts, axis=0, dtype=jnp.int32)
    tiles_per_e = (counts + MOE_TM - 1) // MOE_TM
    tile_end = jnp.cumsum(tiles_per_e)
    tile_start = tile_end - tiles_per_e
    n_used = tile_end[-1]
    filler_end = jnp.cumsum(tiles_per_e * MOE_TM - counts)
    filler = jnp.arange(n_rows - n_pairs, dtype=jnp.int32)[:, None]
    filler_e = jnp.sum(filler_end[None, :] <= filler, axis=1, dtype=jnp.int32)
    keys = jnp.concatenate([2 * e_flat, 2 * filler_e + 1])
    order = jnp.argsort(keys, stable=True).astype(jnp.int32)
    valid = order < n_pairs
    pair = jnp.where(valid, order, 0)
    src_tok = pair // 2
    dst_row = (pair % 2) * n_tokens + pair // 2
    j = jnp.arange(n_tiles, dtype=jnp.int32)
    tile_blk = jnp.minimum(j, n_used - 1)
    tile_e = jnp.sum(tile_end[None, :] <= tile_blk[:, None], axis=1, dtype=jnp.int32)
    own = tile_e[:, None] == experts
    tile_row0 = (tile_blk - jnp.sum(jnp.where(own, tile_start[None, :], 0), axis=1)) * MOE_TM
    own_count = jnp.sum(jnp.where(own, counts[None, :], 0), axis=1)
    tile_rows = jnp.where(j == tile_blk, jnp.clip(own_count - tile_row0, 0, MOE_TM), 0).astype(jnp.int32)
    next_first = jnp.minimum(jnp.sum(jnp.where(own, tile_end[None, :], 0), axis=1), n_used - 1)
    next_e = jnp.sum(jnp.where(next_first[:, None] == j[None, :], tile_e[None, :], 0), axis=1, dtype=jnp.int32)
    return (jnp.concatenate([tile_e, next_e]), tile_blk.astype(jnp.int32), tile_rows,
            n_used.reshape(1).astype(jnp.int32), src_tok.astype(jnp.int32), dst_row.astype(jnp.int32))


def kernel(x, mix_norm_g, w_in, hgrn_lb_logits, hgrn_norm_g, w_branch_a, w_branch_b, w_out, ffn_norm_g,
           router_w_group, router_b_group, router_w_expert, router_b_expert, expert_w1, expert_w3, expert_w2,
           final_norm_g):
    b, seq, d = x.shape
    assert b == 1 and d == D_MODEL and seq % ATT_SUPER == 0 and w_in.shape == (1, D_MODEL, IN_WIDTH)
    h0 = x.reshape(seq, d).astype(F32)

    xn = _rmsnorm(h0, mix_norm_g[0], BF16)
    proj = _matmul(xn, w_in[0], F32, tm=1024, tn=1280, name="in_proj")
    oa = _dilated_attention(proj, seq)
    ob = _hgrn2(proj, hgrn_lb_logits, hgrn_norm_g[0], seq)
    merged = _branch_merge(oa, ob, proj, w_branch_a[0], w_branch_b[0])
    pad = LANES - N_GROUPS - N_EXPERTS
    rw = jnp.concatenate([router_w_group[0], router_w_expert[0], jnp.zeros((d, pad), F32)], axis=1).astype(F32)
    rb = jnp.concatenate([router_b_group[0], router_b_expert[0], jnp.zeros((pad,), F32)]).reshape(1, LANES).astype(F32)
    h1, hn, eid, wk = _out_proj_router(merged, w_out[0].astype(BF16), h0, ffn_norm_g[0], rw, rb)

    tile_e, tile_blk, tile_rows, n_used, src_tok, dst_row = _routing_tables(eid[:, :2], seq)
    w1 = expert_w1[0].reshape(N_EXPERTS, D_MODEL, EXPERT_FF)
    w3 = expert_w3[0].reshape(N_EXPERTS, D_MODEL, EXPERT_FF)
    w2 = expert_w2[0].reshape(N_EXPERTS, EXPERT_FF, D_MODEL)
    y2 = _moe_experts(hn, w1, w3, w2, tile_e, tile_blk, tile_rows, n_used, src_tok, dst_row, seq)

    out = _combine(h1, y2, wk, final_norm_g, x.dtype)
    return out.reshape(b, seq, d)
```
